```python
import math
import jax, jax.numpy as jnp
from jax import lax
import numpy as np

D_MODEL = 1024
BATCH = 32
SEQ = 256
DEPTH = 2
DEC_BATCH = 8
DEC_SEQ = 2048
PAST_LEN = 512

GRID_W = 64
QBLK = 128
ROPE_THETA = 10000.0
EPS = 1e-6

N_BRANCH = 4
BRANCH_W = D_MODEL // N_BRANCH
A_HEADS = 4
A_KV_HEADS = 2
A_HEAD_DIM = BRANCH_W // A_HEADS
B_D_INNER = BRANCH_W
B_HEAD_DIM = 64
B_HEADS = B_D_INNER // B_HEAD_DIM
B_GROUPS = 2
B_STATE = 64
B_CONV_W = 5
B_CHUNK = 128
B_XBC = B_D_INNER + 2 * B_GROUPS * B_STATE
C_HEADS = 4
C_V_DIM = BRANCH_W // C_HEADS
C_QK_DIM = C_V_DIM // 2
D_HEADS = 4
D_HEAD_DIM = BRANCH_W // D_HEADS
NA_ROWS = 8
NA_COLS = 16
N_EXPERTS = 32
TOP_K = 4
D_FF = D_MODEL
SWIGLU_LIMIT = 7.0
SWIGLU_ALPHA = 1.702
MOE_BLOCK = 128

IN_SIZES = (A_HEADS * A_HEAD_DIM, A_KV_HEADS * A_HEAD_DIM, A_KV_HEADS * A_HEAD_DIM,
            B_D_INNER, B_XBC, 2 * B_HEADS,
            C_HEADS * 2 * C_QK_DIM, C_HEADS * 2 * C_QK_DIM, C_HEADS * C_V_DIM,
            D_HEADS * D_HEAD_DIM, D_HEADS * D_HEAD_DIM, D_HEADS * D_HEAD_DIM,
            N_BRANCH * D_MODEL)
IN_COLS = sum(IN_SIZES)

kernel_name = 'hybrid_diffusion_prefix_trunk_step'


def rmsnorm(x, g):
    x32 = x.astype(jnp.float32)
    y = x32 * lax.rsqrt(jnp.mean(x32 * x32, axis=-1, keepdims=True) + EPS)
    return (y * g.astype(jnp.float32)).astype(x.dtype)


def split_columns(u):
    parts = []
    off = 0
    for n in IN_SIZES:
        parts.append(u[..., off:off + n])
        off += n
    return parts


def axial_rope_tables(t, dim):
    pos = jnp.arange(t)
    rows = (pos // GRID_W).astype(jnp.float32)
    cols = (pos % GRID_W).astype(jnp.float32)
    axis_dim = dim // 2
    inv = ROPE_THETA ** (-jnp.arange(0, axis_dim, 2, dtype=jnp.float32) / axis_dim)
    ang_r = rows[:, None] * inv[None, :]
    ang_c = cols[:, None] * inv[None, :]
    ang = jnp.concatenate([ang_r, ang_r, ang_c, ang_c], axis=-1)
    return jnp.cos(ang), jnp.sin(ang)


def apply_rope(x, cos, sin):
    x1, x2, x3, x4 = jnp.split(x, 4, axis=-1)
    rot = jnp.concatenate([-x2, x1, -x4, x3], axis=-1)
    return x * cos[:, None, :].astype(x.dtype) + rot * sin[:, None, :].astype(x.dtype)


def map_query_blocks(fn, batched, positional=()):
    t = batched[0].shape[1]
    nb = t // QBLK
    bx = tuple(jnp.swapaxes(a.reshape(a.shape[0], nb, QBLK, *a.shape[2:]), 0, 1) for a in batched)
    px = tuple(a.reshape(nb, QBLK, *a.shape[1:]) for a in positional)
    out = lax.map(lambda args: fn(*args), bx + px)
    out = jnp.swapaxes(out, 0, 1)
    return out.reshape(out.shape[0], t, *out.shape[3:])


def attention_gqa(q, k, v):
    b, t, hq, d = q.shape
    hkv = k.shape[2]
    scale = d ** -0.5
    qg = q.reshape(b, t, hkv, hq // hkv, d)

    def block(qb):
        s = jnp.einsum('bqhgd,bkhd->bhgqk', qb, k).astype(jnp.float32) * scale
        p = jax.nn.softmax(s, axis=-1).astype(v.dtype)
        return jnp.einsum('bhgqk,bkhd->bqhgd', p, v)

    return map_query_blocks(block, (qg,)).reshape(b, t, hq, d)


def attention_diff(q, k, v, lam):
    scale = C_QK_DIM ** -0.5
    k1 = k[..., :C_QK_DIM]
    k2 = k[..., C_QK_DIM:]

    def block(qb):
        s1 = jnp.einsum('bqhd,bkhd->bhqk', qb[..., :C_QK_DIM], k1).astype(jnp.float32) * scale
        s2 = jnp.einsum('bqhd,bkhd->bhqk', qb[..., C_QK_DIM:], k2).astype(jnp.float32) * scale
        p = jax.nn.softmax(s1, axis=-1) - lam * jax.nn.softmax(s2, axis=-1)
        return jnp.einsum('bhqk,bkhd->bqhd', p.astype(v.dtype), v)

    return map_query_blocks(block, (q,))


def neighbourhood_indices(t):
    rows = t // GRID_W
    kr = min(NA_ROWS, rows)
    pos = jnp.arange(t, dtype=jnp.int32)
    r = pos // GRID_W
    col = pos % GRID_W
    r0 = jnp.clip(r - kr // 2, 0, rows - kr)
    c0 = jnp.clip(col - NA_COLS // 2, 0, GRID_W - NA_COLS)
    key_r = r0[:, None] + jnp.arange(kr, dtype=jnp.int32)[None, :]
    key_c = c0[:, None] + jnp.arange(NA_COLS, dtype=jnp.int32)[None, :]
    key_idx = (key_r[:, :, None] * GRID_W + key_c[:, None, :]).reshape(t, kr * NA_COLS)
    off_r = key_r - r[:, None] + (NA_ROWS - 1)
    off_c = key_c - col[:, None] + (NA_COLS - 1)
    return key_idx, off_r, off_c


def attention_neighbourhood(q, k, v, rpb, ctx_k, ctx_v):
    b, t, h, d = q.shape
    scale = d ** -0.5
    key_idx, off_r, off_c = neighbourhood_indices(t)

    def block(qb, idx, orr, occ):
        kb = k[:, idx]
        vb = v[:, idx]
        bias = rpb[:, orr[:, :, None], occ[:, None, :]].reshape(h, QBLK, -1)
        s_loc = jnp.einsum('bqhd,bqnhd->bhqn', qb, kb).astype(jnp.float32) * scale + bias.astype(jnp.float32)[None]
        s_ctx = jnp.einsum('bqhd,bkhd->bhqk', qb, ctx_k).astype(jnp.float32) * scale
        kn = s_loc.shape[-1]
        p = jax.nn.softmax(jnp.concatenate([s_loc, s_ctx], axis=-1), axis=-1).astype(v.dtype)
        return (jnp.einsum('bhqn,bqnhd->bqhd', p[..., :kn], vb)
                + jnp.einsum('bhqk,bkhd->bqhd', p[..., kn:], ctx_v))

    return map_query_blocks(block, (q,), (key_idx, off_r, off_c))


def centred_dwconv(x, w, bias):
    ch = x.shape[-1]
    y = lax.conv_general_dilated(x, w[:, None, :].astype(x.dtype), window_strides=(1,),
                                 padding=[(B_CONV_W // 2, B_CONV_W // 2)],
                                 dimension_numbers=('NWC', 'WIO', 'NWC'),
                                 feature_group_count=ch)
    return y + bias.astype(x.dtype)


def ssd_scan(x, da, bm, cm, init):
    b, t, h, p = x.shape
    n = bm.shape[-1]
    nc = t // B_CHUNK
    xc = x.reshape(b, nc, B_CHUNK, h, p).astype(jnp.float32)
    bc = bm.reshape(b, nc, B_CHUNK, h, n).astype(jnp.float32)
    cc = cm.reshape(b, nc, B_CHUNK, h, n).astype(jnp.float32)
    a_cs = jnp.cumsum(da.reshape(b, nc, B_CHUNK, h).astype(jnp.float32), axis=2)
    lower = jnp.tril(jnp.ones((B_CHUNK, B_CHUNK), dtype=bool))
    seg = a_cs[:, :, :, None, :] - a_cs[:, :, None, :, :]
    decay = jnp.exp(jnp.where(lower[None, None, :, :, None], seg, -jnp.inf))
    scores = jnp.einsum('bclhn,bcshn->bclsh', cc, bc) * decay
    y_diag = jnp.einsum('bclsh,bcshp->bclhp', scores, xc)
    to_end = jnp.exp(a_cs[:, :, -1:, :] - a_cs)
    chunk_states = jnp.einsum('bclhn,bclhp->bchpn', bc * to_end[..., None], xc)
    chunk_decay = jnp.exp(a_cs[:, :, -1, :])

    def step(state, inp):
        cs, cd = inp
        return state * cd[:, :, None, None] + cs, state

    final, entering = lax.scan(step, init.astype(jnp.float32),
                               (jnp.moveaxis(chunk_states, 1, 0), jnp.moveaxis(chunk_decay, 1, 0)))
    entering = jnp.moveaxis(entering, 0, 1)
    y_off = jnp.einsum('bclhn,bchpn->bclhp', cc * jnp.exp(a_cs)[..., None], entering)
    y = (y_diag + y_off).reshape(b, t, h, p)
    return y.astype(x.dtype), final.astype(x.dtype)


def ssd_branch(z, xbc, dt_raw, lw, init):
    b, t, _ = z.shape
    xbc = jax.nn.silu(centred_dwconv(xbc, lw['b_conv_w'], lw['b_conv_b']))
    xs = xbc[..., :B_D_INNER].reshape(b, t, B_HEADS, B_HEAD_DIM)
    rep = B_HEADS // B_GROUPS
    bm = jnp.repeat(xbc[..., B_D_INNER:B_D_INNER + B_GROUPS * B_STATE].reshape(b, t, B_GROUPS, B_STATE), rep, axis=2)
    cm = jnp.repeat(xbc[..., B_D_INNER + B_GROUPS * B_STATE:].reshape(b, t, B_GROUPS, B_STATE), rep, axis=2)
    dt = jax.nn.softplus(dt_raw.reshape(b, t, 2, B_HEADS).astype(jnp.float32)
                         + lw['b_dt_bias'].astype(jnp.float32))
    da = dt * (-jnp.exp(lw['b_a_log'].astype(jnp.float32)))
    xdt = (xs[:, :, None] * dt[..., None]).astype(xs.dtype)
    y_f, s_f = ssd_scan(xdt[:, :, 0], da[:, :, 0], bm, cm, init[:, 0])
    y_b, s_b = ssd_scan(jnp.flip(xdt[:, :, 1], axis=1), jnp.flip(da[:, :, 1], axis=1),
                        jnp.flip(bm, axis=1), jnp.flip(cm, axis=1), init[:, 1])
    y = y_f + jnp.flip(y_b, axis=1) + lw['b_d'][:, None] * xs
    y = rmsnorm(y.reshape(b, t, B_D_INNER) * jax.nn.silu(z), lw['b_norm_g'])
    return y, jnp.stack([s_f, s_b], axis=1)


def token_mixers(h, lw, lam_init, cache):
    b, t, _ = h.shape
    u = h @ lw['w_in']
    (aq, ak, av, bz, bxbc, bdt, cq, ck, cv, dq, dk, dv, gate_logits) = split_columns(u)
    aq = rmsnorm(aq.reshape(b, t, A_HEADS, A_HEAD_DIM), lw['a_q_g'])
    ak = rmsnorm(ak.reshape(b, t, A_KV_HEADS, A_HEAD_DIM), lw['a_k_g'])
    av = av.reshape(b, t, A_KV_HEADS, A_HEAD_DIM)
    cq = cq.reshape(b, t, C_HEADS, 2 * C_QK_DIM)
    ck = ck.reshape(b, t, C_HEADS, 2 * C_QK_DIM)
    cv = cv.reshape(b, t, C_HEADS, C_V_DIM)
    dq = dq.reshape(b, t, D_HEADS, D_HEAD_DIM)
    dk = dk.reshape(b, t, D_HEADS, D_HEAD_DIM)
    dv = dv.reshape(b, t, D_HEADS, D_HEAD_DIM)
    lq = lw['c_lam'].astype(jnp.float32)
    lam = jnp.exp(jnp.sum(lq[0] * lq[1])) - jnp.exp(jnp.sum(lq[2] * lq[3])) + lam_init

    if cache is None:
        o_a = attention_gqa(aq, ak, av)
        o_c = attention_diff(cq, ck, cv, lam)
        o_d = attention_gqa(dq, dk, dv)
        init = jnp.zeros((b, 2, B_HEADS, B_HEAD_DIM, B_STATE), h.dtype)
        o_b, ssd_state = ssd_branch(bz, bxbc, bdt, lw, init)
        new_cache = (ak, av, ck, cv, dk, dv, ssd_state)
    else:
        ctx_ak, ctx_av, ctx_ck, ctx_cv, ctx_dk, ctx_dv, init = cache
        cos_a, sin_a = axial_rope_tables(t, A_HEAD_DIM)
        cos_c, sin_c = axial_rope_tables(t, C_QK_DIM)
        o_a = attention_gqa(apply_rope(aq, cos_a, sin_a),
                            jnp.concatenate([apply_rope(ak, cos_a, sin_a), ctx_ak.astype(ak.dtype)], axis=1),
                            jnp.concatenate([av, ctx_av.astype(av.dtype)], axis=1))
        cq_r = apply_rope(cq.reshape(b, t, C_HEADS * 2, C_QK_DIM), cos_c, sin_c).reshape(b, t, C_HEADS, 2 * C_QK_DIM)
        ck_r = apply_rope(ck.reshape(b, t, C_HEADS * 2, C_QK_DIM), cos_c, sin_c).reshape(b, t, C_HEADS, 2 * C_QK_DIM)
        o_c = attention_diff(cq_r, jnp.concatenate([ck_r, ctx_ck.astype(ck.dtype)], axis=1),
                             jnp.concatenate([cv, ctx_cv.astype(cv.dtype)], axis=1), lam)
        o_d = attention_neighbourhood(dq, dk, dv, lw['d_rpb'], ctx_dk.astype(dk.dtype), ctx_dv.astype(dv.dtype))
        o_b, _ = ssd_branch(bz, bxbc, bdt, lw, init)
        new_cache = None

    o_c = (rmsnorm(o_c, lw['c_subln_g']) * (1.0 - lam_init)).reshape(b, t, BRANCH_W)
    branches = (o_a.reshape(b, t, BRANCH_W), o_b, o_c, o_d.reshape(b, t, BRANCH_W))
    gates = jax.nn.sigmoid(gate_logits.astype(jnp.float32)).astype(h.dtype).reshape(b, t, N_BRANCH, D_MODEL)
    merged = gates[:, :, 0] * (branches[0] @ lw['w_branch'][0])
    for i in range(1, N_BRANCH):
        merged = merged + gates[:, :, i] * (branches[i] @ lw['w_branch'][i])
    return merged @ lw['w_out'], new_cache


def moe_ffn(h, lw):
    n, d = h.shape
    logits = (h @ lw['router_w'] + lw['router_b']).astype(jnp.float32)
    top_val, top_idx = lax.top_k(logits, TOP_K)
    gate = jax.nn.softmax(top_val, axis=-1).astype(h.dtype)
    n_assign = n * TOP_K
    n_blocks = -(-n_assign // MOE_BLOCK) + N_EXPERTS
    n_rows = n_blocks * MOE_BLOCK
    expert = top_idx.reshape(-1)
    token = jnp.arange(n_assign, dtype=jnp.int32) // TOP_K
    order = jnp.argsort(expert)
    e_sorted = expert[order]
    counts = jnp.bincount(expert, length=N_EXPERTS)
    starts = jnp.cumsum(counts) - counts
    padded = (counts + MOE_BLOCK - 1) // MOE_BLOCK * MOE_BLOCK
    pad_ends = jnp.cumsum(padded)
    pad_starts = pad_ends - padded
    dest = pad_starts[e_sorted] + jnp.arange(n_assign) - starts[e_sorted]
    row_token = jnp.zeros((n_rows,), jnp.int32).at[dest].set(token[order])
    row_gate = jnp.zeros((n_rows,), h.dtype).at[dest].set(gate.reshape(-1)[order])
    block_expert = jnp.minimum(jnp.searchsorted(pad_ends, jnp.arange(n_blocks) * MOE_BLOCK, side='right'),
                               N_EXPERTS - 1)
    xb = h[row_token].reshape(n_blocks, MOE_BLOCK, d)
    w1, b1, w2, b2 = lw['moe_w1'], lw['moe_b1'], lw['moe_w2'], lw['moe_b2']

    def expert_block(args):
        xe, e = args
        u = xe @ w1[e] + b1[e]
        glu = jnp.minimum(u[:, :D_FF], SWIGLU_LIMIT)
        lin = jnp.clip(u[:, D_FF:], -SWIGLU_LIMIT, SWIGLU_LIMIT)
        act = glu * jax.nn.sigmoid(SWIGLU_ALPHA * glu) * (lin + 1.0)
        return act @ w2[e] + b2[e]

    yb = lax.map(expert_block, (xb, block_expert)).reshape(n_rows, d)
    return jax.ops.segment_sum(yb * row_gate[:, None], row_token, num_segments=n)


def trunk_layer(x, mod, lw, lam_init, cache):
    sh1, sc1, g1, sh2, sc2, g2 = jnp.split(mod, 6, axis=-1)
    h = rmsnorm(x, lw['norm1_g']) * (1 + sc1) + sh1
    y, new_cache = token_mixers(h, lw, lam_init, cache)
    x = x + g1 * y
    h = rmsnorm(x, lw['norm2_g']) * (1 + sc2) + sh2
    b, t, d = x.shape
    x = x + g2 * moe_ffn(h.reshape(b * t, d), lw).reshape(b, t, d)
    return x, new_cache


def setup_inputs(seed: int = 0) -> dict:
    key = jax.random.key(seed)
    ks = iter(jax.random.split(key, 48))

    def nrm(shape, scale):
        return jax.random.normal(next(ks), shape, jnp.float32) * scale

    def gain(shape):
        return 1.0 + nrm(shape, 0.1)

    dt = jnp.exp(jax.random.uniform(next(ks), (DEPTH, 2, B_HEADS), jnp.float32)
                 * (math.log(0.1) - math.log(0.001)) + math.log(0.001))
    dt_bias = dt + jnp.log(-jnp.expm1(-dt))
    a_log = jnp.log(jax.random.uniform(next(ks), (DEPTH, 2, B_HEADS), jnp.float32, 1.0, 16.0))
    return {
        'x_prompt': nrm((BATCH, SEQ, D_MODEL), 1.0),
        'x_sample': nrm((DEC_BATCH, DEC_SEQ, D_MODEL), 1.0),
        'c': nrm((DEC_BATCH, D_MODEL), 1.0),
        'cache_a_k': nrm((DEC_BATCH, DEPTH, PAST_LEN, A_KV_HEADS, A_HEAD_DIM), 1.0),
        'cache_a_v': nrm((DEC_BATCH, DEPTH, PAST_LEN, A_KV_HEADS, A_HEAD_DIM), 1.0),
        'cache_c_k': nrm((DEC_BATCH, DEPTH, PAST_LEN, C_HEADS, 2 * C_QK_DIM), 1.0),
        'cache_c_v': nrm((DEC_BATCH, DEPTH, PAST_LEN, C_HEADS, C_V_DIM), 1.0),
        'cache_d_k': nrm((DEC_BATCH, DEPTH, PAST_LEN, D_HEADS, D_HEAD_DIM), 1.0),
        'cache_d_v': nrm((DEC_BATCH, DEPTH, PAST_LEN, D_HEADS, D_HEAD_DIM), 1.0),
        'state_ssd': nrm((DEC_BATCH, DEPTH, 2, B_HEADS, B_HEAD_DIM, B_STATE), 0.5),
        'c_ctx': nrm((D_MODEL,), 1.0),
        'norm1_g': gain((DEPTH, D_MODEL)),
        'norm2_g': gain((DEPTH, D_MODEL)),
        'w_mod': nrm((DEPTH, D_MODEL, 6 * D_MODEL), 0.5 * D_MODEL ** -0.5),
        'b_mod': nrm((DEPTH, 6 * D_MODEL), 0.02),
        'w_in': nrm((DEPTH, D_MODEL, IN_COLS), D_MODEL ** -0.5),
        'a_q_g': gain((DEPTH, A_HEAD_DIM)),
        'a_k_g': gain((DEPTH, A_HEAD_DIM)),
        'b_conv_w': nrm((DEPTH, B_CONV_W, B_XBC), B_CONV_W ** -0.5),
        'b_conv_b': nrm((DEPTH, B_XBC), 0.02),
        'b_dt_bias': dt_bias,
        'b_a_log': a_log,
        'b_d': gain((DEPTH, B_HEADS)),
        'b_norm_g': gain((DEPTH, B_D_INNER)),
        'c_lam': nrm((DEPTH, 4, C_QK_DIM), 0.1),
        'c_subln_g': gain((DEPTH, C_V_DIM)),
        'd_rpb': nrm((DEPTH, D_HEADS, 2 * NA_ROWS - 1, 2 * NA_COLS - 1), 0.1),
        'w_branch': nrm((DEPTH, N_BRANCH, BRANCH_W, D_MODEL), BRANCH_W ** -0.5),
        'w_out': nrm((DEPTH, D_MODEL, D_MODEL), D_MODEL ** -0.5),
        'router_w': nrm((DEPTH, D_MODEL, N_EXPERTS), D_MODEL ** -0.5),
        'router_b': nrm((DEPTH, N_EXPERTS), 0.01),
        'moe_w1': nrm((DEPTH, N_EXPERTS, D_MODEL, 2 * D_FF), D_MODEL ** -0.5),
        'moe_b1': nrm((DEPTH, N_EXPERTS, 2 * D_FF), 0.01),
        'moe_w2': nrm((DEPTH, N_EXPERTS, D_FF, D_MODEL), D_FF ** -0.5),
        'moe_b2': nrm((DEPTH, N_EXPERTS, D_MODEL), 0.01),
        'final_g': gain((D_MODEL,)),
    }


def reference(x_prompt, x_sample, c, cache_a_k, cache_a_v, cache_c_k, cache_c_v, cache_d_k, cache_d_v,
              state_ssd, c_ctx, norm1_g, norm2_g, w_mod, b_mod, w_in, a_q_g, a_k_g, b_conv_w, b_conv_b,
              b_dt_bias, b_a_log, b_d, b_norm_g, c_lam, c_subln_g, d_rpb, w_branch, w_out, router_w, router_b,
              moe_w1, moe_b1, moe_w2, moe_b2, final_g):
    xp = x_prompt
    xs = x_sample
    ak_l, av_l, ck_l, cv_l, dk_l, dv_l, st_l = [], [], [], [], [], [], []
    for l in range(DEPTH):
        lw = {
            'norm1_g': norm1_g[l], 'norm2_g': norm2_g[l], 'w_in': w_in[l],
            'a_q_g': a_q_g[l], 'a_k_g': a_k_g[l],
            'b_conv_w': b_conv_w[l], 'b_conv_b': b_conv_b[l], 'b_dt_bias': b_dt_bias[l],
            'b_a_log': b_a_log[l], 'b_d': b_d[l], 'b_norm_g': b_norm_g[l],
            'c_lam': c_lam[l], 'c_subln_g': c_subln_g[l], 'd_rpb': d_rpb[l],
            'w_branch': w_branch[l], 'w_out': w_out[l],
            'router_w': router_w[l], 'router_b': router_b[l],
            'moe_w1': moe_w1[l], 'moe_b1': moe_b1[l], 'moe_w2': moe_w2[l], 'moe_b2': moe_b2[l],
        }
        lam_init = 0.8 - 0.6 * math.exp(-0.3 * l)
        mod_ctx = (jax.nn.silu(c_ctx) @ w_mod[l] + b_mod[l])[None, None, :]
        mod_lat = (jax.nn.silu(c) @ w_mod[l] + b_mod[l])[:, None, :]
        xp, (ak, av, ck, cv, dk, dv, st) = trunk_layer(xp, mod_ctx, lw, lam_init, None)
        ak_l.append(ak); av_l.append(av); ck_l.append(ck); cv_l.append(cv)
        dk_l.append(dk); dv_l.append(dv); st_l.append(st)
        cache_l = (cache_a_k[:, l], cache_a_v[:, l], cache_c_k[:, l], cache_c_v[:, l],
                   cache_d_k[:, l], cache_d_v[:, l], state_ssd[:, l])
        xs, _ = trunk_layer(xs, mod_lat, lw, lam_init, cache_l)
    y_prompt = rmsnorm(xp, final_g)
    y_sample = rmsnorm(xs, final_g)
    new_a_k = jnp.stack(ak_l, axis=1)
    new_a_v = jnp.stack(av_l, axis=1)
    new_c_k = jnp.stack(ck_l, axis=1)
    new_c_v = jnp.stack(cv_l, axis=1)
    new_d_k = jnp.stack(dk_l, axis=1)
    new_d_v = jnp.stack(dv_l, axis=1)
    new_state_ssd = jnp.stack(st_l, axis=1)
    return (y_prompt, y_sample, new_a_k, new_a_v, new_c_k, new_c_v, new_d_k, new_d_v, new_state_ssd)
```

```python
import functools
import math

import jax
import jax.numpy as jnp
from jax import lax
from jax.experimental import pallas as pl
from jax.experimental.pallas import tpu as pltpu

F32 = jnp.float32
BF16 = jnp.bfloat16

D_MODEL = 1024
N_BRANCH = 4
BRANCH_W = 256
HEAD_W = 64
N_HEADS = 4
GRID_W = 64
ROPE_THETA = 10000.0
EPS = 1e-6
A_KV_HEADS = 2
C_QK_DIM = 32
B_HEADS = 4
B_STATE = 64
B_CHUNK = 128
B_XBC = 512
B_CONV_W = 5
CONV_PAD = 8
NA_ROWS = 8
NA_COLS = 16
N_EXPERTS = 32
TOP_K = 4
D_FF = 1024
SWIGLU_LIMIT = 7.0
SWIGLU_ALPHA = 1.702
LANES = 128
NEG = -1e30

TOKEN_TILE = 256
MOE_ROWS = 256
W1_COLS = 2944
VMEM_LIMIT = 56 * 1024 * 1024


def _cparams(*sem):
    return pltpu.CompilerParams(dimension_semantics=sem, vmem_limit_bytes=VMEM_LIMIT)


def _dot(a, b):
    return jnp.dot(a, b, preferred_element_type=F32)


def _dot_nt(a, b):
    return lax.dot_general(a, b, (((1,), (1,)), ((), ())), preferred_element_type=F32)


def _dot_tn(a, b):
    return lax.dot_general(a, b, (((0,), (0,)), ((), ())), preferred_element_type=F32)


def _sigmoid(x):
    return 1.0 / (1.0 + jnp.exp(-x))


def _split3(x):
    h1 = x.astype(BF16)
    r1 = x - h1.astype(F32)
    h2 = r1.astype(BF16)
    h3 = (r1 - h2.astype(F32)).astype(BF16)
    return h1, h2, h3


def _norm_mod(x, g, shift, scale):
    ms = jnp.mean(x * x, axis=-1, keepdims=True)
    return (x * lax.rsqrt(ms + EPS)) * g * (1.0 + scale) + shift


def _mod_kernel(c_ref, w_ref, b_ref, o_ref):
    c = c_ref[...]
    s = (c * _sigmoid(c)).astype(BF16)
    o_ref[...] = _dot(s, w_ref[...].astype(BF16)) + b_ref[...]


def _modulation(cvec, w_mod, b_mod):
    n = w_mod.shape[1]
    tn = 1536
    return pl.pallas_call(
        _mod_kernel,
        grid=(n // tn,),
        in_specs=[pl.BlockSpec((16, D_MODEL), lambda j: (0, 0)),
                  pl.BlockSpec((D_MODEL, tn), lambda j: (0, j)),
                  pl.BlockSpec((1, tn), lambda j: (0, j))],
        out_specs=pl.BlockSpec((16, tn), lambda j: (0, j)),
        out_shape=jax.ShapeDtypeStruct((16, n), F32),
        compiler_params=_cparams("parallel"),
        name="modulation",
    )(cvec, w_mod, b_mod)


def _mod_row(i, n_prompt_tiles, tiles_per_sample):
    return jnp.where(i < n_prompt_tiles, 0, 1 + (i - n_prompt_tiles) // tiles_per_sample)


def _inproj_kernel(x_ref, mod_ref, g_ref, w_ref, o_ref):
    mod = mod_ref[...]
    h = _norm_mod(x_ref[...], g_ref[...], mod[0:1], mod[1:2])
    o_ref[...] = _dot(h.astype(BF16), w_ref[...])


def _in_proj(x, mod_tab, g, w1, n_prompt_tiles, tiles_per_sample):
    nt = x.shape[0]
    mrow = functools.partial(_mod_row, n_prompt_tiles=n_prompt_tiles, tiles_per_sample=tiles_per_sample)
    return pl.pallas_call(
        _inproj_kernel,
        grid=(nt // TOKEN_TILE,),
        in_specs=[pl.BlockSpec((TOKEN_TILE, D_MODEL), lambda i: (i, 0)),
                  pl.BlockSpec((None, 8, D_MODEL), lambda i: (mrow(i), 0, 0)),
                  pl.BlockSpec((1, D_MODEL), lambda i: (0, 0)),
                  pl.BlockSpec((D_MODEL, W1_COLS), lambda i: (0, 0))],
        out_specs=pl.BlockSpec((TOKEN_TILE, W1_COLS), lambda i: (i, 0)),
        out_shape=jax.ShapeDtypeStruct((nt, W1_COLS), F32),
        compiler_params=_cparams("parallel"),
        name="in_proj",
    )(x, mod_tab, g, w1)


def _mha_kernel(lam_ref, q_ref, k_ref, v_ref, o_ref, *, n_maps, scale):
    q = q_ref[...]
    k = k_ref[...]
    v = v_ref[...]
    tq = q.shape[0]
    lane = lax.broadcasted_iota(jnp.int32, (1, BRANCH_W), 1)
    sub_w = HEAD_W // n_maps
    acc = jnp.zeros((tq, BRANCH_W), F32)
    for h in range(N_HEADS):
        oh = None
        for j in range(n_maps):
            qm = jnp.where((lane // sub_w) == (h * n_maps + j), q, jnp.zeros_like(q))
            s = _dot_nt(qm, k) * scale
            m = jnp.max(s, axis=-1, keepdims=True)
            p = jnp.exp(s - m)
            l = jnp.sum(p, axis=-1, keepdims=True)
            pv = _dot(p.astype(BF16), v) * (1.0 / l)
            oh = pv if j == 0 else oh - lam_ref[0] * pv
        acc = jnp.where((lane // HEAD_W) == h, oh, acc)
    o_ref[...] = acc


def _mha(q, k, v, lam, *, n_maps, scale, tq=256):
    b, t, _ = q.shape
    kk = k.shape[1]
    return pl.pallas_call(
        functools.partial(_mha_kernel, n_maps=n_maps, scale=scale),
        grid=(b, t // tq),
        in_specs=[pl.BlockSpec(memory_space=pltpu.SMEM),
                  pl.BlockSpec((None, tq, BRANCH_W), lambda i, j: (i, j, 0)),
                  pl.BlockSpec((None, kk, BRANCH_W), lambda i, j: (i, 0, 0)),
                  pl.BlockSpec((None, kk, BRANCH_W), lambda i, j: (i, 0, 0))],
        out_specs=pl.BlockSpec((None, tq, BRANCH_W), lambda i, j: (i, j, 0)),
        out_shape=jax.ShapeDtypeStruct((b, t, BRANCH_W), F32),
        compiler_params=_cparams("parallel", "parallel"),
        name=f"mha_maps{n_maps}_k{kk}",
    )(lam, q, k, v)


def _na_row0(r, n_rows):
    return jnp.clip(r - NA_ROWS // 2, 0, n_rows - NA_ROWS)


def _na_kernel(q_ref, k_ref, v_ref, ck_ref, cv_ref, bias_ref, o_ref, *, n_rows, scale):
    r = pl.program_id(1)
    start = pl.multiple_of(_na_row0(r, n_rows) * GRID_W, GRID_W)
    win = NA_ROWS * GRID_W
    q = q_ref[...]
    kw = k_ref[pl.ds(start, win), :]
    vw = v_ref[pl.ds(start, win), :]
    lane = lax.broadcasted_iota(jnp.int32, (1, BRANCH_W), 1)
    qs = jnp.concatenate(
        [jnp.where((lane // HEAD_W) == h, q, jnp.zeros_like(q)) for h in range(N_HEADS)], axis=0)
    s_loc = _dot_nt(qs, kw) * scale + bias_ref[...]
    s_ctx = _dot_nt(qs, ck_ref[...]) * scale
    m = jnp.maximum(jnp.max(s_loc, axis=-1, keepdims=True), jnp.max(s_ctx, axis=-1, keepdims=True))
    p_loc = jnp.exp(s_loc - m)
    p_ctx = jnp.exp(s_ctx - m)
    l = jnp.sum(p_loc, axis=-1, keepdims=True) + jnp.sum(p_ctx, axis=-1, keepdims=True)
    o = (_dot(p_loc.astype(BF16), vw) + _dot(p_ctx.astype(BF16), cv_ref[...])) * (1.0 / l)
    acc = jnp.zeros((GRID_W, BRANCH_W), F32)
    for h in range(N_HEADS):
        acc = jnp.where((lane // HEAD_W) == h, o[h * GRID_W:(h + 1) * GRID_W], acc)
    o_ref[...] = acc


def _na_attention(q, k, v, ck, cv, bias_tab, *, scale):
    b, t, _ = q.shape
    n_rows = t // GRID_W
    kk = ck.shape[1]
    win = NA_ROWS * GRID_W
    return pl.pallas_call(
        functools.partial(_na_kernel, n_rows=n_rows, scale=scale),
        grid=(b, n_rows),
        in_specs=[pl.BlockSpec((None, GRID_W, BRANCH_W), lambda i, r: (i, r, 0)),
                  pl.BlockSpec((None, t, BRANCH_W), lambda i, r: (i, 0, 0)),
                  pl.BlockSpec((None, t, BRANCH_W), lambda i, r: (i, 0, 0)),
                  pl.BlockSpec((None, kk, BRANCH_W), lambda i, r: (i, 0, 0)),
                  pl.BlockSpec((None, kk, BRANCH_W), lambda i, r: (i, 0, 0)),
                  pl.BlockSpec((None, N_HEADS * GRID_W, win), lambda i, r: (r - _na_row0(r, n_rows), 0, 0))],
        out_specs=pl.BlockSpec((None, GRID_W, BRANCH_W), lambda i, r: (i, r, 0)),
        out_shape=jax.ShapeDtypeStruct((b, t, BRANCH_W), F32),
        compiler_params=_cparams("parallel", "parallel"),
        name="na_attention",
    )(q, k, v, ck, cv, bias_tab)


def _na_bias_table(rpb, n_rows):
    var = jnp.arange(NA_ROWS)[:, None, None, None]
    col = jnp.arange(GRID_W)[None, :, None, None]
    j = jnp.arange(NA_ROWS)[None, None, :, None]
    kc = jnp.arange(GRID_W)[None, None, None, :]
    c0 = jnp.clip(col - NA_COLS // 2, 0, GRID_W - NA_COLS)
    valid = (kc >= c0) & (kc < c0 + NA_COLS)
    off_r = j - var + (NA_ROWS - 1)
    off_c = jnp.clip(kc - col + (NA_COLS - 1), 0, 2 * NA_COLS - 2)
    shape = (NA_ROWS, GRID_W, NA_ROWS, GRID_W)
    off_r = jnp.broadcast_to(off_r, shape)
    off_c = jnp.broadcast_to(off_c, shape)
    tab = rpb[:, off_r, off_c]
    tab = jnp.where(jnp.broadcast_to(valid, shape)[None], tab, NEG)
    tab = jnp.transpose(tab, (1, 0, 2, 3, 4))
    return tab.reshape(NA_ROWS, N_HEADS * GRID_W, NA_ROWS * GRID_W).astype(F32)


def _softplus(x):
    return jnp.maximum(x, 0.0) + jnp.log1p(jnp.exp(-jnp.abs(x)))


def _expand_heads(colmat, d, lane256):
    out = jnp.zeros((colmat.shape[0], BRANCH_W), F32)
    for h in range(B_HEADS):
        j = d * B_HEADS + h
        out = jnp.where((lane256 // HEAD_W) == h,
                        jnp.broadcast_to(colmat[:, j:j + 1], (colmat.shape[0], BRANCH_W)), out)
    return out


def _ssd_kernel(z_ref, xbc_ref, dtc_ref, dtr_ref, cw_ref, cb_ref, pc_ref, prb_ref, pra_ref, dsk_ref, ng_ref,
                init_ref, o_ref, st_ref, xc_s, yf_s, s_s, *, seq):
    n_chunks = seq // B_CHUNK
    L = B_CHUNK
    ri = lax.broadcasted_iota(jnp.int32, (L, L), 0)
    ci = lax.broadcasted_iota(jnp.int32, (L, L), 1)
    low = ci <= ri
    upp = ri <= ci
    low_b = jnp.where(low, 1.0, 0.0).astype(BF16)
    upp_b = jnp.where(upp, 1.0, 0.0).astype(BF16)
    lane128 = lax.broadcasted_iota(jnp.int32, (1, LANES), 1)
    lane256 = lax.broadcasted_iota(jnp.int32, (1, BRANCH_W), 1)
    row256 = lax.broadcasted_iota(jnp.int32, (BRANCH_W, 1), 0)
    blockmask = (row256 // (2 * HEAD_W)) == (lax.broadcasted_iota(jnp.int32, (1, LANES), 1) // B_STATE)

    cw = cw_ref[...]
    cb = cb_ref[...]

    def conv_body(c, carry):
        base = pl.multiple_of(c * L, L)
        w = xbc_ref[pl.ds(base, L + 2 * CONV_PAD), :]
        acc = jnp.zeros((L, B_XBC), F32) + cb
        for kk in range(B_CONV_W):
            off = CONV_PAD - B_CONV_W // 2 + kk
            acc = acc + w[off:off + L, :] * cw[kk:kk + 1, :]
        xc_s[pl.ds(base, L), :] = acc * _sigmoid(acc)
        return carry

    lax.fori_loop(0, n_chunks, conv_body, 0)

    dt_bias_c = pc_ref[0:1, :]
    a_c = pc_ref[1:2, :]
    dt_bias_r = prb_ref[...]
    a_r = pra_ref[...]
    dsk = dsk_ref[...]
    ng = ng_ref[...]

    def chunk(c, d):
        base = pl.multiple_of(c * L, L)
        xs = xc_s[pl.ds(base, L), 0:BRANCH_W]
        bm = xc_s[pl.ds(base, L), BRANCH_W:BRANCH_W + LANES].astype(BF16)
        cm = xc_s[pl.ds(base, L), BRANCH_W + LANES:B_XBC].astype(BF16)
        dtc = _softplus(dtc_ref[pl.ds(base, L), :] + dt_bias_c)
        dtr = _softplus(dtr_ref[:, pl.ds(base, L)] + dt_bias_r)
        da_c = dtc * a_c
        da_r = dtr * a_r
        tri_c = low_b if d == 0 else upp_b
        tri_r = upp_b if d == 0 else low_b
        c1, c2, c3 = _split3(da_c)
        cum_c = _dot(tri_c, c1) + _dot(tri_c, c2) + _dot(tri_c, c3)
        r1, r2, r3 = _split3(da_r)
        cum_r = _dot(r1, tri_r) + _dot(r2, tri_r) + _dot(r3, tri_r)
        cum_end = cum_c[L - 1:L, :] if d == 0 else cum_c[0:1, :]
        dmask = low if d == 0 else upp

        g0 = _dot_nt(jnp.where(lane128 < B_STATE, cm, jnp.zeros_like(cm)), bm)
        g1 = _dot_nt(jnp.where(lane128 >= B_STATE, cm, jnp.zeros_like(cm)), bm)
        dt_x = _expand_heads(dtc, d, lane256)
        e_a = _expand_heads(jnp.exp(cum_c), d, lane256)
        t_e = _expand_heads(jnp.exp(cum_end - cum_c), d, lane256)
        xdt = xs * dt_x
        xdt_b = xdt.astype(BF16)
        y = jnp.zeros((L, BRANCH_W), F32)
        for h in range(B_HEADS):
            j = d * B_HEADS + h
            col = jnp.broadcast_to(cum_c[:, j:j + 1], (L, L))
            row = jnp.broadcast_to(cum_r[j:j + 1, :], (L, L))
            dec = jnp.exp(jnp.where(dmask, col - row, NEG))
            sc = ((g0 if h < 2 else g1) * dec).astype(BF16)
            y = jnp.where((lane256 // HEAD_W) == h, _dot(sc, xdt_b), y)
        state = s_s[...]
        y = y + _dot_nt(cm, state.astype(BF16)) * e_a
        zmat = _dot_tn((xdt * t_e).astype(BF16), bm)
        e_end = jnp.exp(cum_end)
        cd = jnp.zeros((BRANCH_W, LANES), F32)
        for h in range(B_HEADS):
            j = d * B_HEADS + h
            cd = jnp.where((row256 // HEAD_W) == h, jnp.broadcast_to(e_end[:, j:j + 1], (BRANCH_W, LANES)), cd)
        s_s[...] = state * cd + jnp.where(blockmask, zmat, 0.0)
        return base, xs, y

    s_s[...] = init_ref[0]

    def fwd_body(c, carry):
        base, _, y = chunk(c, 0)
        yf_s[pl.ds(base, L), :] = y
        return carry

    lax.fori_loop(0, n_chunks, fwd_body, 0)
    st_ref[0] = s_s[...]
    s_s[...] = init_ref[1]

    def bwd_body(i, carry):
        c = n_chunks - 1 - i
        base, xs, y = chunk(c, 1)
        z = z_ref[pl.ds(base, L), :]
        yt = (yf_s[pl.ds(base, L), :] + y + dsk * xs) * (z * _sigmoid(z))
        ms = jnp.mean(yt * yt, axis=-1, keepdims=True)
        o_ref[pl.ds(base, L), :] = yt * lax.rsqrt(ms + EPS) * ng
        return carry

    lax.fori_loop(0, n_chunks, bwd_body, 0)
    st_ref[1] = s_s[...]


def _ssd_branch(z, xbc_pad, dt_col, dt_row, conv_w, conv_b, pc, prb, pra, dsk, ng, init):
    b, t, _ = z.shape
    tp = t + 2 * CONV_PAD
    full = lambda *shape: pl.BlockSpec(shape, lambda i: (0,) * len(shape))
    return pl.pallas_call(
        functools.partial(_ssd_kernel, seq=t),
        grid=(b,),
        in_specs=[pl.BlockSpec((None, t, BRANCH_W), lambda i: (i, 0, 0)),
                  pl.BlockSpec((None, tp, B_XBC), lambda i: (i, 0, 0)),
                  pl.BlockSpec((None, t, LANES), lambda i: (i, 0, 0)),
                  pl.BlockSpec((None, 8, t), lambda i: (i, 0, 0)),
                  full(8, B_XBC), full(1, B_XBC), full(8, LANES), full(8, LANES), full(8, LANES),
                  full(1, BRANCH_W), full(1, BRANCH_W),
                  pl.BlockSpec((None, 2, BRANCH_W, LANES), lambda i: (i, 0, 0, 0))],
        out_specs=[pl.BlockSpec((None, t, BRANCH_W), lambda i: (i, 0, 0)),
                   pl.BlockSpec((None, 2, BRANCH_W, LANES), lambda i: (i, 0, 0, 0))],
        out_shape=[jax.ShapeDtypeStruct((b, t, BRANCH_W), F32),
                   jax.ShapeDtypeStruct((b, 2, BRANCH_W, LANES), F32)],
        scratch_shapes=[pltpu.VMEM((t, B_XBC), F32), pltpu.VMEM((t, BRANCH_W), F32),
                        pltpu.VMEM((BRANCH_W, LANES), F32)],
        compiler_params=_cparams("parallel"),
        name=f"ssd_t{t}",
    )(z, xbc_pad, dt_col, dt_row, conv_w, conv_b, pc, prb, pra, dsk, ng, init)


def _merge_kernel(x_ref, mod_ref, g1_ref, g2_ref, br_ref, wg_ref, wb_ref, wo_ref, rwh_ref, rwl_ref, rb_ref,
                  x1_ref, h2_ref, lg_ref):
    x = x_ref[...]
    mod = mod_ref[...]
    h = _norm_mod(x, g1_ref[...], mod[0:1], mod[1:2]).astype(BF16)
    merged = None
    for i in range(N_BRANCH):
        gate = _sigmoid(_dot(h, wg_ref[:, i * D_MODEL:(i + 1) * D_MODEL]))
        proj = _dot(br_ref[i], wb_ref[i])
        merged = gate * proj if i == 0 else merged + gate * proj
    y = _dot(merged.astype(BF16), wo_ref[...])
    x1 = x + mod[2:3] * y
    x1_ref[...] = x1
    h2 = _norm_mod(x1, g2_ref[...], mod[3:4], mod[4:5])
    hh = h2.astype(BF16)
    hl = (h2 - hh.astype(F32)).astype(BF16)
    h2_ref[...] = hh
    rwh = rwh_ref[...]
    lg_ref[...] = _dot(hh, rwh) + _dot(hl, rwh) + _dot(hh, rwl_ref[...]) + rb_ref[...]


def _merge(x, mod_tab, g1, g2, branches, wg, wb, wo, rwh, rwl, rb, n_prompt_tiles, tiles_per_sample):
    nt = x.shape[0]
    mrow = functools.partial(_mod_row, n_prompt_tiles=n_prompt_tiles, tiles_per_sample=tiles_per_sample)
    const = lambda *shape: pl.BlockSpec(shape, lambda i: (0,) * len(shape))
    return pl.pallas_call(
        _merge_kernel,
        grid=(nt // TOKEN_TILE,),
        in_specs=[pl.BlockSpec((TOKEN_TILE, D_MODEL), lambda i: (i, 0)),
                  pl.BlockSpec((None, 8, D_MODEL), lambda i: (mrow(i), 0, 0)),
                  const(1, D_MODEL), const(1, D_MODEL),
                  pl.BlockSpec((N_BRANCH, TOKEN_TILE, BRANCH_W), lambda i: (0, i, 0)),
                  const(D_MODEL, N_BRANCH * D_MODEL), const(N_BRANCH, BRANCH_W, D_MODEL),
                  const(D_MODEL, D_MODEL), const(D_MODEL, LANES), const(D_MODEL, LANES), const(1, LANES)],
        out_specs=[pl.BlockSpec((TOKEN_TILE, D_MODEL), lambda i: (i, 0)),
                   pl.BlockSpec((TOKEN_TILE, D_MODEL), lambda i: (i, 0)),
                   pl.BlockSpec((TOKEN_TILE, LANES), lambda i: (i, 0))],
        out_shape=[jax.ShapeDtypeStruct((nt, D_MODEL), F32),
                   jax.ShapeDtypeStruct((nt, D_MODEL), BF16),
                   jax.ShapeDtypeStruct((nt, LANES), F32)],
        compiler_params=_cparams("parallel"),
        name="merge",
    )(x, mod_tab, g1, g2, branches, wg, wb, wo, rwh, rwl, rb)


def _moe_kernel(be_ref, nu_ref, x_ref, w1_ref, b1_ref, w2_ref, b2_ref, o_ref, w1_s, w2_s):
    i = pl.program_id(0)
    prev = be_ref[jnp.maximum(i - 1, 0)]
    fresh = jnp.logical_or(i == 0, be_ref[i] != prev)

    @pl.when(jnp.logical_and(fresh, i < nu_ref[0]))
    def _():
        w1_s[...] = w1_ref[...].astype(BF16)
        w2_s[...] = w2_ref[...].astype(BF16)

    @pl.when(i < nu_ref[0])
    def _():
        u = _dot(x_ref[...], w1_s[...]) + b1_ref[...]
        glu = jnp.minimum(u[:, :D_FF], SWIGLU_LIMIT)
        lin = jnp.clip(u[:, D_FF:], -SWIGLU_LIMIT, SWIGLU_LIMIT)
        act = glu * _sigmoid(SWIGLU_ALPHA * glu) * (lin + 1.0)
        o_ref[...] = _dot(act.astype(BF16), w2_s[...]) + b2_ref[...]

    @pl.when(i >= nu_ref[0])
    def _():
        o_ref[...] = jnp.zeros_like(o_ref)


def _moe_experts(block_expert, n_used, xb, w1, b1, w2, b2):
    n_rows = xb.shape[0]
    n_blocks = n_rows // MOE_ROWS
    grid_spec = pltpu.PrefetchScalarGridSpec(
        num_scalar_prefetch=2,
        grid=(n_blocks,),
        in_specs=[pl.BlockSpec((MOE_ROWS, D_MODEL), lambda i, be, nu: (i, 0)),
                  pl.BlockSpec((None, D_MODEL, 2 * D_FF), lambda i, be, nu: (be[i], 0, 0)),
                  pl.BlockSpec((None, 1, 2 * D_FF), lambda i, be, nu: (be[i], 0, 0)),
                  pl.BlockSpec((None, D_FF, D_MODEL), lambda i, be, nu: (be[i], 0, 0)),
                  pl.BlockSpec((None, 1, D_MODEL), lambda i, be, nu: (be[i], 0, 0))],
        out_specs=pl.BlockSpec((MOE_ROWS, D_MODEL), lambda i, be, nu: (i, 0)),
        scratch_shapes=[pltpu.VMEM((D_MODEL, 2 * D_FF), BF16), pltpu.VMEM((D_FF, D_MODEL), BF16)],
    )
    return pl.pallas_call(
        _moe_kernel,
        grid_spec=grid_spec,
        out_shape=jax.ShapeDtypeStruct((n_rows, D_MODEL), F32),
        compiler_params=_cparams("arbitrary"),
        name="moe_experts",
    )(block_expert, n_used, xb, w1, b1, w2, b2)


def _moe_ffn(h2, logits, w1, b1, w2, b2):
    n = h2.shape[0]
    top_val, top_idx = lax.top_k(logits, TOP_K)
    gate = jax.nn.softmax(top_val, axis=-1)
    n_assign = n * TOP_K
    n_blocks = -(-n_assign // MOE_ROWS) + N_EXPERTS
    n_rows = n_blocks * MOE_ROWS
    expert = top_idx.reshape(-1)
    onehot = (expert[:, None] == jnp.arange(N_EXPERTS, dtype=jnp.int32)[None, :]).astype(jnp.int32)
    incl = jnp.cumsum(onehot, axis=0)
    rank = jnp.sum((incl - onehot) * onehot, axis=1)
    counts = incl[-1]
    padded = (counts + MOE_ROWS - 1) // MOE_ROWS * MOE_ROWS
    pad_ends = jnp.cumsum(padded)
    pad_starts = pad_ends - padded
    dest = (pad_starts[expert] + rank).astype(jnp.int32)
    token = jnp.arange(n_assign, dtype=jnp.int32) // TOP_K
    row_token = jnp.zeros((n_rows,), jnp.int32).at[dest].set(token)
    block_expert = jnp.minimum(
        jnp.searchsorted(pad_ends, jnp.arange(n_blocks, dtype=jnp.int32) * MOE_ROWS, side='right'),
        N_EXPERTS - 1).astype(jnp.int32)
    n_used = (pad_ends[-1] // MOE_ROWS).astype(jnp.int32).reshape(1)
    xb = h2[row_token]
    yb = _moe_experts(block_expert, n_used, xb, w1, b1.reshape(N_EXPERTS, 1, -1), w2, b2.reshape(N_EXPERTS, 1, -1))
    picked = yb[dest].reshape(n, TOP_K, D_MODEL)
    return jnp.sum(picked * gate[:, :, None], axis=1)


def _final_norm_kernel(x_ref, g_ref, o_ref):
    x = x_ref[...]
    ms = jnp.mean(x * x, axis=-1, keepdims=True)
    o_ref[...] = x * lax.rsqrt(ms + EPS) * g_ref[...]


def _final_norm(x, g):
    nt = x.shape[0]
    return pl.pallas_call(
        _final_norm_kernel,
        grid=(nt // TOKEN_TILE,),
        in_specs=[pl.BlockSpec((TOKEN_TILE, D_MODEL), lambda i: (i, 0)),
                  pl.BlockSpec((1, D_MODEL), lambda i: (0, 0))],
        out_specs=pl.BlockSpec((TOKEN_TILE, D_MODEL), lambda i: (i, 0)),
        out_shape=jax.ShapeDtypeStruct((nt, D_MODEL), F32),
        compiler_params=_cparams("parallel"),
        name="final_norm",
    )(x, g)


def _head_rmsnorm(x, g):
    y = x * lax.rsqrt(jnp.mean(x * x, axis=-1, keepdims=True) + EPS)
    return y * g


def _rope_tables(t, dim):
    pos = jnp.arange(t)
    rows = (pos // GRID_W).astype(F32)
    cols = (pos % GRID_W).astype(F32)
    axis_dim = dim // 2
    inv = ROPE_THETA ** (-jnp.arange(0, axis_dim, 2, dtype=F32) / axis_dim)
    ang_r = rows[:, None] * inv[None, :]
    ang_c = cols[:, None] * inv[None, :]
    ang = jnp.concatenate([ang_r, ang_r, ang_c, ang_c], axis=-1)
    return jnp.cos(ang), jnp.sin(ang)


def _rope(x, cos, sin):
    x1, x2, x3, x4 = jnp.split(x, 4, axis=-1)
    rot = jnp.concatenate([-x2, x1, -x4, x3], axis=-1)
    return x * cos[:, None, :] + rot * sin[:, None, :]


def _state_to_blocks(init):
    b = init.shape[0]
    st = init.reshape(b, 2, B_HEADS * HEAD_W, B_STATE)
    row = jnp.arange(B_HEADS * HEAD_W)[:, None]
    left = jnp.where(row < 2 * HEAD_W, st, 0.0)
    right = jnp.where(row >= 2 * HEAD_W, st, 0.0)
    return jnp.concatenate([left, right], axis=-1)


def _blocks_to_state(s):
    b = s.shape[0]
    row = jnp.arange(B_HEADS * HEAD_W)[:, None]
    st = jnp.where(row < 2 * HEAD_W, s[..., :B_STATE], s[..., B_STATE:])
    return st.reshape(b, 2, B_HEADS, HEAD_W, B_STATE)


def _ssd_group(ub, lw, init_blocks, b, t):
    z = ub[:, 0:256].reshape(b, t, 256)
    xbc = ub[:, 256:768].reshape(b, t, B_XBC)
    xbc_pad = jnp.pad(xbc, ((0, 0), (CONV_PAD, CONV_PAD), (0, 0)))
    dt_col = ub[:, 768:896].reshape(b, t, LANES)
    dt_row = jnp.swapaxes(dt_col[:, :, :8], 1, 2)
    return _ssd_branch(z, xbc_pad, dt_col, dt_row, lw['conv_w'], lw['conv_b'], lw['pc'], lw['prb'], lw['pra'],
                       lw['dsk'], lw['ng'], init_blocks)


def kernel(x_prompt, x_sample, c, cache_a_k, cache_a_v, cache_c_k, cache_c_v, cache_d_k, cache_d_v, state_ssd, c_ctx, norm1_g, norm2_g, w_mod, b_mod, w_in, a_q_g, a_k_g, b_conv_w, b_conv_b, b_dt_bias, b_a_log, b_d, b_norm_g, c_lam, c_subln_g, d_rpb, w_branch, w_out, router_w, router_b, moe_w1, moe_b1, moe_w2, moe_b2, final_g):
    bp, tp, _ = x_prompt.shape
    bs, ts, _ = x_sample.shape
    depth = w_in.shape[0]
    n_p = bp * tp
    n_s = bs * ts
    assert tp == TOKEN_TILE and ts % TOKEN_TILE == 0
    n_prompt_tiles = n_p // TOKEN_TILE
    tiles_per_sample = ts // TOKEN_TILE
    past = cache_a_k.shape[2]

    x = jnp.concatenate([x_prompt.reshape(n_p, D_MODEL), x_sample.reshape(n_s, D_MODEL)], axis=0)
    cvec = jnp.zeros((16, D_MODEL), F32).at[0].set(c_ctx).at[1:1 + bs].set(c)
    cos_a, sin_a = _rope_tables(ts, HEAD_W)
    cos_c, sin_c = _rope_tables(ts, C_QK_DIM)
    scale_a = HEAD_W ** -0.5
    scale_c = C_QK_DIM ** -0.5
    one = jnp.ones((1,), F32)

    new_cache = [[] for _ in range(7)]
    for l in range(depth):
        lam_init = 0.8 - 0.6 * math.exp(-0.3 * l)
        lq = c_lam[l]
        lam = (jnp.exp(jnp.sum(lq[0] * lq[1])) - jnp.exp(jnp.sum(lq[2] * lq[3])) + lam_init).reshape(1)

        wl = w_in[l]
        w1 = jnp.concatenate([wl[:, 0:512], wl[:, 512:1288], jnp.zeros((D_MODEL, 120), F32),
                              wl[:, 1288:2824]], axis=1).astype(BF16)
        wg = wl[:, 2824:].astype(BF16)
        wb = w_branch[l].astype(BF16)
        wo = w_out[l].astype(BF16)
        rw = jnp.pad(router_w[l], ((0, 0), (0, LANES - N_EXPERTS)))
        rwh = rw.astype(BF16)
        rwl = (rw - rwh.astype(F32)).astype(BF16)
        rb = jnp.pad(router_b[l], (0, LANES - N_EXPERTS)).reshape(1, LANES)

        mod = _modulation(cvec, w_mod[l], b_mod[l].reshape(1, -1))
        mod_tab = jnp.pad(mod.reshape(16, 6, D_MODEL), ((0, 0), (0, 2), (0, 0)))

        g1 = norm1_g[l].reshape(1, D_MODEL)
        g2 = norm2_g[l].reshape(1, D_MODEL)
        u = _in_proj(x, mod_tab, g1, w1, n_prompt_tiles, tiles_per_sample)
        ua, ub, uc, ud = u[:, 0:512], u[:, 512:1408], u[:, 1408:2176], u[:, 2176:2944]

        aq = _head_rmsnorm(ua[:, 0:256].reshape(-1, N_HEADS, HEAD_W), a_q_g[l])
        ak = _head_rmsnorm(ua[:, 256:384].reshape(-1, A_KV_HEADS, HEAD_W), a_k_g[l])
        av = ua[:, 384:512].reshape(-1, A_KV_HEADS, HEAD_W)
        rep = N_HEADS // A_KV_HEADS
        ak_p = ak[:n_p].reshape(bp, tp, A_KV_HEADS, HEAD_W)
        av_p = av[:n_p].reshape(bp, tp, A_KV_HEADS, HEAD_W)
        o_a_p = _mha(aq[:n_p].reshape(bp, tp, BRANCH_W).astype(BF16),
                     jnp.repeat(ak_p, rep, axis=2).reshape(bp, tp, BRANCH_W).astype(BF16),
                     jnp.repeat(av_p, rep, axis=2).reshape(bp, tp, BRANCH_W).astype(BF16),
                     one, n_maps=1, scale=scale_a)
        aq_s = _rope(aq[n_p:].reshape(bs, ts, N_HEADS, HEAD_W), cos_a, sin_a)
        ak_s = _rope(ak[n_p:].reshape(bs, ts, A_KV_HEADS, HEAD_W), cos_a, sin_a)
        k_all = jnp.concatenate([ak_s, cache_a_k[:, l]], axis=1)
        v_all = jnp.concatenate([av[n_p:].reshape(bs, ts, A_KV_HEADS, HEAD_W), cache_a_v[:, l]], axis=1)
        o_a_s = _mha(aq_s.reshape(bs, ts, BRANCH_W).astype(BF16),
                     jnp.repeat(k_all, rep, axis=2).reshape(bs, ts + past, BRANCH_W).astype(BF16),
                     jnp.repeat(v_all, rep, axis=2).reshape(bs, ts + past, BRANCH_W).astype(BF16),
                     one, n_maps=1, scale=scale_a)

        cq, ck, cv = uc[:, 0:256], uc[:, 256:512], uc[:, 512:768]
        o_c_p = _mha(cq[:n_p].reshape(bp, tp, BRANCH_W).astype(BF16),
                     ck[:n_p].reshape(bp, tp, BRANCH_W).astype(BF16),
                     cv[:n_p].reshape(bp, tp, BRANCH_W).astype(BF16), lam, n_maps=2, scale=scale_c)
        cq_r = _rope(cq[n_p:].reshape(bs, ts, 2 * N_HEADS, C_QK_DIM), cos_c, sin_c).reshape(bs, ts, BRANCH_W)
        ck_r = _rope(ck[n_p:].reshape(bs, ts, 2 * N_HEADS, C_QK_DIM), cos_c, sin_c).reshape(bs, ts, BRANCH_W)
        ck_all = jnp.concatenate([ck_r, cache_c_k[:, l].reshape(bs, past, BRANCH_W)], axis=1)
        cv_all = jnp.concatenate([cv[n_p:].reshape(bs, ts, BRANCH_W), cache_c_v[:, l].reshape(bs, past, BRANCH_W)], axis=1)
        o_c_s = _mha(cq_r.astype(BF16), ck_all.astype(BF16), cv_all.astype(BF16), lam, n_maps=2, scale=scale_c)
        o_c = jnp.concatenate([o_c_p.reshape(n_p, BRANCH_W), o_c_s.reshape(n_s, BRANCH_W)], axis=0)
        o_c = (_head_rmsnorm(o_c.reshape(-1, N_HEADS, HEAD_W), c_subln_g[l]) * (1.0 - lam_init)).reshape(-1, BRANCH_W)

        dq, dk, dv = ud[:, 0:256], ud[:, 256:512], ud[:, 512:768]
        o_d_p = _mha(dq[:n_p].reshape(bp, tp, BRANCH_W).astype(BF16),
                     dk[:n_p].reshape(bp, tp, BRANCH_W).astype(BF16),
                     dv[:n_p].reshape(bp, tp, BRANCH_W).astype(BF16), one, n_maps=1, scale=scale_a)
        bias_tab = _na_bias_table(d_rpb[l], ts // GRID_W)
        o_d_s = _na_attention(dq[n_p:].reshape(bs, ts, BRANCH_W).astype(BF16),
                              dk[n_p:].reshape(bs, ts, BRANCH_W).astype(BF16),
                              dv[n_p:].reshape(bs, ts, BRANCH_W).astype(BF16),
                              cache_d_k[:, l].reshape(bs, past, BRANCH_W).astype(BF16),
                              cache_d_v[:, l].reshape(bs, past, BRANCH_W).astype(BF16),
                              bias_tab, scale=scale_a)

        dtb = b_dt_bias[l].reshape(8)
        a_neg = -jnp.exp(b_a_log[l].reshape(8))
        ssd_w = {
            'conv_w': jnp.pad(b_conv_w[l], ((0, 8 - B_CONV_W), (0, 0))),
            'conv_b': b_conv_b[l].reshape(1, B_XBC),
            'pc': jnp.zeros((8, LANES), F32).at[0, :8].set(dtb).at[1, :8].set(a_neg),
            'prb': jnp.broadcast_to(dtb[:, None], (8, LANES)),
            'pra': jnp.broadcast_to(a_neg[:, None], (8, LANES)),
            'dsk': jnp.repeat(b_d[l], HEAD_W).reshape(1, BRANCH_W),
            'ng': b_norm_g[l].reshape(1, BRANCH_W),
        }
        o_b_p, st_p = _ssd_group(ub[:n_p], ssd_w, jnp.zeros((bp, 2, BRANCH_W, LANES), F32), bp, tp)
        o_b_s, _ = _ssd_group(ub[n_p:], ssd_w, _state_to_blocks(state_ssd[:, l]), bs, ts)

        branches = jnp.stack([
            jnp.concatenate([o_a_p.reshape(n_p, BRANCH_W), o_a_s.reshape(n_s, BRANCH_W)], axis=0),
            jnp.concatenate([o_b_p.reshape(n_p, BRANCH_W), o_b_s.reshape(n_s, BRANCH_W)], axis=0),
            o_c,
            jnp.concatenate([o_d_p.reshape(n_p, BRANCH_W), o_d_s.reshape(n_s, BRANCH_W)], axis=0),
        ], axis=0).astype(BF16)

        x1, h2, logits = _merge(x, mod_tab, g1, g2, branches, wg, wb, wo, rwh, rwl, rb,
                                n_prompt_tiles, tiles_per_sample)
        y = _moe_ffn(h2, logits[:, :N_EXPERTS], moe_w1[l], moe_b1[l], moe_w2[l], moe_b2[l])
        g2_rows = jnp.concatenate([jnp.broadcast_to(mod[0:1, 5 * D_MODEL:], (n_p, D_MODEL)),
                                   jnp.repeat(mod[1:1 + bs, 5 * D_MODEL:], ts, axis=0)], axis=0)
        x = x1 + g2_rows * y

        new_cache[0].append(ak_p)
        new_cache[1].append(av_p)
        new_cache[2].append(ck[:n_p].reshape(bp, tp, N_HEADS, 2 * C_QK_DIM))
        new_cache[3].append(cv[:n_p].reshape(bp, tp, N_HEADS, HEAD_W))
        new_cache[4].append(dk[:n_p].reshape(bp, tp, N_HEADS, HEAD_W))
        new_cache[5].append(dv[:n_p].reshape(bp, tp, N_HEADS, HEAD_W))
        new_cache[6].append(_blocks_to_state(st_p))

    yn = _final_norm(x, final_g.reshape(1, D_MODEL))
    y_prompt = yn[:n_p].reshape(bp, tp, D_MODEL)
    y_sample = yn[n_p:].reshape(bs, ts, D_MODEL)
    return (y_prompt, y_sample) + tuple(jnp.stack(v, axis=1) for v in new_cache)
```

```python
import functools
import math

import jax
import jax.numpy as jnp
from jax import lax
from jax.experimental import pallas as pl
from jax.experimental.pallas import tpu as pltpu

F32 = jnp.float32
BF16 = jnp.bfloat16

D_MODEL = 1024
N_BRANCH = 4
BRANCH_W = 256
HEAD_W = 64
N_HEADS = 4
GRID_W = 64
ROPE_THETA = 10000.0
EPS = 1e-6
A_KV_HEADS = 2
C_QK_DIM = 32
B_HEADS = 4
B_STATE = 64
B_CHUNK = 128
B_XBC = 512
B_CONV_W = 5
CONV_PAD = 8
NA_ROWS = 8
NA_COLS = 16
N_EXPERTS = 32
TOP_K = 4
D_FF = 1024
SWIGLU_LIMIT = 7.0
SWIGLU_ALPHA = 1.702
LANES = 128
NEG = -1e30

TOKEN_TILE = 256
MOE_ROWS = 256
W1_COLS = 2944
VMEM_LIMIT = 56 * 1024 * 1024


def _cparams(*sem):
    return pltpu.CompilerParams(dimension_semantics=sem, vmem_limit_bytes=VMEM_LIMIT)


def _dot(a, b):
    return jnp.dot(a, b, preferred_element_type=F32)


def _dot_nt(a, b):
    return lax.dot_general(a, b, (((1,), (1,)), ((), ())), preferred_element_type=F32)


def _dot_tn(a, b):
    return lax.dot_general(a, b, (((0,), (0,)), ((), ())), preferred_element_type=F32)


def _sigmoid(x):
    return 1.0 / (1.0 + jnp.exp(-x))


def _split3(x):
    h1 = x.astype(BF16)
    r1 = x - h1.astype(F32)
    h2 = r1.astype(BF16)
    h3 = (r1 - h2.astype(F32)).astype(BF16)
    return h1, h2, h3


def _norm_mod(x, g, shift, scale):
    ms = jnp.mean(x * x, axis=-1, keepdims=True)
    return (x * lax.rsqrt(ms + EPS)) * g * (1.0 + scale) + shift


def _mod_kernel(c_ref, w_ref, b_ref, o_ref):
    c = c_ref[...]
    s = (c * _sigmoid(c)).astype(BF16)
    o_ref[...] = _dot(s, w_ref[...].astype(BF16)) + b_ref[...]


def _modulation(cvec, w_mod, b_mod):
    n = w_mod.shape[1]
    tn = 1536
    return pl.pallas_call(
        _mod_kernel,
        grid=(n // tn,),
        in_specs=[pl.BlockSpec((16, D_MODEL), lambda j: (0, 0)),
                  pl.BlockSpec((D_MODEL, tn), lambda j: (0, j)),
                  pl.BlockSpec((1, tn), lambda j: (0, j))],
        out_specs=pl.BlockSpec((16, tn), lambda j: (0, j)),
        out_shape=jax.ShapeDtypeStruct((16, n), F32),
        compiler_params=_cparams("parallel"),
        name="modulation",
    )(cvec, w_mod, b_mod)


def _mod_row(i, n_prompt_tiles, tiles_per_sample):
    return jnp.where(i < n_prompt_tiles, 0, 1 + (i - n_prompt_tiles) // tiles_per_sample)


def _inproj_kernel(x_ref, mod_ref, g_ref, w_ref, o_ref):
    mod = mod_ref[...]
    h = _norm_mod(x_ref[...], g_ref[...], mod[0:1], mod[1:2])
    o_ref[...] = _dot(h.astype(BF16), w_ref[...])


def _in_proj(x, mod_tab, g, w1, n_prompt_tiles, tiles_per_sample):
    nt = x.shape[0]
    mrow = functools.partial(_mod_row, n_prompt_tiles=n_prompt_tiles, tiles_per_sample=tiles_per_sample)
    return pl.pallas_call(
        _inproj_kernel,
        grid=(nt // TOKEN_TILE,),
        in_specs=[pl.BlockSpec((TOKEN_TILE, D_MODEL), lambda i: (i, 0)),
                  pl.BlockSpec((None, 8, D_MODEL), lambda i: (mrow(i), 0, 0)),
                  pl.BlockSpec((1, D_MODEL), lambda i: (0, 0)),
                  pl.BlockSpec((D_MODEL, W1_COLS), lambda i: (0, 0))],
        out_specs=pl.BlockSpec((TOKEN_TILE, W1_COLS), lambda i: (i, 0)),
        out_shape=jax.ShapeDtypeStruct((nt, W1_COLS), F32),
        compiler_params=_cparams("parallel"),
        name="in_proj",
    )(x, mod_tab, g, w1)


def _mha_kernel(lam_ref, q_ref, k_ref, v_ref, o_ref, *, n_maps, scale):
    q = q_ref[...]
    k = k_ref[...]
    v = v_ref[...]
    tq = q.shape[0]
    lane = lax.broadcasted_iota(jnp.int32, (1, BRANCH_W), 1)
    sub_w = HEAD_W // n_maps
    acc = jnp.zeros((tq, BRANCH_W), F32)
    for h in range(N_HEADS):
        oh = None
        for j in range(n_maps):
            qm = jnp.where((lane // sub_w) == (h * n_maps + j), q, jnp.zeros_like(q))
            s = _dot_nt(qm, k) * scale
            m = jnp.max(s, axis=-1, keepdims=True)
            p = jnp.exp(s - m)
            l = jnp.sum(p, axis=-1, keepdims=True)
            pv = _dot(p.astype(BF16), v) * (1.0 / l)
            oh = pv if j == 0 else oh - lam_ref[0] * pv
        acc = jnp.where((lane // HEAD_W) == h, oh, acc)
    o_ref[...] = acc


def _mha(q, k, v, lam, *, n_maps, scale, tq=256):
    b, t, _ = q.shape
    kk = k.shape[1]
    return pl.pallas_call(
        functools.partial(_mha_kernel, n_maps=n_maps, scale=scale),
        grid=(b, t // tq),
        in_specs=[pl.BlockSpec(memory_space=pltpu.SMEM),
                  pl.BlockSpec((None, tq, BRANCH_W), lambda i, j: (i, j, 0)),
                  pl.BlockSpec((None, kk, BRANCH_W), lambda i, j: (i, 0, 0)),
                  pl.BlockSpec((None, kk, BRANCH_W), lambda i, j: (i, 0, 0))],
        out_specs=pl.BlockSpec((None, tq, BRANCH_W), lambda i, j: (i, j, 0)),
        out_shape=jax.ShapeDtypeStruct((b, t, BRANCH_W), F32),
        compiler_params=_cparams("parallel", "parallel"),
        name=f"mha_maps{n_maps}_k{kk}",
    )(lam, q, k, v)


def _na_row0(r, n_rows):
    return jnp.clip(r - NA_ROWS // 2, 0, n_rows - NA_ROWS)


def _na_kernel(q_ref, k_ref, v_ref, ck_ref, cv_ref, bias_ref, o_ref, *, n_rows, scale):
    r = pl.program_id(1)
    start = pl.multiple_of(_na_row0(r, n_rows) * GRID_W, GRID_W)
    win = NA_ROWS * GRID_W
    q = q_ref[...]
    kw = k_ref[pl.ds(start, win), :]
    vw = v_ref[pl.ds(start, win), :]
    lane = lax.broadcasted_iota(jnp.int32, (1, BRANCH_W), 1)
    qs = jnp.concatenate(
        [jnp.where((lane // HEAD_W) == h, q, jnp.zeros_like(q)) for h in range(N_HEADS)], axis=0)
    s_loc = _dot_nt(qs, kw) * scale + bias_ref[...]
    s_ctx = _dot_nt(qs, ck_ref[...]) * scale
    m = jnp.maximum(jnp.max(s_loc, axis=-1, keepdims=True), jnp.max(s_ctx, axis=-1, keepdims=True))
    p_loc = jnp.exp(s_loc - m)
    p_ctx = jnp.exp(s_ctx - m)
    l = jnp.sum(p_loc, axis=-1, keepdims=True) + jnp.sum(p_ctx, axis=-1, keepdims=True)
    o = (_dot(p_loc.astype(BF16), vw) + _dot(p_ctx.astype(BF16), cv_ref[...])) * (1.0 / l)
    acc = jnp.zeros((GRID_W, BRANCH_W), F32)
    for h in range(N_HEADS):
        acc = jnp.where((lane // HEAD_W) == h, o[h * GRID_W:(h + 1) * GRID_W], acc)
    o_ref[...] = acc


def _na_attention(q, k, v, ck, cv, bias_tab, *, scale):
    b, t, _ = q.shape
    n_rows = t // GRID_W
    kk = ck.shape[1]
    win = NA_ROWS * GRID_W
    return pl.pallas_call(
        functools.partial(_na_kernel, n_rows=n_rows, scale=scale),
        grid=(b, n_rows),
        in_specs=[pl.BlockSpec((None, GRID_W, BRANCH_W), lambda i, r: (i, r, 0)),
                  pl.BlockSpec((None, t, BRANCH_W), lambda i, r: (i, 0, 0)),
                  pl.BlockSpec((None, t, BRANCH_W), lambda i, r: (i, 0, 0)),
                  pl.BlockSpec((None, kk, BRANCH_W), lambda i, r: (i, 0, 0)),
                  pl.BlockSpec((None, kk, BRANCH_W), lambda i, r: (i, 0, 0)),
                  pl.BlockSpec((None, N_HEADS * GRID_W, win), lambda i, r: (r - _na_row0(r, n_rows), 0, 0))],
        out_specs=pl.BlockSpec((None, GRID_W, BRANCH_W), lambda i, r: (i, r, 0)),
        out_shape=jax.ShapeDtypeStruct((b, t, BRANCH_W), F32),
        compiler_params=_cparams("parallel", "parallel"),
        name="na_attention",
    )(q, k, v, ck, cv, bias_tab)


def _na_bias_table(rpb, n_rows):
    var = jnp.arange(NA_ROWS)[:, None, None, None]
    col = jnp.arange(GRID_W)[None, :, None, None]
    j = jnp.arange(NA_ROWS)[None, None, :, None]
    kc = jnp.arange(GRID_W)[None, None, None, :]
    c0 = jnp.clip(col - NA_COLS // 2, 0, GRID_W - NA_COLS)
    valid = (kc >= c0) & (kc < c0 + NA_COLS)
    off_r = j - var + (NA_ROWS - 1)
    off_c = jnp.clip(kc - col + (NA_COLS - 1), 0, 2 * NA_COLS - 2)
    shape = (NA_ROWS, GRID_W, NA_ROWS, GRID_W)
    off_r = jnp.broadcast_to(off_r, shape)
    off_c = jnp.broadcast_to(off_c, shape)
    tab = rpb[:, off_r, off_c]
    tab = jnp.where(jnp.broadcast_to(valid, shape)[None], tab, NEG)
    tab = jnp.transpose(tab, (1, 0, 2, 3, 4))
    return tab.reshape(NA_ROWS, N_HEADS * GRID_W, NA_ROWS * GRID_W).astype(F32)


def _softplus(x):
    return jnp.maximum(x, 0.0) + jnp.log1p(jnp.exp(-jnp.abs(x)))


def _expand_heads(colmat, d, lane256):
    out = jnp.zeros((colmat.shape[0], BRANCH_W), F32)
    for h in range(B_HEADS):
        j = d * B_HEADS + h
        out = jnp.where((lane256 // HEAD_W) == h,
                        jnp.broadcast_to(colmat[:, j:j + 1], (colmat.shape[0], BRANCH_W)), out)
    return out


def _ssd_kernel(z_ref, xbc_ref, dtc_ref, dtr_ref, cw_ref, cb_ref, pc_ref, prb_ref, pra_ref, dsk_ref, ng_ref,
                init_ref, o_ref, st_ref, xc_s, yf_s, s_s, *, seq):
    n_chunks = seq // B_CHUNK
    L = B_CHUNK
    ri = lax.broadcasted_iota(jnp.int32, (L, L), 0)
    ci = lax.broadcasted_iota(jnp.int32, (L, L), 1)
    low = ci <= ri
    upp = ri <= ci
    low_b = jnp.where(low, 1.0, 0.0).astype(BF16)
    upp_b = jnp.where(upp, 1.0, 0.0).astype(BF16)
    lane128 = lax.broadcasted_iota(jnp.int32, (1, LANES), 1)
    lane256 = lax.broadcasted_iota(jnp.int32, (1, BRANCH_W), 1)
    row256 = lax.broadcasted_iota(jnp.int32, (BRANCH_W, 1), 0)
    blockmask = (row256 // (2 * HEAD_W)) == (lax.broadcasted_iota(jnp.int32, (1, LANES), 1) // B_STATE)

    cw = cw_ref[...]
    cb = cb_ref[...]

    def conv_body(c, carry):
        base = pl.multiple_of(c * L, L)
        w = xbc_ref[pl.ds(base, L + 2 * CONV_PAD), :]
        acc = jnp.zeros((L, B_XBC), F32) + cb
        for kk in range(B_CONV_W):
            off = CONV_PAD - B_CONV_W // 2 + kk
            acc = acc + w[off:off + L, :] * cw[kk:kk + 1, :]
        xc_s[pl.ds(base, L), :] = acc * _sigmoid(acc)
        return carry

    lax.fori_loop(0, n_chunks, conv_body, 0)

    dt_bias_c = pc_ref[0:1, :]
    a_c = pc_ref[1:2, :]
    dt_bias_r = prb_ref[...]
    a_r = pra_ref[...]
    dsk = dsk_ref[...]
    ng = ng_ref[...]

    def chunk(c, d):
        base = pl.multiple_of(c * L, L)
        xs = xc_s[pl.ds(base, L), 0:BRANCH_W]
        bm = xc_s[pl.ds(base, L), BRANCH_W:BRANCH_W + LANES].astype(BF16)
        cm = xc_s[pl.ds(base, L), BRANCH_W + LANES:B_XBC].astype(BF16)
        dtc = _softplus(dtc_ref[pl.ds(base, L), :] + dt_bias_c)
        dtr = _softplus(dtr_ref[:, pl.ds(base, L)] + dt_bias_r)
        da_c = dtc * a_c
        da_r = dtr * a_r
        tri_c = low_b if d == 0 else upp_b
        tri_r = upp_b if d == 0 else low_b
        c1, c2, c3 = _split3(da_c)
        cum_c = _dot(tri_c, c1) + _dot(tri_c, c2) + _dot(tri_c, c3)
        r1, r2, r3 = _split3(da_r)
        cum_r = _dot(r1, tri_r) + _dot(r2, tri_r) + _dot(r3, tri_r)
        cum_end = cum_c[L - 1:L, :] if d == 0 else cum_c[0:1, :]
        dmask = low if d == 0 else upp

        g0 = _dot_nt(jnp.where(lane128 < B_STATE, cm, jnp.zeros_like(cm)), bm)
        g1 = _dot_nt(jnp.where(lane128 >= B_STATE, cm, jnp.zeros_like(cm)), bm)
        dt_x = _expand_heads(dtc, d, lane256)
        e_a = _expand_heads(jnp.exp(cum_c), d, lane256)
        t_e = _expand_heads(jnp.exp(cum_end - cum_c), d, lane256)
        xdt = xs * dt_x
        xdt_b = xdt.astype(BF16)
        y = jnp.zeros((L, BRANCH_W), F32)
        for h in range(B_HEADS):
            j = d * B_HEADS + h
            col = jnp.broadcast_to(cum_c[:, j:j + 1], (L, L))
            row = jnp.broadcast_to(cum_r[j:j + 1, :], (L, L))
            dec = jnp.exp(jnp.where(dmask, col - row, NEG))
            sc = ((g0 if h < 2 else g1) * dec).astype(BF16)
            y = jnp.where((lane256 // HEAD_W) == h, _dot(sc, xdt_b), y)
        state = s_s[...]
        y = y + _dot_nt(cm, state.astype(BF16)) * e_a
        zmat = _dot_tn((xdt * t_e).astype(BF16), bm)
        e_end = jnp.exp(cum_end)
        cd = jnp.zeros((BRANCH_W, LANES), F32)
        for h in range(B_HEADS):
            j = d * B_HEADS + h
            cd = jnp.where((row256 // HEAD_W) == h, jnp.broadcast_to(e_end[:, j:j + 1], (BRANCH_W, LANES)), cd)
        s_s[...] = state * cd + jnp.where(blockmask, zmat, 0.0)
        return base, xs, y

    s_s[...] = init_ref[0]

    def fwd_body(c, carry):
        base, _, y = chunk(c, 0)
        yf_s[pl.ds(base, L), :] = y
        return carry

    lax.fori_loop(0, n_chunks, fwd_body, 0)
    st_ref[0] = s_s[...]
    s_s[...] = init_ref[1]

    def bwd_body(i, carry):
        c = n_chunks - 1 - i
        base, xs, y = chunk(c, 1)
        z = z_ref[pl.ds(base, L), :]
        yt = (yf_s[pl.ds(base, L), :] + y + dsk * xs) * (z * _sigmoid(z))
        ms = jnp.mean(yt * yt, axis=-1, keepdims=True)
        o_ref[pl.ds(base, L), :] = yt * lax.rsqrt(ms + EPS) * ng
        return carry

    lax.fori_loop(0, n_chunks, bwd_body, 0)
    st_ref[1] = s_s[...]


def _ssd_branch(z, xbc_pad, dt_col, dt_row, conv_w, conv_b, pc, prb, pra, dsk, ng, init):
    b, t, _ = z.shape
    tp = t + 2 * CONV_PAD
    full = lambda *shape: pl.BlockSpec(shape, lambda i: (0,) * len(shape))
    return pl.pallas_call(
        functools.partial(_ssd_kernel, seq=t),
        grid=(b,),
        in_specs=[pl.BlockSpec((None, t, BRANCH_W), lambda i: (i, 0, 0)),
                  pl.BlockSpec((None, tp, B_XBC), lambda i: (i, 0, 0)),
                  pl.BlockSpec((None, t, LANES), lambda i: (i, 0, 0)),
                  pl.BlockSpec((None, 8, t), lambda i: (i, 0, 0)),
                  full(8, B_XBC), full(1, B_XBC), full(8, LANES), full(8, LANES), full(8, LANES),
                  full(1, BRANCH_W), full(1, BRANCH_W),
                  pl.BlockSpec((None, 2, BRANCH_W, LANES), lambda i: (i, 0, 0, 0))],
        out_specs=[pl.BlockSpec((None, t, BRANCH_W), lambda i: (i, 0, 0)),
                   pl.BlockSpec((None, 2, BRANCH_W, LANES), lambda i: (i, 0, 0, 0))],
        out_shape=[jax.ShapeDtypeStruct((b, t, BRANCH_W), F32),
                   jax.ShapeDtypeStruct((b, 2, BRANCH_W, LANES), F32)],
        scratch_shapes=[pltpu.VMEM((t, B_XBC), F32), pltpu.VMEM((t, BRANCH_W), F32),
                        pltpu.VMEM((BRANCH_W, LANES), F32)],
        compiler_params=_cparams("parallel"),
        name=f"ssd_t{t}",
    )(z, xbc_pad, dt_col, dt_row, conv_w, conv_b, pc, prb, pra, dsk, ng, init)


def _merge_kernel(x_ref, mod_ref, g1_ref, g2_ref, br_ref, wg_ref, wb_ref, wo_ref, rwh_ref, rwl_ref, rb_ref,
                  x1_ref, h2_ref, gate_ref, eidx_ref):
    x = x_ref[...]
    mod = mod_ref[...]
    h = _norm_mod(x, g1_ref[...], mod[0:1], mod[1:2]).astype(BF16)
    merged = None
    for i in range(N_BRANCH):
        gate = _sigmoid(_dot(h, wg_ref[:, i * D_MODEL:(i + 1) * D_MODEL]))
        proj = _dot(br_ref[i], wb_ref[i])
        merged = gate * proj if i == 0 else merged + gate * proj
    y = _dot(merged.astype(BF16), wo_ref[...])
    x1 = x + mod[2:3] * y
    x1_ref[...] = x1
    h2 = _norm_mod(x1, g2_ref[...], mod[3:4], mod[4:5])
    hh = h2.astype(BF16)
    hl = (h2 - hh.astype(F32)).astype(BF16)
    h2_ref[...] = h2
    rwh = rwh_ref[...]
    logits = _dot(hh, rwh) + _dot(hl, rwh) + _dot(hh, rwl_ref[...]) + rb_ref[...]
    lane = lax.broadcasted_iota(jnp.int32, (1, LANES), 1).astype(F32)
    cur = jnp.where(lane < N_EXPERTS, logits, NEG)
    vals, idxs = [], []
    for _ in range(TOP_K):
        m = jnp.max(cur, axis=-1, keepdims=True)
        am = jnp.min(jnp.where(cur == m, lane, float(LANES)), axis=-1, keepdims=True)
        vals.append(m)
        idxs.append(am)
        cur = jnp.where(lane == am, NEG, cur)
    exps = [jnp.exp(v - vals[0]) for v in vals]
    inv = 1.0 / (exps[0] + exps[1] + exps[2] + exps[3])
    gate = jnp.zeros_like(logits)
    eidx = jnp.zeros_like(logits)
    for k in range(TOP_K):
        gate = jnp.where(lane == k, exps[k] * inv, gate)
        eidx = jnp.where(lane == k, idxs[k], eidx)
    gate_ref[...] = gate
    eidx_ref[...] = eidx.astype(jnp.int32)


def _merge(x, mod_tab, g1, g2, branches, wg, wb, wo, rwh, rwl, rb, n_prompt_tiles, tiles_per_sample):
    nt = x.shape[0]
    mrow = functools.partial(_mod_row, n_prompt_tiles=n_prompt_tiles, tiles_per_sample=tiles_per_sample)
    const = lambda *shape: pl.BlockSpec(shape, lambda i: (0,) * len(shape))
    return pl.pallas_call(
        _merge_kernel,
        grid=(nt // TOKEN_TILE,),
        in_specs=[pl.BlockSpec((TOKEN_TILE, D_MODEL), lambda i: (i, 0)),
                  pl.BlockSpec((None, 8, D_MODEL), lambda i: (mrow(i), 0, 0)),
                  const(1, D_MODEL), const(1, D_MODEL),
                  pl.BlockSpec((N_BRANCH, TOKEN_TILE, BRANCH_W), lambda i: (0, i, 0)),
                  const(D_MODEL, N_BRANCH * D_MODEL), const(N_BRANCH, BRANCH_W, D_MODEL),
                  const(D_MODEL, D_MODEL), const(D_MODEL, LANES), const(D_MODEL, LANES), const(1, LANES)],
        out_specs=[pl.BlockSpec((TOKEN_TILE, D_MODEL), lambda i: (i, 0)),
                   pl.BlockSpec((TOKEN_TILE, D_MODEL), lambda i: (i, 0)),
                   pl.BlockSpec((TOKEN_TILE, LANES), lambda i: (i, 0)),
                   pl.BlockSpec((TOKEN_TILE, LANES), lambda i: (i, 0))],
        out_shape=[jax.ShapeDtypeStruct((nt, D_MODEL), F32),
                   jax.ShapeDtypeStruct((nt, D_MODEL), F32),
                   jax.ShapeDtypeStruct((nt, LANES), F32),
                   jax.ShapeDtypeStruct((nt, LANES), jnp.int32)],
        compiler_params=_cparams("parallel"),
        name="merge",
    )(x, mod_tab, g1, g2, branches, wg, wb, wo, rwh, rwl, rb)


def _moe_kernel(be_ref, tok_ref, tokn_ref, slot_ref, h_hbm, w1_ref, b1_ref, w2_ref, b2_ref, out_hbm,
                w1_s, w2_s, xbuf, ybuf, gsem, ssem):
    i = pl.program_id(0)
    nb = pl.num_programs(0)
    cur = i % 2
    nxt = 1 - cur

    def gather_copy(tref, buf, r):
        return pltpu.make_async_copy(h_hbm.at[pl.ds(tref[0, r], 1), :], xbuf.at[buf, pl.ds(r, 1), :], gsem.at[buf])

    def gather_all(buf):
        return pltpu.make_async_copy(h_hbm.at[pl.ds(0, MOE_ROWS), :], xbuf.at[buf], gsem.at[buf])

    def scatter_copy(buf, r):
        return pltpu.make_async_copy(ybuf.at[buf, pl.ds(r, 1), :], out_hbm.at[pl.ds(slot_ref[0, r], 1), :],
                                     ssem.at[buf])

    def scatter_all(buf):
        return pltpu.make_async_copy(ybuf.at[buf], out_hbm.at[pl.ds(0, MOE_ROWS), :], ssem.at[buf])

    @pl.when(i == 0)
    def _():
        for r in range(MOE_ROWS):
            gather_copy(tok_ref, 0, r).start()

    prev = be_ref[jnp.maximum(i - 1, 0)]

    @pl.when(jnp.logical_or(i == 0, be_ref[i] != prev))
    def _():
        w1_s[...] = w1_ref[...].astype(BF16)
        w2_s[...] = w2_ref[...].astype(BF16)

    @pl.when(i >= 2)
    def _():
        scatter_all(cur).wait()

    for r in range(MOE_ROWS):
        gather_copy(tokn_ref, nxt, r).start()
    gather_all(cur).wait()

    u = _dot(xbuf[cur].astype(BF16), w1_s[...]) + b1_ref[...]
    glu = jnp.minimum(u[:, :D_FF], SWIGLU_LIMIT)
    lin = jnp.clip(u[:, D_FF:], -SWIGLU_LIMIT, SWIGLU_LIMIT)
    act = glu * _sigmoid(SWIGLU_ALPHA * glu) * (lin + 1.0)
    ybuf[cur] = _dot(act.astype(BF16), w2_s[...]) + b2_ref[...]
    for r in range(MOE_ROWS):
        scatter_copy(cur, r).start()

    @pl.when(i == nb - 1)
    def _():
        gather_all(nxt).wait()
        scatter_all(nxt).wait()
        scatter_all(cur).wait()


def _moe_experts(block_expert, row_token, row_slot, h2, w1, b1, w2, b2):
    n_blocks = row_token.shape[0]
    n_rows = n_blocks * MOE_ROWS
    assert n_blocks >= 2
    smem_rows = lambda imap: pl.BlockSpec((None, 1, MOE_ROWS), imap, memory_space=pltpu.SMEM)
    grid_spec = pltpu.PrefetchScalarGridSpec(
        num_scalar_prefetch=1,
        grid=(n_blocks,),
        in_specs=[smem_rows(lambda i, be: (i, 0, 0)),
                  smem_rows(lambda i, be: (jnp.minimum(i + 1, n_blocks - 1), 0, 0)),
                  smem_rows(lambda i, be: (i, 0, 0)),
                  pl.BlockSpec(memory_space=pl.ANY),
                  pl.BlockSpec((None, D_MODEL, 2 * D_FF), lambda i, be: (be[i], 0, 0)),
                  pl.BlockSpec((None, 1, 2 * D_FF), lambda i, be: (be[i], 0, 0)),
                  pl.BlockSpec((None, D_FF, D_MODEL), lambda i, be: (be[i], 0, 0)),
                  pl.BlockSpec((None, 1, D_MODEL), lambda i, be: (be[i], 0, 0))],
        out_specs=pl.BlockSpec(memory_space=pl.ANY),
        scratch_shapes=[pltpu.VMEM((D_MODEL, 2 * D_FF), BF16), pltpu.VMEM((D_FF, D_MODEL), BF16),
                        pltpu.VMEM((2, MOE_ROWS, D_MODEL), F32), pltpu.VMEM((2, MOE_ROWS, D_MODEL), F32),
                        pltpu.SemaphoreType.DMA((2,)), pltpu.SemaphoreType.DMA((2,))],
    )
    return pl.pallas_call(
        _moe_kernel,
        grid_spec=grid_spec,
        out_shape=jax.ShapeDtypeStruct((n_rows, D_MODEL), F32),
        compiler_params=pltpu.CompilerParams(dimension_semantics=("arbitrary",), vmem_limit_bytes=VMEM_LIMIT,
                                             has_side_effects=True),
        name="moe_experts",
    )(block_expert, row_token, row_token, row_slot, h2, w1, b1, w2, b2)


def _moe_dispatch(eidx):
    n = eidx.shape[0]
    n_assign = n * TOP_K
    n_blocks = -(-n_assign // MOE_ROWS) + N_EXPERTS
    n_rows = n_blocks * MOE_ROWS
    expert = eidx.reshape(-1)
    onehot = (expert[:, None] == jnp.arange(N_EXPERTS, dtype=jnp.int32)[None, :]).astype(jnp.int32)
    incl = jnp.cumsum(onehot, axis=0)
    rank = jnp.sum((incl - onehot) * onehot, axis=1)
    counts = incl[-1]
    padded = (counts + MOE_ROWS - 1) // MOE_ROWS * MOE_ROWS
    pad_ends = jnp.cumsum(padded)
    pad_starts = pad_ends - padded
    dest = (pad_starts[expert] + rank).astype(jnp.int32)
    assign = jnp.full((n_rows,), -1, jnp.int32).at[dest].set(jnp.arange(n_assign, dtype=jnp.int32))
    is_pad = assign < 0
    filler = n_assign + jnp.cumsum(is_pad.astype(jnp.int32)) - 1
    row_slot = jnp.where(is_pad, filler, assign)
    row_token = jnp.where(is_pad, 0, assign // TOP_K)
    block_start = jnp.arange(n_blocks, dtype=jnp.int32) * MOE_ROWS
    block_expert = jnp.minimum(jnp.sum((pad_ends[None, :] <= block_start[:, None]).astype(jnp.int32), axis=1),
                               N_EXPERTS - 1).astype(jnp.int32)
    return (block_expert, row_token.reshape(n_blocks, 1, MOE_ROWS), row_slot.reshape(n_blocks, 1, MOE_ROWS))


def _combine_kernel(x1_ref, mod_ref, gate_ref, y4_ref, fg_ref, o_ref, *, final):
    gate = gate_ref[...]
    y = None
    for k in range(TOP_K):
        yk = gate[:, k:k + 1] * y4_ref[:, k * D_MODEL:(k + 1) * D_MODEL]
        y = yk if k == 0 else y + yk
    x2 = x1_ref[...] + mod_ref[5:6, :] * y
    if final:
        ms = jnp.mean(x2 * x2, axis=-1, keepdims=True)
        x2 = x2 * lax.rsqrt(ms + EPS) * fg_ref[...]
    o_ref[...] = x2


def _combine(x1, mod_tab, gate, y_slots, final_g, n_prompt_tiles, tiles_per_sample, *, final):
    nt = x1.shape[0]
    y4 = y_slots.reshape(y_slots.shape[0] // TOP_K, TOP_K * D_MODEL)
    mrow = functools.partial(_mod_row, n_prompt_tiles=n_prompt_tiles, tiles_per_sample=tiles_per_sample)
    return pl.pallas_call(
        functools.partial(_combine_kernel, final=final),
        grid=(nt // TOKEN_TILE,),
        in_specs=[pl.BlockSpec((TOKEN_TILE, D_MODEL), lambda i: (i, 0)),
                  pl.BlockSpec((None, 8, D_MODEL), lambda i: (mrow(i), 0, 0)),
                  pl.BlockSpec((TOKEN_TILE, LANES), lambda i: (i, 0)),
                  pl.BlockSpec((TOKEN_TILE, TOP_K * D_MODEL), lambda i: (i, 0)),
                  pl.BlockSpec((1, D_MODEL), lambda i: (0, 0))],
        out_specs=pl.BlockSpec((TOKEN_TILE, D_MODEL), lambda i: (i, 0)),
        out_shape=jax.ShapeDtypeStruct((nt, D_MODEL), F32),
        compiler_params=_cparams("parallel"),
        name="combine_final" if final else "combine",
    )(x1, mod_tab, gate, y4, final_g)


def _head_rmsnorm(x, g):
    y = x * lax.rsqrt(jnp.mean(x * x, axis=-1, keepdims=True) + EPS)
    return y * g


def _rope_tables(t, dim):
    pos = jnp.arange(t)
    rows = (pos // GRID_W).astype(F32)
    cols = (pos % GRID_W).astype(F32)
    axis_dim = dim // 2
    inv = ROPE_THETA ** (-jnp.arange(0, axis_dim, 2, dtype=F32) / axis_dim)
    ang_r = rows[:, None] * inv[None, :]
    ang_c = cols[:, None] * inv[None, :]
    ang = jnp.concatenate([ang_r, ang_r, ang_c, ang_c], axis=-1)
    return jnp.cos(ang), jnp.sin(ang)


def _rope(x, cos, sin):
    x1, x2, x3, x4 = jnp.split(x, 4, axis=-1)
    rot = jnp.concatenate([-x2, x1, -x4, x3], axis=-1)
    return x * cos[:, None, :] + rot * sin[:, None, :]


def _state_to_blocks(init):
    b = init.shape[0]
    st = init.reshape(b, 2, B_HEADS * HEAD_W, B_STATE)
    row = jnp.arange(B_HEADS * HEAD_W)[:, None]
    left = jnp.where(row < 2 * HEAD_W, st, 0.0)
    right = jnp.where(row >= 2 * HEAD_W, st, 0.0)
    return jnp.concatenate([left, right], axis=-1)


def _blocks_to_state(s):
    b = s.shape[0]
    row = jnp.arange(B_HEADS * HEAD_W)[:, None]
    st = jnp.where(row < 2 * HEAD_W, s[..., :B_STATE], s[..., B_STATE:])
    return st.reshape(b, 2, B_HEADS, HEAD_W, B_STATE)


def _ssd_group(ub, lw, init_blocks, b, t):
    z = ub[:, 0:256].reshape(b, t, 256)
    xbc = ub[:, 256:768].reshape(b, t, B_XBC)
    xbc_pad = jnp.pad(xbc, ((0, 0), (CONV_PAD, CONV_PAD), (0, 0)))
    dt_col = ub[:, 768:896].reshape(b, t, LANES)
    dt_row = jnp.swapaxes(dt_col[:, :, :8], 1, 2)
    return _ssd_branch(z, xbc_pad, dt_col, dt_row, lw['conv_w'], lw['conv_b'], lw['pc'], lw['prb'], lw['pra'],
                       lw['dsk'], lw['ng'], init_blocks)


def kernel(x_prompt, x_sample, c, cache_a_k, cache_a_v, cache_c_k, cache_c_v, cache_d_k, cache_d_v, state_ssd, c_ctx, norm1_g, norm2_g, w_mod, b_mod, w_in, a_q_g, a_k_g, b_conv_w, b_conv_b, b_dt_bias, b_a_log, b_d, b_norm_g, c_lam, c_subln_g, d_rpb, w_branch, w_out, router_w, router_b, moe_w1, moe_b1, moe_w2, moe_b2, final_g):
    bp, tp, _ = x_prompt.shape
    bs, ts, _ = x_sample.shape
    depth = w_in.shape[0]
    n_p = bp * tp
    n_s = bs * ts
    assert tp == TOKEN_TILE and ts % TOKEN_TILE == 0
    n_prompt_tiles = n_p // TOKEN_TILE
    tiles_per_sample = ts // TOKEN_TILE
    past = cache_a_k.shape[2]

    x = jnp.concatenate([x_prompt.reshape(n_p, D_MODEL), x_sample.reshape(n_s, D_MODEL)], axis=0)
    cvec = jnp.zeros((16, D_MODEL), F32).at[0].set(c_ctx).at[1:1 + bs].set(c)
    cos_a, sin_a = _rope_tables(ts, HEAD_W)
    cos_c, sin_c = _rope_tables(ts, C_QK_DIM)
    scale_a = HEAD_W ** -0.5
    scale_c = C_QK_DIM ** -0.5
    one = jnp.ones((1,), F32)

    new_cache = [[] for _ in range(7)]
    for l in range(depth):
        lam_init = 0.8 - 0.6 * math.exp(-0.3 * l)
        lq = c_lam[l]
        lam = (jnp.exp(jnp.sum(lq[0] * lq[1])) - jnp.exp(jnp.sum(lq[2] * lq[3])) + lam_init).reshape(1)

        wl = w_in[l]
        w1 = jnp.concatenate([wl[:, 0:512], wl[:, 512:1288], jnp.zeros((D_MODEL, 120), F32),
                              wl[:, 1288:2824]], axis=1).astype(BF16)
        wg = wl[:, 2824:].astype(BF16)
        wb = w_branch[l].astype(BF16)
        wo = w_out[l].astype(BF16)
        rw = jnp.pad(router_w[l], ((0, 0), (0, LANES - N_EXPERTS)))
        rwh = rw.astype(BF16)
        rwl = (rw - rwh.astype(F32)).astype(BF16)
        rb = jnp.pad(router_b[l], (0, LANES - N_EXPERTS)).reshape(1, LANES)

        mod = _modulation(cvec, w_mod[l], b_mod[l].reshape(1, -1))
        mod_tab = jnp.pad(mod.reshape(16, 6, D_MODEL), ((0, 0), (0, 2), (0, 0)))

        g1 = norm1_g[l].reshape(1, D_MODEL)
        g2 = norm2_g[l].reshape(1, D_MODEL)
        u = _in_proj(x, mod_tab, g1, w1, n_prompt_tiles, tiles_per_sample)
        ua, ub, uc, ud = u[:, 0:512], u[:, 512:1408], u[:, 1408:2176], u[:, 2176:2944]

        aq = _head_rmsnorm(ua[:, 0:256].reshape(-1, N_HEADS, HEAD_W), a_q_g[l])
        ak = _head_rmsnorm(ua[:, 256:384].reshape(-1, A_KV_HEADS, HEAD_W), a_k_g[l])
        av = ua[:, 384:512].reshape(-1, A_KV_HEADS, HEAD_W)
        rep = N_HEADS // A_KV_HEADS
        ak_p = ak[:n_p].reshape(bp, tp, A_KV_HEADS, HEAD_W)
        av_p = av[:n_p].reshape(bp, tp, A_KV_HEADS, HEAD_W)
        o_a_p = _mha(aq[:n_p].reshape(bp, tp, BRANCH_W).astype(BF16),
                     jnp.repeat(ak_p, rep, axis=2).reshape(bp, tp, BRANCH_W).astype(BF16),
                     jnp.repeat(av_p, rep, axis=2).reshape(bp, tp, BRANCH_W).astype(BF16),
                     one, n_maps=1, scale=scale_a)
        aq_s = _rope(aq[n_p:].reshape(bs, ts, N_HEADS, HEAD_W), cos_a, sin_a)
        ak_s = _rope(ak[n_p:].reshape(bs, ts, A_KV_HEADS, HEAD_W), cos_a, sin_a)
        k_all = jnp.concatenate([ak_s, cache_a_k[:, l]], axis=1)
        v_all = jnp.concatenate([av[n_p:].reshape(bs, ts, A_KV_HEADS, HEAD_W), cache_a_v[:, l]], axis=1)
        o_a_s = _mha(aq_s.reshape(bs, ts, BRANCH_W).astype(BF16),
                     jnp.repeat(k_all, rep, axis=2).reshape(bs, ts + past, BRANCH_W).astype(BF16),
                     jnp.repeat(v_all, rep, axis=2).reshape(bs, ts + past, BRANCH_W).astype(BF16),
                     one, n_maps=1, scale=scale_a)

        cq, ck, cv = uc[:, 0:256], uc[:, 256:512], uc[:, 512:768]
        o_c_p = _mha(cq[:n_p].reshape(bp, tp, BRANCH_W).astype(BF16),
                     ck[:n_p].reshape(bp, tp, BRANCH_W).astype(BF16),
                     cv[:n_p].reshape(bp, tp, BRANCH_W).astype(BF16), lam, n_maps=2, scale=scale_c)
        cq_r = _rope(cq[n_p:].reshape(bs, ts, 2 * N_HEADS, C_QK_DIM), cos_c, sin_c).reshape(bs, ts, BRANCH_W)
        ck_r = _rope(ck[n_p:].reshape(bs, ts, 2 * N_HEADS, C_QK_DIM), cos_c, sin_c).reshape(bs, ts, BRANCH_W)
        ck_all = jnp.concatenate([ck_r, cache_c_k[:, l].reshape(bs, past, BRANCH_W)], axis=1)
        cv_all = jnp.concatenate([cv[n_p:].reshape(bs, ts, BRANCH_W), cache_c_v[:, l].reshape(bs, past, BRANCH_W)], axis=1)
        o_c_s = _mha(cq_r.astype(BF16), ck_all.astype(BF16), cv_all.astype(BF16), lam, n_maps=2, scale=scale_c)
        o_c = jnp.concatenate([o_c_p.reshape(n_p, BRANCH_W), o_c_s.reshape(n_s, BRANCH_W)], axis=0)
        o_c = (_head_rmsnorm(o_c.reshape(-1, N_HEADS, HEAD_W), c_subln_g[l]) * (1.0 - lam_init)).reshape(-1, BRANCH_W)

        dq, dk, dv = ud[:, 0:256], ud[:, 256:512], ud[:, 512:768]
        o_d_p = _mha(dq[:n_p].reshape(bp, tp, BRANCH_W).astype(BF16),
                     dk[:n_p].reshape(bp, tp, BRANCH_W).astype(BF16),
                     dv[:n_p].reshape(bp, tp, BRANCH_W).astype(BF16), one, n_maps=1, scale=scale_a)
        bias_tab = _na_bias_table(d_rpb[l], ts // GRID_W)
        o_d_s = _na_attention(dq[n_p:].reshape(bs, ts, BRANCH_W).astype(BF16),
                              dk[n_p:].reshape(bs, ts, BRANCH_W).astype(BF16),
                              dv[n_p:].reshape(bs, ts, BRANCH_W).astype(BF16),
                              cache_d_k[:, l].reshape(bs, past, BRANCH_W).astype(BF16),
                              cache_d_v[:, l].reshape(bs, past, BRANCH_W).astype(BF16),
                              bias_tab, scale=scale_a)

        dtb = b_dt_bias[l].reshape(8)
        a_neg = -jnp.exp(b_a_log[l].reshape(8))
        ssd_w = {
            'conv_w': jnp.pad(b_conv_w[l], ((0, 8 - B_CONV_W), (0, 0))),
            'conv_b': b_conv_b[l].reshape(1, B_XBC),
            'pc': jnp.zeros((8, LANES), F32).at[0, :8].set(dtb).at[1, :8].set(a_neg),
            'prb': jnp.broadcast_to(dtb[:, None], (8, LANES)),
            'pra': jnp.broadcast_to(a_neg[:, None], (8, LANES)),
            'dsk': jnp.repeat(b_d[l], HEAD_W).reshape(1, BRANCH_W),
            'ng': b_norm_g[l].reshape(1, BRANCH_W),
        }
        o_b_p, st_p = _ssd_group(ub[:n_p], ssd_w, jnp.zeros((bp, 2, BRANCH_W, LANES), F32), bp, tp)
        o_b_s, _ = _ssd_group(ub[n_p:], ssd_w, _state_to_blocks(state_ssd[:, l]), bs, ts)

        branches = jnp.stack([
            jnp.concatenate([o_a_p.reshape(n_p, BRANCH_W), o_a_s.reshape(n_s, BRANCH_W)], axis=0),
            jnp.concatenate([o_b_p.reshape(n_p, BRANCH_W), o_b_s.reshape(n_s, BRANCH_W)], axis=0),
            o_c,
            jnp.concatenate([o_d_p.reshape(n_p, BRANCH_W), o_d_s.reshape(n_s, BRANCH_W)], axis=0),
        ], axis=0).astype(BF16)

        x1, h2, gate, eidx = _merge(x, mod_tab, g1, g2, branches, wg, wb, wo, rwh, rwl, rb,
                                    n_prompt_tiles, tiles_per_sample)
        block_expert, row_token, row_slot = _moe_dispatch(eidx[:, :TOP_K])
        y_slots = _moe_experts(block_expert, row_token, row_slot, h2, moe_w1[l],
                               moe_b1[l].reshape(N_EXPERTS, 1, -1), moe_w2[l], moe_b2[l].reshape(N_EXPERTS, 1, -1))
        x = _combine(x1, mod_tab, gate, y_slots, final_g.reshape(1, D_MODEL), n_prompt_tiles, tiles_per_sample,
                     final=(l == depth - 1))

        new_cache[0].append(ak_p)
        new_cache[1].append(av_p)
        new_cache[2].append(ck[:n_p].reshape(bp, tp, N_HEADS, 2 * C_QK_DIM))
        new_cache[3].append(cv[:n_p].reshape(bp, tp, N_HEADS, HEAD_W))
        new_cache[4].append(dk[:n_p].reshape(bp, tp, N_HEADS, HEAD_W))
        new_cache[5].append(dv[:n_p].reshape(bp, tp, N_HEADS, HEAD_W))
        new_cache[6].append(_blocks_to_state(st_p))

    yn = x
    y_prompt = yn[:n_p].reshape(bp, tp, D_MODEL)
    y_sample = yn[n_p:].reshape(bs, ts, D_MODEL)
    return (y_prompt, y_sample) + tuple(jnp.stack(v, axis=1) for v in new_cache)
```

```python
import functools
import math

import jax
import jax.numpy as jnp
import numpy as np
from jax import lax
from jax.experimental import pallas as pl
from jax.experimental.pallas import tpu as pltpu

F32 = jnp.float32
BF16 = jnp.bfloat16

D_MODEL = 1024
N_BRANCH = 4
BRANCH_W = 256
HEAD_W = 64
N_HEADS = 4
GRID_W = 64
ROPE_THETA = 10000.0
EPS = 1e-6
A_KV_HEADS = 2
A_KV_W = A_KV_HEADS * HEAD_W
C_QK_DIM = 32
B_HEADS = 4
B_STATE = 64
B_CHUNK = 128
B_XBC = 512
B_CONV_W = 5
CONV_PAD = 8
NA_ROWS = 8
NA_COLS = 16
N_EXPERTS = 32
TOP_K = 4
D_FF = 1024
SWIGLU_LIMIT = 7.0
SWIGLU_ALPHA = 1.702
LANES = 128
NEG = -1e30

TOKEN_TILE = 256
MOE_ROWS = 256
ATT_W = 3 * BRANCH_W
SSD_W = B_XBC + BRANCH_W + LANES
OFF_A, OFF_B, OFF_C, OFF_D = 0, 512, 512 + SSD_W, 512 + SSD_W + ATT_W
W1_COLS = OFF_D + ATT_W
VMEM_LIMIT = 56 * 1024 * 1024


def _cparams(*sem):
    return pltpu.CompilerParams(dimension_semantics=sem, vmem_limit_bytes=VMEM_LIMIT)


def _dot(a, b):
    return jnp.dot(a, b, preferred_element_type=F32)


def _dot_nt(a, b):
    return lax.dot_general(a, b, (((1,), (1,)), ((), ())), preferred_element_type=F32)


def _dot_tn(a, b):
    return lax.dot_general(a, b, (((0,), (0,)), ((), ())), preferred_element_type=F32)


def _sigmoid(x):
    return 1.0 / (1.0 + jnp.exp(-x))


def _split2(x):
    hi = x.astype(BF16)
    return hi, (x - hi.astype(F32)).astype(BF16)


def _split3(x):
    h1 = x.astype(BF16)
    r1 = x - h1.astype(F32)
    h2 = r1.astype(BF16)
    h3 = (r1 - h2.astype(F32)).astype(BF16)
    return h1, h2, h3


def _norm_mod(x, g, shift, scale):
    ms = jnp.mean(x * x, axis=-1, keepdims=True)
    return (x * lax.rsqrt(ms + EPS)) * g * (1.0 + scale) + shift


def _head_rmsnorm(x, bd, g):
    hi, lo = _split2(x * x)
    ms = _dot(hi, bd) + _dot(lo, bd)
    return x * lax.rsqrt(ms + EPS) * g


def _rope(x, perm, cos, sin):
    hi, lo = _split2(x)
    return x * cos + (_dot(hi, perm) + _dot(lo, perm)) * sin


def _mod_kernel(c_ref, w_ref, b_ref, o_ref):
    c = c_ref[...]
    s = (c * _sigmoid(c)).astype(BF16)
    o_ref[...] = _dot(s, w_ref[...].astype(BF16)) + b_ref[...]


def _modulation(cvec, w_mod, b_mod):
    n = w_mod.shape[1]
    tn = 1536
    return pl.pallas_call(
        _mod_kernel,
        grid=(n // tn,),
        in_specs=[pl.BlockSpec((16, D_MODEL), lambda j: (0, 0)),
                  pl.BlockSpec((D_MODEL, tn), lambda j: (0, j)),
                  pl.BlockSpec((1, tn), lambda j: (0, j))],
        out_specs=pl.BlockSpec((16, tn), lambda j: (0, j)),
        out_shape=jax.ShapeDtypeStruct((16, n), F32),
        compiler_params=_cparams("parallel"),
        name="modulation",
    )(cvec, w_mod, b_mod)


def _mod_row(i, n_prompt_tiles, tiles_per_sample):
    return jnp.where(i < n_prompt_tiles, 0, 1 + (i - n_prompt_tiles) // tiles_per_sample)


def _rope_row(i, n_prompt_tiles, tiles_per_sample):
    return jnp.where(i < n_prompt_tiles, 0, 1 + (i - n_prompt_tiles) % tiles_per_sample)


def _inproj_kernel(x_ref, mod_ref, g_ref, w_ref, rope_ref, bd_ref, pa_ref, pc_ref, ex_ref, gq_ref, gk_ref,
                   atta_ref, attc_ref, attd_ref, kva_ref, kvc_ref, kvd_ref, ssd_ref):
    mod = mod_ref[...]
    h = _norm_mod(x_ref[...], g_ref[...], mod[0:1], mod[1:2])
    u = _dot(h.astype(BF16), w_ref[...])
    cos_a, sin_a = rope_ref[:, 0:256], rope_ref[:, 256:512]
    cos_c, sin_c = rope_ref[:, 512:768], rope_ref[:, 768:1024]
    q_scale = HEAD_W ** -0.5

    q = _rope(_head_rmsnorm(u[:, OFF_A:OFF_A + 256], bd_ref[...], gq_ref[...]), pa_ref[...], cos_a, sin_a)
    k = _rope(_head_rmsnorm(u[:, OFF_A + 256:OFF_A + 384], bd_ref[0:A_KV_W, 0:A_KV_W], gk_ref[...]),
              pa_ref[0:A_KV_W, 0:A_KV_W], cos_a[:, 0:A_KV_W], sin_a[:, 0:A_KV_W])
    v = u[:, OFF_A + 384:OFF_A + 512]
    ex = ex_ref[...]
    atta_ref[:, 0:256] = (q * q_scale).astype(BF16)
    atta_ref[:, 256:512] = _dot(k.astype(BF16), ex).astype(BF16)
    atta_ref[:, 512:768] = _dot(v.astype(BF16), ex).astype(BF16)
    kva_ref[:, 0:A_KV_W] = k
    kva_ref[:, A_KV_W:2 * A_KV_W] = v

    ssd_ref[...] = u[:, OFF_B:OFF_B + SSD_W]

    cq = _rope(u[:, OFF_C:OFF_C + 256], pc_ref[...], cos_c, sin_c)
    ck = _rope(u[:, OFF_C + 256:OFF_C + 512], pc_ref[...], cos_c, sin_c)
    cv = u[:, OFF_C + 512:OFF_C + 768]
    attc_ref[:, 0:256] = cq.astype(BF16)
    attc_ref[:, 256:512] = ck.astype(BF16)
    attc_ref[:, 512:768] = cv.astype(BF16)
    kvc_ref[:, 0:256] = ck
    kvc_ref[:, 256:512] = cv

    attd_ref[:, 0:256] = (u[:, OFF_D:OFF_D + 256] * q_scale).astype(BF16)
    attd_ref[:, 256:768] = u[:, OFF_D + 256:OFF_D + 768].astype(BF16)
    kvd_ref[...] = u[:, OFF_D + 256:OFF_D + 768]


def _in_proj(x, mod_tab, g, w1, rope_tab, bd, pa, pc, ex, gq, gk, n_prompt_tiles, tiles_per_sample):
    nt = x.shape[0]
    mrow = functools.partial(_mod_row, n_prompt_tiles=n_prompt_tiles, tiles_per_sample=tiles_per_sample)
    rrow = functools.partial(_rope_row, n_prompt_tiles=n_prompt_tiles, tiles_per_sample=tiles_per_sample)
    const = lambda *shape: pl.BlockSpec(shape, lambda i: (0,) * len(shape))
    rows = lambda w: pl.BlockSpec((TOKEN_TILE, w), lambda i: (i, 0))
    widths = (ATT_W, ATT_W, ATT_W, 2 * A_KV_W, 2 * BRANCH_W, 2 * BRANCH_W, SSD_W)
    dtypes = (BF16, BF16, BF16, F32, F32, F32, F32)
    return pl.pallas_call(
        _inproj_kernel,
        grid=(nt // TOKEN_TILE,),
        in_specs=[rows(D_MODEL),
                  pl.BlockSpec((None, 8, D_MODEL), lambda i: (mrow(i), 0, 0)),
                  const(1, D_MODEL), const(D_MODEL, W1_COLS),
                  pl.BlockSpec((TOKEN_TILE, 4 * BRANCH_W), lambda i: (rrow(i), 0)),
                  const(BRANCH_W, BRANCH_W), const(BRANCH_W, BRANCH_W), const(BRANCH_W, BRANCH_W),
                  const(A_KV_W, BRANCH_W), const(1, BRANCH_W), const(1, A_KV_W)],
        out_specs=[rows(w) for w in widths],
        out_shape=[jax.ShapeDtypeStruct((nt, w), dt) for w, dt in zip(widths, dtypes)],
        compiler_params=_cparams("parallel"),
        name="in_proj",
    )(x, mod_tab, g, w1, rope_tab, bd, pa, pc, ex, gq, gk)


def _rope_perm(dim):
    p = np.zeros((BRANCH_W, BRANCH_W), np.float32)
    s = dim // 4
    for j in range(BRANCH_W):
        quarter = (j % dim) // s
        if quarter % 2 == 0:
            p[j + s, j] = -1.0
        else:
            p[j - s, j] = 1.0
    return jnp.asarray(p, BF16)


def _rope_table(t, n_identity):
    pos = jnp.arange(t)
    rows = (pos // GRID_W).astype(F32)
    cols = (pos % GRID_W).astype(F32)
    parts = []
    for dim in (HEAD_W, C_QK_DIM):
        axis_dim = dim // 2
        inv = ROPE_THETA ** (-jnp.arange(0, axis_dim, 2, dtype=F32) / axis_dim)
        ang_r = rows[:, None] * inv[None, :]
        ang_c = cols[:, None] * inv[None, :]
        ang = jnp.concatenate([ang_r, ang_r, ang_c, ang_c], axis=-1)
        reps = BRANCH_W // dim
        parts += [jnp.tile(jnp.cos(ang), (1, reps)), jnp.tile(jnp.sin(ang), (1, reps))]
    tab = jnp.concatenate(parts, axis=1)
    ident = jnp.concatenate([jnp.ones((n_identity, BRANCH_W), F32), jnp.zeros((n_identity, BRANCH_W), F32)] * 2, axis=1)
    return jnp.concatenate([ident, tab], axis=0)


def _mha_kernel(lam_ref, q_ref, k_ref, v_ref, *rest, n_maps, scale, has_ctx, subln_scale):
    rest = list(rest)
    ck_ref, cv_ref = (rest.pop(0), rest.pop(0)) if has_ctx else (None, None)
    g_ref, bd_ref = (rest.pop(0), rest.pop(0)) if subln_scale is not None else (None, None)
    o_ref = rest.pop(0)
    q = q_ref[...]
    k = k_ref[...]
    v = v_ref[...]
    tq = q.shape[0]
    lane = lax.broadcasted_iota(jnp.int32, (1, BRANCH_W), 1)
    sub_w = HEAD_W // n_maps
    acc = jnp.zeros((tq, BRANCH_W), F32)
    for h in range(N_HEADS):
        oh = None
        for j in range(n_maps):
            qm = jnp.where((lane // sub_w) == (h * n_maps + j), q, jnp.zeros_like(q))
            s = _dot_nt(qm, k)
            if scale != 1.0:
                s = s * scale
            m = jnp.max(s, axis=-1, keepdims=True)
            if has_ctx:
                sc = _dot_nt(qm, ck_ref[...])
                if scale != 1.0:
                    sc = sc * scale
                m = jnp.maximum(m, jnp.max(sc, axis=-1, keepdims=True))
                pc = jnp.exp(sc - m)
            p = jnp.exp(s - m)
            l = jnp.sum(p, axis=-1, keepdims=True)
            pv = _dot(p.astype(BF16), v)
            if has_ctx:
                l = l + jnp.sum(pc, axis=-1, keepdims=True)
                pv = pv + _dot(pc.astype(BF16), cv_ref[...])
            pv = pv * (1.0 / l)
            oh = pv if j == 0 else oh - lam_ref[0] * pv
        acc = jnp.where((lane // HEAD_W) == h, oh, acc)
    if subln_scale is not None:
        acc = _head_rmsnorm(acc, bd_ref[...], g_ref[...]) * subln_scale
    o_ref[...] = acc.astype(o_ref.dtype)


def _mha(att, lam, *, n_seq, seq, row0, n_maps, scale, ctx=None, subln=None, tq=256):
    qb0 = row0 // tq
    kb0 = row0 // seq
    nq = seq // tq
    assert row0 % seq == 0 and seq % tq == 0
    in_specs = [pl.BlockSpec(memory_space=pltpu.SMEM),
                pl.BlockSpec((tq, BRANCH_W), lambda i, j: (qb0 + i * nq + j, 0)),
                pl.BlockSpec((seq, BRANCH_W), lambda i, j: (kb0 + i, 1)),
                pl.BlockSpec((seq, BRANCH_W), lambda i, j: (kb0 + i, 2))]
    args = [lam, att, att, att]
    kk = seq
    if ctx is not None:
        kc = ctx[0].shape[1]
        kk += kc
        in_specs += [pl.BlockSpec((None, kc, BRANCH_W), lambda i, j: (i, 0, 0))] * 2
        args += list(ctx)
    subln_scale = None
    if subln is not None:
        in_specs += [pl.BlockSpec((1, BRANCH_W), lambda i, j: (0, 0)),
                     pl.BlockSpec((BRANCH_W, BRANCH_W), lambda i, j: (0, 0))]
        args += [subln[0], subln[1]]
        subln_scale = subln[2]
    return pl.pallas_call(
        functools.partial(_mha_kernel, n_maps=n_maps, scale=scale, has_ctx=ctx is not None, subln_scale=subln_scale),
        grid=(n_seq, nq),
        in_specs=in_specs,
        out_specs=pl.BlockSpec((tq, BRANCH_W), lambda i, j: (i * nq + j, 0)),
        out_shape=jax.ShapeDtypeStruct((n_seq * seq, BRANCH_W), BF16),
        compiler_params=_cparams("parallel", "parallel"),
        name=f"mha_maps{n_maps}_k{kk}",
    )(*args)


def _na_row0(r, n_rows):
    return jnp.clip(r - NA_ROWS // 2, 0, n_rows - NA_ROWS)


def _na_kernel(q_ref, k_ref, v_ref, ck_ref, cv_ref, bias_ref, o_ref, *, n_rows):
    r = pl.program_id(1)
    start = pl.multiple_of(_na_row0(r, n_rows) * GRID_W, GRID_W)
    win = NA_ROWS * GRID_W
    q = q_ref[...]
    kw = k_ref[pl.ds(start, win), :]
    vw = v_ref[pl.ds(start, win), :]
    lane = lax.broadcasted_iota(jnp.int32, (1, BRANCH_W), 1)
    qs = jnp.concatenate(
        [jnp.where((lane // HEAD_W) == h, q, jnp.zeros_like(q)) for h in range(N_HEADS)], axis=0)
    s_loc = _dot_nt(qs, kw) + bias_ref[...]
    s_ctx = _dot_nt(qs, ck_ref[...])
    m = jnp.maximum(jnp.max(s_loc, axis=-1, keepdims=True), jnp.max(s_ctx, axis=-1, keepdims=True))
    p_loc = jnp.exp(s_loc - m)
    p_ctx = jnp.exp(s_ctx - m)
    l = jnp.sum(p_loc, axis=-1, keepdims=True) + jnp.sum(p_ctx, axis=-1, keepdims=True)
    o = (_dot(p_loc.astype(BF16), vw) + _dot(p_ctx.astype(BF16), cv_ref[...])) * (1.0 / l)
    acc = jnp.zeros((GRID_W, BRANCH_W), F32)
    for h in range(N_HEADS):
        acc = jnp.where((lane // HEAD_W) == h, o[h * GRID_W:(h + 1) * GRID_W], acc)
    o_ref[...] = acc.astype(o_ref.dtype)


def _na_attention(att, ck, cv, bias_tab, *, n_seq, seq, row0):
    n_rows = seq // GRID_W
    kk = ck.shape[1]
    win = NA_ROWS * GRID_W
    qb0 = row0 // GRID_W
    kb0 = row0 // seq
    assert row0 % seq == 0
    return pl.pallas_call(
        functools.partial(_na_kernel, n_rows=n_rows),
        grid=(n_seq, n_rows),
        in_specs=[pl.BlockSpec((GRID_W, BRANCH_W), lambda i, r: (qb0 + i * n_rows + r, 0)),
                  pl.BlockSpec((seq, BRANCH_W), lambda i, r: (kb0 + i, 1)),
                  pl.BlockSpec((seq, BRANCH_W), lambda i, r: (kb0 + i, 2)),
                  pl.BlockSpec((None, kk, BRANCH_W), lambda i, r: (i, 0, 0)),
                  pl.BlockSpec((None, kk, BRANCH_W), lambda i, r: (i, 0, 0)),
                  pl.BlockSpec((None, N_HEADS * GRID_W, win), lambda i, r: (r - _na_row0(r, n_rows), 0, 0))],
        out_specs=pl.BlockSpec((GRID_W, BRANCH_W), lambda i, r: (i * n_rows + r, 0)),
        out_shape=jax.ShapeDtypeStruct((n_seq * seq, BRANCH_W), BF16),
        compiler_params=_cparams("parallel", "parallel"),
        name="na_attention",
    )(att, att, att, ck, cv, bias_tab)


def _na_bias_table(rpb, n_rows):
    var = jnp.arange(NA_ROWS)[:, None, None, None]
    col = jnp.arange(GRID_W)[None, :, None, None]
    j = jnp.arange(NA_ROWS)[None, None, :, None]
    kc = jnp.arange(GRID_W)[None, None, None, :]
    c0 = jnp.clip(col - NA_COLS // 2, 0, GRID_W - NA_COLS)
    valid = (kc >= c0) & (kc < c0 + NA_COLS)
    off_r = j - var + (NA_ROWS - 1)
    off_c = jnp.clip(kc - col + (NA_COLS - 1), 0, 2 * NA_COLS - 2)
    shape = (NA_ROWS, GRID_W, NA_ROWS, GRID_W)
    off_r = jnp.broadcast_to(off_r, shape)
    off_c = jnp.broadcast_to(off_c, shape)
    tab = rpb[:, off_r, off_c]
    tab = jnp.where(jnp.broadcast_to(valid, shape)[None], tab, NEG)
    tab = jnp.transpose(tab, (1, 0, 2, 3, 4))
    return tab.reshape(NA_ROWS, N_HEADS * GRID_W, NA_ROWS * GRID_W).astype(F32)


def _softplus(x):
    return jnp.maximum(x, 0.0) + jnp.log1p(jnp.exp(-jnp.abs(x)))


def _expand_heads(colmat, d, lane256):
    out = jnp.zeros((colmat.shape[0], BRANCH_W), F32)
    for h in range(B_HEADS):
        j = d * B_HEADS + h
        out = jnp.where((lane256 // HEAD_W) == h,
                        jnp.broadcast_to(colmat[:, j:j + 1], (colmat.shape[0], BRANCH_W)), out)
    return out


def _ssd_kernel(xbc_ref, z_ref, dt_ref, cw_ref, cb_ref, pc_ref, dsk_ref, ng_ref, init_ref,
                o_ref, st_ref, xpad_s, xc_s, yf_s, s_s, *, seq):
    n_chunks = seq // B_CHUNK
    L = B_CHUNK
    ri = lax.broadcasted_iota(jnp.int32, (L, L), 0)
    ci = lax.broadcasted_iota(jnp.int32, (L, L), 1)
    low = ci <= ri
    upp = ri <= ci
    low_b = jnp.where(low, 1.0, 0.0).astype(BF16)
    upp_b = jnp.where(upp, 1.0, 0.0).astype(BF16)
    lane128 = lax.broadcasted_iota(jnp.int32, (1, LANES), 1)
    lane256 = lax.broadcasted_iota(jnp.int32, (1, BRANCH_W), 1)
    row256 = lax.broadcasted_iota(jnp.int32, (BRANCH_W, 1), 0)
    blockmask = (row256 // (2 * HEAD_W)) == (lane128 // B_STATE)

    xpad_s[0:CONV_PAD, :] = jnp.zeros((CONV_PAD, B_XBC), F32)
    xpad_s[seq + CONV_PAD:seq + 2 * CONV_PAD, :] = jnp.zeros((CONV_PAD, B_XBC), F32)

    def pad_body(c, carry):
        base = pl.multiple_of(c * L, L)
        xpad_s[pl.ds(pl.multiple_of(base + CONV_PAD, CONV_PAD), L), :] = xbc_ref[pl.ds(base, L), :]
        return carry

    lax.fori_loop(0, n_chunks, pad_body, 0)
    cw = cw_ref[...]
    cb = cb_ref[...]

    def conv_body(c, carry):
        base = pl.multiple_of(c * L, L)
        w = xpad_s[pl.ds(base, L + 2 * CONV_PAD), :]
        acc = jnp.zeros((L, B_XBC), F32) + cb
        for kk in range(B_CONV_W):
            off = CONV_PAD - B_CONV_W // 2 + kk
            acc = acc + w[off:off + L, :] * cw[kk:kk + 1, :]
        xc_s[pl.ds(base, L), :] = acc * _sigmoid(acc)
        return carry

    lax.fori_loop(0, n_chunks, conv_body, 0)

    dt_bias = pc_ref[0:1, :]
    a_neg = pc_ref[1:2, :]
    dsk = dsk_ref[...]
    ng = ng_ref[...]

    def chunk(c, d):
        base = pl.multiple_of(c * L, L)
        xs = xc_s[pl.ds(base, L), 0:BRANCH_W]
        bm = xc_s[pl.ds(base, L), BRANCH_W:BRANCH_W + LANES].astype(BF16)
        cm = xc_s[pl.ds(base, L), BRANCH_W + LANES:B_XBC].astype(BF16)
        dtc = _softplus(dt_ref[pl.ds(base, L), :] + dt_bias)
        da_c = dtc * a_neg
        da_r = da_c.T
        tri_c = low_b if d == 0 else upp_b
        tri_r = upp_b if d == 0 else low_b
        c1, c2, c3 = _split3(da_c)
        cum_c = _dot(tri_c, c1) + _dot(tri_c, c2) + _dot(tri_c, c3)
        r1, r2, r3 = _split3(da_r)
        cum_r = _dot(r1, tri_r) + _dot(r2, tri_r) + _dot(r3, tri_r)
        cum_end = cum_c[L - 1:L, :] if d == 0 else cum_c[0:1, :]
        dmask = low if d == 0 else upp

        g0 = _dot_nt(jnp.where(lane128 < B_STATE, cm, jnp.zeros_like(cm)), bm)
        g1 = _dot_nt(jnp.where(lane128 >= B_STATE, cm, jnp.zeros_like(cm)), bm)
        dt_x = _expand_heads(dtc, d, lane256)
        e_a = _expand_heads(jnp.exp(cum_c), d, lane256)
        t_e = _expand_heads(jnp.exp(cum_end - cum_c), d, lane256)
        xdt = xs * dt_x
        xdt_b = xdt.astype(BF16)
        y = jnp.zeros((L, BRANCH_W), F32)
        for h in range(B_HEADS):
            j = d * B_HEADS + h
            col = jnp.broadcast_to(cum_c[:, j:j + 1], (L, L))
            row = jnp.broadcast_to(cum_r[j:j + 1, :], (L, L))
            dec = jnp.exp(jnp.where(dmask, col - row, NEG))
            sc = ((g0 if h < 2 else g1) * dec).astype(BF16)
            y = jnp.where((lane256 // HEAD_W) == h, _dot(sc, xdt_b), y)
        state = s_s[...]
        y = y + _dot_nt(cm, state.astype(BF16)) * e_a
        zmat = _dot_tn((xdt * t_e).astype(BF16), bm)
        e_end = jnp.exp(cum_end)
        cd = jnp.zeros((BRANCH_W, LANES), F32)
        for h in range(B_HEADS):
            j = d * B_HEADS + h
            cd = jnp.where((row256 // HEAD_W) == h, jnp.broadcast_to(e_end[:, j:j + 1], (BRANCH_W, LANES)), cd)
        s_s[...] = state * cd + jnp.where(blockmask, zmat, 0.0)
        return base, xs, y

    s_s[...] = init_ref[0]

    def fwd_body(c, carry):
        base, _, y = chunk(c, 0)
        yf_s[pl.ds(base, L), :] = y
        return carry

    lax.fori_loop(0, n_chunks, fwd_body, 0)
    st_ref[0] = s_s[...]
    s_s[...] = init_ref[1]

    def bwd_body(i, carry):
        c = n_chunks - 1 - i
        base, xs, y = chunk(c, 1)
        z = z_ref[pl.ds(base, L), :]
        yt = (yf_s[pl.ds(base, L), :] + y + dsk * xs) * (z * _sigmoid(z))
        ms = jnp.mean(yt * yt, axis=-1, keepdims=True)
        o_ref[pl.ds(base, L), :] = (yt * lax.rsqrt(ms + EPS) * ng).astype(o_ref.dtype)
        return carry

    lax.fori_loop(0, n_chunks, bwd_body, 0)
    st_ref[1] = s_s[...]


def _ssd_branch(ssd_in, conv_w, conv_b, pc, dsk, ng, init, *, n_seq, seq, row0):
    b0 = row0 // seq
    assert row0 % seq == 0
    full = lambda *shape: pl.BlockSpec(shape, lambda i: (0,) * len(shape))
    return pl.pallas_call(
        functools.partial(_ssd_kernel, seq=seq),
        grid=(n_seq,),
        in_specs=[pl.BlockSpec((seq, B_XBC), lambda i: (b0 + i, 0)),
                  pl.BlockSpec((seq, BRANCH_W), lambda i: (b0 + i, B_XBC // BRANCH_W)),
                  pl.BlockSpec((seq, LANES), lambda i: (b0 + i, (B_XBC + BRANCH_W) // LANES)),
                  full(8, B_XBC), full(1, B_XBC), full(8, LANES), full(1, BRANCH_W), full(1, BRANCH_W),
                  pl.BlockSpec((None, 2, BRANCH_W, LANES), lambda i: (i, 0, 0, 0))],
        out_specs=[pl.BlockSpec((seq, BRANCH_W), lambda i: (i, 0)),
                   pl.BlockSpec((None, 2, BRANCH_W, LANES), lambda i: (i, 0, 0, 0))],
        out_shape=[jax.ShapeDtypeStruct((n_seq * seq, BRANCH_W), BF16),
                   jax.ShapeDtypeStruct((n_seq, 2, BRANCH_W, LANES), F32)],
        scratch_shapes=[pltpu.VMEM((seq + 2 * CONV_PAD, B_XBC), F32), pltpu.VMEM((seq, B_XBC), F32),
                        pltpu.VMEM((seq, BRANCH_W), F32), pltpu.VMEM((BRANCH_W, LANES), F32)],
        compiler_params=_cparams("parallel"),
        name=f"ssd_t{seq}",
    )(ssd_in, ssd_in, ssd_in, conv_w, conv_b, pc, dsk, ng, init)


def _merge_kernel(x_ref, mod_ref, g1_ref, g2_ref, *rest, n_prompt_tiles):
    br_p = rest[0:N_BRANCH]
    br_s = rest[N_BRANCH:2 * N_BRANCH]
    wg_ref, wb_ref, wo_ref, rwh_ref, rwl_ref, rb_ref, x1_ref, h2_ref, gate_ref, eidx_ref = rest[2 * N_BRANCH:]
    is_prompt = pl.program_id(0) < n_prompt_tiles
    x = x_ref[...]
    mod = mod_ref[...]
    h = _norm_mod(x, g1_ref[...], mod[0:1], mod[1:2]).astype(BF16)
    merged = None
    for i in range(N_BRANCH):
        gate = _sigmoid(_dot(h, wg_ref[:, i * D_MODEL:(i + 1) * D_MODEL]))
        br = jnp.where(is_prompt, br_p[i][...], br_s[i][...])
        proj = _dot(br, wb_ref[i])
        merged = gate * proj if i == 0 else merged + gate * proj
    y = _dot(merged.astype(BF16), wo_ref[...])
    x1 = x + mod[2:3] * y
    x1_ref[...] = x1
    h2 = _norm_mod(x1, g2_ref[...], mod[3:4], mod[4:5])
    hh, hl = _split2(h2)
    h2_ref[...] = h2
    rwh = rwh_ref[...]
    logits = _dot(hh, rwh) + _dot(hl, rwh) + _dot(hh, rwl_ref[...]) + rb_ref[...]
    lane = lax.broadcasted_iota(jnp.int32, (1, LANES), 1).astype(F32)
    cur = jnp.where(lane < N_EXPERTS, logits, NEG)
    vals, idxs = [], []
    for _ in range(TOP_K):
        m = jnp.max(cur, axis=-1, keepdims=True)
        am = jnp.min(jnp.where(cur == m, lane, float(LANES)), axis=-1, keepdims=True)
        vals.append(m)
        idxs.append(am)
        cur = jnp.where(lane == am, NEG, cur)
    exps = [jnp.exp(v - vals[0]) for v in vals]
    inv = 1.0 / (exps[0] + exps[1] + exps[2] + exps[3])
    gate = jnp.zeros_like(logits)
    eidx = jnp.zeros_like(logits)
    for k in range(TOP_K):
        gate = jnp.where(lane == k, exps[k] * inv, gate)
        eidx = jnp.where(lane == k, idxs[k], eidx)
    gate_ref[...] = gate
    eidx_ref[...] = eidx.astype(jnp.int32)


def _merge(x, mod_tab, g1, g2, br_p, br_s, wg, wb, wo, rwh, rwl, rb, n_prompt_tiles, tiles_per_sample):
    nt = x.shape[0]
    mrow = functools.partial(_mod_row, n_prompt_tiles=n_prompt_tiles, tiles_per_sample=tiles_per_sample)
    const = lambda *shape: pl.BlockSpec(shape, lambda i: (0,) * len(shape))
    rows = lambda w: pl.BlockSpec((TOKEN_TILE, w), lambda i: (i, 0))
    p_rows = pl.BlockSpec((TOKEN_TILE, BRANCH_W), lambda i: (jnp.minimum(i, n_prompt_tiles - 1), 0))
    s_rows = pl.BlockSpec((TOKEN_TILE, BRANCH_W), lambda i: (jnp.maximum(i - n_prompt_tiles, 0), 0))
    return pl.pallas_call(
        functools.partial(_merge_kernel, n_prompt_tiles=n_prompt_tiles),
        grid=(nt // TOKEN_TILE,),
        in_specs=[rows(D_MODEL),
                  pl.BlockSpec((None, 8, D_MODEL), lambda i: (mrow(i), 0, 0)),
                  const(1, D_MODEL), const(1, D_MODEL)]
                 + [p_rows] * N_BRANCH + [s_rows] * N_BRANCH
                 + [const(D_MODEL, N_BRANCH * D_MODEL), const(N_BRANCH, BRANCH_W, D_MODEL),
                    const(D_MODEL, D_MODEL), const(D_MODEL, LANES), const(D_MODEL, LANES), const(1, LANES)],
        out_specs=[rows(D_MODEL), rows(D_MODEL), rows(LANES), rows(LANES)],
        out_shape=[jax.ShapeDtypeStruct((nt, D_MODEL), F32),
                   jax.ShapeDtypeStruct((nt, D_MODEL), F32),
                   jax.ShapeDtypeStruct((nt, LANES), F32),
                   jax.ShapeDtypeStruct((nt, LANES), jnp.int32)],
        compiler_params=_cparams("parallel"),
        name="merge",
    )(x, mod_tab, g1, g2, *br_p, *br_s, wg, wb, wo, rwh, rwl, rb)


def _moe_kernel(be_ref, tok_ref, tokn_ref, slot_ref, h_hbm, w1_ref, b1_ref, w2_ref, b2_ref, out_hbm,
                w1_s, w2_s, xbuf, ybuf, gsem, ssem):
    i = pl.program_id(0)
    nb = pl.num_programs(0)
    cur = i % 2
    nxt = 1 - cur

    def gather_copy(tref, buf, r):
        return pltpu.make_async_copy(h_hbm.at[pl.ds(tref[0, r], 1), :], xbuf.at[buf, pl.ds(r, 1), :], gsem.at[buf])

    def gather_all(buf):
        return pltpu.make_async_copy(h_hbm.at[pl.ds(0, MOE_ROWS), :], xbuf.at[buf], gsem.at[buf])

    def scatter_copy(buf, r):
        return pltpu.make_async_copy(ybuf.at[buf, pl.ds(r, 1), :], out_hbm.at[pl.ds(slot_ref[0, r], 1), :],
                                     ssem.at[buf])

    def scatter_all(buf):
        return pltpu.make_async_copy(ybuf.at[buf], out_hbm.at[pl.ds(0, MOE_ROWS), :], ssem.at[buf])

    @pl.when(i == 0)
    def _():
        for r in range(MOE_ROWS):
            gather_copy(tok_ref, 0, r).start()

    prev = be_ref[jnp.maximum(i - 1, 0)]

    @pl.when(jnp.logical_or(i == 0, be_ref[i] != prev))
    def _():
        w1_s[...] = w1_ref[...].astype(BF16)
        w2_s[...] = w2_ref[...].astype(BF16)

    @pl.when(i >= 2)
    def _():
        scatter_all(cur).wait()

    for r in range(MOE_ROWS):
        gather_copy(tokn_ref, nxt, r).start()
    gather_all(cur).wait()

    u = _dot(xbuf[cur].astype(BF16), w1_s[...]) + b1_ref[...]
    glu = jnp.minimum(u[:, :D_FF], SWIGLU_LIMIT)
    lin = jnp.clip(u[:, D_FF:], -SWIGLU_LIMIT, SWIGLU_LIMIT)
    act = glu * _sigmoid(SWIGLU_ALPHA * glu) * (lin + 1.0)
    ybuf[cur] = _dot(act.astype(BF16), w2_s[...]) + b2_ref[...]
    for r in range(MOE_ROWS):
        scatter_copy(cur, r).start()

    @pl.when(i == nb - 1)
    def _():
        gather_all(nxt).wait()
        scatter_all(nxt).wait()
        scatter_all(cur).wait()


def _moe_experts(block_expert, row_token, row_slot, h2, w1, b1, w2, b2):
    n_blocks = row_token.shape[0]
    n_rows = n_blocks * MOE_ROWS
    assert n_blocks >= 2
    smem_rows = lambda imap: pl.BlockSpec((None, 1, MOE_ROWS), imap, memory_space=pltpu.SMEM)
    grid_spec = pltpu.PrefetchScalarGridSpec(
        num_scalar_prefetch=1,
        grid=(n_blocks,),
        in_specs=[smem_rows(lambda i, be: (i, 0, 0)),
                  smem_rows(lambda i, be: (jnp.minimum(i + 1, n_blocks - 1), 0, 0)),
                  smem_rows(lambda i, be: (i, 0, 0)),
                  pl.BlockSpec(memory_space=pl.ANY),
                  pl.BlockSpec((None, D_MODEL, 2 * D_FF), lambda i, be: (be[i], 0, 0)),
                  pl.BlockSpec((None, 1, 2 * D_FF), lambda i, be: (be[i], 0, 0)),
                  pl.BlockSpec((None, D_FF, D_MODEL), lambda i, be: (be[i], 0, 0)),
                  pl.BlockSpec((None, 1, D_MODEL), lambda i, be: (be[i], 0, 0))],
        out_specs=pl.BlockSpec(memory_space=pl.ANY),
        scratch_shapes=[pltpu.VMEM((D_MODEL, 2 * D_FF), BF16), pltpu.VMEM((D_FF, D_MODEL), BF16),
                        pltpu.VMEM((2, MOE_ROWS, D_MODEL), F32), pltpu.VMEM((2, MOE_ROWS, D_MODEL), F32),
                        pltpu.SemaphoreType.DMA((2,)), pltpu.SemaphoreType.DMA((2,))],
    )
    return pl.pallas_call(
        _moe_kernel,
        grid_spec=grid_spec,
        out_shape=jax.ShapeDtypeStruct((n_rows, D_MODEL), F32),
        compiler_params=pltpu.CompilerParams(dimension_semantics=("arbitrary",), vmem_limit_bytes=VMEM_LIMIT,
                                             has_side_effects=True),
        name="moe_experts",
    )(block_expert, row_token, row_token, row_slot, h2, w1, b1, w2, b2)


def _moe_dispatch(eidx):
    n = eidx.shape[0]
    n_assign = n * TOP_K
    n_blocks = -(-n_assign // MOE_ROWS) + N_EXPERTS
    n_rows = n_blocks * MOE_ROWS
    expert = eidx.reshape(-1)
    onehot = (expert[:, None] == jnp.arange(N_EXPERTS, dtype=jnp.int32)[None, :]).astype(jnp.int32)
    incl = jnp.cumsum(onehot, axis=0)
    counts = incl[-1]
    padded = (counts + MOE_ROWS - 1) // MOE_ROWS * MOE_ROWS
    pad_ends = jnp.cumsum(padded)
    pad_starts = pad_ends - padded
    dest = jnp.sum((incl - onehot + pad_starts[None, :]) * onehot, axis=1).astype(jnp.int32)
    assign = jnp.full((n_rows,), -1, jnp.int32).at[dest].set(jnp.arange(n_assign, dtype=jnp.int32))
    is_pad = assign < 0
    filler = n_assign + jnp.cumsum(is_pad.astype(jnp.int32)) - 1
    row_slot = jnp.where(is_pad, filler, assign)
    row_token = jnp.where(is_pad, 0, assign // TOP_K)
    block_start = jnp.arange(n_blocks, dtype=jnp.int32) * MOE_ROWS
    block_expert = jnp.minimum(jnp.sum((pad_ends[None, :] <= block_start[:, None]).astype(jnp.int32), axis=1),
                               N_EXPERTS - 1).astype(jnp.int32)
    return (block_expert, row_token.reshape(n_blocks, 1, MOE_ROWS), row_slot.reshape(n_blocks, 1, MOE_ROWS))


def _combine_kernel(x1_ref, mod_ref, gate_ref, y4_ref, fg_ref, o_ref, *, final):
    gate = gate_ref[...]
    y = None
    for k in range(TOP_K):
        yk = gate[:, k:k + 1] * y4_ref[:, k * D_MODEL:(k + 1) * D_MODEL]
        y = yk if k == 0 else y + yk
    x2 = x1_ref[...] + mod_ref[5:6, :] * y
    if final:
        ms = jnp.mean(x2 * x2, axis=-1, keepdims=True)
        x2 = x2 * lax.rsqrt(ms + EPS) * fg_ref[...]
    o_ref[...] = x2


def _combine(x1, mod_tab, gate, y_slots, final_g, n_prompt_tiles, tiles_per_sample, *, final):
    nt = x1.shape[0]
    y4 = y_slots.reshape(y_slots.shape[0] // TOP_K, TOP_K * D_MODEL)
    mrow = functools.partial(_mod_row, n_prompt_tiles=n_prompt_tiles, tiles_per_sample=tiles_per_sample)
    return pl.pallas_call(
        functools.partial(_combine_kernel, final=final),
        grid=(nt // TOKEN_TILE,),
        in_specs=[pl.BlockSpec((TOKEN_TILE, D_MODEL), lambda i: (i, 0)),
                  pl.BlockSpec((None, 8, D_MODEL), lambda i: (mrow(i), 0, 0)),
                  pl.BlockSpec((TOKEN_TILE, LANES), lambda i: (i, 0)),
                  pl.BlockSpec((TOKEN_TILE, TOP_K * D_MODEL), lambda i: (i, 0)),
                  pl.BlockSpec((1, D_MODEL), lambda i: (0, 0))],
        out_specs=pl.BlockSpec((TOKEN_TILE, D_MODEL), lambda i: (i, 0)),
        out_shape=jax.ShapeDtypeStruct((nt, D_MODEL), F32),
        compiler_params=_cparams("parallel"),
        name="combine_final" if final else "combine",
    )(x1, mod_tab, gate, y4, final_g)


def _state_to_blocks(init):
    b = init.shape[0]
    st = init.reshape(b, 2, B_HEADS * HEAD_W, B_STATE)
    row = jnp.arange(B_HEADS * HEAD_W)[:, None]
    left = jnp.where(row < 2 * HEAD_W, st, 0.0)
    right = jnp.where(row >= 2 * HEAD_W, st, 0.0)
    return jnp.concatenate([left, right], axis=-1)


def _blocks_to_state(s):
    b = s.shape[0]
    row = jnp.arange(B_HEADS * HEAD_W)[:, None]
    st = jnp.where(row < 2 * HEAD_W, s[..., :B_STATE], s[..., B_STATE:])
    return st.reshape(b, 2, B_HEADS, HEAD_W, B_STATE)


def kernel(x_prompt, x_sample, c, cache_a_k, cache_a_v, cache_c_k, cache_c_v, cache_d_k, cache_d_v, state_ssd, c_ctx, norm1_g, norm2_g, w_mod, b_mod, w_in, a_q_g, a_k_g, b_conv_w, b_conv_b, b_dt_bias, b_a_log, b_d, b_norm_g, c_lam, c_subln_g, d_rpb, w_branch, w_out, router_w, router_b, moe_w1, moe_b1, moe_w2, moe_b2, final_g):
    bp, tp, _ = x_prompt.shape
    bs, ts, _ = x_sample.shape
    depth = w_in.shape[0]
    n_p = bp * tp
    n_s = bs * ts
    assert tp == TOKEN_TILE and ts % TOKEN_TILE == 0 and n_p % ts == 0
    n_prompt_tiles = n_p // TOKEN_TILE
    tiles_per_sample = ts // TOKEN_TILE
    past = cache_a_k.shape[2]

    x = jnp.concatenate([x_prompt.reshape(n_p, D_MODEL), x_sample.reshape(n_s, D_MODEL)], axis=0)
    cvec = jnp.zeros((16, D_MODEL), F32).at[0].set(c_ctx).at[1:1 + bs].set(c)
    rope_tab = _rope_table(ts, TOKEN_TILE)
    perm_a = _rope_perm(HEAD_W)
    perm_c = _rope_perm(C_QK_DIM)
    head_of = np.arange(BRANCH_W) // HEAD_W
    bd = jnp.asarray((head_of[:, None] == head_of[None, :]) / HEAD_W, BF16)
    kv_of = np.arange(A_KV_W) // HEAD_W
    ex = jnp.asarray((kv_of[:, None] == (head_of // (N_HEADS // A_KV_HEADS))[None, :])
                     & ((np.arange(A_KV_W) % HEAD_W)[:, None] == (np.arange(BRANCH_W) % HEAD_W)[None, :]), BF16)
    scale_c = C_QK_DIM ** -0.5
    one = jnp.ones((1,), F32)
    rep = N_HEADS // A_KV_HEADS
    zero_state = jnp.zeros((bp, 2, BRANCH_W, LANES), F32)

    new_cache = [[] for _ in range(7)]
    for l in range(depth):
        lam_init = 0.8 - 0.6 * math.exp(-0.3 * l)
        lq = c_lam[l]
        lam = (jnp.exp(jnp.sum(lq[0] * lq[1])) - jnp.exp(jnp.sum(lq[2] * lq[3])) + lam_init).reshape(1)

        wl = w_in[l]
        w1 = jnp.concatenate([wl[:, 0:512], wl[:, 768:1280], wl[:, 512:768], wl[:, 1280:1288],
                              jnp.zeros((D_MODEL, LANES - 8), F32), wl[:, 1288:2824]], axis=1).astype(BF16)
        wg = wl[:, 2824:].astype(BF16)
        wb = w_branch[l].astype(BF16)
        wo = w_out[l].astype(BF16)
        rw = jnp.pad(router_w[l], ((0, 0), (0, LANES - N_EXPERTS)))
        rwh = rw.astype(BF16)
        rwl = (rw - rwh.astype(F32)).astype(BF16)
        rb = jnp.pad(router_b[l], (0, LANES - N_EXPERTS)).reshape(1, LANES)

        mod = _modulation(cvec, w_mod[l], b_mod[l].reshape(1, -1))
        mod_tab = jnp.pad(mod.reshape(16, 6, D_MODEL), ((0, 0), (0, 2), (0, 0)))

        g1 = norm1_g[l].reshape(1, D_MODEL)
        g2 = norm2_g[l].reshape(1, D_MODEL)
        att_a, att_c, att_d, kv_a, kv_c, kv_d, ssd_in = _in_proj(
            x, mod_tab, g1, w1, rope_tab, bd, perm_a, perm_c, ex,
            jnp.tile(a_q_g[l], N_HEADS).reshape(1, BRANCH_W), jnp.tile(a_k_g[l], A_KV_HEADS).reshape(1, A_KV_W),
            n_prompt_tiles, tiles_per_sample)

        ctx_ak = jnp.repeat(cache_a_k[:, l], rep, axis=2).reshape(bs, past, BRANCH_W).astype(BF16)
        ctx_av = jnp.repeat(cache_a_v[:, l], rep, axis=2).reshape(bs, past, BRANCH_W).astype(BF16)
        o_a_p = _mha(att_a, one, n_seq=bp, seq=tp, row0=0, n_maps=1, scale=1.0)
        o_a_s = _mha(att_a, one, n_seq=bs, seq=ts, row0=n_p, n_maps=1, scale=1.0, ctx=(ctx_ak, ctx_av))

        subln = (jnp.tile(c_subln_g[l], N_HEADS).reshape(1, BRANCH_W), bd, 1.0 - lam_init)
        ctx_ck = cache_c_k[:, l].reshape(bs, past, BRANCH_W).astype(BF16)
        ctx_cv = cache_c_v[:, l].reshape(bs, past, BRANCH_W).astype(BF16)
        o_c_p = _mha(att_c, lam, n_seq=bp, seq=tp, row0=0, n_maps=2, scale=scale_c, subln=subln)
        o_c_s = _mha(att_c, lam, n_seq=bs, seq=ts, row0=n_p, n_maps=2, scale=scale_c, ctx=(ctx_ck, ctx_cv),
                     subln=subln)

        o_d_p = _mha(att_d, one, n_seq=bp, seq=tp, row0=0, n_maps=1, scale=1.0)
        o_d_s = _na_attention(att_d, cache_d_k[:, l].reshape(bs, past, BRANCH_W).astype(BF16),
                              cache_d_v[:, l].reshape(bs, past, BRANCH_W).astype(BF16),
                              _na_bias_table(d_rpb[l], ts // GRID_W), n_seq=bs, seq=ts, row0=n_p)

        dtb = b_dt_bias[l].reshape(8)
        a_neg = -jnp.exp(b_a_log[l].reshape(8))
        ssd_args = (jnp.pad(b_conv_w[l], ((0, 8 - B_CONV_W), (0, 0))), b_conv_b[l].reshape(1, B_XBC),
                    jnp.zeros((8, LANES), F32).at[0, :8].set(dtb).at[1, :8].set(a_neg),
                    jnp.repeat(b_d[l], HEAD_W).reshape(1, BRANCH_W), b_norm_g[l].reshape(1, BRANCH_W))
        o_b_p, st_p = _ssd_branch(ssd_in, *ssd_args, zero_state, n_seq=bp, seq=tp, row0=0)
        o_b_s, _ = _ssd_branch(ssd_in, *ssd_args, _state_to_blocks(state_ssd[:, l]), n_seq=bs, seq=ts, row0=n_p)

        x1, h2, gate, eidx = _merge(x, mod_tab, g1, g2, (o_a_p, o_b_p, o_c_p, o_d_p), (o_a_s, o_b_s, o_c_s, o_d_s),
                                    wg, wb, wo, rwh, rwl, rb, n_prompt_tiles, tiles_per_sample)
        block_expert, row_token, row_slot = _moe_dispatch(eidx[:, :TOP_K])
        y_slots = _moe_experts(block_expert, row_token, row_slot, h2, moe_w1[l],
                               moe_b1[l].reshape(N_EXPERTS, 1, -1), moe_w2[l], moe_b2[l].reshape(N_EXPERTS, 1, -1))
        x = _combine(x1, mod_tab, gate, y_slots, final_g.reshape(1, D_MODEL), n_prompt_tiles, tiles_per_sample,
                     final=(l == depth - 1))

        new_cache[0].append(kv_a[:n_p, :A_KV_W].reshape(bp, tp, A_KV_HEADS, HEAD_W))
        new_cache[1].append(kv_a[:n_p, A_KV_W:].reshape(bp, tp, A_KV_HEADS, HEAD_W))
        new_cache[2].append(kv_c[:n_p, :BRANCH_W].reshape(bp, tp, N_HEADS, 2 * C_QK_DIM))
        new_cache[3].append(kv_c[:n_p, BRANCH_W:].reshape(bp, tp, N_HEADS, HEAD_W))
        new_cache[4].append(kv_d[:n_p, :BRANCH_W].reshape(bp, tp, N_HEADS, HEAD_W))
        new_cache[5].append(kv_d[:n_p, BRANCH_W:].reshape(bp, tp, N_HEADS, HEAD_W))
        new_cache[6].append(_blocks_to_state(st_p))

    y_prompt = x[:n_p].reshape(bp, tp, D_MODEL)
    y_sample = x[n_p:].reshape(bs, ts, D_MODEL)
    return (y_prompt, y_sample) + tuple(jnp.stack(v, axis=1) for v in new_cache)
```

```python
import functools
import math

import jax
import jax.numpy as jnp
import numpy as np
from jax import lax
from jax.experimental import pallas as pl
from jax.experimental.pallas import tpu as pltpu

F32 = jnp.float32
BF16 = jnp.bfloat16

D_MODEL = 1024
N_BRANCH = 4
BRANCH_W = 256
HEAD_W = 64
N_HEADS = 4
GRID_W = 64
ROPE_THETA = 10000.0
EPS = 1e-6
A_KV_HEADS = 2
A_KV_W = A_KV_HEADS * HEAD_W
C_QK_DIM = 32
B_HEADS = 4
B_STATE = 64
B_CHUNK = 128
B_XBC = 512
B_CONV_W = 5
CONV_PAD = 8
NA_ROWS = 8
NA_COLS = 16
N_EXPERTS = 32
TOP_K = 4
D_FF = 1024
SWIGLU_LIMIT = 7.0
SWIGLU_ALPHA = 1.702
LANES = 128
NEG = -1e30

TOKEN_TILE = 256
MOE_ROWS = 256
ATT_W = 3 * BRANCH_W
SSD_W = B_XBC + BRANCH_W + LANES
OFF_A, OFF_B, OFF_C, OFF_D = 0, 512, 512 + SSD_W, 512 + SSD_W + ATT_W
W1_COLS = OFF_D + ATT_W
VMEM_LIMIT = 56 * 1024 * 1024


def _cparams(*sem):
    return pltpu.CompilerParams(dimension_semantics=sem, vmem_limit_bytes=VMEM_LIMIT)


def _dot(a, b):
    return jnp.dot(a, b, preferred_element_type=F32)


def _dot_nt(a, b):
    return lax.dot_general(a, b, (((1,), (1,)), ((), ())), preferred_element_type=F32)


def _dot_tn(a, b):
    return lax.dot_general(a, b, (((0,), (0,)), ((), ())), preferred_element_type=F32)


def _sigmoid(x):
    return 1.0 / (1.0 + jnp.exp(-x))


def _split2(x):
    hi = x.astype(BF16)
    return hi, (x - hi.astype(F32)).astype(BF16)


def _split3(x):
    h1 = x.astype(BF16)
    r1 = x - h1.astype(F32)
    h2 = r1.astype(BF16)
    h3 = (r1 - h2.astype(F32)).astype(BF16)
    return h1, h2, h3


def _norm_mod(x, g, shift, scale):
    ms = jnp.mean(x * x, axis=-1, keepdims=True)
    return (x * lax.rsqrt(ms + EPS)) * g * (1.0 + scale) + shift


def _head_rmsnorm(x, bd, g):
    hi, lo = _split2(x * x)
    ms = _dot(hi, bd) + _dot(lo, bd)
    return x * lax.rsqrt(ms + EPS) * g


def _rope(x, perm, cos, sin):
    hi, lo = _split2(x)
    return x * cos + (_dot(hi, perm) + _dot(lo, perm)) * sin


def _mod_kernel(c_ref, w_ref, b_ref, o_ref):
    c = c_ref[...]
    s = (c * _sigmoid(c)).astype(BF16)
    o_ref[...] = _dot(s, w_ref[...].astype(BF16)) + b_ref[...]


def _modulation(cvec, w_mod, b_mod):
    n = w_mod.shape[1]
    tn = 1536
    return pl.pallas_call(
        _mod_kernel,
        grid=(n // tn,),
        in_specs=[pl.BlockSpec((16, D_MODEL), lambda j: (0, 0)),
                  pl.BlockSpec((D_MODEL, tn), lambda j: (0, j)),
                  pl.BlockSpec((1, tn), lambda j: (0, j))],
        out_specs=pl.BlockSpec((16, tn), lambda j: (0, j)),
        out_shape=jax.ShapeDtypeStruct((16, n), F32),
        compiler_params=_cparams("parallel"),
        name="modulation",
    )(cvec, w_mod, b_mod)


def _mod_row(i, n_prompt_tiles, tiles_per_sample):
    return jnp.where(i < n_prompt_tiles, 0, 1 + (i - n_prompt_tiles) // tiles_per_sample)


def _rope_row(i, n_prompt_tiles, tiles_per_sample):
    return jnp.where(i < n_prompt_tiles, 0, 1 + (i - n_prompt_tiles) % tiles_per_sample)


def _inproj_kernel(x_ref, mod_ref, g_ref, w_ref, rope_ref, bd_ref, pa_ref, pc_ref, ex_ref, gq_ref, gk_ref,
                   atta_ref, attc_ref, attd_ref, kva_ref, kvc_ref, kvd_ref, ssd_ref):
    mod = mod_ref[...]
    h = _norm_mod(x_ref[...], g_ref[...], mod[0:1], mod[1:2])
    u = _dot(h.astype(BF16), w_ref[...])
    cos_a, sin_a = rope_ref[:, 0:256], rope_ref[:, 256:512]
    cos_c, sin_c = rope_ref[:, 512:768], rope_ref[:, 768:1024]
    q_scale = HEAD_W ** -0.5

    q = _rope(_head_rmsnorm(u[:, OFF_A:OFF_A + 256], bd_ref[...], gq_ref[...]), pa_ref[...], cos_a, sin_a)
    k = _rope(_head_rmsnorm(u[:, OFF_A + 256:OFF_A + 384], bd_ref[0:A_KV_W, 0:A_KV_W], gk_ref[...]),
              pa_ref[0:A_KV_W, 0:A_KV_W], cos_a[:, 0:A_KV_W], sin_a[:, 0:A_KV_W])
    v = u[:, OFF_A + 384:OFF_A + 512]
    ex = ex_ref[...]
    atta_ref[:, 0:256] = (q * q_scale).astype(BF16)
    atta_ref[:, 256:512] = _dot(k.astype(BF16), ex).astype(BF16)
    atta_ref[:, 512:768] = _dot(v.astype(BF16), ex).astype(BF16)
    kva_ref[:, 0:A_KV_W] = k
    kva_ref[:, A_KV_W:2 * A_KV_W] = v

    ssd_ref[...] = u[:, OFF_B:OFF_B + SSD_W]

    cq = _rope(u[:, OFF_C:OFF_C + 256], pc_ref[...], cos_c, sin_c)
    ck = _rope(u[:, OFF_C + 256:OFF_C + 512], pc_ref[...], cos_c, sin_c)
    cv = u[:, OFF_C + 512:OFF_C + 768]
    attc_ref[:, 0:256] = cq.astype(BF16)
    attc_ref[:, 256:512] = ck.astype(BF16)
    attc_ref[:, 512:768] = cv.astype(BF16)
    kvc_ref[:, 0:256] = ck
    kvc_ref[:, 256:512] = cv

    attd_ref[:, 0:256] = (u[:, OFF_D:OFF_D + 256] * q_scale).astype(BF16)
    attd_ref[:, 256:768] = u[:, OFF_D + 256:OFF_D + 768].astype(BF16)
    kvd_ref[...] = u[:, OFF_D + 256:OFF_D + 768]


def _in_proj(x, mod_tab, g, w1, rope_tab, bd, pa, pc, ex, gq, gk, n_prompt_tiles, tiles_per_sample):
    nt = x.shape[0]
    mrow = functools.partial(_mod_row, n_prompt_tiles=n_prompt_tiles, tiles_per_sample=tiles_per_sample)
    rrow = functools.partial(_rope_row, n_prompt_tiles=n_prompt_tiles, tiles_per_sample=tiles_per_sample)
    const = lambda *shape: pl.BlockSpec(shape, lambda i: (0,) * len(shape))
    rows = lambda w: pl.BlockSpec((TOKEN_TILE, w), lambda i: (i, 0))
    widths = (ATT_W, ATT_W, ATT_W, 2 * A_KV_W, 2 * BRANCH_W, 2 * BRANCH_W, SSD_W)
    dtypes = (BF16, BF16, BF16, F32, F32, F32, F32)
    return pl.pallas_call(
        _inproj_kernel,
        grid=(nt // TOKEN_TILE,),
        in_specs=[rows(D_MODEL),
                  pl.BlockSpec((None, 8, D_MODEL), lambda i: (mrow(i), 0, 0)),
                  const(1, D_MODEL), const(D_MODEL, W1_COLS),
                  pl.BlockSpec((TOKEN_TILE, 4 * BRANCH_W), lambda i: (rrow(i), 0)),
                  const(BRANCH_W, BRANCH_W), const(BRANCH_W, BRANCH_W), const(BRANCH_W, BRANCH_W),
                  const(A_KV_W, BRANCH_W), const(1, BRANCH_W), const(1, A_KV_W)],
        out_specs=[rows(w) for w in widths],
        out_shape=[jax.ShapeDtypeStruct((nt, w), dt) for w, dt in zip(widths, dtypes)],
        compiler_params=_cparams("parallel"),
        name="in_proj",
    )(x, mod_tab, g, w1, rope_tab, bd, pa, pc, ex, gq, gk)


def _rope_perm(dim):
    p = np.zeros((BRANCH_W, BRANCH_W), np.float32)
    s = dim // 4
    for j in range(BRANCH_W):
        quarter = (j % dim) // s
        if quarter % 2 == 0:
            p[j + s, j] = -1.0
        else:
            p[j - s, j] = 1.0
    return jnp.asarray(p, BF16)


def _rope_table(t, n_identity):
    pos = jnp.arange(t)
    rows = (pos // GRID_W).astype(F32)
    cols = (pos % GRID_W).astype(F32)
    parts = []
    for dim in (HEAD_W, C_QK_DIM):
        axis_dim = dim // 2
        inv = ROPE_THETA ** (-jnp.arange(0, axis_dim, 2, dtype=F32) / axis_dim)
        ang_r = rows[:, None] * inv[None, :]
        ang_c = cols[:, None] * inv[None, :]
        ang = jnp.concatenate([ang_r, ang_r, ang_c, ang_c], axis=-1)
        reps = BRANCH_W // dim
        parts += [jnp.tile(jnp.cos(ang), (1, reps)), jnp.tile(jnp.sin(ang), (1, reps))]
    tab = jnp.concatenate(parts, axis=1)
    ident = jnp.concatenate([jnp.ones((n_identity, BRANCH_W), F32), jnp.zeros((n_identity, BRANCH_W), F32)] * 2, axis=1)
    return jnp.concatenate([ident, tab], axis=0)


def _mha_kernel(lam_ref, q_ref, k_ref, v_ref, *rest, n_maps, scale, has_ctx, subln_scale):
    rest = list(rest)
    ck_ref, cv_ref = (rest.pop(0), rest.pop(0)) if has_ctx else (None, None)
    g_ref, bd_ref = (rest.pop(0), rest.pop(0)) if subln_scale is not None else (None, None)
    o_ref = rest.pop(0)
    q = q_ref[...]
    k = k_ref[...]
    v = v_ref[...]
    tq = q.shape[0]
    lane = lax.broadcasted_iota(jnp.int32, (1, BRANCH_W), 1)
    sub_w = HEAD_W // n_maps
    acc = jnp.zeros((tq, BRANCH_W), F32)
    for h in range(N_HEADS):
        oh = None
        for j in range(n_maps):
            qm = jnp.where((lane // sub_w) == (h * n_maps + j), q, jnp.zeros_like(q))
            s = _dot_nt(qm, k)
            if scale != 1.0:
                s = s * scale
            m = jnp.max(s, axis=-1, keepdims=True)
            if has_ctx:
                sc = _dot_nt(qm, ck_ref[...])
                if scale != 1.0:
                    sc = sc * scale
                m = jnp.maximum(m, jnp.max(sc, axis=-1, keepdims=True))
                pc = jnp.exp(sc - m)
            p = jnp.exp(s - m)
            l = jnp.sum(p, axis=-1, keepdims=True)
            pv = _dot(p.astype(BF16), v)
            if has_ctx:
                l = l + jnp.sum(pc, axis=-1, keepdims=True)
                pv = pv + _dot(pc.astype(BF16), cv_ref[...])
            pv = pv * (1.0 / l)
            oh = pv if j == 0 else oh - lam_ref[0] * pv
        acc = jnp.where((lane // HEAD_W) == h, oh, acc)
    if subln_scale is not None:
        acc = _head_rmsnorm(acc, bd_ref[...], g_ref[...]) * subln_scale
    o_ref[...] = acc.astype(o_ref.dtype)


def _mha(att, lam, *, n_seq, seq, row0, n_maps, scale, ctx=None, subln=None, tq=256):
    qb0 = row0 // tq
    kb0 = row0 // seq
    nq = seq // tq
    assert row0 % seq == 0 and seq % tq == 0
    in_specs = [pl.BlockSpec(memory_space=pltpu.SMEM),
                pl.BlockSpec((tq, BRANCH_W), lambda i, j: (qb0 + i * nq + j, 0)),
                pl.BlockSpec((seq, BRANCH_W), lambda i, j: (kb0 + i, 1)),
                pl.BlockSpec((seq, BRANCH_W), lambda i, j: (kb0 + i, 2))]
    args = [lam, att, att, att]
    kk = seq
    if ctx is not None:
        kc = ctx[0].shape[1]
        kk += kc
        in_specs += [pl.BlockSpec((None, kc, BRANCH_W), lambda i, j: (i, 0, 0))] * 2
        args += list(ctx)
    subln_scale = None
    if subln is not None:
        in_specs += [pl.BlockSpec((1, BRANCH_W), lambda i, j: (0, 0)),
                     pl.BlockSpec((BRANCH_W, BRANCH_W), lambda i, j: (0, 0))]
        args += [subln[0], subln[1]]
        subln_scale = subln[2]
    return pl.pallas_call(
        functools.partial(_mha_kernel, n_maps=n_maps, scale=scale, has_ctx=ctx is not None, subln_scale=subln_scale),
        grid=(n_seq, nq),
        in_specs=in_specs,
        out_specs=pl.BlockSpec((tq, BRANCH_W), lambda i, j: (i * nq + j, 0)),
        out_shape=jax.ShapeDtypeStruct((n_seq * seq, BRANCH_W), BF16),
        compiler_params=_cparams("parallel", "parallel"),
        name=f"mha_maps{n_maps}_k{kk}",
    )(*args)


def _na_row0(r, n_rows):
    return jnp.clip(r - NA_ROWS // 2, 0, n_rows - NA_ROWS)


def _na_kernel(q_ref, k_ref, v_ref, ck_ref, cv_ref, bias_ref, o_ref, *, n_rows):
    r = pl.program_id(1)
    start = pl.multiple_of(_na_row0(r, n_rows) * GRID_W, GRID_W)
    win = NA_ROWS * GRID_W
    q = q_ref[...]
    kw = k_ref[pl.ds(start, win), :]
    vw = v_ref[pl.ds(start, win), :]
    lane = lax.broadcasted_iota(jnp.int32, (1, BRANCH_W), 1)
    qs = jnp.concatenate(
        [jnp.where((lane // HEAD_W) == h, q, jnp.zeros_like(q)) for h in range(N_HEADS)], axis=0)
    s_loc = _dot_nt(qs, kw) + bias_ref[...]
    s_ctx = _dot_nt(qs, ck_ref[...])
    m = jnp.maximum(jnp.max(s_loc, axis=-1, keepdims=True), jnp.max(s_ctx, axis=-1, keepdims=True))
    p_loc = jnp.exp(s_loc - m)
    p_ctx = jnp.exp(s_ctx - m)
    l = jnp.sum(p_loc, axis=-1, keepdims=True) + jnp.sum(p_ctx, axis=-1, keepdims=True)
    o = (_dot(p_loc.astype(BF16), vw) + _dot(p_ctx.astype(BF16), cv_ref[...])) * (1.0 / l)
    acc = jnp.zeros((GRID_W, BRANCH_W), F32)
    for h in range(N_HEADS):
        acc = jnp.where((lane // HEAD_W) == h, o[h * GRID_W:(h + 1) * GRID_W], acc)
    o_ref[...] = acc.astype(o_ref.dtype)


def _na_attention(att, ck, cv, bias_tab, *, n_seq, seq, row0):
    n_rows = seq // GRID_W
    kk = ck.shape[1]
    win = NA_ROWS * GRID_W
    qb0 = row0 // GRID_W
    kb0 = row0 // seq
    assert row0 % seq == 0
    return pl.pallas_call(
        functools.partial(_na_kernel, n_rows=n_rows),
        grid=(n_seq, n_rows),
        in_specs=[pl.BlockSpec((GRID_W, BRANCH_W), lambda i, r: (qb0 + i * n_rows + r, 0)),
                  pl.BlockSpec((seq, BRANCH_W), lambda i, r: (kb0 + i, 1)),
                  pl.BlockSpec((seq, BRANCH_W), lambda i, r: (kb0 + i, 2)),
                  pl.BlockSpec((None, kk, BRANCH_W), lambda i, r: (i, 0, 0)),
                  pl.BlockSpec((None, kk, BRANCH_W), lambda i, r: (i, 0, 0)),
                  pl.BlockSpec((None, N_HEADS * GRID_W, win), lambda i, r: (r - _na_row0(r, n_rows), 0, 0))],
        out_specs=pl.BlockSpec((GRID_W, BRANCH_W), lambda i, r: (i * n_rows + r, 0)),
        out_shape=jax.ShapeDtypeStruct((n_seq * seq, BRANCH_W), BF16),
        compiler_params=_cparams("parallel", "parallel"),
        name="na_attention",
    )(att, att, att, ck, cv, bias_tab)


def _na_bias_table(rpb, n_rows):
    var = np.arange(NA_ROWS)[:, None, None]
    j = np.arange(NA_ROWS)[None, :, None]
    sel_r = (np.arange(2 * NA_ROWS - 1)[None, None, :] == j - var + (NA_ROWS - 1)).astype(np.float32)
    col = np.arange(GRID_W)[:, None, None]
    kc = np.arange(GRID_W)[None, :, None]
    c0 = np.clip(col - NA_COLS // 2, 0, GRID_W - NA_COLS)
    valid = ((kc >= c0) & (kc < c0 + NA_COLS))[:, :, 0]
    sel_c = (np.arange(2 * NA_COLS - 1)[None, None, :] == kc - col + (NA_COLS - 1)).astype(np.float32)
    tab = jnp.einsum('hrc,vjr,xkc->vhxjk', rpb.astype(F32), sel_r, sel_c, precision=lax.Precision.HIGHEST)
    tab = jnp.where(valid[None, None, :, None, :], tab, NEG)
    return tab.reshape(NA_ROWS, N_HEADS * GRID_W, NA_ROWS * GRID_W)


def _softplus(x):
    return jnp.maximum(x, 0.0) + jnp.log1p(jnp.exp(-jnp.abs(x)))


def _expand_heads(colmat, d, lane256):
    out = jnp.zeros((colmat.shape[0], BRANCH_W), F32)
    for h in range(B_HEADS):
        j = d * B_HEADS + h
        out = jnp.where((lane256 // HEAD_W) == h,
                        jnp.broadcast_to(colmat[:, j:j + 1], (colmat.shape[0], BRANCH_W)), out)
    return out


def _ssd_kernel(xbc_ref, z_ref, dt_ref, cw_ref, cb_ref, pc_ref, dsk_ref, ng_ref, init_ref,
                o_ref, st_ref, xpad_s, xc_s, yf_s, s_s, *, seq):
    n_chunks = seq // B_CHUNK
    L = B_CHUNK
    ri = lax.broadcasted_iota(jnp.int32, (L, L), 0)
    ci = lax.broadcasted_iota(jnp.int32, (L, L), 1)
    low = ci <= ri
    upp = ri <= ci
    low_b = jnp.where(low, 1.0, 0.0).astype(BF16)
    upp_b = jnp.where(upp, 1.0, 0.0).astype(BF16)
    lane128 = lax.broadcasted_iota(jnp.int32, (1, LANES), 1)
    lane256 = lax.broadcasted_iota(jnp.int32, (1, BRANCH_W), 1)
    row256 = lax.broadcasted_iota(jnp.int32, (BRANCH_W, 1), 0)
    blockmask = (row256 // (2 * HEAD_W)) == (lane128 // B_STATE)

    xpad_s[0:CONV_PAD, :] = jnp.zeros((CONV_PAD, B_XBC), F32)
    xpad_s[seq + CONV_PAD:seq + 2 * CONV_PAD, :] = jnp.zeros((CONV_PAD, B_XBC), F32)

    def pad_body(c, carry):
        base = pl.multiple_of(c * L, L)
        xpad_s[pl.ds(pl.multiple_of(base + CONV_PAD, CONV_PAD), L), :] = xbc_ref[pl.ds(base, L), :]
        return carry

    lax.fori_loop(0, n_chunks, pad_body, 0)
    cw = cw_ref[...]
    cb = cb_ref[...]

    def conv_body(c, carry):
        base = pl.multiple_of(c * L, L)
        w = xpad_s[pl.ds(base, L + 2 * CONV_PAD), :]
        acc = jnp.zeros((L, B_XBC), F32) + cb
        for kk in range(B_CONV_W):
            off = CONV_PAD - B_CONV_W // 2 + kk
            acc = acc + w[off:off + L, :] * cw[kk:kk + 1, :]
        xc_s[pl.ds(base, L), :] = acc * _sigmoid(acc)
        return carry

    lax.fori_loop(0, n_chunks, conv_body, 0)

    dt_bias = pc_ref[0:1, :]
    a_neg = pc_ref[1:2, :]
    dsk = dsk_ref[...]
    ng = ng_ref[...]

    def chunk(c, d):
        base = pl.multiple_of(c * L, L)
        xs = xc_s[pl.ds(base, L), 0:BRANCH_W]
        bm = xc_s[pl.ds(base, L), BRANCH_W:BRANCH_W + LANES].astype(BF16)
        cm = xc_s[pl.ds(base, L), BRANCH_W + LANES:B_XBC].astype(BF16)
        dtc = _softplus(dt_ref[pl.ds(base, L), :] + dt_bias)
        da_c = dtc * a_neg
        da_r = da_c.T
        tri_c = low_b if d == 0 else upp_b
        tri_r = upp_b if d == 0 else low_b
        c1, c2, c3 = _split3(da_c)
        cum_c = _dot(tri_c, c1) + _dot(tri_c, c2) + _dot(tri_c, c3)
        r1, r2, r3 = _split3(da_r)
        cum_r = _dot(r1, tri_r) + _dot(r2, tri_r) + _dot(r3, tri_r)
        cum_end = cum_c[L - 1:L, :] if d == 0 else cum_c[0:1, :]
        dmask = low if d == 0 else upp

        g0 = _dot_nt(jnp.where(lane128 < B_STATE, cm, jnp.zeros_like(cm)), bm)
        g1 = _dot_nt(jnp.where(lane128 >= B_STATE, cm, jnp.zeros_like(cm)), bm)
        dt_x = _expand_heads(dtc, d, lane256)
        e_a = _expand_heads(jnp.exp(cum_c), d, lane256)
        t_e = _expand_heads(jnp.exp(cum_end - cum_c), d, lane256)
        xdt = xs * dt_x
        xdt_b = xdt.astype(BF16)
        y = jnp.zeros((L, BRANCH_W), F32)
        for h in range(B_HEADS):
            j = d * B_HEADS + h
            col = jnp.broadcast_to(cum_c[:, j:j + 1], (L, L))
            row = jnp.broadcast_to(cum_r[j:j + 1, :], (L, L))
            dec = jnp.exp(jnp.where(dmask, col - row, NEG))
            sc = ((g0 if h < 2 else g1) * dec).astype(BF16)
            y = jnp.where((lane256 // HEAD_W) == h, _dot(sc, xdt_b), y)
        state = s_s[...]
        y = y + _dot_nt(cm, state.astype(BF16)) * e_a
        zmat = _dot_tn((xdt * t_e).astype(BF16), bm)
        e_end = jnp.exp(cum_end)
        cd = jnp.zeros((BRANCH_W, LANES), F32)
        for h in range(B_HEADS):
            j = d * B_HEADS + h
            cd = jnp.where((row256 // HEAD_W) == h, jnp.broadcast_to(e_end[:, j:j + 1], (BRANCH_W, LANES)), cd)
        s_s[...] = state * cd + jnp.where(blockmask, zmat, 0.0)
        return base, xs, y

    s_s[...] = init_ref[0]

    def fwd_body(c, carry):
        base, _, y = chunk(c, 0)
        yf_s[pl.ds(base, L), :] = y
        return carry

    lax.fori_loop(0, n_chunks, fwd_body, 0)
    st_ref[0] = s_s[...]
    s_s[...] = init_ref[1]

    def bwd_body(i, carry):
        c = n_chunks - 1 - i
        base, xs, y = chunk(c, 1)
        z = z_ref[pl.ds(base, L), :]
        yt = (yf_s[pl.ds(base, L), :] + y + dsk * xs) * (z * _sigmoid(z))
        ms = jnp.mean(yt * yt, axis=-1, keepdims=True)
        o_ref[pl.ds(base, L), :] = (yt * lax.rsqrt(ms + EPS) * ng).astype(o_ref.dtype)
        return carry

    lax.fori_loop(0, n_chunks, bwd_body, 0)
    st_ref[1] = s_s[...]


def _ssd_branch(ssd_in, conv_w, conv_b, pc, dsk, ng, init, *, n_seq, seq, row0):
    b0 = row0 // seq
    assert row0 % seq == 0
    full = lambda *shape: pl.BlockSpec(shape, lambda i: (0,) * len(shape))
    return pl.pallas_call(
        functools.partial(_ssd_kernel, seq=seq),
        grid=(n_seq,),
        in_specs=[pl.BlockSpec((seq, B_XBC), lambda i: (b0 + i, 0)),
                  pl.BlockSpec((seq, BRANCH_W), lambda i: (b0 + i, B_XBC // BRANCH_W)),
                  pl.BlockSpec((seq, LANES), lambda i: (b0 + i, (B_XBC + BRANCH_W) // LANES)),
                  full(8, B_XBC), full(1, B_XBC), full(8, LANES), full(1, BRANCH_W), full(1, BRANCH_W),
                  pl.BlockSpec((None, 2, BRANCH_W, LANES), lambda i: (i, 0, 0, 0))],
        out_specs=[pl.BlockSpec((seq, BRANCH_W), lambda i: (i, 0)),
                   pl.BlockSpec((None, 2, BRANCH_W, LANES), lambda i: (i, 0, 0, 0))],
        out_shape=[jax.ShapeDtypeStruct((n_seq * seq, BRANCH_W), BF16),
                   jax.ShapeDtypeStruct((n_seq, 2, BRANCH_W, LANES), F32)],
        scratch_shapes=[pltpu.VMEM((seq + 2 * CONV_PAD, B_XBC), F32), pltpu.VMEM((seq, B_XBC), F32),
                        pltpu.VMEM((seq, BRANCH_W), F32), pltpu.VMEM((BRANCH_W, LANES), F32)],
        compiler_params=_cparams("parallel"),
        name=f"ssd_t{seq}",
    )(ssd_in, ssd_in, ssd_in, conv_w, conv_b, pc, dsk, ng, init)


def _merge_kernel(x_ref, mod_ref, g1_ref, g2_ref, *rest, n_prompt_tiles):
    br_p = rest[0:N_BRANCH]
    br_s = rest[N_BRANCH:2 * N_BRANCH]
    wg_ref, wb_ref, wo_ref, rwh_ref, rwl_ref, rb_ref, x1_ref, h2_ref, gate_ref, eidx_ref = rest[2 * N_BRANCH:]
    is_prompt = pl.program_id(0) < n_prompt_tiles
    x = x_ref[...]
    mod = mod_ref[...]
    h = _norm_mod(x, g1_ref[...], mod[0:1], mod[1:2]).astype(BF16)
    merged = None
    for i in range(N_BRANCH):
        gate = _sigmoid(_dot(h, wg_ref[:, i * D_MODEL:(i + 1) * D_MODEL]))
        br = jnp.where(is_prompt, br_p[i][...], br_s[i][...])
        proj = _dot(br, wb_ref[i])
        merged = gate * proj if i == 0 else merged + gate * proj
    y = _dot(merged.astype(BF16), wo_ref[...])
    x1 = x + mod[2:3] * y
    x1_ref[...] = x1
    h2 = _norm_mod(x1, g2_ref[...], mod[3:4], mod[4:5])
    hh, hl = _split2(h2)
    h2_ref[...] = h2
    rwh = rwh_ref[...]
    logits = _dot(hh, rwh) + _dot(hl, rwh) + _dot(hh, rwl_ref[...]) + rb_ref[...]
    lane = lax.broadcasted_iota(jnp.int32, (1, LANES), 1).astype(F32)
    cur = jnp.where(lane < N_EXPERTS, logits, NEG)
    vals, idxs = [], []
    for _ in range(TOP_K):
        m = jnp.max(cur, axis=-1, keepdims=True)
        am = jnp.min(jnp.where(cur == m, lane, float(LANES)), axis=-1, keepdims=True)
        vals.append(m)
        idxs.append(am)
        cur = jnp.where(lane == am, NEG, cur)
    exps = [jnp.exp(v - vals[0]) for v in vals]
    inv = 1.0 / (exps[0] + exps[1] + exps[2] + exps[3])
    gate = jnp.zeros_like(logits)
    eidx = jnp.zeros_like(logits)
    for k in range(TOP_K):
        gate = jnp.where(lane == k, exps[k] * inv, gate)
        eidx = jnp.where(lane == k, idxs[k], eidx)
    gate_ref[...] = gate
    eidx_ref[...] = eidx.astype(jnp.int32)


def _merge(x, mod_tab, g1, g2, br_p, br_s, wg, wb, wo, rwh, rwl, rb, n_prompt_tiles, tiles_per_sample):
    nt = x.shape[0]
    mrow = functools.partial(_mod_row, n_prompt_tiles=n_prompt_tiles, tiles_per_sample=tiles_per_sample)
    const = lambda *shape: pl.BlockSpec(shape, lambda i: (0,) * len(shape))
    rows = lambda w: pl.BlockSpec((TOKEN_TILE, w), lambda i: (i, 0))
    p_rows = pl.BlockSpec((TOKEN_TILE, BRANCH_W), lambda i: (jnp.minimum(i, n_prompt_tiles - 1), 0))
    s_rows = pl.BlockSpec((TOKEN_TILE, BRANCH_W), lambda i: (jnp.maximum(i - n_prompt_tiles, 0), 0))
    return pl.pallas_call(
        functools.partial(_merge_kernel, n_prompt_tiles=n_prompt_tiles),
        grid=(nt // TOKEN_TILE,),
        in_specs=[rows(D_MODEL),
                  pl.BlockSpec((None, 8, D_MODEL), lambda i: (mrow(i), 0, 0)),
                  const(1, D_MODEL), const(1, D_MODEL)]
                 + [p_rows] * N_BRANCH + [s_rows] * N_BRANCH
                 + [const(D_MODEL, N_BRANCH * D_MODEL), const(N_BRANCH, BRANCH_W, D_MODEL),
                    const(D_MODEL, D_MODEL), const(D_MODEL, LANES), const(D_MODEL, LANES), const(1, LANES)],
        out_specs=[rows(D_MODEL), rows(D_MODEL), rows(LANES), rows(LANES)],
        out_shape=[jax.ShapeDtypeStruct((nt, D_MODEL), F32),
                   jax.ShapeDtypeStruct((nt, D_MODEL), F32),
                   jax.ShapeDtypeStruct((nt, LANES), F32),
                   jax.ShapeDtypeStruct((nt, LANES), jnp.int32)],
        compiler_params=_cparams("parallel"),
        name="merge",
    )(x, mod_tab, g1, g2, *br_p, *br_s, wg, wb, wo, rwh, rwl, rb)


MOE_GROUPS = 8
MOE_CHUNK = 2 * D_FF // MOE_GROUPS


def _moe_step(i, nb, tok_ref, tokn_ref, slotp_ref, slot_ref, h_hbm, b1_ref, b2_ref, out_hbm, w1_s, w2_s,
              x_cur, x_nxt, y_cur, y_prev, g_cur, g_nxt, s_cur, s_prev):
    def gather_copy(tref, buf, sem, r):
        return pltpu.make_async_copy(h_hbm.at[pl.ds(tref[0, r], 1), :], buf.at[pl.ds(r, 1), :], sem)

    def gather_all(buf, sem):
        return pltpu.make_async_copy(h_hbm.at[pl.ds(0, MOE_ROWS), :], buf, sem)

    def scatter_copy(sref, buf, sem, r):
        return pltpu.make_async_copy(buf.at[pl.ds(r, 1), :], out_hbm.at[pl.ds(sref[0, r], 1), :], sem)

    def scatter_all(buf, sem):
        return pltpu.make_async_copy(buf, out_hbm.at[pl.ds(0, MOE_ROWS), :], sem)

    @pl.when(i == 0)
    def _():
        y_prev[...] = jnp.zeros_like(y_prev)
        for r in range(MOE_ROWS):
            gather_copy(tok_ref, x_cur, g_cur, r).start()

    @pl.when(i >= 1)
    def _():
        scatter_all(y_cur, s_cur).wait()

    gather_all(x_cur, g_cur).wait()
    x = x_cur[...].astype(BF16)
    per = MOE_ROWS // MOE_GROUPS
    acts = []
    for c in range(MOE_GROUPS // 2):
        halves = []
        for part in range(2):
            grp = 2 * c + part
            for r in range(grp * per, (grp + 1) * per):
                gather_copy(tokn_ref, x_nxt, g_nxt, r).start()
                scatter_copy(slotp_ref, y_prev, s_prev, r).start()
            lo = part * D_FF + c * MOE_CHUNK
            halves.append(_dot(x, w1_s[:, lo:lo + MOE_CHUNK]) + b1_ref[:, lo:lo + MOE_CHUNK])
        glu = jnp.minimum(halves[0], SWIGLU_LIMIT)
        lin = jnp.clip(halves[1], -SWIGLU_LIMIT, SWIGLU_LIMIT)
        acts.append((glu * _sigmoid(SWIGLU_ALPHA * glu) * (lin + 1.0)).astype(BF16))
    act = jnp.concatenate(acts, axis=1)
    y_cur[...] = _dot(act, w2_s[...]) + b2_ref[...]

    @pl.when(i == nb - 1)
    def _():
        for r in range(MOE_ROWS):
            scatter_copy(slot_ref, y_cur, s_cur, r).start()
        gather_all(x_nxt, g_nxt).wait()
        scatter_all(y_prev, s_prev).wait()
        scatter_all(y_cur, s_cur).wait()


def _moe_kernel(be_ref, tok_ref, tokn_ref, slotp_ref, slot_ref, h_hbm, w1_ref, b1_ref, w2_ref, b2_ref, out_hbm,
                w1_s, w2_s, xa, xb, ya, yb, gsem, ssem):
    i = pl.program_id(0)
    nb = pl.num_programs(0)
    prev = be_ref[jnp.maximum(i - 1, 0)]

    @pl.when(jnp.logical_or(i == 0, be_ref[i] != prev))
    def _():
        w1_s[...] = w1_ref[...].astype(BF16)
        w2_s[...] = w2_ref[...].astype(BF16)

    common = (i, nb, tok_ref, tokn_ref, slotp_ref, slot_ref, h_hbm, b1_ref, b2_ref, out_hbm, w1_s, w2_s)

    @pl.when(i % 2 == 0)
    def _():
        _moe_step(*common, xa, xb, ya, yb, gsem.at[0], gsem.at[1], ssem.at[0], ssem.at[1])

    @pl.when(i % 2 == 1)
    def _():
        _moe_step(*common, xb, xa, yb, ya, gsem.at[1], gsem.at[0], ssem.at[1], ssem.at[0])


def _moe_experts(block_expert, row_token, row_slot, h2, w1, b1, w2, b2, layer):
    n_blocks = row_token.shape[0]
    n_rows = n_blocks * MOE_ROWS
    assert n_blocks >= 2
    spare = (n_rows + jnp.arange(MOE_ROWS, dtype=jnp.int32)).reshape(1, 1, MOE_ROWS)
    slot_prev = jnp.concatenate([spare, row_slot[:-1]], axis=0)
    smem_rows = lambda imap: pl.BlockSpec((None, 1, MOE_ROWS), imap, memory_space=pltpu.SMEM)
    grid_spec = pltpu.PrefetchScalarGridSpec(
        num_scalar_prefetch=1,
        grid=(n_blocks,),
        in_specs=[smem_rows(lambda i, be: (i, 0, 0)),
                  smem_rows(lambda i, be: (jnp.minimum(i + 1, n_blocks - 1), 0, 0)),
                  smem_rows(lambda i, be: (i, 0, 0)),
                  smem_rows(lambda i, be: (i, 0, 0)),
                  pl.BlockSpec(memory_space=pl.ANY),
                  pl.BlockSpec((None, None, D_MODEL, 2 * D_FF), lambda i, be: (layer, be[i], 0, 0)),
                  pl.BlockSpec((None, None, 1, 2 * D_FF), lambda i, be: (layer, be[i], 0, 0)),
                  pl.BlockSpec((None, None, D_FF, D_MODEL), lambda i, be: (layer, be[i], 0, 0)),
                  pl.BlockSpec((None, None, 1, D_MODEL), lambda i, be: (layer, be[i], 0, 0))],
        out_specs=pl.BlockSpec(memory_space=pl.ANY),
        scratch_shapes=[pltpu.VMEM((D_MODEL, 2 * D_FF), BF16), pltpu.VMEM((D_FF, D_MODEL), BF16)]
                       + [pltpu.VMEM((MOE_ROWS, D_MODEL), F32)] * 4
                       + [pltpu.SemaphoreType.DMA((2,)), pltpu.SemaphoreType.DMA((2,))],
    )
    return pl.pallas_call(
        _moe_kernel,
        grid_spec=grid_spec,
        out_shape=jax.ShapeDtypeStruct((n_rows + MOE_ROWS, D_MODEL), F32),
        compiler_params=pltpu.CompilerParams(dimension_semantics=("arbitrary",), vmem_limit_bytes=VMEM_LIMIT,
                                             has_side_effects=True),
        name="moe_experts",
    )(block_expert, row_token, row_token, slot_prev, row_slot, h2, w1, b1, w2, b2)


def _moe_dispatch(eidx):
    n = eidx.shape[0]
    n_assign = n * TOP_K
    n_blocks = -(-n_assign // MOE_ROWS) + N_EXPERTS
    n_rows = n_blocks * MOE_ROWS
    expert = eidx.reshape(-1)
    onehot = (expert[:, None] == jnp.arange(N_EXPERTS, dtype=jnp.int32)[None, :]).astype(jnp.int32)
    incl = jnp.cumsum(onehot, axis=0)
    counts = incl[-1]
    padded = (counts + MOE_ROWS - 1) // MOE_ROWS * MOE_ROWS
    pad_ends = jnp.cumsum(padded)
    pad_starts = pad_ends - padded
    dest = jnp.sum((incl - onehot + pad_starts[None, :]) * onehot, axis=1).astype(jnp.int32)
    assign = jnp.full((n_rows,), -1, jnp.int32).at[dest].set(jnp.arange(n_assign, dtype=jnp.int32))
    is_pad = assign < 0
    filler = n_assign + jnp.cumsum(is_pad.astype(jnp.int32)) - 1
    row_slot = jnp.where(is_pad, filler, (assign % TOP_K) * n + assign // TOP_K)
    row_token = jnp.where(is_pad, 0, assign // TOP_K)
    block_start = jnp.arange(n_blocks, dtype=jnp.int32) * MOE_ROWS
    block_expert = jnp.minimum(jnp.sum((pad_ends[None, :] <= block_start[:, None]).astype(jnp.int32), axis=1),
                               N_EXPERTS - 1).astype(jnp.int32)
    return (block_expert, row_token.reshape(n_blocks, 1, MOE_ROWS), row_slot.reshape(n_blocks, 1, MOE_ROWS))


def _combine_kernel(x1_ref, mod_ref, gate_ref, y0_ref, y1_ref, y2_ref, y3_ref, fg_ref, o_ref, *, final):
    gate = gate_ref[...]
    y = None
    for k, yk_ref in enumerate((y0_ref, y1_ref, y2_ref, y3_ref)):
        yk = gate[:, k:k + 1] * yk_ref[...]
        y = yk if k == 0 else y + yk
    x2 = x1_ref[...] + mod_ref[5:6, :] * y
    if final:
        ms = jnp.mean(x2 * x2, axis=-1, keepdims=True)
        x2 = x2 * lax.rsqrt(ms + EPS) * fg_ref[...]
    o_ref[...] = x2


def _combine(x1, mod_tab, gate, y_slots, final_g, n_prompt_tiles, tiles_per_sample, *, final):
    nt = x1.shape[0]
    tiles = nt // TOKEN_TILE
    mrow = functools.partial(_mod_row, n_prompt_tiles=n_prompt_tiles, tiles_per_sample=tiles_per_sample)
    slot_rows = lambda k: pl.BlockSpec((TOKEN_TILE, D_MODEL), lambda i: (k * tiles + i, 0))
    return pl.pallas_call(
        functools.partial(_combine_kernel, final=final),
        grid=(tiles,),
        in_specs=[pl.BlockSpec((TOKEN_TILE, D_MODEL), lambda i: (i, 0)),
                  pl.BlockSpec((None, 8, D_MODEL), lambda i: (mrow(i), 0, 0)),
                  pl.BlockSpec((TOKEN_TILE, LANES), lambda i: (i, 0))]
                 + [slot_rows(k) for k in range(TOP_K)]
                 + [pl.BlockSpec((1, D_MODEL), lambda i: (0, 0))],
        out_specs=pl.BlockSpec((TOKEN_TILE, D_MODEL), lambda i: (i, 0)),
        out_shape=jax.ShapeDtypeStruct((nt, D_MODEL), F32),
        compiler_params=_cparams("parallel"),
        name="combine_final" if final else "combine",
    )(x1, mod_tab, gate, y_slots, y_slots, y_slots, y_slots, final_g)


def _state_to_blocks(init):
    b = init.shape[0]
    st = init.reshape(b, 2, B_HEADS * HEAD_W, B_STATE)
    row = jnp.arange(B_HEADS * HEAD_W)[:, None]
    left = jnp.where(row < 2 * HEAD_W, st, 0.0)
    right = jnp.where(row >= 2 * HEAD_W, st, 0.0)
    return jnp.concatenate([left, right], axis=-1)


def _blocks_to_state(s):
    b = s.shape[0]
    row = jnp.arange(B_HEADS * HEAD_W)[:, None]
    st = jnp.where(row < 2 * HEAD_W, s[..., :B_STATE], s[..., B_STATE:])
    return st.reshape(b, 2, B_HEADS, HEAD_W, B_STATE)


def kernel(x_prompt, x_sample, c, cache_a_k, cache_a_v, cache_c_k, cache_c_v, cache_d_k, cache_d_v, state_ssd, c_ctx, norm1_g, norm2_g, w_mod, b_mod, w_in, a_q_g, a_k_g, b_conv_w, b_conv_b, b_dt_bias, b_a_log, b_d, b_norm_g, c_lam, c_subln_g, d_rpb, w_branch, w_out, router_w, router_b, moe_w1, moe_b1, moe_w2, moe_b2, final_g):
    bp, tp, _ = x_prompt.shape
    bs, ts, _ = x_sample.shape
    depth = w_in.shape[0]
    n_p = bp * tp
    n_s = bs * ts
    assert tp == TOKEN_TILE and ts % TOKEN_TILE == 0 and n_p % ts == 0
    n_prompt_tiles = n_p // TOKEN_TILE
    tiles_per_sample = ts // TOKEN_TILE
    past = cache_a_k.shape[2]

    x = jnp.concatenate([x_prompt.reshape(n_p, D_MODEL), x_sample.reshape(n_s, D_MODEL)], axis=0)
    cvec = jnp.zeros((16, D_MODEL), F32).at[0].set(c_ctx).at[1:1 + bs].set(c)
    rope_tab = _rope_table(ts, TOKEN_TILE)
    perm_a = _rope_perm(HEAD_W)
    perm_c = _rope_perm(C_QK_DIM)
    head_of = np.arange(BRANCH_W) // HEAD_W
    bd = jnp.asarray((head_of[:, None] == head_of[None, :]) / HEAD_W, BF16)
    kv_of = np.arange(A_KV_W) // HEAD_W
    ex = jnp.asarray((kv_of[:, None] == (head_of // (N_HEADS // A_KV_HEADS))[None, :])
                     & ((np.arange(A_KV_W) % HEAD_W)[:, None] == (np.arange(BRANCH_W) % HEAD_W)[None, :]), BF16)
    scale_c = C_QK_DIM ** -0.5
    one = jnp.ones((1,), F32)
    rep = N_HEADS // A_KV_HEADS
    zero_state = jnp.zeros((bp, 2, BRANCH_W, LANES), F32)

    new_cache = [[] for _ in range(7)]
    for l in range(depth):
        lam_init = 0.8 - 0.6 * math.exp(-0.3 * l)
        lq = c_lam[l]
        lam = (jnp.exp(jnp.sum(lq[0] * lq[1])) - jnp.exp(jnp.sum(lq[2] * lq[3])) + lam_init).reshape(1)

        wl = w_in[l]
        w1 = jnp.concatenate([wl[:, 0:512], wl[:, 768:1280], wl[:, 512:768], wl[:, 1280:1288],
                              jnp.zeros((D_MODEL, LANES - 8), F32), wl[:, 1288:2824]], axis=1).astype(BF16)
        wg = wl[:, 2824:].astype(BF16)
        wb = w_branch[l].astype(BF16)
        wo = w_out[l].astype(BF16)
        rw = jnp.pad(router_w[l], ((0, 0), (0, LANES - N_EXPERTS)))
        rwh = rw.astype(BF16)
        rwl = (rw - rwh.astype(F32)).astype(BF16)
        rb = jnp.pad(router_b[l], (0, LANES - N_EXPERTS)).reshape(1, LANES)

        mod = _modulation(cvec, w_mod[l], b_mod[l].reshape(1, -1))
        mod_tab = jnp.pad(mod.reshape(16, 6, D_MODEL), ((0, 0), (0, 2), (0, 0)))

        g1 = norm1_g[l].reshape(1, D_MODEL)
        g2 = norm2_g[l].reshape(1, D_MODEL)
        att_a, att_c, att_d, kv_a, kv_c, kv_d, ssd_in = _in_proj(
            x, mod_tab, g1, w1, rope_tab, bd, perm_a, perm_c, ex,
            jnp.tile(a_q_g[l], N_HEADS).reshape(1, BRANCH_W), jnp.tile(a_k_g[l], A_KV_HEADS).reshape(1, A_KV_W),
            n_prompt_tiles, tiles_per_sample)

        ctx_ak = jnp.repeat(cache_a_k[:, l], rep, axis=2).reshape(bs, past, BRANCH_W).astype(BF16)
        ctx_av = jnp.repeat(cache_a_v[:, l], rep, axis=2).reshape(bs, past, BRANCH_W).astype(BF16)
        o_a_p = _mha(att_a, one, n_seq=bp, seq=tp, row0=0, n_maps=1, scale=1.0)
        o_a_s = _mha(att_a, one, n_seq=bs, seq=ts, row0=n_p, n_maps=1, scale=1.0, ctx=(ctx_ak, ctx_av))

        subln = (jnp.tile(c_subln_g[l], N_HEADS).reshape(1, BRANCH_W), bd, 1.0 - lam_init)
        ctx_ck = cache_c_k[:, l].reshape(bs, past, BRANCH_W).astype(BF16)
        ctx_cv = cache_c_v[:, l].reshape(bs, past, BRANCH_W).astype(BF16)
        o_c_p = _mha(att_c, lam, n_seq=bp, seq=tp, row0=0, n_maps=2, scale=scale_c, subln=subln)
        o_c_s = _mha(att_c, lam, n_seq=bs, seq=ts, row0=n_p, n_maps=2, scale=scale_c, ctx=(ctx_ck, ctx_cv),
                     subln=subln)

        o_d_p = _mha(att_d, one, n_seq=bp, seq=tp, row0=0, n_maps=1, scale=1.0)
        o_d_s = _na_attention(att_d, cache_d_k[:, l].reshape(bs, past, BRANCH_W).astype(BF16),
                              cache_d_v[:, l].reshape(bs, past, BRANCH_W).astype(BF16),
                              _na_bias_table(d_rpb[l], ts // GRID_W), n_seq=bs, seq=ts, row0=n_p)

        dtb = b_dt_bias[l].reshape(8)
        a_neg = -jnp.exp(b_a_log[l].reshape(8))
        ssd_args = (jnp.pad(b_conv_w[l], ((0, 8 - B_CONV_W), (0, 0))), b_conv_b[l].reshape(1, B_XBC),
                    jnp.zeros((8, LANES), F32).at[0, :8].set(dtb).at[1, :8].set(a_neg),
                    jnp.repeat(b_d[l], HEAD_W).reshape(1, BRANCH_W), b_norm_g[l].reshape(1, BRANCH_W))
        o_b_p, st_p = _ssd_branch(ssd_in, *ssd_args, zero_state, n_seq=bp, seq=tp, row0=0)
        o_b_s, _ = _ssd_branch(ssd_in, *ssd_args, _state_to_blocks(state_ssd[:, l]), n_seq=bs, seq=ts, row0=n_p)

        x1, h2, gate, eidx = _merge(x, mod_tab, g1, g2, (o_a_p, o_b_p, o_c_p, o_d_p), (o_a_s, o_b_s, o_c_s, o_d_s),
                                    wg, wb, wo, rwh, rwl, rb, n_prompt_tiles, tiles_per_sample)
        block_expert, row_token, row_slot = _moe_dispatch(eidx[:, :TOP_K])
        y_slots = _moe_experts(block_expert, row_token, row_slot, h2, moe_w1,
                               moe_b1.reshape(depth, N_EXPERTS, 1, -1), moe_w2,
                               moe_b2.reshape(depth, N_EXPERTS, 1, -1), l)
        x = _combine(x1, mod_tab, gate, y_slots, final_g.reshape(1, D_MODEL), n_prompt_tiles, tiles_per_sample,
                     final=(l == depth - 1))

        new_cache[0].append(kv_a[:n_p, :A_KV_W].reshape(bp, tp, A_KV_HEADS, HEAD_W))
        new_cache[1].append(kv_a[:n_p, A_KV_W:].reshape(bp, tp, A_KV_HEADS, HEAD_W))
        new_cache[2].append(kv_c[:n_p, :BRANCH_W].reshape(bp, tp, N_HEADS, 2 * C_QK_DIM))
        new_cache[3].append(kv_c[:n_p, BRANCH_W:].reshape(bp, tp, N_HEADS, HEAD_W))
        new_cache[4].append(kv_d[:n_p, :BRANCH_W].reshape(bp, tp, N_HEADS, HEAD_W))
        new_cache[5].append(kv_d[:n_p, BRANCH_W:].reshape(bp, tp, N_HEADS, HEAD_W))
        new_cache[6].append(_blocks_to_state(st_p))

    y_prompt = x[:n_p].reshape(bp, tp, D_MODEL)
    y_sample = x[n_p:].reshape(bs, ts, D_MODEL)
    return (y_prompt, y_sample) + tuple(jnp.stack(v, axis=1) for v in new_cache)
```

```python
import functools
import math

import jax
import jax.numpy as jnp
import numpy as np
from jax import lax
from jax.experimental import pallas as pl
from jax.experimental.pallas import tpu as pltpu

F32 = jnp.float32
BF16 = jnp.bfloat16

D_MODEL = 1024
N_BRANCH = 4
BRANCH_W = 256
HEAD_W = 64
N_HEADS = 4
GRID_W = 64
ROPE_THETA = 10000.0
EPS = 1e-6
A_KV_HEADS = 2
A_KV_W = A_KV_HEADS * HEAD_W
C_QK_DIM = 32
B_HEADS = 4
B_STATE = 64
B_CHUNK = 128
B_XBC = 512
B_CONV_W = 5
CONV_PAD = 8
NA_ROWS = 8
NA_COLS = 16
N_EXPERTS = 32
TOP_K = 4
D_FF = 1024
SWIGLU_LIMIT = 7.0
SWIGLU_ALPHA = 1.702
LANES = 128
ROW_TILE = D_MODEL // LANES
NEG = -1e30

TOKEN_TILE = 256
MOE_ROWS = 256
ATT_W = 3 * BRANCH_W
SSD_W = B_XBC + BRANCH_W + LANES
OFF_A, OFF_B, OFF_C, OFF_D = 0, 512, 512 + SSD_W, 512 + SSD_W + ATT_W
W1_COLS = OFF_D + ATT_W
VMEM_LIMIT = 56 * 1024 * 1024


def _cparams(*sem):
    return pltpu.CompilerParams(dimension_semantics=sem, vmem_limit_bytes=VMEM_LIMIT)


def _dot(a, b):
    return jnp.dot(a, b, preferred_element_type=F32)


def _dot_nt(a, b):
    return lax.dot_general(a, b, (((1,), (1,)), ((), ())), preferred_element_type=F32)


def _dot_tn(a, b):
    return lax.dot_general(a, b, (((0,), (0,)), ((), ())), preferred_element_type=F32)


def _sigmoid(x):
    return 1.0 / (1.0 + jnp.exp(-x))


def _split2(x):
    hi = x.astype(BF16)
    return hi, (x - hi.astype(F32)).astype(BF16)


def _split3(x):
    h1 = x.astype(BF16)
    r1 = x - h1.astype(F32)
    h2 = r1.astype(BF16)
    h3 = (r1 - h2.astype(F32)).astype(BF16)
    return h1, h2, h3


def _norm_mod(x, g, shift, scale):
    ms = jnp.mean(x * x, axis=-1, keepdims=True)
    return (x * lax.rsqrt(ms + EPS)) * g * (1.0 + scale) + shift


def _head_rmsnorm(x, bd, g):
    hi, lo = _split2(x * x)
    ms = _dot(hi, bd) + _dot(lo, bd)
    return x * lax.rsqrt(ms + EPS) * g


def _rope(x, perm, cos, sin):
    hi, lo = _split2(x)
    return x * cos + (_dot(hi, perm) + _dot(lo, perm)) * sin


def _mod_kernel(c_ref, w_ref, b_ref, o_ref):
    c = c_ref[...]
    s = (c * _sigmoid(c)).astype(BF16)
    o_ref[...] = _dot(s, w_ref[...].astype(BF16)) + b_ref[...]


def _modulation(cvec, w_mod, b_mod):
    n = w_mod.shape[1]
    tn = 1536
    return pl.pallas_call(
        _mod_kernel,
        grid=(n // tn,),
        in_specs=[pl.BlockSpec((16, D_MODEL), lambda j: (0, 0)),
                  pl.BlockSpec((D_MODEL, tn), lambda j: (0, j)),
                  pl.BlockSpec((1, tn), lambda j: (0, j))],
        out_specs=pl.BlockSpec((16, tn), lambda j: (0, j)),
        out_shape=jax.ShapeDtypeStruct((16, n), F32),
        compiler_params=_cparams("parallel"),
        name="modulation",
    )(cvec, w_mod, b_mod)


def _mod_row(i, n_prompt_tiles, tiles_per_sample):
    return jnp.where(i < n_prompt_tiles, 0, 1 + (i - n_prompt_tiles) // tiles_per_sample)


def _rope_row(i, n_prompt_tiles, tiles_per_sample):
    return jnp.where(i < n_prompt_tiles, 0, 1 + (i - n_prompt_tiles) % tiles_per_sample)


def _inproj_kernel(x_ref, mod_ref, g_ref, w_ref, rope_ref, bd_ref, pa_ref, pc_ref, ex_ref, gq_ref, gk_ref,
                   atta_ref, attc_ref, attd_ref, kva_ref, kvc_ref, kvd_ref, ssd_ref):
    mod = mod_ref[...]
    h = _norm_mod(x_ref[...], g_ref[...], mod[0:1], mod[1:2])
    u = _dot(h.astype(BF16), w_ref[...])
    cos_a, sin_a = rope_ref[:, 0:256], rope_ref[:, 256:512]
    cos_c, sin_c = rope_ref[:, 512:768], rope_ref[:, 768:1024]
    q_scale = HEAD_W ** -0.5

    q = _rope(_head_rmsnorm(u[:, OFF_A:OFF_A + 256], bd_ref[...], gq_ref[...]), pa_ref[...], cos_a, sin_a)
    k = _rope(_head_rmsnorm(u[:, OFF_A + 256:OFF_A + 384], bd_ref[0:A_KV_W, 0:A_KV_W], gk_ref[...]),
              pa_ref[0:A_KV_W, 0:A_KV_W], cos_a[:, 0:A_KV_W], sin_a[:, 0:A_KV_W])
    v = u[:, OFF_A + 384:OFF_A + 512]
    ex = ex_ref[...]
    atta_ref[:, 0:256] = (q * q_scale).astype(BF16)
    atta_ref[:, 256:512] = _dot(k.astype(BF16), ex).astype(BF16)
    atta_ref[:, 512:768] = _dot(v.astype(BF16), ex).astype(BF16)
    kva_ref[:, 0:A_KV_W] = k
    kva_ref[:, A_KV_W:2 * A_KV_W] = v

    ssd_ref[...] = u[:, OFF_B:OFF_B + SSD_W]

    cq = _rope(u[:, OFF_C:OFF_C + 256], pc_ref[...], cos_c, sin_c)
    ck = _rope(u[:, OFF_C + 256:OFF_C + 512], pc_ref[...], cos_c, sin_c)
    cv = u[:, OFF_C + 512:OFF_C + 768]
    attc_ref[:, 0:256] = cq.astype(BF16)
    attc_ref[:, 256:512] = ck.astype(BF16)
    attc_ref[:, 512:768] = cv.astype(BF16)
    kvc_ref[:, 0:256] = ck
    kvc_ref[:, 256:512] = cv

    attd_ref[:, 0:256] = (u[:, OFF_D:OFF_D + 256] * q_scale).astype(BF16)
    attd_ref[:, 256:768] = u[:, OFF_D + 256:OFF_D + 768].astype(BF16)
    kvd_ref[...] = u[:, OFF_D + 256:OFF_D + 768]


def _in_proj(x, mod_tab, g, w1, rope_tab, bd, pa, pc, ex, gq, gk, n_prompt_tiles, tiles_per_sample):
    nt = x.shape[0]
    mrow = functools.partial(_mod_row, n_prompt_tiles=n_prompt_tiles, tiles_per_sample=tiles_per_sample)
    rrow = functools.partial(_rope_row, n_prompt_tiles=n_prompt_tiles, tiles_per_sample=tiles_per_sample)
    const = lambda *shape: pl.BlockSpec(shape, lambda i: (0,) * len(shape))
    rows = lambda w: pl.BlockSpec((TOKEN_TILE, w), lambda i: (i, 0))
    widths = (ATT_W, ATT_W, ATT_W, 2 * A_KV_W, 2 * BRANCH_W, 2 * BRANCH_W, SSD_W)
    dtypes = (BF16, BF16, BF16, F32, F32, F32, F32)
    return pl.pallas_call(
        _inproj_kernel,
        grid=(nt // TOKEN_TILE,),
        in_specs=[rows(D_MODEL),
                  pl.BlockSpec((None, 8, D_MODEL), lambda i: (mrow(i), 0, 0)),
                  const(1, D_MODEL), const(D_MODEL, W1_COLS),
                  pl.BlockSpec((TOKEN_TILE, 4 * BRANCH_W), lambda i: (rrow(i), 0)),
                  const(BRANCH_W, BRANCH_W), const(BRANCH_W, BRANCH_W), const(BRANCH_W, BRANCH_W),
                  const(A_KV_W, BRANCH_W), const(1, BRANCH_W), const(1, A_KV_W)],
        out_specs=[rows(w) for w in widths],
        out_shape=[jax.ShapeDtypeStruct((nt, w), dt) for w, dt in zip(widths, dtypes)],
        compiler_params=_cparams("parallel"),
        name="in_proj",
    )(x, mod_tab, g, w1, rope_tab, bd, pa, pc, ex, gq, gk)


def _rope_perm(dim):
    p = np.zeros((BRANCH_W, BRANCH_W), np.float32)
    s = dim // 4
    for j in range(BRANCH_W):
        quarter = (j % dim) // s
        if quarter % 2 == 0:
            p[j + s, j] = -1.0
        else:
            p[j - s, j] = 1.0
    return jnp.asarray(p, BF16)


def _rope_table(t, n_identity):
    pos = jnp.arange(t)
    rows = (pos // GRID_W).astype(F32)
    cols = (pos % GRID_W).astype(F32)
    parts = []
    for dim in (HEAD_W, C_QK_DIM):
        axis_dim = dim // 2
        inv = ROPE_THETA ** (-jnp.arange(0, axis_dim, 2, dtype=F32) / axis_dim)
        ang_r = rows[:, None] * inv[None, :]
        ang_c = cols[:, None] * inv[None, :]
        ang = jnp.concatenate([ang_r, ang_r, ang_c, ang_c], axis=-1)
        reps = BRANCH_W // dim
        parts += [jnp.tile(jnp.cos(ang), (1, reps)), jnp.tile(jnp.sin(ang), (1, reps))]
    tab = jnp.concatenate(parts, axis=1)
    ident = jnp.concatenate([jnp.ones((n_identity, BRANCH_W), F32), jnp.zeros((n_identity, BRANCH_W), F32)] * 2, axis=1)
    return jnp.concatenate([ident, tab], axis=0)


def _mha_kernel(lam_ref, q_ref, k_ref, v_ref, *rest, n_maps, scale, has_ctx, subln_scale):
    rest = list(rest)
    ck_ref, cv_ref = (rest.pop(0), rest.pop(0)) if has_ctx else (None, None)
    g_ref, bd_ref = (rest.pop(0), rest.pop(0)) if subln_scale is not None else (None, None)
    o_ref = rest.pop(0)
    q = q_ref[...]
    k = k_ref[...]
    v = v_ref[...]
    tq = q.shape[0]
    lane = lax.broadcasted_iota(jnp.int32, (1, BRANCH_W), 1)
    sub_w = HEAD_W // n_maps
    acc = jnp.zeros((tq, BRANCH_W), F32)
    for h in range(N_HEADS):
        oh = None
        for j in range(n_maps):
            qm = jnp.where((lane // sub_w) == (h * n_maps + j), q, jnp.zeros_like(q))
            s = _dot_nt(qm, k)
            if scale != 1.0:
                s = s * scale
            m = jnp.max(s, axis=-1, keepdims=True)
            if has_ctx:
                sc = _dot_nt(qm, ck_ref[...])
                if scale != 1.0:
                    sc = sc * scale
                m = jnp.maximum(m, jnp.max(sc, axis=-1, keepdims=True))
                pc = jnp.exp(sc - m)
            p = jnp.exp(s - m)
            l = jnp.sum(p, axis=-1, keepdims=True)
            pv = _dot(p.astype(BF16), v)
            if has_ctx:
                l = l + jnp.sum(pc, axis=-1, keepdims=True)
                pv = pv + _dot(pc.astype(BF16), cv_ref[...])
            pv = pv * (1.0 / l)
            oh = pv if j == 0 else oh - lam_ref[0] * pv
        acc = jnp.where((lane // HEAD_W) == h, oh, acc)
    if subln_scale is not None:
        acc = _head_rmsnorm(acc, bd_ref[...], g_ref[...]) * subln_scale
    o_ref[...] = acc.astype(o_ref.dtype)


def _mha(att, lam, *, n_seq, seq, row0, n_maps, scale, ctx=None, subln=None, tq=256):
    qb0 = row0 // tq
    kb0 = row0 // seq
    nq = seq // tq
    assert row0 % seq == 0 and seq % tq == 0
    in_specs = [pl.BlockSpec(memory_space=pltpu.SMEM),
                pl.BlockSpec((tq, BRANCH_W), lambda i, j: (qb0 + i * nq + j, 0)),
                pl.BlockSpec((seq, BRANCH_W), lambda i, j: (kb0 + i, 1)),
                pl.BlockSpec((seq, BRANCH_W), lambda i, j: (kb0 + i, 2))]
    args = [lam, att, att, att]
    kk = seq
    if ctx is not None:
        kc = ctx[0].shape[1]
        kk += kc
        in_specs += [pl.BlockSpec((None, kc, BRANCH_W), lambda i, j: (i, 0, 0))] * 2
        args += list(ctx)
    subln_scale = None
    if subln is not None:
        in_specs += [pl.BlockSpec((1, BRANCH_W), lambda i, j: (0, 0)),
                     pl.BlockSpec((BRANCH_W, BRANCH_W), lambda i, j: (0, 0))]
        args += [subln[0], subln[1]]
        subln_scale = subln[2]
    return pl.pallas_call(
        functools.partial(_mha_kernel, n_maps=n_maps, scale=scale, has_ctx=ctx is not None, subln_scale=subln_scale),
        grid=(n_seq, nq),
        in_specs=in_specs,
        out_specs=pl.BlockSpec((tq, BRANCH_W), lambda i, j: (i * nq + j, 0)),
        out_shape=jax.ShapeDtypeStruct((n_seq * seq, BRANCH_W), BF16),
        compiler_params=_cparams("parallel", "parallel"),
        name=f"mha_maps{n_maps}_k{kk}",
    )(*args)


def _na_row0(r, n_rows):
    return jnp.clip(r - NA_ROWS // 2, 0, n_rows - NA_ROWS)


def _na_kernel(q_ref, k_ref, v_ref, ck_ref, cv_ref, bias_ref, o_ref, *, n_rows):
    r = pl.program_id(1)
    start = pl.multiple_of(_na_row0(r, n_rows) * GRID_W, GRID_W)
    win = NA_ROWS * GRID_W
    q = q_ref[...]
    kw = k_ref[pl.ds(start, win), :]
    vw = v_ref[pl.ds(start, win), :]
    lane = lax.broadcasted_iota(jnp.int32, (1, BRANCH_W), 1)
    qs = jnp.concatenate(
        [jnp.where((lane // HEAD_W) == h, q, jnp.zeros_like(q)) for h in range(N_HEADS)], axis=0)
    s_loc = _dot_nt(qs, kw) + bias_ref[...]
    s_ctx = _dot_nt(qs, ck_ref[...])
    m = jnp.maximum(jnp.max(s_loc, axis=-1, keepdims=True), jnp.max(s_ctx, axis=-1, keepdims=True))
    p_loc = jnp.exp(s_loc - m)
    p_ctx = jnp.exp(s_ctx - m)
    l = jnp.sum(p_loc, axis=-1, keepdims=True) + jnp.sum(p_ctx, axis=-1, keepdims=True)
    o = (_dot(p_loc.astype(BF16), vw) + _dot(p_ctx.astype(BF16), cv_ref[...])) * (1.0 / l)
    acc = jnp.zeros((GRID_W, BRANCH_W), F32)
    for h in range(N_HEADS):
        acc = jnp.where((lane // HEAD_W) == h, o[h * GRID_W:(h + 1) * GRID_W], acc)
    o_ref[...] = acc.astype(o_ref.dtype)


def _na_attention(att, ck, cv, bias_tab, *, n_seq, seq, row0):
    n_rows = seq // GRID_W
    kk = ck.shape[1]
    win = NA_ROWS * GRID_W
    qb0 = row0 // GRID_W
    kb0 = row0 // seq
    assert row0 % seq == 0
    return pl.pallas_call(
        functools.partial(_na_kernel, n_rows=n_rows),
        grid=(n_seq, n_rows),
        in_specs=[pl.BlockSpec((GRID_W, BRANCH_W), lambda i, r: (qb0 + i * n_rows + r, 0)),
                  pl.BlockSpec((seq, BRANCH_W), lambda i, r: (kb0 + i, 1)),
                  pl.BlockSpec((seq, BRANCH_W), lambda i, r: (kb0 + i, 2)),
                  pl.BlockSpec((None, kk, BRANCH_W), lambda i, r: (i, 0, 0)),
                  pl.BlockSpec((None, kk, BRANCH_W), lambda i, r: (i, 0, 0)),
                  pl.BlockSpec((None, N_HEADS * GRID_W, win), lambda i, r: (r - _na_row0(r, n_rows), 0, 0))],
        out_specs=pl.BlockSpec((GRID_W, BRANCH_W), lambda i, r: (i * n_rows + r, 0)),
        out_shape=jax.ShapeDtypeStruct((n_seq * seq, BRANCH_W), BF16),
        compiler_params=_cparams("parallel", "parallel"),
        name="na_attention",
    )(att, att, att, ck, cv, bias_tab)


def _na_bias_table(rpb, n_rows):
    var = np.arange(NA_ROWS)[:, None, None]
    j = np.arange(NA_ROWS)[None, :, None]
    sel_r = (np.arange(2 * NA_ROWS - 1)[None, None, :] == j - var + (NA_ROWS - 1)).astype(np.float32)
    col = np.arange(GRID_W)[:, None, None]
    kc = np.arange(GRID_W)[None, :, None]
    c0 = np.clip(col - NA_COLS // 2, 0, GRID_W - NA_COLS)
    valid = ((kc >= c0) & (kc < c0 + NA_COLS))[:, :, 0]
    sel_c = (np.arange(2 * NA_COLS - 1)[None, None, :] == kc - col + (NA_COLS - 1)).astype(np.float32)
    tab = jnp.einsum('hrc,vjr,xkc->vhxjk', rpb.astype(F32), sel_r, sel_c, precision=lax.Precision.HIGHEST)
    tab = jnp.where(valid[None, None, :, None, :], tab, NEG)
    return tab.reshape(NA_ROWS, N_HEADS * GRID_W, NA_ROWS * GRID_W)


def _softplus(x):
    return jnp.maximum(x, 0.0) + jnp.log1p(jnp.exp(-jnp.abs(x)))


def _expand_heads(colmat, d, lane256):
    out = jnp.zeros((colmat.shape[0], BRANCH_W), F32)
    for h in range(B_HEADS):
        j = d * B_HEADS + h
        out = jnp.where((lane256 // HEAD_W) == h,
                        jnp.broadcast_to(colmat[:, j:j + 1], (colmat.shape[0], BRANCH_W)), out)
    return out


def _ssd_kernel(xbc_ref, z_ref, dt_ref, cw_ref, cb_ref, pc_ref, dsk_ref, ng_ref, init_ref,
                o_ref, st_ref, xpad_s, xc_s, yf_s, s_s, *, seq):
    n_chunks = seq // B_CHUNK
    L = B_CHUNK
    ri = lax.broadcasted_iota(jnp.int32, (L, L), 0)
    ci = lax.broadcasted_iota(jnp.int32, (L, L), 1)
    low = ci <= ri
    upp = ri <= ci
    low_b = jnp.where(low, 1.0, 0.0).astype(BF16)
    upp_b = jnp.where(upp, 1.0, 0.0).astype(BF16)
    lane128 = lax.broadcasted_iota(jnp.int32, (1, LANES), 1)
    lane256 = lax.broadcasted_iota(jnp.int32, (1, BRANCH_W), 1)
    row256 = lax.broadcasted_iota(jnp.int32, (BRANCH_W, 1), 0)
    blockmask = (row256 // (2 * HEAD_W)) == (lane128 // B_STATE)

    xpad_s[0:CONV_PAD, :] = jnp.zeros((CONV_PAD, B_XBC), F32)
    xpad_s[seq + CONV_PAD:seq + 2 * CONV_PAD, :] = jnp.zeros((CONV_PAD, B_XBC), F32)

    def pad_body(c, carry):
        base = pl.multiple_of(c * L, L)
        xpad_s[pl.ds(pl.multiple_of(base + CONV_PAD, CONV_PAD), L), :] = xbc_ref[pl.ds(base, L), :]
        return carry

    lax.fori_loop(0, n_chunks, pad_body, 0)
    cw = cw_ref[...]
    cb = cb_ref[...]

    def conv_body(c, carry):
        base = pl.multiple_of(c * L, L)
        w = xpad_s[pl.ds(base, L + 2 * CONV_PAD), :]
        acc = jnp.zeros((L, B_XBC), F32) + cb
        for kk in range(B_CONV_W):
            off = CONV_PAD - B_CONV_W // 2 + kk
            acc = acc + w[off:off + L, :] * cw[kk:kk + 1, :]
        xc_s[pl.ds(base, L), :] = acc * _sigmoid(acc)
        return carry

    lax.fori_loop(0, n_chunks, conv_body, 0)

    dt_bias = pc_ref[0:1, :]
    a_neg = pc_ref[1:2, :]
    dsk = dsk_ref[...]
    ng = ng_ref[...]

    def chunk(c, d):
        base = pl.multiple_of(c * L, L)
        xs = xc_s[pl.ds(base, L), 0:BRANCH_W]
        bm = xc_s[pl.ds(base, L), BRANCH_W:BRANCH_W + LANES].astype(BF16)
        cm = xc_s[pl.ds(base, L), BRANCH_W + LANES:B_XBC].astype(BF16)
        dtc = _softplus(dt_ref[pl.ds(base, L), :] + dt_bias)
        da_c = dtc * a_neg
        da_r = da_c.T
        tri_c = low_b if d == 0 else upp_b
        tri_r = upp_b if d == 0 else low_b
        c1, c2, c3 = _split3(da_c)
        cum_c = _dot(tri_c, c1) + _dot(tri_c, c2) + _dot(tri_c, c3)
        r1, r2, r3 = _split3(da_r)
        cum_r = _dot(r1, tri_r) + _dot(r2, tri_r) + _dot(r3, tri_r)
        cum_end = cum_c[L - 1:L, :] if d == 0 else cum_c[0:1, :]
        dmask = low if d == 0 else upp

        g0 = _dot_nt(jnp.where(lane128 < B_STATE, cm, jnp.zeros_like(cm)), bm)
        g1 = _dot_nt(jnp.where(lane128 >= B_STATE, cm, jnp.zeros_like(cm)), bm)
        dt_x = _expand_heads(dtc, d, lane256)
        e_a = _expand_heads(jnp.exp(cum_c), d, lane256)
        t_e = _expand_heads(jnp.exp(cum_end - cum_c), d, lane256)
        xdt = xs * dt_x
        xdt_b = xdt.astype(BF16)
        y = jnp.zeros((L, BRANCH_W), F32)
        for h in range(B_HEADS):
            j = d * B_HEADS + h
            col = jnp.broadcast_to(cum_c[:, j:j + 1], (L, L))
            row = jnp.broadcast_to(cum_r[j:j + 1, :], (L, L))
            dec = jnp.exp(jnp.where(dmask, col - row, NEG))
            sc = ((g0 if h < 2 else g1) * dec).astype(BF16)
            y = jnp.where((lane256 // HEAD_W) == h, _dot(sc, xdt_b), y)
        state = s_s[...]
        y = y + _dot_nt(cm, state.astype(BF16)) * e_a
        zmat = _dot_tn((xdt * t_e).astype(BF16), bm)
        e_end = jnp.exp(cum_end)
        cd = jnp.zeros((BRANCH_W, LANES), F32)
        for h in range(B_HEADS):
            j = d * B_HEADS + h
            cd = jnp.where((row256 // HEAD_W) == h, jnp.broadcast_to(e_end[:, j:j + 1], (BRANCH_W, LANES)), cd)
        s_s[...] = state * cd + jnp.where(blockmask, zmat, 0.0)
        return base, xs, y

    s_s[...] = init_ref[0]

    def fwd_body(c, carry):
        base, _, y = chunk(c, 0)
        yf_s[pl.ds(base, L), :] = y
        return carry

    lax.fori_loop(0, n_chunks, fwd_body, 0)
    st_ref[0] = s_s[...]
    s_s[...] = init_ref[1]

    def bwd_body(i, carry):
        c = n_chunks - 1 - i
        base, xs, y = chunk(c, 1)
        z = z_ref[pl.ds(base, L), :]
        yt = (yf_s[pl.ds(base, L), :] + y + dsk * xs) * (z * _sigmoid(z))
        ms = jnp.mean(yt * yt, axis=-1, keepdims=True)
        o_ref[pl.ds(base, L), :] = (yt * lax.rsqrt(ms + EPS) * ng).astype(o_ref.dtype)
        return carry

    lax.fori_loop(0, n_chunks, bwd_body, 0)
    st_ref[1] = s_s[...]


def _ssd_branch(ssd_in, conv_w, conv_b, pc, dsk, ng, init, *, n_seq, seq, row0):
    b0 = row0 // seq
    assert row0 % seq == 0
    full = lambda *shape: pl.BlockSpec(shape, lambda i: (0,) * len(shape))
    return pl.pallas_call(
        functools.partial(_ssd_kernel, seq=seq),
        grid=(n_seq,),
        in_specs=[pl.BlockSpec((seq, B_XBC), lambda i: (b0 + i, 0)),
                  pl.BlockSpec((seq, BRANCH_W), lambda i: (b0 + i, B_XBC // BRANCH_W)),
                  pl.BlockSpec((seq, LANES), lambda i: (b0 + i, (B_XBC + BRANCH_W) // LANES)),
                  full(8, B_XBC), full(1, B_XBC), full(8, LANES), full(1, BRANCH_W), full(1, BRANCH_W),
                  pl.BlockSpec((None, 2, BRANCH_W, LANES), lambda i: (i, 0, 0, 0))],
        out_specs=[pl.BlockSpec((seq, BRANCH_W), lambda i: (i, 0)),
                   pl.BlockSpec((None, 2, BRANCH_W, LANES), lambda i: (i, 0, 0, 0))],
        out_shape=[jax.ShapeDtypeStruct((n_seq * seq, BRANCH_W), BF16),
                   jax.ShapeDtypeStruct((n_seq, 2, BRANCH_W, LANES), F32)],
        scratch_shapes=[pltpu.VMEM((seq + 2 * CONV_PAD, B_XBC), F32), pltpu.VMEM((seq, B_XBC), F32),
                        pltpu.VMEM((seq, BRANCH_W), F32), pltpu.VMEM((BRANCH_W, LANES), F32)],
        compiler_params=_cparams("parallel"),
        name=f"ssd_t{seq}",
    )(ssd_in, ssd_in, ssd_in, conv_w, conv_b, pc, dsk, ng, init)


def _merge_kernel(x_ref, mod_ref, g1_ref, g2_ref, *rest, n_prompt_tiles):
    br_p = rest[0:N_BRANCH]
    br_s = rest[N_BRANCH:2 * N_BRANCH]
    wg_ref, wb_ref, wo_ref, rwh_ref, rwl_ref, rb_ref, x1_ref, h2_ref, gate_ref, eidx_ref = rest[2 * N_BRANCH:]
    is_prompt = pl.program_id(0) < n_prompt_tiles
    x = x_ref[...]
    mod = mod_ref[...]
    h = _norm_mod(x, g1_ref[...], mod[0:1], mod[1:2]).astype(BF16)
    merged = None
    for i in range(N_BRANCH):
        gate = _sigmoid(_dot(h, wg_ref[:, i * D_MODEL:(i + 1) * D_MODEL]))
        br = jnp.where(is_prompt, br_p[i][...], br_s[i][...])
        proj = _dot(br, wb_ref[i])
        merged = gate * proj if i == 0 else merged + gate * proj
    y = _dot(merged.astype(BF16), wo_ref[...])
    x1 = x + mod[2:3] * y
    x1_ref[...] = x1
    h2 = _norm_mod(x1, g2_ref[...], mod[3:4], mod[4:5])
    hh, hl = _split2(h2)
    for j in range(ROW_TILE):
        h2_ref[pl.ds(j, TOKEN_TILE, stride=ROW_TILE), :] = h2[:, j * LANES:(j + 1) * LANES]
    rwh = rwh_ref[...]
    logits = _dot(hh, rwh) + _dot(hl, rwh) + _dot(hh, rwl_ref[...]) + rb_ref[...]
    lane = lax.broadcasted_iota(jnp.int32, (1, LANES), 1).astype(F32)
    cur = jnp.where(lane < N_EXPERTS, logits, NEG)
    vals, idxs = [], []
    for _ in range(TOP_K):
        m = jnp.max(cur, axis=-1, keepdims=True)
        am = jnp.min(jnp.where(cur == m, lane, float(LANES)), axis=-1, keepdims=True)
        vals.append(m)
        idxs.append(am)
        cur = jnp.where(lane == am, NEG, cur)
    exps = [jnp.exp(v - vals[0]) for v in vals]
    inv = 1.0 / (exps[0] + exps[1] + exps[2] + exps[3])
    gate = jnp.zeros_like(logits)
    eidx = jnp.zeros_like(logits)
    for k in range(TOP_K):
        gate = jnp.where(lane == k, exps[k] * inv, gate)
        eidx = jnp.where(lane == k, idxs[k], eidx)
    gate_ref[...] = gate
    eidx_ref[...] = eidx.astype(jnp.int32)


def _merge(x, mod_tab, g1, g2, br_p, br_s, wg, wb, wo, rwh, rwl, rb, n_prompt_tiles, tiles_per_sample):
    nt = x.shape[0]
    mrow = functools.partial(_mod_row, n_prompt_tiles=n_prompt_tiles, tiles_per_sample=tiles_per_sample)
    const = lambda *shape: pl.BlockSpec(shape, lambda i: (0,) * len(shape))
    rows = lambda w: pl.BlockSpec((TOKEN_TILE, w), lambda i: (i, 0))
    p_rows = pl.BlockSpec((TOKEN_TILE, BRANCH_W), lambda i: (jnp.minimum(i, n_prompt_tiles - 1), 0))
    s_rows = pl.BlockSpec((TOKEN_TILE, BRANCH_W), lambda i: (jnp.maximum(i - n_prompt_tiles, 0), 0))
    return pl.pallas_call(
        functools.partial(_merge_kernel, n_prompt_tiles=n_prompt_tiles),
        grid=(nt // TOKEN_TILE,),
        in_specs=[rows(D_MODEL),
                  pl.BlockSpec((None, 8, D_MODEL), lambda i: (mrow(i), 0, 0)),
                  const(1, D_MODEL), const(1, D_MODEL)]
                 + [p_rows] * N_BRANCH + [s_rows] * N_BRANCH
                 + [const(D_MODEL, N_BRANCH * D_MODEL), const(N_BRANCH, BRANCH_W, D_MODEL),
                    const(D_MODEL, D_MODEL), const(D_MODEL, LANES), const(D_MODEL, LANES), const(1, LANES)],
        out_specs=[rows(D_MODEL), pl.BlockSpec((TOKEN_TILE * ROW_TILE, LANES), lambda i: (i, 0)),
                   rows(LANES), rows(LANES)],
        out_shape=[jax.ShapeDtypeStruct((nt, D_MODEL), F32),
                   jax.ShapeDtypeStruct((nt * ROW_TILE, LANES), F32),
                   jax.ShapeDtypeStruct((nt, LANES), F32),
                   jax.ShapeDtypeStruct((nt, LANES), jnp.int32)],
        compiler_params=_cparams("parallel"),
        name="merge",
    )(x, mod_tab, g1, g2, *br_p, *br_s, wg, wb, wo, rwh, rwl, rb)


MOE_GROUPS = 8
MOE_CHUNK = 2 * D_FF // MOE_GROUPS


def _moe_step(i, last, tok_ref, tokn_ref, slotp_ref, slot_ref, h_hbm, b1_ref, b2_ref, out_hbm, w1_s, w2_s,
              x_cur, x_nxt, y_cur, y_prev, g_cur, g_nxt, s_cur, s_prev):
    block_sublanes = MOE_ROWS * ROW_TILE

    def tile(ref, t):
        return ref.at[pl.ds(pl.multiple_of(t * ROW_TILE, ROW_TILE), ROW_TILE), :]

    def gather_copy(tref, buf, sem, r):
        return pltpu.make_async_copy(tile(h_hbm, tref[0, r]), buf.at[pl.ds(r * ROW_TILE, ROW_TILE), :], sem)

    def gather_all(buf, sem):
        return pltpu.make_async_copy(h_hbm.at[pl.ds(0, block_sublanes), :], buf, sem)

    def scatter_copy(sref, buf, sem, r):
        return pltpu.make_async_copy(buf.at[pl.ds(r * ROW_TILE, ROW_TILE), :], tile(out_hbm, sref[0, r]), sem)

    def scatter_all(buf, sem):
        return pltpu.make_async_copy(buf, out_hbm.at[pl.ds(0, block_sublanes), :], sem)

    @pl.when(i == 0)
    def _():
        y_prev[...] = jnp.zeros_like(y_prev)
        for r in range(MOE_ROWS):
            gather_copy(tok_ref, x_cur, g_cur, r).start()

    @pl.when(i >= 1)
    def _():
        scatter_all(y_cur, s_cur).wait()

    gather_all(x_cur, g_cur).wait()
    x = jnp.concatenate([x_cur[pl.ds(j, MOE_ROWS, stride=ROW_TILE), :] for j in range(ROW_TILE)],
                        axis=1).astype(BF16)
    per = MOE_ROWS // MOE_GROUPS
    acts = []
    for c in range(MOE_GROUPS // 2):
        halves = []
        for part in range(2):
            grp = 2 * c + part
            for r in range(grp * per, (grp + 1) * per):
                gather_copy(tokn_ref, x_nxt, g_nxt, r).start()
                scatter_copy(slotp_ref, y_prev, s_prev, r).start()
            lo = part * D_FF + c * MOE_CHUNK
            halves.append(_dot(x, w1_s[:, lo:lo + MOE_CHUNK]) + b1_ref[:, lo:lo + MOE_CHUNK])
        glu = jnp.minimum(halves[0], SWIGLU_LIMIT)
        lin = jnp.clip(halves[1], -SWIGLU_LIMIT, SWIGLU_LIMIT)
        acts.append((glu * _sigmoid(SWIGLU_ALPHA * glu) * (lin + 1.0)).astype(BF16))
    act = jnp.concatenate(acts, axis=1)
    y = _dot(act, w2_s[...]) + b2_ref[...]
    for j in range(ROW_TILE):
        y_cur[pl.ds(j, MOE_ROWS, stride=ROW_TILE), :] = y[:, j * LANES:(j + 1) * LANES]

    @pl.when(i == last)
    def _():
        for r in range(MOE_ROWS):
            scatter_copy(slot_ref, y_cur, s_cur, r).start()
        gather_all(x_nxt, g_nxt).wait()
        scatter_all(y_prev, s_prev).wait()
        scatter_all(y_cur, s_cur).wait()


def _moe_kernel(be_ref, nu_ref, tok_ref, tokn_ref, slotp_ref, slot_ref, h_hbm, w1_ref, b1_ref, w2_ref, b2_ref,
                out_hbm, w1_s, w2_s, xa, xb, ya, yb, gsem, ssem):
    i = pl.program_id(0)
    n_used = nu_ref[0]
    prev = be_ref[jnp.maximum(i - 1, 0)]

    @pl.when(jnp.logical_and(i < n_used, jnp.logical_or(i == 0, be_ref[i] != prev)))
    def _():
        w1_s[...] = w1_ref[...].astype(BF16)
        w2_s[...] = w2_ref[...].astype(BF16)

    common = (i, n_used - 1, tok_ref, tokn_ref, slotp_ref, slot_ref, h_hbm, b1_ref, b2_ref, out_hbm, w1_s, w2_s)

    @pl.when(jnp.logical_and(i < n_used, i % 2 == 0))
    def _():
        _moe_step(*common, xa, xb, ya, yb, gsem.at[0], gsem.at[1], ssem.at[0], ssem.at[1])

    @pl.when(jnp.logical_and(i < n_used, i % 2 == 1))
    def _():
        _moe_step(*common, xb, xa, yb, ya, gsem.at[1], gsem.at[0], ssem.at[1], ssem.at[0])


def _moe_experts(block_expert, n_used, row_token, row_slot, h2, w1, b1, w2, b2, layer):
    n_blocks = row_token.shape[0]
    n_rows = n_blocks * MOE_ROWS
    spare = (n_rows + jnp.arange(MOE_ROWS, dtype=jnp.int32)).reshape(1, 1, MOE_ROWS)
    slot_prev = jnp.concatenate([spare, row_slot[:-1]], axis=0)
    smem_rows = lambda imap: pl.BlockSpec((None, 1, MOE_ROWS), imap, memory_space=pltpu.SMEM)
    expert_block = lambda *shape: pl.BlockSpec((None, None) + shape, lambda i, be, nu: (layer, be[i], 0, 0))
    row_buf = pltpu.VMEM((MOE_ROWS * ROW_TILE, LANES), F32)
    grid_spec = pltpu.PrefetchScalarGridSpec(
        num_scalar_prefetch=2,
        grid=(n_blocks,),
        in_specs=[smem_rows(lambda i, be, nu: (i, 0, 0)),
                  smem_rows(lambda i, be, nu: (jnp.minimum(i + 1, n_blocks - 1), 0, 0)),
                  smem_rows(lambda i, be, nu: (i, 0, 0)),
                  smem_rows(lambda i, be, nu: (i, 0, 0)),
                  pl.BlockSpec(memory_space=pl.ANY),
                  expert_block(D_MODEL, 2 * D_FF), expert_block(1, 2 * D_FF),
                  expert_block(D_FF, D_MODEL), expert_block(1, D_MODEL)],
        out_specs=pl.BlockSpec(memory_space=pl.ANY),
        scratch_shapes=[pltpu.VMEM((D_MODEL, 2 * D_FF), BF16), pltpu.VMEM((D_FF, D_MODEL), BF16),
                        row_buf, row_buf, row_buf, row_buf,
                        pltpu.SemaphoreType.DMA((2,)), pltpu.SemaphoreType.DMA((2,))],
    )
    return pl.pallas_call(
        _moe_kernel,
        grid_spec=grid_spec,
        out_shape=jax.ShapeDtypeStruct(((n_rows + MOE_ROWS) * ROW_TILE, LANES), F32),
        compiler_params=pltpu.CompilerParams(dimension_semantics=("arbitrary",), vmem_limit_bytes=VMEM_LIMIT,
                                             has_side_effects=True),
        name="moe_experts",
    )(block_expert, n_used, row_token, row_token, slot_prev, row_slot, h2, w1, b1, w2, b2)


def _moe_dispatch(eidx):
    n = eidx.shape[0]
    n_assign = n * TOP_K
    n_blocks = -(-n_assign // MOE_ROWS) + N_EXPERTS
    n_rows = n_blocks * MOE_ROWS
    expert = eidx.reshape(-1)
    onehot = (expert[:, None] == jnp.arange(N_EXPERTS, dtype=jnp.int32)[None, :]).astype(jnp.int32)
    incl = jnp.cumsum(onehot, axis=0)
    counts = incl[-1]
    padded = (counts + MOE_ROWS - 1) // MOE_ROWS * MOE_ROWS
    pad_ends = jnp.cumsum(padded)
    pad_starts = pad_ends - padded
    dest = jnp.sum((incl - onehot + pad_starts[None, :]) * onehot, axis=1).astype(jnp.int32)
    assign = jnp.full((n_rows,), -1, jnp.int32).at[dest].set(jnp.arange(n_assign, dtype=jnp.int32))
    is_pad = assign < 0
    filler = n_assign + jnp.cumsum(is_pad.astype(jnp.int32)) - 1
    row_slot = jnp.where(is_pad, filler, (assign % TOP_K) * n + assign // TOP_K)
    row_token = jnp.where(is_pad, 0, assign // TOP_K)
    block_start = jnp.arange(n_blocks, dtype=jnp.int32) * MOE_ROWS
    block_expert = jnp.minimum(jnp.sum((pad_ends[None, :] <= block_start[:, None]).astype(jnp.int32), axis=1),
                               N_EXPERTS - 1).astype(jnp.int32)
    n_used = (pad_ends[-1] // MOE_ROWS).astype(jnp.int32).reshape(1)
    return (block_expert, n_used, row_token.reshape(n_blocks, 1, MOE_ROWS),
            row_slot.reshape(n_blocks, 1, MOE_ROWS))


def _combine_kernel(x1_ref, mod_ref, gate_ref, y0_ref, y1_ref, y2_ref, y3_ref, fg_ref, o_ref, *, final):
    gate = gate_ref[...]
    y = None
    for k, yk_ref in enumerate((y0_ref, y1_ref, y2_ref, y3_ref)):
        rows = jnp.concatenate([yk_ref[pl.ds(j, TOKEN_TILE, stride=ROW_TILE), :] for j in range(ROW_TILE)], axis=1)
        yk = gate[:, k:k + 1] * rows
        y = yk if k == 0 else y + yk
    x2 = x1_ref[...] + mod_ref[5:6, :] * y
    if final:
        ms = jnp.mean(x2 * x2, axis=-1, keepdims=True)
        x2 = x2 * lax.rsqrt(ms + EPS) * fg_ref[...]
    o_ref[...] = x2


def _combine(x1, mod_tab, gate, y_slots, final_g, n_prompt_tiles, tiles_per_sample, *, final):
    nt = x1.shape[0]
    tiles = nt // TOKEN_TILE
    mrow = functools.partial(_mod_row, n_prompt_tiles=n_prompt_tiles, tiles_per_sample=tiles_per_sample)
    slot_rows = lambda k: pl.BlockSpec((TOKEN_TILE * ROW_TILE, LANES), lambda i: (k * tiles + i, 0))
    return pl.pallas_call(
        functools.partial(_combine_kernel, final=final),
        grid=(tiles,),
        in_specs=[pl.BlockSpec((TOKEN_TILE, D_MODEL), lambda i: (i, 0)),
                  pl.BlockSpec((None, 8, D_MODEL), lambda i: (mrow(i), 0, 0)),
                  pl.BlockSpec((TOKEN_TILE, LANES), lambda i: (i, 0))]
                 + [slot_rows(k) for k in range(TOP_K)]
                 + [pl.BlockSpec((1, D_MODEL), lambda i: (0, 0))],
        out_specs=pl.BlockSpec((TOKEN_TILE, D_MODEL), lambda i: (i, 0)),
        out_shape=jax.ShapeDtypeStruct((nt, D_MODEL), F32),
        compiler_params=_cparams("parallel"),
        name="combine_final" if final else "combine",
    )(x1, mod_tab, gate, y_slots, y_slots, y_slots, y_slots, final_g)


def _state_to_blocks(init):
    b = init.shape[0]
    st = init.reshape(b, 2, B_HEADS * HEAD_W, B_STATE)
    row = jnp.arange(B_HEADS * HEAD_W)[:, None]
    left = jnp.where(row < 2 * HEAD_W, st, 0.0)
    right = jnp.where(row >= 2 * HEAD_W, st, 0.0)
    return jnp.concatenate([left, right], axis=-1)


def _blocks_to_state(s):
    b = s.shape[0]
    row = jnp.arange(B_HEADS * HEAD_W)[:, None]
    st = jnp.where(row < 2 * HEAD_W, s[..., :B_STATE], s[..., B_STATE:])
    return st.reshape(b, 2, B_HEADS, HEAD_W, B_STATE)


def kernel(x_prompt, x_sample, c, cache_a_k, cache_a_v, cache_c_k, cache_c_v, cache_d_k, cache_d_v, state_ssd, c_ctx, norm1_g, norm2_g, w_mod, b_mod, w_in, a_q_g, a_k_g, b_conv_w, b_conv_b, b_dt_bias, b_a_log, b_d, b_norm_g, c_lam, c_subln_g, d_rpb, w_branch, w_out, router_w, router_b, moe_w1, moe_b1, moe_w2, moe_b2, final_g):
    bp, tp, _ = x_prompt.shape
    bs, ts, _ = x_sample.shape
    depth = w_in.shape[0]
    n_p = bp * tp
    n_s = bs * ts
    assert tp == TOKEN_TILE and ts % TOKEN_TILE == 0 and n_p % ts == 0
    n_prompt_tiles = n_p // TOKEN_TILE
    tiles_per_sample = ts // TOKEN_TILE
    past = cache_a_k.shape[2]

    x = jnp.concatenate([x_prompt.reshape(n_p, D_MODEL), x_sample.reshape(n_s, D_MODEL)], axis=0)
    cvec = jnp.zeros((16, D_MODEL), F32).at[0].set(c_ctx).at[1:1 + bs].set(c)
    rope_tab = _rope_table(ts, TOKEN_TILE)
    perm_a = _rope_perm(HEAD_W)
    perm_c = _rope_perm(C_QK_DIM)
    head_of = np.arange(BRANCH_W) // HEAD_W
    bd = jnp.asarray((head_of[:, None] == head_of[None, :]) / HEAD_W, BF16)
    kv_of = np.arange(A_KV_W) // HEAD_W
    ex = jnp.asarray((kv_of[:, None] == (head_of // (N_HEADS // A_KV_HEADS))[None, :])
                     & ((np.arange(A_KV_W) % HEAD_W)[:, None] == (np.arange(BRANCH_W) % HEAD_W)[None, :]), BF16)
    scale_c = C_QK_DIM ** -0.5
    one = jnp.ones((1,), F32)
    rep = N_HEADS // A_KV_HEADS
    zero_state = jnp.zeros((bp, 2, BRANCH_W, LANES), F32)

    new_cache = [[] for _ in range(7)]
    for l in range(depth):
        lam_init = 0.8 - 0.6 * math.exp(-0.3 * l)
        lq = c_lam[l]
        lam = (jnp.exp(jnp.sum(lq[0] * lq[1])) - jnp.exp(jnp.sum(lq[2] * lq[3])) + lam_init).reshape(1)

        wl = w_in[l]
        w1 = jnp.concatenate([wl[:, 0:512], wl[:, 768:1280], wl[:, 512:768], wl[:, 1280:1288],
                              jnp.zeros((D_MODEL, LANES - 8), F32), wl[:, 1288:2824]], axis=1).astype(BF16)
        wg = wl[:, 2824:].astype(BF16)
        wb = w_branch[l].astype(BF16)
        wo = w_out[l].astype(BF16)
        rw = jnp.pad(router_w[l], ((0, 0), (0, LANES - N_EXPERTS)))
        rwh = rw.astype(BF16)
        rwl = (rw - rwh.astype(F32)).astype(BF16)
        rb = jnp.pad(router_b[l], (0, LANES - N_EXPERTS)).reshape(1, LANES)

        mod = _modulation(cvec, w_mod[l], b_mod[l].reshape(1, -1))
        mod_tab = jnp.pad(mod.reshape(16, 6, D_MODEL), ((0, 0), (0, 2), (0, 0)))

        g1 = norm1_g[l].reshape(1, D_MODEL)
        g2 = norm2_g[l].reshape(1, D_MODEL)
        att_a, att_c, att_d, kv_a, kv_c, kv_d, ssd_in = _in_proj(
            x, mod_tab, g1, w1, rope_tab, bd, perm_a, perm_c, ex,
            jnp.tile(a_q_g[l], N_HEADS).reshape(1, BRANCH_W), jnp.tile(a_k_g[l], A_KV_HEADS).reshape(1, A_KV_W),
            n_prompt_tiles, tiles_per_sample)

        ctx_ak = jnp.repeat(cache_a_k[:, l], rep, axis=2).reshape(bs, past, BRANCH_W).astype(BF16)
        ctx_av = jnp.repeat(cache_a_v[:, l], rep, axis=2).reshape(bs, past, BRANCH_W).astype(BF16)
        o_a_p = _mha(att_a, one, n_seq=bp, seq=tp, row0=0, n_maps=1, scale=1.0)
        o_a_s = _mha(att_a, one, n_seq=bs, seq=ts, row0=n_p, n_maps=1, scale=1.0, ctx=(ctx_ak, ctx_av))

        subln = (jnp.tile(c_subln_g[l], N_HEADS).reshape(1, BRANCH_W), bd, 1.0 - lam_init)
        ctx_ck = cache_c_k[:, l].reshape(bs, past, BRANCH_W).astype(BF16)
        ctx_cv = cache_c_v[:, l].reshape(bs, past, BRANCH_W).astype(BF16)
        o_c_p = _mha(att_c, lam, n_seq=bp, seq=tp, row0=0, n_maps=2, scale=scale_c, subln=subln)
        o_c_s = _mha(att_c, lam, n_seq=bs, seq=ts, row0=n_p, n_maps=2, scale=scale_c, ctx=(ctx_ck, ctx_cv),
                     subln=subln)

        o_d_p = _mha(att_d, one, n_seq=bp, seq=tp, row0=0, n_maps=1, scale=1.0)
        o_d_s = _na_attention(att_d, cache_d_k[:, l].reshape(bs, past, BRANCH_W).astype(BF16),
                              cache_d_v[:, l].reshape(bs, past, BRANCH_W).astype(BF16),
                              _na_bias_table(d_rpb[l], ts // GRID_W), n_seq=bs, seq=ts, row0=n_p)

        dtb = b_dt_bias[l].reshape(8)
        a_neg = -jnp.exp(b_a_log[l].reshape(8))
        ssd_args = (jnp.pad(b_conv_w[l], ((0, 8 - B_CONV_W), (0, 0))), b_conv_b[l].reshape(1, B_XBC),
                    jnp.zeros((8, LANES), F32).at[0, :8].set(dtb).at[1, :8].set(a_neg),
                    jnp.repeat(b_d[l], HEAD_W).reshape(1, BRANCH_W), b_norm_g[l].reshape(1, BRANCH_W))
        o_b_p, st_p = _ssd_branch(ssd_in, *ssd_args, zero_state, n_seq=bp, seq=tp, row0=0)
        o_b_s, _ = _ssd_branch(ssd_in, *ssd_args, _state_to_blocks(state_ssd[:, l]), n_seq=bs, seq=ts, row0=n_p)

        x1, h2, gate, eidx = _merge(x, mod_tab, g1, g2, (o_a_p, o_b_p, o_c_p, o_d_p), (o_a_s, o_b_s, o_c_s, o_d_s),
                                    wg, wb, wo, rwh, rwl, rb, n_prompt_tiles, tiles_per_sample)
        block_expert, n_used, row_token, row_slot = _moe_dispatch(eidx[:, :TOP_K])
        y_slots = _moe_experts(block_expert, n_used, row_token, row_slot, h2, moe_w1,
                               moe_b1.reshape(depth, N_EXPERTS, 1, -1), moe_w2,
                               moe_b2.reshape(depth, N_EXPERTS, 1, -1), l)
        x = _combine(x1, mod_tab, gate, y_slots, final_g.reshape(1, D_MODEL), n_prompt_tiles, tiles_per_sample,
                     final=(l == depth - 1))

        new_cache[0].append(kv_a[:n_p, :A_KV_W].reshape(bp, tp, A_KV_HEADS, HEAD_W))
        new_cache[1].append(kv_a[:n_p, A_KV_W:].reshape(bp, tp, A_KV_HEADS, HEAD_W))
        new_cache[2].append(kv_c[:n_p, :BRANCH_W].reshape(bp, tp, N_HEADS, 2 * C_QK_DIM))
        new_cache[3].append(kv_c[:n_p, BRANCH_W:].reshape(bp, tp, N_HEADS, HEAD_W))
        new_cache[4].append(kv_d[:n_p, :BRANCH_W].reshape(bp, tp, N_HEADS, HEAD_W))
        new_cache[5].append(kv_d[:n_p, BRANCH_W:].reshape(bp, tp, N_HEADS, HEAD_W))
        new_cache[6].append(_blocks_to_state(st_p))

    y_prompt = x[:n_p].reshape(bp, tp, D_MODEL)
    y_sample = x[n_p:].reshape(bs, ts, D_MODEL)
    return (y_prompt, y_sample) + tuple(jnp.stack(v, axis=1) for v in new_cache)
```

```python
import functools
import math

import jax
import jax.numpy as jnp
import numpy as np
from jax import lax
from jax.experimental import pallas as pl
from jax.experimental.pallas import tpu as pltpu

F32 = jnp.float32
BF16 = jnp.bfloat16

D_MODEL = 1024
N_BRANCH = 4
BRANCH_W = 256
HEAD_W = 64
N_HEADS = 4
GRID_W = 64
ROPE_THETA = 10000.0
EPS = 1e-6
A_KV_HEADS = 2
A_KV_W = A_KV_HEADS * HEAD_W
C_QK_DIM = 32
B_HEADS = 4
B_STATE = 64
B_CHUNK = 128
B_XBC = 512
B_CONV_W = 5
CONV_PAD = 8
NA_ROWS = 8
NA_COLS = 16
N_EXPERTS = 32
TOP_K = 4
D_FF = 1024
SWIGLU_LIMIT = 7.0
SWIGLU_ALPHA = 1.702
LANES = 128
ROW_TILE = D_MODEL // LANES
NEG = -1e30

TOKEN_TILE = 256
MOE_ROWS = 256
ATT_W = 3 * BRANCH_W
SSD_W = B_XBC + BRANCH_W + LANES
OFF_A, OFF_B, OFF_C, OFF_D = 0, 512, 512 + SSD_W, 512 + SSD_W + ATT_W
W1_COLS = OFF_D + ATT_W
VMEM_LIMIT = 56 * 1024 * 1024


def _cparams(*sem):
    return pltpu.CompilerParams(dimension_semantics=sem, vmem_limit_bytes=VMEM_LIMIT)


def _dot(a, b):
    return jnp.dot(a, b, preferred_element_type=F32)


def _dot_nt(a, b):
    return lax.dot_general(a, b, (((1,), (1,)), ((), ())), preferred_element_type=F32)


def _dot_tn(a, b):
    return lax.dot_general(a, b, (((0,), (0,)), ((), ())), preferred_element_type=F32)


def _sigmoid(x):
    return 1.0 / (1.0 + jnp.exp(-x))


def _split2(x):
    hi = x.astype(BF16)
    return hi, (x - hi.astype(F32)).astype(BF16)


def _split3(x):
    h1 = x.astype(BF16)
    r1 = x - h1.astype(F32)
    h2 = r1.astype(BF16)
    h3 = (r1 - h2.astype(F32)).astype(BF16)
    return h1, h2, h3


def _norm_mod(x, g, shift, scale):
    ms = jnp.mean(x * x, axis=-1, keepdims=True)
    return (x * lax.rsqrt(ms + EPS)) * g * (1.0 + scale) + shift


def _head_rmsnorm(x, bd, g):
    hi, lo = _split2(x * x)
    ms = _dot(hi, bd) + _dot(lo, bd)
    return x * lax.rsqrt(ms + EPS) * g


def _rope(x, perm, cos, sin):
    hi, lo = _split2(x)
    return x * cos + (_dot(hi, perm) + _dot(lo, perm)) * sin


def _mod_kernel(c_ref, w_ref, b_ref, o_ref):
    c = c_ref[...]
    s = (c * _sigmoid(c)).astype(BF16)
    o_ref[...] = _dot(s, w_ref[...].astype(BF16)) + b_ref[...]


def _modulation(cvec, w_mod, b_mod):
    n = w_mod.shape[1]
    tn = 1536
    return pl.pallas_call(
        _mod_kernel,
        grid=(n // tn,),
        in_specs=[pl.BlockSpec((16, D_MODEL), lambda j: (0, 0)),
                  pl.BlockSpec((D_MODEL, tn), lambda j: (0, j)),
                  pl.BlockSpec((1, tn), lambda j: (0, j))],
        out_specs=pl.BlockSpec((16, tn), lambda j: (0, j)),
        out_shape=jax.ShapeDtypeStruct((16, n), F32),
        compiler_params=_cparams("parallel"),
        name="modulation",
    )(cvec, w_mod, b_mod)


def _mod_row(i, n_prompt_tiles, tiles_per_sample):
    return jnp.where(i < n_prompt_tiles, 0, 1 + (i - n_prompt_tiles) // tiles_per_sample)


def _rope_row(i, n_prompt_tiles, tiles_per_sample):
    return jnp.where(i < n_prompt_tiles, 0, 1 + (i - n_prompt_tiles) % tiles_per_sample)


def _inproj_kernel(x_ref, mod_ref, g_ref, w_ref, rope_ref, bd_ref, pa_ref, pc_ref, ex_ref, gq_ref, gk_ref,
                   atta_ref, attc_ref, attd_ref, kva_ref, kvc_ref, kvd_ref, ssd_ref):
    mod = mod_ref[...]
    h = _norm_mod(x_ref[...], g_ref[...], mod[0:1], mod[1:2])
    u = _dot(h.astype(BF16), w_ref[...])
    cos_a, sin_a = rope_ref[:, 0:256], rope_ref[:, 256:512]
    cos_c, sin_c = rope_ref[:, 512:768], rope_ref[:, 768:1024]
    q_scale = HEAD_W ** -0.5

    q = _rope(_head_rmsnorm(u[:, OFF_A:OFF_A + 256], bd_ref[...], gq_ref[...]), pa_ref[...], cos_a, sin_a)
    k = _rope(_head_rmsnorm(u[:, OFF_A + 256:OFF_A + 384], bd_ref[0:A_KV_W, 0:A_KV_W], gk_ref[...]),
              pa_ref[0:A_KV_W, 0:A_KV_W], cos_a[:, 0:A_KV_W], sin_a[:, 0:A_KV_W])
    v = u[:, OFF_A + 384:OFF_A + 512]
    ex = ex_ref[...]
    atta_ref[:, 0:256] = (q * q_scale).astype(BF16)
    atta_ref[:, 256:512] = _dot(k.astype(BF16), ex).astype(BF16)
    atta_ref[:, 512:768] = _dot(v.astype(BF16), ex).astype(BF16)
    kva_ref[:, 0:A_KV_W] = k
    kva_ref[:, A_KV_W:2 * A_KV_W] = v

    ssd_ref[...] = u[:, OFF_B:OFF_B + SSD_W]

    cq = _rope(u[:, OFF_C:OFF_C + 256], pc_ref[...], cos_c, sin_c)
    ck = _rope(u[:, OFF_C + 256:OFF_C + 512], pc_ref[...], cos_c, sin_c)
    cv = u[:, OFF_C + 512:OFF_C + 768]
    attc_ref[:, 0:256] = cq.astype(BF16)
    attc_ref[:, 256:512] = ck.astype(BF16)
    attc_ref[:, 512:768] = cv.astype(BF16)
    kvc_ref[:, 0:256] = ck
    kvc_ref[:, 256:512] = cv

    attd_ref[:, 0:256] = (u[:, OFF_D:OFF_D + 256] * q_scale).astype(BF16)
    attd_ref[:, 256:768] = u[:, OFF_D + 256:OFF_D + 768].astype(BF16)
    kvd_ref[...] = u[:, OFF_D + 256:OFF_D + 768]


def _in_proj(x, mod_tab, g, w1, rope_tab, bd, pa, pc, ex, gq, gk, n_prompt_tiles, tiles_per_sample):
    nt = x.shape[0]
    mrow = functools.partial(_mod_row, n_prompt_tiles=n_prompt_tiles, tiles_per_sample=tiles_per_sample)
    rrow = functools.partial(_rope_row, n_prompt_tiles=n_prompt_tiles, tiles_per_sample=tiles_per_sample)
    const = lambda *shape: pl.BlockSpec(shape, lambda i: (0,) * len(shape))
    rows = lambda w: pl.BlockSpec((TOKEN_TILE, w), lambda i: (i, 0))
    widths = (ATT_W, ATT_W, ATT_W, 2 * A_KV_W, 2 * BRANCH_W, 2 * BRANCH_W, SSD_W)
    dtypes = (BF16, BF16, BF16, F32, F32, F32, F32)
    return pl.pallas_call(
        _inproj_kernel,
        grid=(nt // TOKEN_TILE,),
        in_specs=[rows(D_MODEL),
                  pl.BlockSpec((None, 8, D_MODEL), lambda i: (mrow(i), 0, 0)),
                  const(1, D_MODEL), const(D_MODEL, W1_COLS),
                  pl.BlockSpec((TOKEN_TILE, 4 * BRANCH_W), lambda i: (rrow(i), 0)),
                  const(BRANCH_W, BRANCH_W), const(BRANCH_W, BRANCH_W), const(BRANCH_W, BRANCH_W),
                  const(A_KV_W, BRANCH_W), const(1, BRANCH_W), const(1, A_KV_W)],
        out_specs=[rows(w) for w in widths],
        out_shape=[jax.ShapeDtypeStruct((nt, w), dt) for w, dt in zip(widths, dtypes)],
        compiler_params=_cparams("parallel"),
        name="in_proj",
    )(x, mod_tab, g, w1, rope_tab, bd, pa, pc, ex, gq, gk)


def _rope_perm(dim):
    p = np.zeros((BRANCH_W, BRANCH_W), np.float32)
    s = dim // 4
    for j in range(BRANCH_W):
        quarter = (j % dim) // s
        if quarter % 2 == 0:
            p[j + s, j] = -1.0
        else:
            p[j - s, j] = 1.0
    return jnp.asarray(p, BF16)


def _rope_table(t, n_identity):
    pos = jnp.arange(t)
    rows = (pos // GRID_W).astype(F32)
    cols = (pos % GRID_W).astype(F32)
    parts = []
    for dim in (HEAD_W, C_QK_DIM):
        axis_dim = dim // 2
        inv = ROPE_THETA ** (-jnp.arange(0, axis_dim, 2, dtype=F32) / axis_dim)
        ang_r = rows[:, None] * inv[None, :]
        ang_c = cols[:, None] * inv[None, :]
        ang = jnp.concatenate([ang_r, ang_r, ang_c, ang_c], axis=-1)
        reps = BRANCH_W // dim
        parts += [jnp.tile(jnp.cos(ang), (1, reps)), jnp.tile(jnp.sin(ang), (1, reps))]
    tab = jnp.concatenate(parts, axis=1)
    ident = jnp.concatenate([jnp.ones((n_identity, BRANCH_W), F32), jnp.zeros((n_identity, BRANCH_W), F32)] * 2, axis=1)
    return jnp.concatenate([ident, tab], axis=0)


def _mha_kernel(lam_ref, q_ref, k_ref, v_ref, *rest, n_maps, scale, has_ctx, subln_scale):
    rest = list(rest)
    ck_ref, cv_ref = (rest.pop(0), rest.pop(0)) if has_ctx else (None, None)
    g_ref, bd_ref = (rest.pop(0), rest.pop(0)) if subln_scale is not None else (None, None)
    o_ref = rest.pop(0)
    q = q_ref[...]
    k = k_ref[...]
    v = v_ref[...]
    tq = q.shape[0]
    lane = lax.broadcasted_iota(jnp.int32, (1, BRANCH_W), 1)
    sub_w = HEAD_W // n_maps
    acc = jnp.zeros((tq, BRANCH_W), F32)
    for h in range(N_HEADS):
        oh = None
        for j in range(n_maps):
            qm = jnp.where((lane // sub_w) == (h * n_maps + j), q, jnp.zeros_like(q))
            s = _dot_nt(qm, k)
            if scale != 1.0:
                s = s * scale
            m = jnp.max(s, axis=-1, keepdims=True)
            if has_ctx:
                sc = _dot_nt(qm, ck_ref[...])
                if scale != 1.0:
                    sc = sc * scale
                m = jnp.maximum(m, jnp.max(sc, axis=-1, keepdims=True))
                pc = jnp.exp(sc - m)
            p = jnp.exp(s - m)
            l = jnp.sum(p, axis=-1, keepdims=True)
            pv = _dot(p.astype(BF16), v)
            if has_ctx:
                l = l + jnp.sum(pc, axis=-1, keepdims=True)
                pv = pv + _dot(pc.astype(BF16), cv_ref[...])
            pv = pv * (1.0 / l)
            oh = pv if j == 0 else oh - lam_ref[0] * pv
        acc = jnp.where((lane // HEAD_W) == h, oh, acc)
    if subln_scale is not None:
        acc = _head_rmsnorm(acc, bd_ref[...], g_ref[...]) * subln_scale
    o_ref[...] = acc.astype(o_ref.dtype)


def _mha(att, lam, *, n_seq, seq, row0, n_maps, scale, ctx=None, subln=None, tq=256):
    qb0 = row0 // tq
    kb0 = row0 // seq
    nq = seq // tq
    assert row0 % seq == 0 and seq % tq == 0
    in_specs = [pl.BlockSpec(memory_space=pltpu.SMEM),
                pl.BlockSpec((tq, BRANCH_W), lambda i, j: (qb0 + i * nq + j, 0)),
                pl.BlockSpec((seq, BRANCH_W), lambda i, j: (kb0 + i, 1)),
                pl.BlockSpec((seq, BRANCH_W), lambda i, j: (kb0 + i, 2))]
    args = [lam, att, att, att]
    kk = seq
    if ctx is not None:
        kc = ctx[0].shape[1]
        kk += kc
        in_specs += [pl.BlockSpec((None, kc, BRANCH_W), lambda i, j: (i, 0, 0))] * 2
        args += list(ctx)
    subln_scale = None
    if subln is not None:
        in_specs += [pl.BlockSpec((1, BRANCH_W), lambda i, j: (0, 0)),
                     pl.BlockSpec((BRANCH_W, BRANCH_W), lambda i, j: (0, 0))]
        args += [subln[0], subln[1]]
        subln_scale = subln[2]
    return pl.pallas_call(
        functools.partial(_mha_kernel, n_maps=n_maps, scale=scale, has_ctx=ctx is not None, subln_scale=subln_scale),
        grid=(n_seq, nq),
        in_specs=in_specs,
        out_specs=pl.BlockSpec((tq, BRANCH_W), lambda i, j: (i * nq + j, 0)),
        out_shape=jax.ShapeDtypeStruct((n_seq * seq, BRANCH_W), BF16),
        compiler_params=_cparams("parallel", "parallel"),
        name=f"mha_maps{n_maps}_k{kk}",
    )(*args)


def _na_row0(r, n_rows):
    return jnp.clip(r - NA_ROWS // 2, 0, n_rows - NA_ROWS)


def _na_kernel(q_ref, k_ref, v_ref, ck_ref, cv_ref, bias_ref, o_ref, *, n_rows):
    r = pl.program_id(1)
    start = pl.multiple_of(_na_row0(r, n_rows) * GRID_W, GRID_W)
    win = NA_ROWS * GRID_W
    q = q_ref[...]
    kw = k_ref[pl.ds(start, win), :]
    vw = v_ref[pl.ds(start, win), :]
    lane = lax.broadcasted_iota(jnp.int32, (1, BRANCH_W), 1)
    qs = jnp.concatenate(
        [jnp.where((lane // HEAD_W) == h, q, jnp.zeros_like(q)) for h in range(N_HEADS)], axis=0)
    s_loc = _dot_nt(qs, kw) + bias_ref[...]
    s_ctx = _dot_nt(qs, ck_ref[...])
    m = jnp.maximum(jnp.max(s_loc, axis=-1, keepdims=True), jnp.max(s_ctx, axis=-1, keepdims=True))
    p_loc = jnp.exp(s_loc - m)
    p_ctx = jnp.exp(s_ctx - m)
    l = jnp.sum(p_loc, axis=-1, keepdims=True) + jnp.sum(p_ctx, axis=-1, keepdims=True)
    o = (_dot(p_loc.astype(BF16), vw) + _dot(p_ctx.astype(BF16), cv_ref[...])) * (1.0 / l)
    acc = jnp.zeros((GRID_W, BRANCH_W), F32)
    for h in range(N_HEADS):
        acc = jnp.where((lane // HEAD_W) == h, o[h * GRID_W:(h + 1) * GRID_W], acc)
    o_ref[...] = acc.astype(o_ref.dtype)


def _na_attention(att, ck, cv, bias_tab, *, n_seq, seq, row0):
    n_rows = seq // GRID_W
    kk = ck.shape[1]
    win = NA_ROWS * GRID_W
    qb0 = row0 // GRID_W
    kb0 = row0 // seq
    assert row0 % seq == 0
    return pl.pallas_call(
        functools.partial(_na_kernel, n_rows=n_rows),
        grid=(n_seq, n_rows),
        in_specs=[pl.BlockSpec((GRID_W, BRANCH_W), lambda i, r: (qb0 + i * n_rows + r, 0)),
                  pl.BlockSpec((seq, BRANCH_W), lambda i, r: (kb0 + i, 1)),
                  pl.BlockSpec((seq, BRANCH_W), lambda i, r: (kb0 + i, 2)),
                  pl.BlockSpec((None, kk, BRANCH_W), lambda i, r: (i, 0, 0)),
                  pl.BlockSpec((None, kk, BRANCH_W), lambda i, r: (i, 0, 0)),
                  pl.BlockSpec((None, N_HEADS * GRID_W, win), lambda i, r: (r - _na_row0(r, n_rows), 0, 0))],
        out_specs=pl.BlockSpec((GRID_W, BRANCH_W), lambda i, r: (i * n_rows + r, 0)),
        out_shape=jax.ShapeDtypeStruct((n_seq * seq, BRANCH_W), BF16),
        compiler_params=_cparams("parallel", "parallel"),
        name="na_attention",
    )(att, att, att, ck, cv, bias_tab)


def _na_bias_table(rpb, n_rows):
    var = np.arange(NA_ROWS)[:, None, None]
    j = np.arange(NA_ROWS)[None, :, None]
    sel_r = (np.arange(2 * NA_ROWS - 1)[None, None, :] == j - var + (NA_ROWS - 1)).astype(np.float32)
    col = np.arange(GRID_W)[:, None, None]
    kc = np.arange(GRID_W)[None, :, None]
    c0 = np.clip(col - NA_COLS // 2, 0, GRID_W - NA_COLS)
    valid = ((kc >= c0) & (kc < c0 + NA_COLS))[:, :, 0]
    sel_c = (np.arange(2 * NA_COLS - 1)[None, None, :] == kc - col + (NA_COLS - 1)).astype(np.float32)
    tab = jnp.einsum('hrc,vjr,xkc->vhxjk', rpb.astype(F32), sel_r, sel_c, precision=lax.Precision.HIGHEST)
    tab = jnp.where(valid[None, None, :, None, :], tab, NEG)
    return tab.reshape(NA_ROWS, N_HEADS * GRID_W, NA_ROWS * GRID_W)


def _softplus(x):
    return jnp.maximum(x, 0.0) + jnp.log1p(jnp.exp(-jnp.abs(x)))


def _expand_heads(colmat, d, lane256):
    out = jnp.zeros((colmat.shape[0], BRANCH_W), F32)
    for h in range(B_HEADS):
        j = d * B_HEADS + h
        out = jnp.where((lane256 // HEAD_W) == h,
                        jnp.broadcast_to(colmat[:, j:j + 1], (colmat.shape[0], BRANCH_W)), out)
    return out


def _ssd_kernel(xbc_ref, z_ref, dt_ref, cw_ref, cb_ref, pc_ref, dsk_ref, ng_ref, init_ref,
                o_ref, st_ref, xpad_s, xc_s, yf_s, s_s, *, seq):
    n_chunks = seq // B_CHUNK
    L = B_CHUNK
    ri = lax.broadcasted_iota(jnp.int32, (L, L), 0)
    ci = lax.broadcasted_iota(jnp.int32, (L, L), 1)
    low = ci <= ri
    upp = ri <= ci
    low_b = jnp.where(low, 1.0, 0.0).astype(BF16)
    upp_b = jnp.where(upp, 1.0, 0.0).astype(BF16)
    lane128 = lax.broadcasted_iota(jnp.int32, (1, LANES), 1)
    lane256 = lax.broadcasted_iota(jnp.int32, (1, BRANCH_W), 1)
    row256 = lax.broadcasted_iota(jnp.int32, (BRANCH_W, 1), 0)
    blockmask = (row256 // (2 * HEAD_W)) == (lane128 // B_STATE)

    xpad_s[0:CONV_PAD, :] = jnp.zeros((CONV_PAD, B_XBC), F32)
    xpad_s[seq + CONV_PAD:seq + 2 * CONV_PAD, :] = jnp.zeros((CONV_PAD, B_XBC), F32)

    def pad_body(c, carry):
        base = pl.multiple_of(c * L, L)
        xpad_s[pl.ds(pl.multiple_of(base + CONV_PAD, CONV_PAD), L), :] = xbc_ref[pl.ds(base, L), :]
        return carry

    lax.fori_loop(0, n_chunks, pad_body, 0)
    cw = cw_ref[...]
    cb = cb_ref[...]

    def conv_body(c, carry):
        base = pl.multiple_of(c * L, L)
        w = xpad_s[pl.ds(base, L + 2 * CONV_PAD), :]
        acc = jnp.zeros((L, B_XBC), F32) + cb
        for kk in range(B_CONV_W):
            off = CONV_PAD - B_CONV_W // 2 + kk
            acc = acc + w[off:off + L, :] * cw[kk:kk + 1, :]
        xc_s[pl.ds(base, L), :] = acc * _sigmoid(acc)
        return carry

    lax.fori_loop(0, n_chunks, conv_body, 0)

    dt_bias = pc_ref[0:1, :]
    a_neg = pc_ref[1:2, :]
    dsk = dsk_ref[...]
    ng = ng_ref[...]

    def chunk(c, d):
        base = pl.multiple_of(c * L, L)
        xs = xc_s[pl.ds(base, L), 0:BRANCH_W]
        bm = xc_s[pl.ds(base, L), BRANCH_W:BRANCH_W + LANES].astype(BF16)
        cm = xc_s[pl.ds(base, L), BRANCH_W + LANES:B_XBC].astype(BF16)
        dtc = _softplus(dt_ref[pl.ds(base, L), :] + dt_bias)
        da_c = dtc * a_neg
        da_r = da_c.T
        tri_c = low_b if d == 0 else upp_b
        tri_r = upp_b if d == 0 else low_b
        c1, c2, c3 = _split3(da_c)
        cum_c = _dot(tri_c, c1) + _dot(tri_c, c2) + _dot(tri_c, c3)
        r1, r2, r3 = _split3(da_r)
        cum_r = _dot(r1, tri_r) + _dot(r2, tri_r) + _dot(r3, tri_r)
        cum_end = cum_c[L - 1:L, :] if d == 0 else cum_c[0:1, :]
        dmask = low if d == 0 else upp

        g0 = _dot_nt(jnp.where(lane128 < B_STATE, cm, jnp.zeros_like(cm)), bm)
        g1 = _dot_nt(jnp.where(lane128 >= B_STATE, cm, jnp.zeros_like(cm)), bm)
        dt_x = _expand_heads(dtc, d, lane256)
        e_a = _expand_heads(jnp.exp(cum_c), d, lane256)
        t_e = _expand_heads(jnp.exp(cum_end - cum_c), d, lane256)
        xdt = xs * dt_x
        xdt_b = xdt.astype(BF16)
        y = jnp.zeros((L, BRANCH_W), F32)
        for h in range(B_HEADS):
            j = d * B_HEADS + h
            col = jnp.broadcast_to(cum_c[:, j:j + 1], (L, L))
            row = jnp.broadcast_to(cum_r[j:j + 1, :], (L, L))
            dec = jnp.exp(jnp.where(dmask, col - row, NEG))
            sc = ((g0 if h < 2 else g1) * dec).astype(BF16)
            y = jnp.where((lane256 // HEAD_W) == h, _dot(sc, xdt_b), y)
        state = s_s[...]
        y = y + _dot_nt(cm, state.astype(BF16)) * e_a
        zmat = _dot_tn((xdt * t_e).astype(BF16), bm)
        e_end = jnp.exp(cum_end)
        cd = jnp.zeros((BRANCH_W, LANES), F32)
        for h in range(B_HEADS):
            j = d * B_HEADS + h
            cd = jnp.where((row256 // HEAD_W) == h, jnp.broadcast_to(e_end[:, j:j + 1], (BRANCH_W, LANES)), cd)
        s_s[...] = state * cd + jnp.where(blockmask, zmat, 0.0)
        return base, xs, y

    s_s[...] = init_ref[0]

    def fwd_body(c, carry):
        base, _, y = chunk(c, 0)
        yf_s[pl.ds(base, L), :] = y
        return carry

    lax.fori_loop(0, n_chunks, fwd_body, 0)
    st_ref[0] = s_s[...]
    s_s[...] = init_ref[1]

    def bwd_body(i, carry):
        c = n_chunks - 1 - i
        base, xs, y = chunk(c, 1)
        z = z_ref[pl.ds(base, L), :]
        yt = (yf_s[pl.ds(base, L), :] + y + dsk * xs) * (z * _sigmoid(z))
        ms = jnp.mean(yt * yt, axis=-1, keepdims=True)
        o_ref[pl.ds(base, L), :] = (yt * lax.rsqrt(ms + EPS) * ng).astype(o_ref.dtype)
        return carry

    lax.fori_loop(0, n_chunks, bwd_body, 0)
    st_ref[1] = s_s[...]


def _ssd_branch(ssd_in, conv_w, conv_b, pc, dsk, ng, init, *, n_seq, seq, row0):
    b0 = row0 // seq
    assert row0 % seq == 0
    full = lambda *shape: pl.BlockSpec(shape, lambda i: (0,) * len(shape))
    return pl.pallas_call(
        functools.partial(_ssd_kernel, seq=seq),
        grid=(n_seq,),
        in_specs=[pl.BlockSpec((seq, B_XBC), lambda i: (b0 + i, 0)),
                  pl.BlockSpec((seq, BRANCH_W), lambda i: (b0 + i, B_XBC // BRANCH_W)),
                  pl.BlockSpec((seq, LANES), lambda i: (b0 + i, (B_XBC + BRANCH_W) // LANES)),
                  full(8, B_XBC), full(1, B_XBC), full(8, LANES), full(1, BRANCH_W), full(1, BRANCH_W),
                  pl.BlockSpec((None, 2, BRANCH_W, LANES), lambda i: (i, 0, 0, 0))],
        out_specs=[pl.BlockSpec((seq, BRANCH_W), lambda i: (i, 0)),
                   pl.BlockSpec((None, 2, BRANCH_W, LANES), lambda i: (i, 0, 0, 0))],
        out_shape=[jax.ShapeDtypeStruct((n_seq * seq, BRANCH_W), BF16),
                   jax.ShapeDtypeStruct((n_seq, 2, BRANCH_W, LANES), F32)],
        scratch_shapes=[pltpu.VMEM((seq + 2 * CONV_PAD, B_XBC), F32), pltpu.VMEM((seq, B_XBC), F32),
                        pltpu.VMEM((seq, BRANCH_W), F32), pltpu.VMEM((BRANCH_W, LANES), F32)],
        compiler_params=_cparams("parallel"),
        name=f"ssd_t{seq}",
    )(ssd_in, ssd_in, ssd_in, conv_w, conv_b, pc, dsk, ng, init)


def _merge_kernel(x_ref, mod_ref, g1_ref, g2_ref, *rest, n_prompt_tiles):
    br_p = rest[0:N_BRANCH]
    br_s = rest[N_BRANCH:2 * N_BRANCH]
    wg_ref, wb_ref, wo_ref, rwh_ref, rwl_ref, rb_ref, x1_ref, h2_ref, gate_ref, eidx_ref = rest[2 * N_BRANCH:]
    is_prompt = pl.program_id(0) < n_prompt_tiles
    x = x_ref[...]
    mod = mod_ref[...]
    h = _norm_mod(x, g1_ref[...], mod[0:1], mod[1:2]).astype(BF16)
    merged = None
    for i in range(N_BRANCH):
        gate = _sigmoid(_dot(h, wg_ref[:, i * D_MODEL:(i + 1) * D_MODEL]))
        br = jnp.where(is_prompt, br_p[i][...], br_s[i][...])
        proj = _dot(br, wb_ref[i])
        merged = gate * proj if i == 0 else merged + gate * proj
    y = _dot(merged.astype(BF16), wo_ref[...])
    x1 = x + mod[2:3] * y
    x1_ref[...] = x1
    h2 = _norm_mod(x1, g2_ref[...], mod[3:4], mod[4:5])
    hh, hl = _split2(h2)
    for j in range(ROW_TILE):
        h2_ref[pl.ds(j, TOKEN_TILE, stride=ROW_TILE), :] = h2[:, j * LANES:(j + 1) * LANES]
    rwh = rwh_ref[...]
    logits = _dot(hh, rwh) + _dot(hl, rwh) + _dot(hh, rwl_ref[...]) + rb_ref[...]
    lane = lax.broadcasted_iota(jnp.int32, (1, LANES), 1).astype(F32)
    cur = jnp.where(lane < N_EXPERTS, logits, NEG)
    vals, idxs = [], []
    for _ in range(TOP_K):
        m = jnp.max(cur, axis=-1, keepdims=True)
        am = jnp.min(jnp.where(cur == m, lane, float(LANES)), axis=-1, keepdims=True)
        vals.append(m)
        idxs.append(am)
        cur = jnp.where(lane == am, NEG, cur)
    exps = [jnp.exp(v - vals[0]) for v in vals]
    inv = 1.0 / (exps[0] + exps[1] + exps[2] + exps[3])
    gate = jnp.zeros_like(logits)
    eidx = jnp.zeros_like(logits)
    for k in range(TOP_K):
        gate = jnp.where(lane == k, exps[k] * inv, gate)
        eidx = jnp.where(lane == k, idxs[k], eidx)
    gate_ref[...] = gate
    eidx_ref[...] = eidx.astype(jnp.int32)


def _merge(x, mod_tab, g1, g2, br_p, br_s, wg, wb, wo, rwh, rwl, rb, n_prompt_tiles, tiles_per_sample):
    nt = x.shape[0]
    mrow = functools.partial(_mod_row, n_prompt_tiles=n_prompt_tiles, tiles_per_sample=tiles_per_sample)
    const = lambda *shape: pl.BlockSpec(shape, lambda i: (0,) * len(shape))
    rows = lambda w: pl.BlockSpec((TOKEN_TILE, w), lambda i: (i, 0))
    p_rows = pl.BlockSpec((TOKEN_TILE, BRANCH_W), lambda i: (jnp.minimum(i, n_prompt_tiles - 1), 0))
    s_rows = pl.BlockSpec((TOKEN_TILE, BRANCH_W), lambda i: (jnp.maximum(i - n_prompt_tiles, 0), 0))
    return pl.pallas_call(
        functools.partial(_merge_kernel, n_prompt_tiles=n_prompt_tiles),
        grid=(nt // TOKEN_TILE,),
        in_specs=[rows(D_MODEL),
                  pl.BlockSpec((None, 8, D_MODEL), lambda i: (mrow(i), 0, 0)),
                  const(1, D_MODEL), const(1, D_MODEL)]
                 + [p_rows] * N_BRANCH + [s_rows] * N_BRANCH
                 + [const(D_MODEL, N_BRANCH * D_MODEL), const(N_BRANCH, BRANCH_W, D_MODEL),
                    const(D_MODEL, D_MODEL), const(D_MODEL, LANES), const(D_MODEL, LANES), const(1, LANES)],
        out_specs=[rows(D_MODEL), pl.BlockSpec((TOKEN_TILE * ROW_TILE, LANES), lambda i: (i, 0)),
                   rows(LANES), rows(LANES)],
        out_shape=[jax.ShapeDtypeStruct((nt, D_MODEL), F32),
                   jax.ShapeDtypeStruct((nt * ROW_TILE, LANES), F32),
                   jax.ShapeDtypeStruct((nt, LANES), F32),
                   jax.ShapeDtypeStruct((nt, LANES), jnp.int32)],
        compiler_params=_cparams("parallel"),
        name="merge",
    )(x, mod_tab, g1, g2, *br_p, *br_s, wg, wb, wo, rwh, rwl, rb)


MOE_GROUPS = 8
MOE_CHUNK = 2 * D_FF // MOE_GROUPS


def _moe_step(i, last, tok_ref, tokn_ref, slotp_ref, slot_ref, h_hbm, b1_ref, b2_ref, out_hbm, w1_s, w2_s,
              x_cur, x_nxt, y_cur, y_prev, g_cur, g_nxt, s_cur, s_prev):
    block_sublanes = MOE_ROWS * ROW_TILE

    def tile(ref, t):
        return ref.at[pl.ds(pl.multiple_of(t * ROW_TILE, ROW_TILE), ROW_TILE), :]

    def gather_copy(tref, buf, sem, r):
        return pltpu.make_async_copy(tile(h_hbm, tref[0, r]), buf.at[pl.ds(r * ROW_TILE, ROW_TILE), :], sem)

    def gather_all(buf, sem):
        return pltpu.make_async_copy(h_hbm.at[pl.ds(0, block_sublanes), :], buf, sem)

    def scatter_copy(sref, buf, sem, r):
        return pltpu.make_async_copy(buf.at[pl.ds(r * ROW_TILE, ROW_TILE), :], tile(out_hbm, sref[0, r]), sem)

    def scatter_all(buf, sem):
        return pltpu.make_async_copy(buf, out_hbm.at[pl.ds(0, block_sublanes), :], sem)

    @pl.when(i == 0)
    def _():
        y_prev[...] = jnp.zeros_like(y_prev)
        for r in range(MOE_ROWS):
            gather_copy(tok_ref, x_cur, g_cur, r).start()

    @pl.when(i >= 1)
    def _():
        scatter_all(y_cur, s_cur).wait()

    gather_all(x_cur, g_cur).wait()
    x = jnp.concatenate([x_cur[pl.ds(j, MOE_ROWS, stride=ROW_TILE), :] for j in range(ROW_TILE)],
                        axis=1).astype(BF16)
    per = MOE_ROWS // MOE_GROUPS
    acts = []
    for c in range(MOE_GROUPS // 2):
        halves = []
        for part in range(2):
            grp = 2 * c + part
            for r in range(grp * per, (grp + 1) * per):
                gather_copy(tokn_ref, x_nxt, g_nxt, r).start()
                scatter_copy(slotp_ref, y_prev, s_prev, r).start(priority=1)
            lo = part * D_FF + c * MOE_CHUNK
            halves.append(_dot(x, w1_s[:, lo:lo + MOE_CHUNK]) + b1_ref[:, lo:lo + MOE_CHUNK])
        glu = jnp.minimum(halves[0], SWIGLU_LIMIT)
        lin = jnp.clip(halves[1], -SWIGLU_LIMIT, SWIGLU_LIMIT)
        acts.append((glu * _sigmoid(SWIGLU_ALPHA * glu) * (lin + 1.0)).astype(BF16))
    act = jnp.concatenate(acts, axis=1)
    y = _dot(act, w2_s[...]) + b2_ref[...]
    for j in range(ROW_TILE):
        y_cur[pl.ds(j, MOE_ROWS, stride=ROW_TILE), :] = y[:, j * LANES:(j + 1) * LANES]

    @pl.when(i == last)
    def _():
        for r in range(MOE_ROWS):
            scatter_copy(slot_ref, y_cur, s_cur, r).start()
        gather_all(x_nxt, g_nxt).wait()
        scatter_all(y_prev, s_prev).wait()
        scatter_all(y_cur, s_cur).wait()


def _moe_kernel(be_ref, nu_ref, tok_ref, tokn_ref, slotp_ref, slot_ref, h_hbm, w1_ref, b1_ref, w2_ref, b2_ref,
                out_hbm, w1_s, w2_s, xa, xb, ya, yb, zbuf, gsem, ssem, zsem):
    i = pl.program_id(0)
    n_used = nu_ref[0]
    prev = be_ref[jnp.maximum(i - 1, 0)]

    @pl.when(jnp.logical_and(i < n_used, jnp.logical_or(i == 0, be_ref[i] != prev)))
    def _():
        w1_s[...] = w1_ref[...].astype(BF16)
        w2_s[...] = w2_ref[...].astype(BF16)

    common = (i, n_used - 1, tok_ref, tokn_ref, slotp_ref, slot_ref, h_hbm, b1_ref, b2_ref, out_hbm, w1_s, w2_s)

    @pl.when(jnp.logical_and(i < n_used, i % 2 == 0))
    def _():
        _moe_step(*common, xa, xb, ya, yb, gsem.at[0], gsem.at[1], ssem.at[0], ssem.at[1])

    @pl.when(jnp.logical_and(i < n_used, i % 2 == 1))
    def _():
        _moe_step(*common, xb, xa, yb, ya, gsem.at[1], gsem.at[0], ssem.at[1], ssem.at[0])

    @pl.when(i >= n_used)
    def _():
        zbuf[...] = jnp.zeros_like(zbuf)
        first = pl.multiple_of(slot_ref[0, 0] * ROW_TILE, ROW_TILE)
        fill = pltpu.make_async_copy(zbuf, out_hbm.at[pl.ds(first, MOE_ROWS * ROW_TILE), :], zsem.at[0])
        fill.start()
        fill.wait()


def _moe_experts(block_expert, n_used, row_token, row_slot, h2, w1, b1, w2, b2, layer):
    n_blocks = row_token.shape[0]
    n_rows = n_blocks * MOE_ROWS
    spare = (n_rows + jnp.arange(MOE_ROWS, dtype=jnp.int32)).reshape(1, 1, MOE_ROWS)
    slot_prev = jnp.concatenate([spare, row_slot[:-1]], axis=0)
    smem_rows = lambda imap: pl.BlockSpec((None, 1, MOE_ROWS), imap, memory_space=pltpu.SMEM)
    expert_block = lambda *shape: pl.BlockSpec((None, None) + shape, lambda i, be, nu: (layer, be[i], 0, 0))
    row_buf = pltpu.VMEM((MOE_ROWS * ROW_TILE, LANES), F32)
    grid_spec = pltpu.PrefetchScalarGridSpec(
        num_scalar_prefetch=2,
        grid=(n_blocks,),
        in_specs=[smem_rows(lambda i, be, nu: (i, 0, 0)),
                  smem_rows(lambda i, be, nu: (jnp.minimum(i + 1, n_blocks - 1), 0, 0)),
                  smem_rows(lambda i, be, nu: (i, 0, 0)),
                  smem_rows(lambda i, be, nu: (i, 0, 0)),
                  pl.BlockSpec(memory_space=pl.ANY),
                  expert_block(D_MODEL, 2 * D_FF), expert_block(1, 2 * D_FF),
                  expert_block(D_FF, D_MODEL), expert_block(1, D_MODEL)],
        out_specs=pl.BlockSpec(memory_space=pl.ANY),
        scratch_shapes=[pltpu.VMEM((D_MODEL, 2 * D_FF), BF16), pltpu.VMEM((D_FF, D_MODEL), BF16),
                        row_buf, row_buf, row_buf, row_buf, row_buf,
                        pltpu.SemaphoreType.DMA((2,)), pltpu.SemaphoreType.DMA((2,)),
                        pltpu.SemaphoreType.DMA((1,))],
    )
    return pl.pallas_call(
        _moe_kernel,
        grid_spec=grid_spec,
        out_shape=jax.ShapeDtypeStruct(((n_rows + MOE_ROWS) * ROW_TILE, LANES), F32),
        compiler_params=pltpu.CompilerParams(dimension_semantics=("arbitrary",), vmem_limit_bytes=VMEM_LIMIT,
                                             has_side_effects=True),
        name="moe_experts",
    )(block_expert, n_used, row_token, row_token, slot_prev, row_slot, h2, w1, b1, w2, b2)


def _moe_dispatch(eidx):
    n = eidx.shape[0]
    n_assign = n * TOP_K
    n_blocks = -(-n_assign // MOE_ROWS) + N_EXPERTS
    n_rows = n_blocks * MOE_ROWS
    expert = eidx.reshape(-1)
    onehot = (expert[:, None] == jnp.arange(N_EXPERTS, dtype=jnp.int32)[None, :]).astype(jnp.int32)
    incl = jnp.cumsum(onehot, axis=0)
    counts = incl[-1]
    padded = (counts + MOE_ROWS - 1) // MOE_ROWS * MOE_ROWS
    pad_ends = jnp.cumsum(padded)
    pad_starts = pad_ends - padded
    dest = jnp.sum((incl - onehot + pad_starts[None, :]) * onehot, axis=1).astype(jnp.int32)
    assign = jnp.full((n_rows,), -1, jnp.int32).at[dest].set(jnp.arange(n_assign, dtype=jnp.int32))
    is_pad = assign < 0
    filler = n_assign + jnp.cumsum(is_pad.astype(jnp.int32)) - 1
    row_slot = jnp.where(is_pad, filler, (assign % TOP_K) * n + assign // TOP_K)
    row_token = jnp.where(is_pad, 0, assign // TOP_K)
    block_start = jnp.arange(n_blocks, dtype=jnp.int32) * MOE_ROWS
    block_expert = jnp.minimum(jnp.sum((pad_ends[None, :] <= block_start[:, None]).astype(jnp.int32), axis=1),
                               N_EXPERTS - 1).astype(jnp.int32)
    n_used = (pad_ends[-1] // MOE_ROWS).astype(jnp.int32).reshape(1)
    return (block_expert, n_used, row_token.reshape(n_blocks, 1, MOE_ROWS),
            row_slot.reshape(n_blocks, 1, MOE_ROWS))


def _combine_kernel(x1_ref, mod_ref, gate_ref, y0_ref, y1_ref, y2_ref, y3_ref, fg_ref, o_ref, *, final):
    gate = gate_ref[...]
    y = None
    for k, yk_ref in enumerate((y0_ref, y1_ref, y2_ref, y3_ref)):
        rows = jnp.concatenate([yk_ref[pl.ds(j, TOKEN_TILE, stride=ROW_TILE), :] for j in range(ROW_TILE)], axis=1)
        yk = gate[:, k:k + 1] * rows
        y = yk if k == 0 else y + yk
    x2 = x1_ref[...] + mod_ref[5:6, :] * y
    if final:
        ms = jnp.mean(x2 * x2, axis=-1, keepdims=True)
        x2 = x2 * lax.rsqrt(ms + EPS) * fg_ref[...]
    o_ref[...] = x2


def _combine(x1, mod_tab, gate, y_slots, final_g, n_prompt_tiles, tiles_per_sample, *, final):
    nt = x1.shape[0]
    tiles = nt // TOKEN_TILE
    mrow = functools.partial(_mod_row, n_prompt_tiles=n_prompt_tiles, tiles_per_sample=tiles_per_sample)
    slot_rows = lambda k: pl.BlockSpec((TOKEN_TILE * ROW_TILE, LANES), lambda i: (k * tiles + i, 0))
    return pl.pallas_call(
        functools.partial(_combine_kernel, final=final),
        grid=(tiles,),
        in_specs=[pl.BlockSpec((TOKEN_TILE, D_MODEL), lambda i: (i, 0)),
                  pl.BlockSpec((None, 8, D_MODEL), lambda i: (mrow(i), 0, 0)),
                  pl.BlockSpec((TOKEN_TILE, LANES), lambda i: (i, 0))]
                 + [slot_rows(k) for k in range(TOP_K)]
                 + [pl.BlockSpec((1, D_MODEL), lambda i: (0, 0))],
        out_specs=pl.BlockSpec((TOKEN_TILE, D_MODEL), lambda i: (i, 0)),
        out_shape=jax.ShapeDtypeStruct((nt, D_MODEL), F32),
        compiler_params=_cparams("parallel"),
        name="combine_final" if final else "combine",
    )(x1, mod_tab, gate, y_slots, y_slots, y_slots, y_slots, final_g)


def _state_to_blocks(init):
    b = init.shape[0]
    st = init.reshape(b, 2, B_HEADS * HEAD_W, B_STATE)
    row = jnp.arange(B_HEADS * HEAD_W)[:, None]
    left = jnp.where(row < 2 * HEAD_W, st, 0.0)
    right = jnp.where(row >= 2 * HEAD_W, st, 0.0)
    return jnp.concatenate([left, right], axis=-1)


def _blocks_to_state(s):
    b = s.shape[0]
    row = jnp.arange(B_HEADS * HEAD_W)[:, None]
    st = jnp.where(row < 2 * HEAD_W, s[..., :B_STATE], s[..., B_STATE:])
    return st.reshape(b, 2, B_HEADS, HEAD_W, B_STATE)


def kernel(x_prompt, x_sample, c, cache_a_k, cache_a_v, cache_c_k, cache_c_v, cache_d_k, cache_d_v, state_ssd, c_ctx, norm1_g, norm2_g, w_mod, b_mod, w_in, a_q_g, a_k_g, b_conv_w, b_conv_b, b_dt_bias, b_a_log, b_d, b_norm_g, c_lam, c_subln_g, d_rpb, w_branch, w_out, router_w, router_b, moe_w1, moe_b1, moe_w2, moe_b2, final_g):
    bp, tp, _ = x_prompt.shape
    bs, ts, _ = x_sample.shape
    depth = w_in.shape[0]
    n_p = bp * tp
    n_s = bs * ts
    assert tp == TOKEN_TILE and ts % TOKEN_TILE == 0 and n_p % ts == 0
    n_prompt_tiles = n_p // TOKEN_TILE
    tiles_per_sample = ts // TOKEN_TILE
    past = cache_a_k.shape[2]

    x = jnp.concatenate([x_prompt.reshape(n_p, D_MODEL), x_sample.reshape(n_s, D_MODEL)], axis=0)
    cvec = jnp.zeros((16, D_MODEL), F32).at[0].set(c_ctx).at[1:1 + bs].set(c)
    rope_tab = _rope_table(ts, TOKEN_TILE)
    perm_a = _rope_perm(HEAD_W)
    perm_c = _rope_perm(C_QK_DIM)
    head_of = np.arange(BRANCH_W) // HEAD_W
    bd = jnp.asarray((head_of[:, None] == head_of[None, :]) / HEAD_W, BF16)
    kv_of = np.arange(A_KV_W) // HEAD_W
    ex = jnp.asarray((kv_of[:, None] == (head_of // (N_HEADS // A_KV_HEADS))[None, :])
                     & ((np.arange(A_KV_W) % HEAD_W)[:, None] == (np.arange(BRANCH_W) % HEAD_W)[None, :]), BF16)
    scale_c = C_QK_DIM ** -0.5
    one = jnp.ones((1,), F32)
    rep = N_HEADS // A_KV_HEADS
    zero_state = jnp.zeros((bp, 2, BRANCH_W, LANES), F32)

    new_cache = [[] for _ in range(7)]
    for l in range(depth):
        lam_init = 0.8 - 0.6 * math.exp(-0.3 * l)
        lq = c_lam[l]
        lam = (jnp.exp(jnp.sum(lq[0] * lq[1])) - jnp.exp(jnp.sum(lq[2] * lq[3])) + lam_init).reshape(1)

        wl = w_in[l]
        w1 = jnp.concatenate([wl[:, 0:512], wl[:, 768:1280], wl[:, 512:768], wl[:, 1280:1288],
                              jnp.zeros((D_MODEL, LANES - 8), F32), wl[:, 1288:2824]], axis=1).astype(BF16)
        wg = wl[:, 2824:].astype(BF16)
        wb = w_branch[l].astype(BF16)
        wo = w_out[l].astype(BF16)
        rw = jnp.pad(router_w[l], ((0, 0), (0, LANES - N_EXPERTS)))
        rwh = rw.astype(BF16)
        rwl = (rw - rwh.astype(F32)).astype(BF16)
        rb = jnp.pad(router_b[l], (0, LANES - N_EXPERTS)).reshape(1, LANES)

        mod = _modulation(cvec, w_mod[l], b_mod[l].reshape(1, -1))
        mod_tab = jnp.pad(mod.reshape(16, 6, D_MODEL), ((0, 0), (0, 2), (0, 0)))

        g1 = norm1_g[l].reshape(1, D_MODEL)
        g2 = norm2_g[l].reshape(1, D_MODEL)
        att_a, att_c, att_d, kv_a, kv_c, kv_d, ssd_in = _in_proj(
            x, mod_tab, g1, w1, rope_tab, bd, perm_a, perm_c, ex,
            jnp.tile(a_q_g[l], N_HEADS).reshape(1, BRANCH_W), jnp.tile(a_k_g[l], A_KV_HEADS).reshape(1, A_KV_W),
            n_prompt_tiles, tiles_per_sample)

        ctx_ak = jnp.repeat(cache_a_k[:, l], rep, axis=2).reshape(bs, past, BRANCH_W).astype(BF16)
        ctx_av = jnp.repeat(cache_a_v[:, l], rep, axis=2).reshape(bs, past, BRANCH_W).astype(BF16)
        o_a_p = _mha(att_a, one, n_seq=bp, seq=tp, row0=0, n_maps=1, scale=1.0)
        o_a_s = _mha(att_a, one, n_seq=bs, seq=ts, row0=n_p, n_maps=1, scale=1.0, ctx=(ctx_ak, ctx_av))

        subln = (jnp.tile(c_subln_g[l], N_HEADS).reshape(1, BRANCH_W), bd, 1.0 - lam_init)
        ctx_ck = cache_c_k[:, l].reshape(bs, past, BRANCH_W).astype(BF16)
        ctx_cv = cache_c_v[:, l].reshape(bs, past, BRANCH_W).astype(BF16)
        o_c_p = _mha(att_c, lam, n_seq=bp, seq=tp, row0=0, n_maps=2, scale=scale_c, subln=subln)
        o_c_s = _mha(att_c, lam, n_seq=bs, seq=ts, row0=n_p, n_maps=2, scale=scale_c, ctx=(ctx_ck, ctx_cv),
                     subln=subln)

        o_d_p = _mha(att_d, one, n_seq=bp, seq=tp, row0=0, n_maps=1, scale=1.0)
        o_d_s = _na_attention(att_d, cache_d_k[:, l].reshape(bs, past, BRANCH_W).astype(BF16),
                              cache_d_v[:, l].reshape(bs, past, BRANCH_W).astype(BF16),
                              _na_bias_table(d_rpb[l], ts // GRID_W), n_seq=bs, seq=ts, row0=n_p)

        dtb = b_dt_bias[l].reshape(8)
        a_neg = -jnp.exp(b_a_log[l].reshape(8))
        ssd_args = (jnp.pad(b_conv_w[l], ((0, 8 - B_CONV_W), (0, 0))), b_conv_b[l].reshape(1, B_XBC),
                    jnp.zeros((8, LANES), F32).at[0, :8].set(dtb).at[1, :8].set(a_neg),
                    jnp.repeat(b_d[l], HEAD_W).reshape(1, BRANCH_W), b_norm_g[l].reshape(1, BRANCH_W))
        o_b_p, st_p = _ssd_branch(ssd_in, *ssd_args, zero_state, n_seq=bp, seq=tp, row0=0)
        o_b_s, _ = _ssd_branch(ssd_in, *ssd_args, _state_to_blocks(state_ssd[:, l]), n_seq=bs, seq=ts, row0=n_p)

        x1, h2, gate, eidx = _merge(x, mod_tab, g1, g2, (o_a_p, o_b_p, o_c_p, o_d_p), (o_a_s, o_b_s, o_c_s, o_d_s),
                                    wg, wb, wo, rwh, rwl, rb, n_prompt_tiles, tiles_per_sample)
        block_expert, n_used, row_token, row_slot = _moe_dispatch(eidx[:, :TOP_K])
        y_slots = _moe_experts(block_expert, n_used, row_token, row_slot, h2, moe_w1,
                               moe_b1.reshape(depth, N_EXPERTS, 1, -1), moe_w2,
                               moe_b2.reshape(depth, N_EXPERTS, 1, -1), l)
        x = _combine(x1, mod_tab, gate, y_slots, final_g.reshape(1, D_MODEL), n_prompt_tiles, tiles_per_sample,
                     final=(l == depth - 1))

        new_cache[0].append(kv_a[:n_p, :A_KV_W].reshape(bp, tp, A_KV_HEADS, HEAD_W))
        new_cache[1].append(kv_a[:n_p, A_KV_W:].reshape(bp, tp, A_KV_HEADS, HEAD_W))
        new_cache[2].append(kv_c[:n_p, :BRANCH_W].reshape(bp, tp, N_HEADS, 2 * C_QK_DIM))
        new_cache[3].append(kv_c[:n_p, BRANCH_W:].reshape(bp, tp, N_HEADS, HEAD_W))
        new_cache[4].append(kv_d[:n_p, :BRANCH_W].reshape(bp, tp, N_HEADS, HEAD_W))
        new_cache[5].append(kv_d[:n_p, BRANCH_W:].reshape(bp, tp, N_HEADS, HEAD_W))
        new_cache[6].append(_blocks_to_state(st_p))

    y_prompt = x[:n_p].reshape(bp, tp, D_MODEL)
    y_sample = x[n_p:].reshape(bs, ts, D_MODEL)
    return (y_prompt, y_sample) + tuple(jnp.stack(v, axis=1) for v in new_cache)
```

```python
import functools
import math

import jax
import jax.numpy as jnp
import numpy as np
from jax import lax
from jax.experimental import pallas as pl
from jax.experimental.pallas import tpu as pltpu

F32 = jnp.float32
BF16 = jnp.bfloat16

D_MODEL = 1024
N_BRANCH = 4
BRANCH_W = 256
HEAD_W = 64
N_HEADS = 4
GRID_W = 64
ROPE_THETA = 10000.0
EPS = 1e-6
A_KV_HEADS = 2
A_KV_W = A_KV_HEADS * HEAD_W
C_QK_DIM = 32
B_HEADS = 4
B_STATE = 64
B_CHUNK = 128
B_XBC = 512
B_CONV_W = 5
CONV_PAD = 8
NA_ROWS = 8
NA_COLS = 16
N_EXPERTS = 32
TOP_K = 4
D_FF = 1024
SWIGLU_LIMIT = 7.0
SWIGLU_ALPHA = 1.702
LANES = 128
ROW_TILE = D_MODEL // LANES
NEG = -1e30

TOKEN_TILE = 512
MOE_ROWS = 256
ATT_W = 3 * BRANCH_W
SSD_W = B_XBC + BRANCH_W + LANES
OFF_A, OFF_B, OFF_C, OFF_D = 0, 512, 512 + SSD_W, 512 + SSD_W + ATT_W
W1_COLS = OFF_D + ATT_W
VMEM_LIMIT = 56 * 1024 * 1024


def _cparams(*sem):
    return pltpu.CompilerParams(dimension_semantics=sem, vmem_limit_bytes=VMEM_LIMIT)


def _dot(a, b):
    return jnp.dot(a, b, preferred_element_type=F32)


def _dot_nt(a, b):
    return lax.dot_general(a, b, (((1,), (1,)), ((), ())), preferred_element_type=F32)


def _dot_tn(a, b):
    return lax.dot_general(a, b, (((0,), (0,)), ((), ())), preferred_element_type=F32)


def _sigmoid(x):
    return 0.5 * jnp.tanh(0.5 * x) + 0.5


def _split2(x):
    hi = x.astype(BF16)
    return hi, (x - hi.astype(F32)).astype(BF16)


def _split3(x):
    h1 = x.astype(BF16)
    r1 = x - h1.astype(F32)
    h2 = r1.astype(BF16)
    h3 = (r1 - h2.astype(F32)).astype(BF16)
    return h1, h2, h3


def _norm_mod(x, g, shift, scale):
    ms = jnp.mean(x * x, axis=-1, keepdims=True)
    return (x * lax.rsqrt(ms + EPS)) * g * (1.0 + scale) + shift


def _head_rmsnorm(x, bd, g):
    hi, lo = _split2(x * x)
    ms = _dot(hi, bd) + _dot(lo, bd)
    return x * lax.rsqrt(ms + EPS) * g


def _rope(x, perm, cos, sin):
    hi, lo = _split2(x)
    return x * cos + (_dot(hi, perm) + _dot(lo, perm)) * sin


def _mod_kernel(c_ref, w_ref, b_ref, o_ref):
    c = c_ref[...]
    s = (c * _sigmoid(c)).astype(BF16)
    o_ref[...] = _dot(s, w_ref[...].astype(BF16)) + b_ref[...]


def _modulation(cvec, w_mod, b_mod):
    n = w_mod.shape[1]
    tn = 1536
    return pl.pallas_call(
        _mod_kernel,
        grid=(n // tn,),
        in_specs=[pl.BlockSpec((16, D_MODEL), lambda j: (0, 0)),
                  pl.BlockSpec((D_MODEL, tn), lambda j: (0, j)),
                  pl.BlockSpec((1, tn), lambda j: (0, j))],
        out_specs=pl.BlockSpec((16, tn), lambda j: (0, j)),
        out_shape=jax.ShapeDtypeStruct((16, n), F32),
        compiler_params=_cparams("parallel"),
        name="modulation",
    )(cvec, w_mod, b_mod)


def _mod_row(i, n_prompt_tiles, tiles_per_sample):
    return jnp.where(i < n_prompt_tiles, 0, 1 + (i - n_prompt_tiles) // tiles_per_sample)


def _rope_row(i, n_prompt_tiles, tiles_per_sample):
    return jnp.where(i < n_prompt_tiles, 0, 1 + (i - n_prompt_tiles) % tiles_per_sample)


def _inproj_kernel(x_ref, mod_ref, g_ref, w_ref, rope_ref, bd_ref, pa_ref, pc_ref, ex_ref, gq_ref, gk_ref,
                   atta_ref, attc_ref, attd_ref, kva_ref, kvc_ref, kvd_ref, ssd_ref):
    mod = mod_ref[...]
    h = _norm_mod(x_ref[...], g_ref[...], mod[0:1], mod[1:2])
    u = _dot(h.astype(BF16), w_ref[...])
    cos_a, sin_a = rope_ref[:, 0:256], rope_ref[:, 256:512]
    cos_c, sin_c = rope_ref[:, 512:768], rope_ref[:, 768:1024]
    q_scale = HEAD_W ** -0.5

    q = _rope(_head_rmsnorm(u[:, OFF_A:OFF_A + 256], bd_ref[...], gq_ref[...]), pa_ref[...], cos_a, sin_a)
    k = _rope(_head_rmsnorm(u[:, OFF_A + 256:OFF_A + 384], bd_ref[0:A_KV_W, 0:A_KV_W], gk_ref[...]),
              pa_ref[0:A_KV_W, 0:A_KV_W], cos_a[:, 0:A_KV_W], sin_a[:, 0:A_KV_W])
    v = u[:, OFF_A + 384:OFF_A + 512]
    ex = ex_ref[...]
    atta_ref[:, 0:256] = (q * q_scale).astype(BF16)
    atta_ref[:, 256:512] = _dot(k.astype(BF16), ex).astype(BF16)
    atta_ref[:, 512:768] = _dot(v.astype(BF16), ex).astype(BF16)
    kva_ref[:, 0:A_KV_W] = k
    kva_ref[:, A_KV_W:2 * A_KV_W] = v

    ssd_ref[...] = u[:, OFF_B:OFF_B + SSD_W]

    cq = _rope(u[:, OFF_C:OFF_C + 256], pc_ref[...], cos_c, sin_c)
    ck = _rope(u[:, OFF_C + 256:OFF_C + 512], pc_ref[...], cos_c, sin_c)
    cv = u[:, OFF_C + 512:OFF_C + 768]
    attc_ref[:, 0:256] = cq.astype(BF16)
    attc_ref[:, 256:512] = ck.astype(BF16)
    attc_ref[:, 512:768] = cv.astype(BF16)
    kvc_ref[:, 0:256] = ck
    kvc_ref[:, 256:512] = cv

    attd_ref[:, 0:256] = (u[:, OFF_D:OFF_D + 256] * q_scale).astype(BF16)
    attd_ref[:, 256:768] = u[:, OFF_D + 256:OFF_D + 768].astype(BF16)
    kvd_ref[...] = u[:, OFF_D + 256:OFF_D + 768]


def _in_proj(x, mod_tab, g, w1, rope_tab, bd, pa, pc, ex, gq, gk, n_prompt_tiles, tiles_per_sample):
    nt = x.shape[0]
    mrow = functools.partial(_mod_row, n_prompt_tiles=n_prompt_tiles, tiles_per_sample=tiles_per_sample)
    rrow = functools.partial(_rope_row, n_prompt_tiles=n_prompt_tiles, tiles_per_sample=tiles_per_sample)
    const = lambda *shape: pl.BlockSpec(shape, lambda i: (0,) * len(shape))
    rows = lambda w: pl.BlockSpec((TOKEN_TILE, w), lambda i: (i, 0))
    widths = (ATT_W, ATT_W, ATT_W, 2 * A_KV_W, 2 * BRANCH_W, 2 * BRANCH_W, SSD_W)
    dtypes = (BF16, BF16, BF16, F32, F32, F32, F32)
    return pl.pallas_call(
        _inproj_kernel,
        grid=(nt // TOKEN_TILE,),
        in_specs=[rows(D_MODEL),
                  pl.BlockSpec((None, 8, D_MODEL), lambda i: (mrow(i), 0, 0)),
                  const(1, D_MODEL), const(D_MODEL, W1_COLS),
                  pl.BlockSpec((TOKEN_TILE, 4 * BRANCH_W), lambda i: (rrow(i), 0)),
                  const(BRANCH_W, BRANCH_W), const(BRANCH_W, BRANCH_W), const(BRANCH_W, BRANCH_W),
                  const(A_KV_W, BRANCH_W), const(1, BRANCH_W), const(1, A_KV_W)],
        out_specs=[rows(w) for w in widths],
        out_shape=[jax.ShapeDtypeStruct((nt, w), dt) for w, dt in zip(widths, dtypes)],
        compiler_params=_cparams("parallel"),
        name="in_proj",
    )(x, mod_tab, g, w1, rope_tab, bd, pa, pc, ex, gq, gk)


def _rope_perm(dim):
    p = np.zeros((BRANCH_W, BRANCH_W), np.float32)
    s = dim // 4
    for j in range(BRANCH_W):
        quarter = (j % dim) // s
        if quarter % 2 == 0:
            p[j + s, j] = -1.0
        else:
            p[j - s, j] = 1.0
    return jnp.asarray(p, BF16)


def _rope_table(t, n_identity):
    pos = jnp.arange(t)
    rows = (pos // GRID_W).astype(F32)
    cols = (pos % GRID_W).astype(F32)
    parts = []
    for dim in (HEAD_W, C_QK_DIM):
        axis_dim = dim // 2
        inv = ROPE_THETA ** (-jnp.arange(0, axis_dim, 2, dtype=F32) / axis_dim)
        ang_r = rows[:, None] * inv[None, :]
        ang_c = cols[:, None] * inv[None, :]
        ang = jnp.concatenate([ang_r, ang_r, ang_c, ang_c], axis=-1)
        reps = BRANCH_W // dim
        parts += [jnp.tile(jnp.cos(ang), (1, reps)), jnp.tile(jnp.sin(ang), (1, reps))]
    tab = jnp.concatenate(parts, axis=1)
    ident = jnp.concatenate([jnp.ones((n_identity, BRANCH_W), F32), jnp.zeros((n_identity, BRANCH_W), F32)] * 2, axis=1)
    return jnp.concatenate([ident, tab], axis=0)


def _mha_kernel(lam_ref, q_ref, k_ref, v_ref, *rest, n_maps, scale, has_ctx, subln_scale):
    rest = list(rest)
    ck_ref, cv_ref = (rest.pop(0), rest.pop(0)) if has_ctx else (None, None)
    g_ref, bd_ref = (rest.pop(0), rest.pop(0)) if subln_scale is not None else (None, None)
    o_ref = rest.pop(0)
    q = q_ref[...]
    k = k_ref[...]
    v = v_ref[...]
    tq = q.shape[0]
    lane = lax.broadcasted_iota(jnp.int32, (1, BRANCH_W), 1)
    sub_w = HEAD_W // n_maps
    acc = jnp.zeros((tq, BRANCH_W), F32)
    for h in range(N_HEADS):
        oh = None
        for j in range(n_maps):
            qm = jnp.where((lane // sub_w) == (h * n_maps + j), q, jnp.zeros_like(q))
            s = _dot_nt(qm, k)
            if scale != 1.0:
                s = s * scale
            m = jnp.max(s, axis=-1, keepdims=True)
            if has_ctx:
                sc = _dot_nt(qm, ck_ref[...])
                if scale != 1.0:
                    sc = sc * scale
                m = jnp.maximum(m, jnp.max(sc, axis=-1, keepdims=True))
                pc = jnp.exp(sc - m)
            p = jnp.exp(s - m)
            l = jnp.sum(p, axis=-1, keepdims=True)
            pv = _dot(p.astype(BF16), v)
            if has_ctx:
                l = l + jnp.sum(pc, axis=-1, keepdims=True)
                pv = pv + _dot(pc.astype(BF16), cv_ref[...])
            pv = pv * (1.0 / l)
            oh = pv if j == 0 else oh - lam_ref[0] * pv
        acc = jnp.where((lane // HEAD_W) == h, oh, acc)
    if subln_scale is not None:
        acc = _head_rmsnorm(acc, bd_ref[...], g_ref[...]) * subln_scale
    o_ref[...] = acc.astype(o_ref.dtype)


def _mha(att, lam, *, n_seq, seq, row0, n_maps, scale, ctx=None, subln=None, tq=256):
    qb0 = row0 // tq
    kb0 = row0 // seq
    nq = seq // tq
    assert row0 % seq == 0 and seq % tq == 0
    in_specs = [pl.BlockSpec(memory_space=pltpu.SMEM),
                pl.BlockSpec((tq, BRANCH_W), lambda i, j: (qb0 + i * nq + j, 0)),
                pl.BlockSpec((seq, BRANCH_W), lambda i, j: (kb0 + i, 1)),
                pl.BlockSpec((seq, BRANCH_W), lambda i, j: (kb0 + i, 2))]
    args = [lam, att, att, att]
    kk = seq
    if ctx is not None:
        kc = ctx[0].shape[1]
        kk += kc
        in_specs += [pl.BlockSpec((None, kc, BRANCH_W), lambda i, j: (i, 0, 0))] * 2
        args += list(ctx)
    subln_scale = None
    if subln is not None:
        in_specs += [pl.BlockSpec((1, BRANCH_W), lambda i, j: (0, 0)),
                     pl.BlockSpec((BRANCH_W, BRANCH_W), lambda i, j: (0, 0))]
        args += [subln[0], subln[1]]
        subln_scale = subln[2]
    return pl.pallas_call(
        functools.partial(_mha_kernel, n_maps=n_maps, scale=scale, has_ctx=ctx is not None, subln_scale=subln_scale),
        grid=(n_seq, nq),
        in_specs=in_specs,
        out_specs=pl.BlockSpec((tq, BRANCH_W), lambda i, j: (i * nq + j, 0)),
        out_shape=jax.ShapeDtypeStruct((n_seq * seq, BRANCH_W), BF16),
        compiler_params=_cparams("parallel", "parallel"),
        name=f"mha_maps{n_maps}_k{kk}",
    )(*args)


def _na_row0(r, n_rows):
    return jnp.clip(r - NA_ROWS // 2, 0, n_rows - NA_ROWS)


def _na_kernel(q_ref, k_ref, v_ref, ck_ref, cv_ref, bias_ref, o_ref, *, n_rows):
    r = pl.program_id(1)
    start = pl.multiple_of(_na_row0(r, n_rows) * GRID_W, GRID_W)
    win = NA_ROWS * GRID_W
    q = q_ref[...]
    kw = k_ref[pl.ds(start, win), :]
    vw = v_ref[pl.ds(start, win), :]
    lane = lax.broadcasted_iota(jnp.int32, (1, BRANCH_W), 1)
    qs = jnp.concatenate(
        [jnp.where((lane // HEAD_W) == h, q, jnp.zeros_like(q)) for h in range(N_HEADS)], axis=0)
    s_loc = _dot_nt(qs, kw) + bias_ref[...]
    s_ctx = _dot_nt(qs, ck_ref[...])
    m = jnp.maximum(jnp.max(s_loc, axis=-1, keepdims=True), jnp.max(s_ctx, axis=-1, keepdims=True))
    p_loc = jnp.exp(s_loc - m)
    p_ctx = jnp.exp(s_ctx - m)
    l = jnp.sum(p_loc, axis=-1, keepdims=True) + jnp.sum(p_ctx, axis=-1, keepdims=True)
    o = (_dot(p_loc.astype(BF16), vw) + _dot(p_ctx.astype(BF16), cv_ref[...])) * (1.0 / l)
    acc = jnp.zeros((GRID_W, BRANCH_W), F32)
    for h in range(N_HEADS):
        acc = jnp.where((lane // HEAD_W) == h, o[h * GRID_W:(h + 1) * GRID_W], acc)
    o_ref[...] = acc.astype(o_ref.dtype)


def _na_attention(att, ck, cv, bias_tab, *, n_seq, seq, row0):
    n_rows = seq // GRID_W
    kk = ck.shape[1]
    win = NA_ROWS * GRID_W
    qb0 = row0 // GRID_W
    kb0 = row0 // seq
    assert row0 % seq == 0
    return pl.pallas_call(
        functools.partial(_na_kernel, n_rows=n_rows),
        grid=(n_seq, n_rows),
        in_specs=[pl.BlockSpec((GRID_W, BRANCH_W), lambda i, r: (qb0 + i * n_rows + r, 0)),
                  pl.BlockSpec((seq, BRANCH_W), lambda i, r: (kb0 + i, 1)),
                  pl.BlockSpec((seq, BRANCH_W), lambda i, r: (kb0 + i, 2)),
                  pl.BlockSpec((None, kk, BRANCH_W), lambda i, r: (i, 0, 0)),
                  pl.BlockSpec((None, kk, BRANCH_W), lambda i, r: (i, 0, 0)),
                  pl.BlockSpec((None, N_HEADS * GRID_W, win), lambda i, r: (r - _na_row0(r, n_rows), 0, 0))],
        out_specs=pl.BlockSpec((GRID_W, BRANCH_W), lambda i, r: (i * n_rows + r, 0)),
        out_shape=jax.ShapeDtypeStruct((n_seq * seq, BRANCH_W), BF16),
        compiler_params=_cparams("parallel", "parallel"),
        name="na_attention",
    )(att, att, att, ck, cv, bias_tab)


def _na_bias_table(rpb, n_rows):
    var = np.arange(NA_ROWS)[:, None, None]
    j = np.arange(NA_ROWS)[None, :, None]
    sel_r = (np.arange(2 * NA_ROWS - 1)[None, None, :] == j - var + (NA_ROWS - 1)).astype(np.float32)
    col = np.arange(GRID_W)[:, None, None]
    kc = np.arange(GRID_W)[None, :, None]
    c0 = np.clip(col - NA_COLS // 2, 0, GRID_W - NA_COLS)
    valid = ((kc >= c0) & (kc < c0 + NA_COLS))[:, :, 0]
    sel_c = (np.arange(2 * NA_COLS - 1)[None, None, :] == kc - col + (NA_COLS - 1)).astype(np.float32)
    tab = jnp.einsum('hrc,vjr,xkc->vhxjk', rpb.astype(F32), sel_r, sel_c, precision=lax.Precision.HIGHEST)
    tab = jnp.where(valid[None, None, :, None, :], tab, NEG)
    return tab.reshape(NA_ROWS, N_HEADS * GRID_W, NA_ROWS * GRID_W)


def _softplus(x):
    return jnp.maximum(x, 0.0) + jnp.log1p(jnp.exp(-jnp.abs(x)))


def _expand_heads(colmat, d, lane256):
    out = jnp.zeros((colmat.shape[0], BRANCH_W), F32)
    for h in range(B_HEADS):
        j = d * B_HEADS + h
        out = jnp.where((lane256 // HEAD_W) == h,
                        jnp.broadcast_to(colmat[:, j:j + 1], (colmat.shape[0], BRANCH_W)), out)
    return out


def _ssd_kernel(xbc_ref, z_ref, dt_ref, cw_ref, cb_ref, pc_ref, dsk_ref, ng_ref, init_ref,
                o_ref, st_ref, xpad_s, xc_s, yf_s, s_s, *, seq):
    n_chunks = seq // B_CHUNK
    L = B_CHUNK
    ri = lax.broadcasted_iota(jnp.int32, (L, L), 0)
    ci = lax.broadcasted_iota(jnp.int32, (L, L), 1)
    low = ci <= ri
    upp = ri <= ci
    low_b = jnp.where(low, 1.0, 0.0).astype(BF16)
    upp_b = jnp.where(upp, 1.0, 0.0).astype(BF16)
    lane128 = lax.broadcasted_iota(jnp.int32, (1, LANES), 1)
    lane256 = lax.broadcasted_iota(jnp.int32, (1, BRANCH_W), 1)
    row256 = lax.broadcasted_iota(jnp.int32, (BRANCH_W, 1), 0)
    blockmask = (row256 // (2 * HEAD_W)) == (lane128 // B_STATE)

    xpad_s[0:CONV_PAD, :] = jnp.zeros((CONV_PAD, B_XBC), F32)
    xpad_s[seq + CONV_PAD:seq + 2 * CONV_PAD, :] = jnp.zeros((CONV_PAD, B_XBC), F32)

    def pad_body(c, carry):
        base = pl.multiple_of(c * L, L)
        xpad_s[pl.ds(pl.multiple_of(base + CONV_PAD, CONV_PAD), L), :] = xbc_ref[pl.ds(base, L), :]
        return carry

    lax.fori_loop(0, n_chunks, pad_body, 0)
    cw = cw_ref[...]
    cb = cb_ref[...]

    def conv_body(c, carry):
        base = pl.multiple_of(c * L, L)
        w = xpad_s[pl.ds(base, L + 2 * CONV_PAD), :]
        acc = jnp.zeros((L, B_XBC), F32) + cb
        for kk in range(B_CONV_W):
            off = CONV_PAD - B_CONV_W // 2 + kk
            acc = acc + w[off:off + L, :] * cw[kk:kk + 1, :]
        xc_s[pl.ds(base, L), :] = acc * _sigmoid(acc)
        return carry

    lax.fori_loop(0, n_chunks, conv_body, 0)

    dt_bias = pc_ref[0:1, :]
    a_neg = pc_ref[1:2, :]
    dsk = dsk_ref[...]
    ng = ng_ref[...]

    def chunk(c, d):
        base = pl.multiple_of(c * L, L)
        xs = xc_s[pl.ds(base, L), 0:BRANCH_W]
        bm = xc_s[pl.ds(base, L), BRANCH_W:BRANCH_W + LANES].astype(BF16)
        cm = xc_s[pl.ds(base, L), BRANCH_W + LANES:B_XBC].astype(BF16)
        dtc = _softplus(dt_ref[pl.ds(base, L), :] + dt_bias)
        da_c = dtc * a_neg
        da_r = da_c.T
        tri_c = low_b if d == 0 else upp_b
        tri_r = upp_b if d == 0 else low_b
        c1, c2, c3 = _split3(da_c)
        cum_c = _dot(tri_c, c1) + _dot(tri_c, c2) + _dot(tri_c, c3)
        r1, r2, r3 = _split3(da_r)
        cum_r = _dot(r1, tri_r) + _dot(r2, tri_r) + _dot(r3, tri_r)
        cum_end = cum_c[L - 1:L, :] if d == 0 else cum_c[0:1, :]
        dmask = low if d == 0 else upp

        g0 = _dot_nt(jnp.where(lane128 < B_STATE, cm, jnp.zeros_like(cm)), bm)
        g1 = _dot_nt(jnp.where(lane128 >= B_STATE, cm, jnp.zeros_like(cm)), bm)
        dt_x = _expand_heads(dtc, d, lane256)
        e_a = _expand_heads(jnp.exp(cum_c), d, lane256)
        t_e = _expand_heads(jnp.exp(cum_end - cum_c), d, lane256)
        xdt = xs * dt_x
        xdt_b = xdt.astype(BF16)
        y = jnp.zeros((L, BRANCH_W), F32)
        for h in range(B_HEADS):
            j = d * B_HEADS + h
            col = jnp.broadcast_to(cum_c[:, j:j + 1], (L, L))
            row = jnp.broadcast_to(cum_r[j:j + 1, :], (L, L))
            dec = jnp.exp(jnp.where(dmask, col - row, NEG))
            sc = ((g0 if h < 2 else g1) * dec).astype(BF16)
            y = jnp.where((lane256 // HEAD_W) == h, _dot(sc, xdt_b), y)
        state = s_s[...]
        y = y + _dot_nt(cm, state.astype(BF16)) * e_a
        zmat = _dot_tn((xdt * t_e).astype(BF16), bm)
        e_end = jnp.exp(cum_end)
        cd = jnp.zeros((BRANCH_W, LANES), F32)
        for h in range(B_HEADS):
            j = d * B_HEADS + h
            cd = jnp.where((row256 // HEAD_W) == h, jnp.broadcast_to(e_end[:, j:j + 1], (BRANCH_W, LANES)), cd)
        s_s[...] = state * cd + jnp.where(blockmask, zmat, 0.0)
        return base, xs, y

    s_s[...] = init_ref[0]

    def fwd_body(c, carry):
        base, _, y = chunk(c, 0)
        yf_s[pl.ds(base, L), :] = y
        return carry

    lax.fori_loop(0, n_chunks, fwd_body, 0)
    st_ref[0] = s_s[...]
    s_s[...] = init_ref[1]

    def bwd_body(i, carry):
        c = n_chunks - 1 - i
        base, xs, y = chunk(c, 1)
        z = z_ref[pl.ds(base, L), :]
        yt = (yf_s[pl.ds(base, L), :] + y + dsk * xs) * (z * _sigmoid(z))
        ms = jnp.mean(yt * yt, axis=-1, keepdims=True)
        o_ref[pl.ds(base, L), :] = (yt * lax.rsqrt(ms + EPS) * ng).astype(o_ref.dtype)
        return carry

    lax.fori_loop(0, n_chunks, bwd_body, 0)
    st_ref[1] = s_s[...]


def _ssd_branch(ssd_in, conv_w, conv_b, pc, dsk, ng, init, *, n_seq, seq, row0):
    b0 = row0 // seq
    assert row0 % seq == 0
    full = lambda *shape: pl.BlockSpec(shape, lambda i: (0,) * len(shape))
    return pl.pallas_call(
        functools.partial(_ssd_kernel, seq=seq),
        grid=(n_seq,),
        in_specs=[pl.BlockSpec((seq, B_XBC), lambda i: (b0 + i, 0)),
                  pl.BlockSpec((seq, BRANCH_W), lambda i: (b0 + i, B_XBC // BRANCH_W)),
                  pl.BlockSpec((seq, LANES), lambda i: (b0 + i, (B_XBC + BRANCH_W) // LANES)),
                  full(8, B_XBC), full(1, B_XBC), full(8, LANES), full(1, BRANCH_W), full(1, BRANCH_W),
                  pl.BlockSpec((None, 2, BRANCH_W, LANES), lambda i: (i, 0, 0, 0))],
        out_specs=[pl.BlockSpec((seq, BRANCH_W), lambda i: (i, 0)),
                   pl.BlockSpec((None, 2, BRANCH_W, LANES), lambda i: (i, 0, 0, 0))],
        out_shape=[jax.ShapeDtypeStruct((n_seq * seq, BRANCH_W), BF16),
                   jax.ShapeDtypeStruct((n_seq, 2, BRANCH_W, LANES), F32)],
        scratch_shapes=[pltpu.VMEM((seq + 2 * CONV_PAD, B_XBC), F32), pltpu.VMEM((seq, B_XBC), F32),
                        pltpu.VMEM((seq, BRANCH_W), F32), pltpu.VMEM((BRANCH_W, LANES), F32)],
        compiler_params=_cparams("parallel"),
        name=f"ssd_t{seq}",
    )(ssd_in, ssd_in, ssd_in, conv_w, conv_b, pc, dsk, ng, init)


def _merge_kernel(x_ref, mod_ref, g1_ref, g2_ref, *rest, n_prompt_tiles):
    br_p = rest[0:N_BRANCH]
    br_s = rest[N_BRANCH:2 * N_BRANCH]
    wg_ref, wb_ref, wo_ref, rwh_ref, rwl_ref, rb_ref, x1_ref, h2_ref, gate_ref, eidx_ref = rest[2 * N_BRANCH:]
    is_prompt = pl.program_id(0) < n_prompt_tiles
    x = x_ref[...]
    mod = mod_ref[...]
    h = _norm_mod(x, g1_ref[...], mod[0:1], mod[1:2]).astype(BF16)
    merged = None
    for i in range(N_BRANCH):
        gate = _sigmoid(_dot(h, wg_ref[:, i * D_MODEL:(i + 1) * D_MODEL]))
        br = jnp.where(is_prompt, br_p[i][...], br_s[i][...])
        proj = _dot(br, wb_ref[i])
        merged = gate * proj if i == 0 else merged + gate * proj
    y = _dot(merged.astype(BF16), wo_ref[...])
    x1 = x + mod[2:3] * y
    x1_ref[...] = x1
    h2 = _norm_mod(x1, g2_ref[...], mod[3:4], mod[4:5])
    hh, hl = _split2(h2)
    for j in range(ROW_TILE):
        h2_ref[pl.ds(j, TOKEN_TILE, stride=ROW_TILE), :] = h2[:, j * LANES:(j + 1) * LANES]
    rwh = rwh_ref[...]
    logits = _dot(hh, rwh) + _dot(hl, rwh) + _dot(hh, rwl_ref[...]) + rb_ref[...]
    lane = lax.broadcasted_iota(jnp.int32, (1, LANES), 1).astype(F32)
    cur = jnp.where(lane < N_EXPERTS, logits, NEG)
    vals, idxs = [], []
    for _ in range(TOP_K):
        m = jnp.max(cur, axis=-1, keepdims=True)
        am = jnp.min(jnp.where(cur == m, lane, float(LANES)), axis=-1, keepdims=True)
        vals.append(m)
        idxs.append(am)
        cur = jnp.where(lane == am, NEG, cur)
    exps = [jnp.exp(v - vals[0]) for v in vals]
    inv = 1.0 / (exps[0] + exps[1] + exps[2] + exps[3])
    gate = jnp.zeros_like(logits)
    eidx = jnp.zeros_like(logits)
    for k in range(TOP_K):
        gate = jnp.where(lane == k, exps[k] * inv, gate)
        eidx = jnp.where(lane == k, idxs[k], eidx)
    gate_ref[...] = gate
    eidx_ref[...] = eidx.astype(jnp.int32)


def _merge(x, mod_tab, g1, g2, br_p, br_s, wg, wb, wo, rwh, rwl, rb, n_prompt_tiles, tiles_per_sample):
    nt = x.shape[0]
    mrow = functools.partial(_mod_row, n_prompt_tiles=n_prompt_tiles, tiles_per_sample=tiles_per_sample)
    const = lambda *shape: pl.BlockSpec(shape, lambda i: (0,) * len(shape))
    rows = lambda w: pl.BlockSpec((TOKEN_TILE, w), lambda i: (i, 0))
    p_rows = pl.BlockSpec((TOKEN_TILE, BRANCH_W), lambda i: (jnp.minimum(i, n_prompt_tiles - 1), 0))
    s_rows = pl.BlockSpec((TOKEN_TILE, BRANCH_W), lambda i: (jnp.maximum(i - n_prompt_tiles, 0), 0))
    return pl.pallas_call(
        functools.partial(_merge_kernel, n_prompt_tiles=n_prompt_tiles),
        grid=(nt // TOKEN_TILE,),
        in_specs=[rows(D_MODEL),
                  pl.BlockSpec((None, 8, D_MODEL), lambda i: (mrow(i), 0, 0)),
                  const(1, D_MODEL), const(1, D_MODEL)]
                 + [p_rows] * N_BRANCH + [s_rows] * N_BRANCH
                 + [const(D_MODEL, N_BRANCH * D_MODEL), const(N_BRANCH, BRANCH_W, D_MODEL),
                    const(D_MODEL, D_MODEL), const(D_MODEL, LANES), const(D_MODEL, LANES), const(1, LANES)],
        out_specs=[rows(D_MODEL), pl.BlockSpec((TOKEN_TILE * ROW_TILE, LANES), lambda i: (i, 0)),
                   rows(LANES), rows(LANES)],
        out_shape=[jax.ShapeDtypeStruct((nt, D_MODEL), F32),
                   jax.ShapeDtypeStruct((nt * ROW_TILE, LANES), F32),
                   jax.ShapeDtypeStruct((nt, LANES), F32),
                   jax.ShapeDtypeStruct((nt, LANES), jnp.int32)],
        compiler_params=_cparams("parallel"),
        name="merge",
    )(x, mod_tab, g1, g2, *br_p, *br_s, wg, wb, wo, rwh, rwl, rb)


MOE_GROUPS = 8
MOE_CHUNK = 2 * D_FF // MOE_GROUPS


def _moe_step(i, last, tok_ref, tokn_ref, slotp_ref, slot_ref, h_hbm, b1_ref, b2_ref, out_hbm, w1_s, w2_s,
              x_cur, x_nxt, y_cur, y_prev, g_cur, g_nxt, s_cur, s_prev):
    block_sublanes = MOE_ROWS * ROW_TILE

    def tile(ref, t):
        return ref.at[pl.ds(pl.multiple_of(t * ROW_TILE, ROW_TILE), ROW_TILE), :]

    def gather_copy(tref, buf, sem, r):
        return pltpu.make_async_copy(tile(h_hbm, tref[0, r]), buf.at[pl.ds(r * ROW_TILE, ROW_TILE), :], sem)

    def gather_all(buf, sem):
        return pltpu.make_async_copy(h_hbm.at[pl.ds(0, block_sublanes), :], buf, sem)

    def scatter_copy(sref, buf, sem, r):
        return pltpu.make_async_copy(buf.at[pl.ds(r * ROW_TILE, ROW_TILE), :], tile(out_hbm, sref[0, r]), sem)

    def scatter_all(buf, sem):
        return pltpu.make_async_copy(buf, out_hbm.at[pl.ds(0, block_sublanes), :], sem)

    @pl.when(i == 0)
    def _():
        y_prev[...] = jnp.zeros_like(y_prev)
        for r in range(MOE_ROWS):
            gather_copy(tok_ref, x_cur, g_cur, r).start()

    @pl.when(i >= 1)
    def _():
        scatter_all(y_cur, s_cur).wait()

    gather_all(x_cur, g_cur).wait()
    x = jnp.concatenate([x_cur[pl.ds(j, MOE_ROWS, stride=ROW_TILE), :] for j in range(ROW_TILE)],
                        axis=1).astype(BF16)
    per = MOE_ROWS // MOE_GROUPS
    acts = []
    for c in range(MOE_GROUPS // 2):
        halves = []
        for part in range(2):
            grp = 2 * c + part
            for r in range(grp * per, (grp + 1) * per):
                gather_copy(tokn_ref, x_nxt, g_nxt, r).start()
                scatter_copy(slotp_ref, y_prev, s_prev, r).start(priority=1)
            lo = part * D_FF + c * MOE_CHUNK
            halves.append(_dot(x, w1_s[:, lo:lo + MOE_CHUNK]) + b1_ref[:, lo:lo + MOE_CHUNK])
        glu = jnp.minimum(halves[0], SWIGLU_LIMIT)
        lin = jnp.clip(halves[1], -SWIGLU_LIMIT, SWIGLU_LIMIT)
        acts.append((glu * _sigmoid(SWIGLU_ALPHA * glu) * (lin + 1.0)).astype(BF16))
    act = jnp.concatenate(acts, axis=1)
    y = _dot(act, w2_s[...]) + b2_ref[...]
    for j in range(ROW_TILE):
        y_cur[pl.ds(j, MOE_ROWS, stride=ROW_TILE), :] = y[:, j * LANES:(j + 1) * LANES]

    @pl.when(i == last)
    def _():
        for r in range(MOE_ROWS):
            scatter_copy(slot_ref, y_cur, s_cur, r).start()
        gather_all(x_nxt, g_nxt).wait()
        scatter_all(y_prev, s_prev).wait()
        scatter_all(y_cur, s_cur).wait()


def _moe_kernel(be_ref, nu_ref, tok_ref, tokn_ref, slotp_ref, slot_ref, h_hbm, w1_ref, b1_ref, w2_ref, b2_ref,
                out_hbm, w1_s, w2_s, xa, xb, ya, yb, zbuf, gsem, ssem, zsem):
    i = pl.program_id(0)
    n_used = nu_ref[0]
    prev = be_ref[jnp.maximum(i - 1, 0)]

    @pl.when(jnp.logical_and(i < n_used, jnp.logical_or(i == 0, be_ref[i] != prev)))
    def _():
        w1_s[...] = w1_ref[...].astype(BF16)
        w2_s[...] = w2_ref[...].astype(BF16)

    common = (i, n_used - 1, tok_ref, tokn_ref, slotp_ref, slot_ref, h_hbm, b1_ref, b2_ref, out_hbm, w1_s, w2_s)

    @pl.when(jnp.logical_and(i < n_used, i % 2 == 0))
    def _():
        _moe_step(*common, xa, xb, ya, yb, gsem.at[0], gsem.at[1], ssem.at[0], ssem.at[1])

    @pl.when(jnp.logical_and(i < n_used, i % 2 == 1))
    def _():
        _moe_step(*common, xb, xa, yb, ya, gsem.at[1], gsem.at[0], ssem.at[1], ssem.at[0])

    @pl.when(i >= n_used)
    def _():
        zbuf[...] = jnp.zeros_like(zbuf)
        first = pl.multiple_of(slot_ref[0, 0] * ROW_TILE, ROW_TILE)
        fill = pltpu.make_async_copy(zbuf, out_hbm.at[pl.ds(first, MOE_ROWS * ROW_TILE), :], zsem.at[0])
        fill.start()
        fill.wait()


def _moe_experts(block_expert, n_used, row_token, row_slot, h2, w1, b1, w2, b2, layer):
    n_blocks = row_token.shape[0]
    n_rows = n_blocks * MOE_ROWS
    spare = (n_rows + jnp.arange(MOE_ROWS, dtype=jnp.int32)).reshape(1, 1, MOE_ROWS)
    slot_prev = jnp.concatenate([spare, row_slot[:-1]], axis=0)
    smem_rows = lambda imap: pl.BlockSpec((None, 1, MOE_ROWS), imap, memory_space=pltpu.SMEM)
    expert_block = lambda *shape: pl.BlockSpec((None, None) + shape, lambda i, be, nu: (layer, be[i], 0, 0))
    row_buf = pltpu.VMEM((MOE_ROWS * ROW_TILE, LANES), F32)
    grid_spec = pltpu.PrefetchScalarGridSpec(
        num_scalar_prefetch=2,
        grid=(n_blocks,),
        in_specs=[smem_rows(lambda i, be, nu: (i, 0, 0)),
                  smem_rows(lambda i, be, nu: (jnp.minimum(i + 1, n_blocks - 1), 0, 0)),
                  smem_rows(lambda i, be, nu: (i, 0, 0)),
                  smem_rows(lambda i, be, nu: (i, 0, 0)),
                  pl.BlockSpec(memory_space=pl.ANY),
                  expert_block(D_MODEL, 2 * D_FF), expert_block(1, 2 * D_FF),
                  expert_block(D_FF, D_MODEL), expert_block(1, D_MODEL)],
        out_specs=pl.BlockSpec(memory_space=pl.ANY),
        scratch_shapes=[pltpu.VMEM((D_MODEL, 2 * D_FF), BF16), pltpu.VMEM((D_FF, D_MODEL), BF16),
                        row_buf, row_buf, row_buf, row_buf, row_buf,
                        pltpu.SemaphoreType.DMA((2,)), pltpu.SemaphoreType.DMA((2,)),
                        pltpu.SemaphoreType.DMA((1,))],
    )
    return pl.pallas_call(
        _moe_kernel,
        grid_spec=grid_spec,
        out_shape=jax.ShapeDtypeStruct(((n_rows + MOE_ROWS) * ROW_TILE, LANES), F32),
        compiler_params=pltpu.CompilerParams(dimension_semantics=("arbitrary",), vmem_limit_bytes=VMEM_LIMIT,
                                             has_side_effects=True),
        name="moe_experts",
    )(block_expert, n_used, row_token, row_token, slot_prev, row_slot, h2, w1, b1, w2, b2)


def _moe_dispatch(eidx):
    n = eidx.shape[0]
    n_assign = n * TOP_K
    n_blocks = -(-n_assign // MOE_ROWS) + N_EXPERTS
    id_bits = (n_assign - 1).bit_length()
    assert N_EXPERTS << id_bits < 2 ** 31
    expert = eidx.reshape(-1)
    ids = jnp.arange(n_assign, dtype=jnp.int32)
    order = jnp.sort((expert << id_bits) | ids) & ((1 << id_bits) - 1)
    counts = jnp.sum((expert[:, None] == jnp.arange(N_EXPERTS, dtype=jnp.int32)[None, :]).astype(jnp.int32), axis=0)
    starts = jnp.cumsum(counts) - counts
    padded = (counts + MOE_ROWS - 1) // MOE_ROWS * MOE_ROWS
    pad_ends = jnp.cumsum(padded)
    pad_starts = pad_ends - padded
    block_start = jnp.arange(n_blocks, dtype=jnp.int32) * MOE_ROWS
    block_expert = jnp.minimum(jnp.sum((pad_ends[None, :] <= block_start[:, None]).astype(jnp.int32), axis=1),
                               N_EXPERTS - 1).astype(jnp.int32)
    first_row = block_start - pad_starts[block_expert]
    n_real = counts[block_expert] - first_row
    window_start = jnp.clip(starts[block_expert] + first_row, 0, n_assign)
    order_pad = jnp.concatenate([order, jnp.zeros((MOE_ROWS,), jnp.int32)])
    assign = jax.vmap(lambda s: lax.dynamic_slice(order_pad, (s,), (MOE_ROWS,)))(window_start)
    r = jnp.arange(MOE_ROWS, dtype=jnp.int32)[None, :]
    real = r < n_real[:, None]
    real_before = (starts + counts)[block_expert]
    filler = n_assign + block_start[:, None] + r - real_before[:, None]
    row_slot = jnp.where(real, (assign % TOP_K) * n + assign // TOP_K, filler).astype(jnp.int32)
    row_token = jnp.where(real, assign // TOP_K, 0).astype(jnp.int32)
    n_used = (pad_ends[-1] // MOE_ROWS).astype(jnp.int32).reshape(1)
    return (block_expert, n_used, row_token.reshape(n_blocks, 1, MOE_ROWS),
            row_slot.reshape(n_blocks, 1, MOE_ROWS))


def _combine_kernel(x1_ref, mod_ref, gate_ref, y0_ref, y1_ref, y2_ref, y3_ref, fg_ref, o_ref, *, final):
    gate = gate_ref[...]
    y = None
    for k, yk_ref in enumerate((y0_ref, y1_ref, y2_ref, y3_ref)):
        rows = jnp.concatenate([yk_ref[pl.ds(j, TOKEN_TILE, stride=ROW_TILE), :] for j in range(ROW_TILE)], axis=1)
        yk = gate[:, k:k + 1] * rows
        y = yk if k == 0 else y + yk
    x2 = x1_ref[...] + mod_ref[5:6, :] * y
    if final:
        ms = jnp.mean(x2 * x2, axis=-1, keepdims=True)
        x2 = x2 * lax.rsqrt(ms + EPS) * fg_ref[...]
    o_ref[...] = x2


def _combine(x1, mod_tab, gate, y_slots, final_g, n_prompt_tiles, tiles_per_sample, *, final):
    nt = x1.shape[0]
    tiles = nt // TOKEN_TILE
    mrow = functools.partial(_mod_row, n_prompt_tiles=n_prompt_tiles, tiles_per_sample=tiles_per_sample)
    slot_rows = lambda k: pl.BlockSpec((TOKEN_TILE * ROW_TILE, LANES), lambda i: (k * tiles + i, 0))
    return pl.pallas_call(
        functools.partial(_combine_kernel, final=final),
        grid=(tiles,),
        in_specs=[pl.BlockSpec((TOKEN_TILE, D_MODEL), lambda i: (i, 0)),
                  pl.BlockSpec((None, 8, D_MODEL), lambda i: (mrow(i), 0, 0)),
                  pl.BlockSpec((TOKEN_TILE, LANES), lambda i: (i, 0))]
                 + [slot_rows(k) for k in range(TOP_K)]
                 + [pl.BlockSpec((1, D_MODEL), lambda i: (0, 0))],
        out_specs=pl.BlockSpec((TOKEN_TILE, D_MODEL), lambda i: (i, 0)),
        out_shape=jax.ShapeDtypeStruct((nt, D_MODEL), F32),
        compiler_params=_cparams("parallel"),
        name="combine_final" if final else "combine",
    )(x1, mod_tab, gate, y_slots, y_slots, y_slots, y_slots, final_g)


def _state_to_blocks(init):
    b = init.shape[0]
    st = init.reshape(b, 2, B_HEADS * HEAD_W, B_STATE)
    row = jnp.arange(B_HEADS * HEAD_W)[:, None]
    left = jnp.where(row < 2 * HEAD_W, st, 0.0)
    right = jnp.where(row >= 2 * HEAD_W, st, 0.0)
    return jnp.concatenate([left, right], axis=-1)


def _blocks_to_state(s):
    b = s.shape[0]
    row = jnp.arange(B_HEADS * HEAD_W)[:, None]
    st = jnp.where(row < 2 * HEAD_W, s[..., :B_STATE], s[..., B_STATE:])
    return st.reshape(b, 2, B_HEADS, HEAD_W, B_STATE)


def kernel(x_prompt, x_sample, c, cache_a_k, cache_a_v, cache_c_k, cache_c_v, cache_d_k, cache_d_v, state_ssd, c_ctx, norm1_g, norm2_g, w_mod, b_mod, w_in, a_q_g, a_k_g, b_conv_w, b_conv_b, b_dt_bias, b_a_log, b_d, b_norm_g, c_lam, c_subln_g, d_rpb, w_branch, w_out, router_w, router_b, moe_w1, moe_b1, moe_w2, moe_b2, final_g):
    bp, tp, _ = x_prompt.shape
    bs, ts, _ = x_sample.shape
    depth = w_in.shape[0]
    n_p = bp * tp
    n_s = bs * ts
    assert n_p % TOKEN_TILE == 0 and ts % TOKEN_TILE == 0 and n_p % ts == 0
    n_prompt_tiles = n_p // TOKEN_TILE
    tiles_per_sample = ts // TOKEN_TILE
    past = cache_a_k.shape[2]

    x = jnp.concatenate([x_prompt.reshape(n_p, D_MODEL), x_sample.reshape(n_s, D_MODEL)], axis=0)
    cvec = jnp.zeros((16, D_MODEL), F32).at[0].set(c_ctx).at[1:1 + bs].set(c)
    rope_tab = _rope_table(ts, TOKEN_TILE)
    perm_a = _rope_perm(HEAD_W)
    perm_c = _rope_perm(C_QK_DIM)
    head_of = np.arange(BRANCH_W) // HEAD_W
    bd = jnp.asarray((head_of[:, None] == head_of[None, :]) / HEAD_W, BF16)
    kv_of = np.arange(A_KV_W) // HEAD_W
    ex = jnp.asarray((kv_of[:, None] == (head_of // (N_HEADS // A_KV_HEADS))[None, :])
                     & ((np.arange(A_KV_W) % HEAD_W)[:, None] == (np.arange(BRANCH_W) % HEAD_W)[None, :]), BF16)
    scale_c = C_QK_DIM ** -0.5
    one = jnp.ones((1,), F32)
    rep = N_HEADS // A_KV_HEADS
    zero_state = jnp.zeros((bp, 2, BRANCH_W, LANES), F32)

    new_cache = [[] for _ in range(7)]
    for l in range(depth):
        lam_init = 0.8 - 0.6 * math.exp(-0.3 * l)
        lq = c_lam[l]
        lam = (jnp.exp(jnp.sum(lq[0] * lq[1])) - jnp.exp(jnp.sum(lq[2] * lq[3])) + lam_init).reshape(1)

        wl = w_in[l]
        w1 = jnp.concatenate([wl[:, 0:512], wl[:, 768:1280], wl[:, 512:768], wl[:, 1280:1288],
                              jnp.zeros((D_MODEL, LANES - 8), F32), wl[:, 1288:2824]], axis=1).astype(BF16)
        wg = wl[:, 2824:].astype(BF16)
        wb = w_branch[l].astype(BF16)
        wo = w_out[l].astype(BF16)
        rw = jnp.pad(router_w[l], ((0, 0), (0, LANES - N_EXPERTS)))
        rwh = rw.astype(BF16)
        rwl = (rw - rwh.astype(F32)).astype(BF16)
        rb = jnp.pad(router_b[l], (0, LANES - N_EXPERTS)).reshape(1, LANES)

        mod = _modulation(cvec, w_mod[l], b_mod[l].reshape(1, -1))
        mod_tab = jnp.pad(mod.reshape(16, 6, D_MODEL), ((0, 0), (0, 2), (0, 0)))

        g1 = norm1_g[l].reshape(1, D_MODEL)
        g2 = norm2_g[l].reshape(1, D_MODEL)
        att_a, att_c, att_d, kv_a, kv_c, kv_d, ssd_in = _in_proj(
            x, mod_tab, g1, w1, rope_tab, bd, perm_a, perm_c, ex,
            jnp.tile(a_q_g[l], N_HEADS).reshape(1, BRANCH_W), jnp.tile(a_k_g[l], A_KV_HEADS).reshape(1, A_KV_W),
            n_prompt_tiles, tiles_per_sample)

        ctx_ak = jnp.repeat(cache_a_k[:, l], rep, axis=2).reshape(bs, past, BRANCH_W).astype(BF16)
        ctx_av = jnp.repeat(cache_a_v[:, l], rep, axis=2).reshape(bs, past, BRANCH_W).astype(BF16)
        o_a_p = _mha(att_a, one, n_seq=bp, seq=tp, row0=0, n_maps=1, scale=1.0)
        o_a_s = _mha(att_a, one, n_seq=bs, seq=ts, row0=n_p, n_maps=1, scale=1.0, ctx=(ctx_ak, ctx_av))

        subln = (jnp.tile(c_subln_g[l], N_HEADS).reshape(1, BRANCH_W), bd, 1.0 - lam_init)
        ctx_ck = cache_c_k[:, l].reshape(bs, past, BRANCH_W).astype(BF16)
        ctx_cv = cache_c_v[:, l].reshape(bs, past, BRANCH_W).astype(BF16)
        o_c_p = _mha(att_c, lam, n_seq=bp, seq=tp, row0=0, n_maps=2, scale=scale_c, subln=subln)
        o_c_s = _mha(att_c, lam, n_seq=bs, seq=ts, row0=n_p, n_maps=2, scale=scale_c, ctx=(ctx_ck, ctx_cv),
                     subln=subln)

        o_d_p = _mha(att_d, one, n_seq=bp, seq=tp, row0=0, n_maps=1, scale=1.0)
        o_d_s = _na_attention(att_d, cache_d_k[:, l].reshape(bs, past, BRANCH_W).astype(BF16),
                              cache_d_v[:, l].reshape(bs, past, BRANCH_W).astype(BF16),
                              _na_bias_table(d_rpb[l], ts // GRID_W), n_seq=bs, seq=ts, row0=n_p)

        dtb = b_dt_bias[l].reshape(8)
        a_neg = -jnp.exp(b_a_log[l].reshape(8))
        ssd_args = (jnp.pad(b_conv_w[l], ((0, 8 - B_CONV_W), (0, 0))), b_conv_b[l].reshape(1, B_XBC),
                    jnp.zeros((8, LANES), F32).at[0, :8].set(dtb).at[1, :8].set(a_neg),
                    jnp.repeat(b_d[l], HEAD_W).reshape(1, BRANCH_W), b_norm_g[l].reshape(1, BRANCH_W))
        o_b_p, st_p = _ssd_branch(ssd_in, *ssd_args, zero_state, n_seq=bp, seq=tp, row0=0)
        o_b_s, _ = _ssd_branch(ssd_in, *ssd_args, _state_to_blocks(state_ssd[:, l]), n_seq=bs, seq=ts, row0=n_p)

        x1, h2, gate, eidx = _merge(x, mod_tab, g1, g2, (o_a_p, o_b_p, o_c_p, o_d_p), (o_a_s, o_b_s, o_c_s, o_d_s),
                                    wg, wb, wo, rwh, rwl, rb, n_prompt_tiles, tiles_per_sample)
        block_expert, n_used, row_token, row_slot = _moe_dispatch(eidx[:, :TOP_K])
        y_slots = _moe_experts(block_expert, n_used, row_token, row_slot, h2, moe_w1,
                               moe_b1.reshape(depth, N_EXPERTS, 1, -1), moe_w2,
                               moe_b2.reshape(depth, N_EXPERTS, 1, -1), l)
        x = _combine(x1, mod_tab, gate, y_slots, final_g.reshape(1, D_MODEL), n_prompt_tiles, tiles_per_sample,
                     final=(l == depth - 1))

        new_cache[0].append(kv_a[:n_p, :A_KV_W].reshape(bp, tp, A_KV_HEADS, HEAD_W))
        new_cache[1].append(kv_a[:n_p, A_KV_W:].reshape(bp, tp, A_KV_HEADS, HEAD_W))
        new_cache[2].append(kv_c[:n_p, :BRANCH_W].reshape(bp, tp, N_HEADS, 2 * C_QK_DIM))
        new_cache[3].append(kv_c[:n_p, BRANCH_W:].reshape(bp, tp, N_HEADS, HEAD_W))
        new_cache[4].append(kv_d[:n_p, :BRANCH_W].reshape(bp, tp, N_HEADS, HEAD_W))
        new_cache[5].append(kv_d[:n_p, BRANCH_W:].reshape(bp, tp, N_HEADS, HEAD_W))
        new_cache[6].append(_blocks_to_state(st_p))

    y_prompt = x[:n_p].reshape(bp, tp, D_MODEL)
    y_sample = x[n_p:].reshape(bs, ts, D_MODEL)
    return (y_prompt, y_sample) + tuple(jnp.stack(v, axis=1) for v in new_cache)
```

```python
import functools
import math

import jax
import jax.numpy as jnp
import numpy as np
from jax import lax
from jax.experimental import pallas as pl
from jax.experimental.pallas import tpu as pltpu

F32 = jnp.float32
BF16 = jnp.bfloat16

D_MODEL = 1024
N_BRANCH = 4
BRANCH_W = 256
HEAD_W = 64
N_HEADS = 4
GRID_W = 64
ROPE_THETA = 10000.0
EPS = 1e-6
A_KV_HEADS = 2
A_KV_W = A_KV_HEADS * HEAD_W
C_QK_DIM = 32
B_HEADS = 4
B_STATE = 64
B_CHUNK = 128
B_XBC = 512
B_CONV_W = 5
CONV_PAD = 8
NA_ROWS = 8
NA_COLS = 16
N_EXPERTS = 32
TOP_K = 4
D_FF = 1024
SWIGLU_LIMIT = 7.0
SWIGLU_ALPHA = 1.702
LANES = 128
ROW_TILE = D_MODEL // LANES
NEG = -1e30

TOKEN_TILE = 512
MOE_ROWS = 256
ATT_W = 3 * BRANCH_W
SSD_W = B_XBC + BRANCH_W + LANES
OFF_A, OFF_B, OFF_C, OFF_D = 0, 512, 512 + SSD_W, 512 + SSD_W + ATT_W
W1_COLS = OFF_D + ATT_W
VMEM_LIMIT = 56 * 1024 * 1024


def _cparams(*sem):
    return pltpu.CompilerParams(dimension_semantics=sem, vmem_limit_bytes=VMEM_LIMIT)


def _dot(a, b):
    return jnp.dot(a, b, preferred_element_type=F32)


def _dot_nt(a, b):
    return lax.dot_general(a, b, (((1,), (1,)), ((), ())), preferred_element_type=F32)


def _dot_tn(a, b):
    return lax.dot_general(a, b, (((0,), (0,)), ((), ())), preferred_element_type=F32)


def _sigmoid(x):
    return 0.5 * jnp.tanh(0.5 * x) + 0.5


def _split2(x):
    hi = x.astype(BF16)
    return hi, (x - hi.astype(F32)).astype(BF16)


def _split3(x):
    h1 = x.astype(BF16)
    r1 = x - h1.astype(F32)
    h2 = r1.astype(BF16)
    h3 = (r1 - h2.astype(F32)).astype(BF16)
    return h1, h2, h3


def _norm_mod(x, g, shift, scale):
    ms = jnp.mean(x * x, axis=-1, keepdims=True)
    return (x * lax.rsqrt(ms + EPS)) * g * (1.0 + scale) + shift


def _head_rmsnorm(x, bd, g):
    hi, lo = _split2(x * x)
    ms = _dot(hi, bd) + _dot(lo, bd)
    return x * lax.rsqrt(ms + EPS) * g


def _rope(x, perm, cos, sin):
    hi, lo = _split2(x)
    return x * cos + (_dot(hi, perm) + _dot(lo, perm)) * sin


def _mod_kernel(c_ref, w_ref, b_ref, o_ref):
    c = c_ref[...]
    s = (c * _sigmoid(c)).astype(BF16)
    o_ref[...] = _dot(s, w_ref[...].astype(BF16)) + b_ref[...]


def _modulation(cvec, w_mod, b_mod):
    n = w_mod.shape[1]
    tn = 1536
    return pl.pallas_call(
        _mod_kernel,
        grid=(n // tn,),
        in_specs=[pl.BlockSpec((16, D_MODEL), lambda j: (0, 0)),
                  pl.BlockSpec((D_MODEL, tn), lambda j: (0, j)),
                  pl.BlockSpec((1, tn), lambda j: (0, j))],
        out_specs=pl.BlockSpec((16, tn), lambda j: (0, j)),
        out_shape=jax.ShapeDtypeStruct((16, n), F32),
        compiler_params=_cparams("parallel"),
        name="modulation",
    )(cvec, w_mod, b_mod)


def _mod_row(i, n_prompt_tiles, tiles_per_sample):
    return jnp.where(i < n_prompt_tiles, 0, 1 + (i - n_prompt_tiles) // tiles_per_sample)


def _rope_row(i, n_prompt_tiles, tiles_per_sample):
    return jnp.where(i < n_prompt_tiles, 0, 1 + (i - n_prompt_tiles) % tiles_per_sample)


def _inproj_kernel(x_ref, mod_ref, g_ref, w_ref, rope_ref, bd_ref, pa_ref, pc_ref, ex_ref, gq_ref, gk_ref,
                   atta_ref, attc_ref, attd_ref, kva_ref, kvc_ref, kvd_ref, ssd_ref):
    mod = mod_ref[...]
    h = _norm_mod(x_ref[...], g_ref[...], mod[0:1], mod[1:2])
    u = _dot(h.astype(BF16), w_ref[...])
    cos_a, sin_a = rope_ref[:, 0:256], rope_ref[:, 256:512]
    cos_c, sin_c = rope_ref[:, 512:768], rope_ref[:, 768:1024]
    q_scale = HEAD_W ** -0.5

    q = _rope(_head_rmsnorm(u[:, OFF_A:OFF_A + 256], bd_ref[...], gq_ref[...]), pa_ref[...], cos_a, sin_a)
    k = _rope(_head_rmsnorm(u[:, OFF_A + 256:OFF_A + 384], bd_ref[0:A_KV_W, 0:A_KV_W], gk_ref[...]),
              pa_ref[0:A_KV_W, 0:A_KV_W], cos_a[:, 0:A_KV_W], sin_a[:, 0:A_KV_W])
    v = u[:, OFF_A + 384:OFF_A + 512]
    ex = ex_ref[...]
    atta_ref[:, 0:256] = (q * q_scale).astype(BF16)
    atta_ref[:, 256:512] = _dot(k.astype(BF16), ex).astype(BF16)
    atta_ref[:, 512:768] = _dot(v.astype(BF16), ex).astype(BF16)
    kva_ref[:, 0:A_KV_W] = k
    kva_ref[:, A_KV_W:2 * A_KV_W] = v

    ssd_ref[...] = u[:, OFF_B:OFF_B + SSD_W]

    cq = _rope(u[:, OFF_C:OFF_C + 256], pc_ref[...], cos_c, sin_c)
    ck = _rope(u[:, OFF_C + 256:OFF_C + 512], pc_ref[...], cos_c, sin_c)
    cv = u[:, OFF_C + 512:OFF_C + 768]
    attc_ref[:, 0:256] = cq.astype(BF16)
    attc_ref[:, 256:512] = ck.astype(BF16)
    attc_ref[:, 512:768] = cv.astype(BF16)
    kvc_ref[:, 0:256] = ck
    kvc_ref[:, 256:512] = cv

    attd_ref[:, 0:256] = (u[:, OFF_D:OFF_D + 256] * q_scale).astype(BF16)
    attd_ref[:, 256:768] = u[:, OFF_D + 256:OFF_D + 768].astype(BF16)
    kvd_ref[...] = u[:, OFF_D + 256:OFF_D + 768]


def _in_proj(x, mod_tab, g, w1, rope_tab, bd, pa, pc, ex, gq, gk, n_prompt_tiles, tiles_per_sample):
    nt = x.shape[0]
    mrow = functools.partial(_mod_row, n_prompt_tiles=n_prompt_tiles, tiles_per_sample=tiles_per_sample)
    rrow = functools.partial(_rope_row, n_prompt_tiles=n_prompt_tiles, tiles_per_sample=tiles_per_sample)
    const = lambda *shape: pl.BlockSpec(shape, lambda i: (0,) * len(shape))
    rows = lambda w: pl.BlockSpec((TOKEN_TILE, w), lambda i: (i, 0))
    widths = (ATT_W, ATT_W, ATT_W, 2 * A_KV_W, 2 * BRANCH_W, 2 * BRANCH_W, SSD_W)
    dtypes = (BF16, BF16, BF16, F32, F32, F32, F32)
    return pl.pallas_call(
        _inproj_kernel,
        grid=(nt // TOKEN_TILE,),
        in_specs=[rows(D_MODEL),
                  pl.BlockSpec((None, 8, D_MODEL), lambda i: (mrow(i), 0, 0)),
                  const(1, D_MODEL), const(D_MODEL, W1_COLS),
                  pl.BlockSpec((TOKEN_TILE, 4 * BRANCH_W), lambda i: (rrow(i), 0)),
                  const(BRANCH_W, BRANCH_W), const(BRANCH_W, BRANCH_W), const(BRANCH_W, BRANCH_W),
                  const(A_KV_W, BRANCH_W), const(1, BRANCH_W), const(1, A_KV_W)],
        out_specs=[rows(w) for w in widths],
        out_shape=[jax.ShapeDtypeStruct((nt, w), dt) for w, dt in zip(widths, dtypes)],
        compiler_params=_cparams("parallel"),
        name="in_proj",
    )(x, mod_tab, g, w1, rope_tab, bd, pa, pc, ex, gq, gk)


def _rope_perm(dim):
    p = np.zeros((BRANCH_W, BRANCH_W), np.float32)
    s = dim // 4
    for j in range(BRANCH_W):
        quarter = (j % dim) // s
        if quarter % 2 == 0:
            p[j + s, j] = -1.0
        else:
            p[j - s, j] = 1.0
    return jnp.asarray(p, BF16)


def _rope_table(t, n_identity):
    pos = jnp.arange(t)
    rows = (pos // GRID_W).astype(F32)
    cols = (pos % GRID_W).astype(F32)
    parts = []
    for dim in (HEAD_W, C_QK_DIM):
        axis_dim = dim // 2
        inv = ROPE_THETA ** (-jnp.arange(0, axis_dim, 2, dtype=F32) / axis_dim)
        ang_r = rows[:, None] * inv[None, :]
        ang_c = cols[:, None] * inv[None, :]
        ang = jnp.concatenate([ang_r, ang_r, ang_c, ang_c], axis=-1)
        reps = BRANCH_W // dim
        parts += [jnp.tile(jnp.cos(ang), (1, reps)), jnp.tile(jnp.sin(ang), (1, reps))]
    tab = jnp.concatenate(parts, axis=1)
    ident = jnp.concatenate([jnp.ones((n_identity, BRANCH_W), F32), jnp.zeros((n_identity, BRANCH_W), F32)] * 2, axis=1)
    return jnp.concatenate([ident, tab], axis=0)


def _mha_kernel(lam_ref, q_ref, k_ref, v_ref, *rest, n_maps, scale, has_ctx, subln_scale):
    rest = list(rest)
    ck_ref, cv_ref = (rest.pop(0), rest.pop(0)) if has_ctx else (None, None)
    g_ref, bd_ref = (rest.pop(0), rest.pop(0)) if subln_scale is not None else (None, None)
    o_ref = rest.pop(0)
    q = q_ref[...]
    k = k_ref[...]
    v = v_ref[...]
    tq = q.shape[0]
    lane = lax.broadcasted_iota(jnp.int32, (1, BRANCH_W), 1)
    sub_w = HEAD_W // n_maps
    acc = jnp.zeros((tq, BRANCH_W), F32)
    for h in range(N_HEADS):
        oh = None
        for j in range(n_maps):
            qm = jnp.where((lane // sub_w) == (h * n_maps + j), q, jnp.zeros_like(q))
            s = _dot_nt(qm, k)
            if scale != 1.0:
                s = s * scale
            m = jnp.max(s, axis=-1, keepdims=True)
            if has_ctx:
                sc = _dot_nt(qm, ck_ref[...])
                if scale != 1.0:
                    sc = sc * scale
                m = jnp.maximum(m, jnp.max(sc, axis=-1, keepdims=True))
                pc = jnp.exp(sc - m)
            p = jnp.exp(s - m)
            l = jnp.sum(p, axis=-1, keepdims=True)
            pv = _dot(p.astype(BF16), v)
            if has_ctx:
                l = l + jnp.sum(pc, axis=-1, keepdims=True)
                pv = pv + _dot(pc.astype(BF16), cv_ref[...])
            pv = pv * (1.0 / l)
            oh = pv if j == 0 else oh - lam_ref[0] * pv
        acc = jnp.where((lane // HEAD_W) == h, oh, acc)
    if subln_scale is not None:
        acc = _head_rmsnorm(acc, bd_ref[...], g_ref[...]) * subln_scale
    o_ref[...] = acc.astype(o_ref.dtype)


def _mha(att, lam, *, n_seq, seq, row0, n_maps, scale, ctx=None, subln=None, tq=256):
    qb0 = row0 // tq
    kb0 = row0 // seq
    nq = seq // tq
    assert row0 % seq == 0 and seq % tq == 0
    in_specs = [pl.BlockSpec(memory_space=pltpu.SMEM),
                pl.BlockSpec((tq, BRANCH_W), lambda i, j: (qb0 + i * nq + j, 0)),
                pl.BlockSpec((seq, BRANCH_W), lambda i, j: (kb0 + i, 1)),
                pl.BlockSpec((seq, BRANCH_W), lambda i, j: (kb0 + i, 2))]
    args = [lam, att, att, att]
    kk = seq
    if ctx is not None:
        kc = ctx[0].shape[1]
        kk += kc
        in_specs += [pl.BlockSpec((None, kc, BRANCH_W), lambda i, j: (i, 0, 0))] * 2
        args += list(ctx)
    subln_scale = None
    if subln is not None:
        in_specs += [pl.BlockSpec((1, BRANCH_W), lambda i, j: (0, 0)),
                     pl.BlockSpec((BRANCH_W, BRANCH_W), lambda i, j: (0, 0))]
        args += [subln[0], subln[1]]
        subln_scale = subln[2]
    return pl.pallas_call(
        functools.partial(_mha_kernel, n_maps=n_maps, scale=scale, has_ctx=ctx is not None, subln_scale=subln_scale),
        grid=(n_seq, nq),
        in_specs=in_specs,
        out_specs=pl.BlockSpec((tq, BRANCH_W), lambda i, j: (i * nq + j, 0)),
        out_shape=jax.ShapeDtypeStruct((n_seq * seq, BRANCH_W), BF16),
        compiler_params=_cparams("parallel", "parallel"),
        name=f"mha_maps{n_maps}_k{kk}",
    )(*args)


def _na_row0(r, n_rows):
    return jnp.clip(r - NA_ROWS // 2, 0, n_rows - NA_ROWS)


def _na_kernel(q_ref, k_ref, v_ref, ck_ref, cv_ref, bias_ref, o_ref, *, n_rows):
    win = NA_ROWS * GRID_W
    lane = lax.broadcasted_iota(jnp.int32, (1, BRANCH_W), 1)
    ck = ck_ref[...]
    cv = cv_ref[...]

    def row_body(r, carry):
        r0 = _na_row0(r, n_rows)
        start = pl.multiple_of(r0 * GRID_W, GRID_W)
        rows = pl.ds(pl.multiple_of(r * GRID_W, GRID_W), GRID_W)
        q = q_ref[rows, :]
        kw = k_ref[pl.ds(start, win), :]
        vw = v_ref[pl.ds(start, win), :]
        qs = jnp.concatenate(
            [jnp.where((lane // HEAD_W) == h, q, jnp.zeros_like(q)) for h in range(N_HEADS)], axis=0)
        s_loc = _dot_nt(qs, kw) + bias_ref[r - r0]
        s_ctx = _dot_nt(qs, ck)
        m = jnp.maximum(jnp.max(s_loc, axis=-1, keepdims=True), jnp.max(s_ctx, axis=-1, keepdims=True))
        p_loc = jnp.exp(s_loc - m)
        p_ctx = jnp.exp(s_ctx - m)
        l = jnp.sum(p_loc, axis=-1, keepdims=True) + jnp.sum(p_ctx, axis=-1, keepdims=True)
        o = (_dot(p_loc.astype(BF16), vw) + _dot(p_ctx.astype(BF16), cv)) * (1.0 / l)
        acc = jnp.zeros((GRID_W, BRANCH_W), F32)
        for h in range(N_HEADS):
            acc = jnp.where((lane // HEAD_W) == h, o[h * GRID_W:(h + 1) * GRID_W], acc)
        o_ref[rows, :] = acc.astype(o_ref.dtype)
        return carry

    lax.fori_loop(0, n_rows, row_body, 0)


def _na_attention(att, ck, cv, bias_tab, *, n_seq, seq, row0):
    n_rows = seq // GRID_W
    kk = ck.shape[1]
    win = NA_ROWS * GRID_W
    b0 = row0 // seq
    assert row0 % seq == 0
    return pl.pallas_call(
        functools.partial(_na_kernel, n_rows=n_rows),
        grid=(n_seq,),
        in_specs=[pl.BlockSpec((seq, BRANCH_W), lambda i: (b0 + i, 0)),
                  pl.BlockSpec((seq, BRANCH_W), lambda i: (b0 + i, 1)),
                  pl.BlockSpec((seq, BRANCH_W), lambda i: (b0 + i, 2)),
                  pl.BlockSpec((None, kk, BRANCH_W), lambda i: (i, 0, 0)),
                  pl.BlockSpec((None, kk, BRANCH_W), lambda i: (i, 0, 0)),
                  pl.BlockSpec((NA_ROWS, N_HEADS * GRID_W, win), lambda i: (0, 0, 0))],
        out_specs=pl.BlockSpec((seq, BRANCH_W), lambda i: (i, 0)),
        out_shape=jax.ShapeDtypeStruct((n_seq * seq, BRANCH_W), BF16),
        compiler_params=_cparams("parallel"),
        name="na_attention",
    )(att, att, att, ck, cv, bias_tab)


def _na_bias_table(rpb, n_rows):
    var = np.arange(NA_ROWS)[:, None, None]
    j = np.arange(NA_ROWS)[None, :, None]
    sel_r = (np.arange(2 * NA_ROWS - 1)[None, None, :] == j - var + (NA_ROWS - 1)).astype(np.float32)
    col = np.arange(GRID_W)[:, None, None]
    kc = np.arange(GRID_W)[None, :, None]
    c0 = np.clip(col - NA_COLS // 2, 0, GRID_W - NA_COLS)
    valid = ((kc >= c0) & (kc < c0 + NA_COLS))[:, :, 0]
    sel_c = (np.arange(2 * NA_COLS - 1)[None, None, :] == kc - col + (NA_COLS - 1)).astype(np.float32)
    tab = jnp.einsum('hrc,vjr,xkc->vhxjk', rpb.astype(F32), sel_r, sel_c, precision=lax.Precision.HIGHEST)
    tab = jnp.where(valid[None, None, :, None, :], tab, NEG)
    return tab.reshape(NA_ROWS, N_HEADS * GRID_W, NA_ROWS * GRID_W)


def _softplus(x):
    return jnp.maximum(x, 0.0) + jnp.log1p(jnp.exp(-jnp.abs(x)))


def _expand_heads(colmat, d, lane256):
    out = jnp.zeros((colmat.shape[0], BRANCH_W), F32)
    for h in range(B_HEADS):
        j = d * B_HEADS + h
        out = jnp.where((lane256 // HEAD_W) == h,
                        jnp.broadcast_to(colmat[:, j:j + 1], (colmat.shape[0], BRANCH_W)), out)
    return out


def _ssd_kernel(xbc_ref, z_ref, dt_ref, cw_ref, cb_ref, pc_ref, dsk_ref, ng_ref, init_ref,
                o_ref, st_ref, xpad_s, xc_s, yf_s, s_s, *, seq):
    n_chunks = seq // B_CHUNK
    L = B_CHUNK
    ri = lax.broadcasted_iota(jnp.int32, (L, L), 0)
    ci = lax.broadcasted_iota(jnp.int32, (L, L), 1)
    low = ci <= ri
    upp = ri <= ci
    low_b = jnp.where(low, 1.0, 0.0).astype(BF16)
    upp_b = jnp.where(upp, 1.0, 0.0).astype(BF16)
    lane128 = lax.broadcasted_iota(jnp.int32, (1, LANES), 1)
    lane256 = lax.broadcasted_iota(jnp.int32, (1, BRANCH_W), 1)
    row256 = lax.broadcasted_iota(jnp.int32, (BRANCH_W, 1), 0)
    blockmask = (row256 // (2 * HEAD_W)) == (lane128 // B_STATE)

    xpad_s[0:CONV_PAD, :] = jnp.zeros((CONV_PAD, B_XBC), F32)
    xpad_s[seq + CONV_PAD:seq + 2 * CONV_PAD, :] = jnp.zeros((CONV_PAD, B_XBC), F32)

    def pad_body(c, carry):
        base = pl.multiple_of(c * L, L)
        xpad_s[pl.ds(pl.multiple_of(base + CONV_PAD, CONV_PAD), L), :] = xbc_ref[pl.ds(base, L), :]
        return carry

    lax.fori_loop(0, n_chunks, pad_body, 0)
    cw = cw_ref[...]
    cb = cb_ref[...]

    def conv_body(c, carry):
        base = pl.multiple_of(c * L, L)
        w = xpad_s[pl.ds(base, L + 2 * CONV_PAD), :]
        acc = jnp.zeros((L, B_XBC), F32) + cb
        for kk in range(B_CONV_W):
            off = CONV_PAD - B_CONV_W // 2 + kk
            acc = acc + w[off:off + L, :] * cw[kk:kk + 1, :]
        xc_s[pl.ds(base, L), :] = acc * _sigmoid(acc)
        return carry

    lax.fori_loop(0, n_chunks, conv_body, 0)

    dt_bias = pc_ref[0:1, :]
    a_neg = pc_ref[1:2, :]
    dsk = dsk_ref[...]
    ng = ng_ref[...]

    def chunk(c, d):
        base = pl.multiple_of(c * L, L)
        xs = xc_s[pl.ds(base, L), 0:BRANCH_W]
        bm = xc_s[pl.ds(base, L), BRANCH_W:BRANCH_W + LANES].astype(BF16)
        cm = xc_s[pl.ds(base, L), BRANCH_W + LANES:B_XBC].astype(BF16)
        dtc = _softplus(dt_ref[pl.ds(base, L), :] + dt_bias)
        da_c = dtc * a_neg
        da_r = da_c.T
        tri_c = low_b if d == 0 else upp_b
        tri_r = upp_b if d == 0 else low_b
        c1, c2, c3 = _split3(da_c)
        cum_c = _dot(tri_c, c1) + _dot(tri_c, c2) + _dot(tri_c, c3)
        r1, r2, r3 = _split3(da_r)
        cum_r = _dot(r1, tri_r) + _dot(r2, tri_r) + _dot(r3, tri_r)
        cum_end = cum_c[L - 1:L, :] if d == 0 else cum_c[0:1, :]
        dmask = low if d == 0 else upp

        g0 = _dot_nt(jnp.where(lane128 < B_STATE, cm, jnp.zeros_like(cm)), bm)
        g1 = _dot_nt(jnp.where(lane128 >= B_STATE, cm, jnp.zeros_like(cm)), bm)
        dt_x = _expand_heads(dtc, d, lane256)
        e_a = _expand_heads(jnp.exp(cum_c), d, lane256)
        t_e = _expand_heads(jnp.exp(cum_end - cum_c), d, lane256)
        xdt = xs * dt_x
        xdt_b = xdt.astype(BF16)
        y = jnp.zeros((L, BRANCH_W), F32)
        for h in range(B_HEADS):
            j = d * B_HEADS + h
            col = jnp.broadcast_to(cum_c[:, j:j + 1], (L, L))
            row = jnp.broadcast_to(cum_r[j:j + 1, :], (L, L))
            dec = jnp.exp(jnp.where(dmask, col - row, NEG))
            sc = ((g0 if h < 2 else g1) * dec).astype(BF16)
            y = jnp.where((lane256 // HEAD_W) == h, _dot(sc, xdt_b), y)
        state = s_s[...]
        y = y + _dot_nt(cm, state.astype(BF16)) * e_a
        zmat = _dot_tn((xdt * t_e).astype(BF16), bm)
        e_end = jnp.exp(cum_end)
        cd = jnp.zeros((BRANCH_W, LANES), F32)
        for h in range(B_HEADS):
            j = d * B_HEADS + h
            cd = jnp.where((row256 // HEAD_W) == h, jnp.broadcast_to(e_end[:, j:j + 1], (BRANCH_W, LANES)), cd)
        s_s[...] = state * cd + jnp.where(blockmask, zmat, 0.0)
        return base, xs, y

    s_s[...] = init_ref[0]

    def fwd_body(c, carry):
        base, _, y = chunk(c, 0)
        yf_s[pl.ds(base, L), :] = y
        return carry

    lax.fori_loop(0, n_chunks, fwd_body, 0)
    st_ref[0] = s_s[...]
    s_s[...] = init_ref[1]

    def bwd_body(i, carry):
        c = n_chunks - 1 - i
        base, xs, y = chunk(c, 1)
        z = z_ref[pl.ds(base, L), :]
        yt = (yf_s[pl.ds(base, L), :] + y + dsk * xs) * (z * _sigmoid(z))
        ms = jnp.mean(yt * yt, axis=-1, keepdims=True)
        o_ref[pl.ds(base, L), :] = (yt * lax.rsqrt(ms + EPS) * ng).astype(o_ref.dtype)
        return carry

    lax.fori_loop(0, n_chunks, bwd_body, 0)
    st_ref[1] = s_s[...]


def _ssd_branch(ssd_in, conv_w, conv_b, pc, dsk, ng, init, *, n_seq, seq, row0):
    b0 = row0 // seq
    assert row0 % seq == 0
    full = lambda *shape: pl.BlockSpec(shape, lambda i: (0,) * len(shape))
    return pl.pallas_call(
        functools.partial(_ssd_kernel, seq=seq),
        grid=(n_seq,),
        in_specs=[pl.BlockSpec((seq, B_XBC), lambda i: (b0 + i, 0)),
                  pl.BlockSpec((seq, BRANCH_W), lambda i: (b0 + i, B_XBC // BRANCH_W)),
                  pl.BlockSpec((seq, LANES), lambda i: (b0 + i, (B_XBC + BRANCH_W) // LANES)),
                  full(8, B_XBC), full(1, B_XBC), full(8, LANES), full(1, BRANCH_W), full(1, BRANCH_W),
                  pl.BlockSpec((None, 2, BRANCH_W, LANES), lambda i: (i, 0, 0, 0))],
        out_specs=[pl.BlockSpec((seq, BRANCH_W), lambda i: (i, 0)),
                   pl.BlockSpec((None, 2, BRANCH_W, LANES), lambda i: (i, 0, 0, 0))],
        out_shape=[jax.ShapeDtypeStruct((n_seq * seq, BRANCH_W), BF16),
                   jax.ShapeDtypeStruct((n_seq, 2, BRANCH_W, LANES), F32)],
        scratch_shapes=[pltpu.VMEM((seq + 2 * CONV_PAD, B_XBC), F32), pltpu.VMEM((seq, B_XBC), F32),
                        pltpu.VMEM((seq, BRANCH_W), F32), pltpu.VMEM((BRANCH_W, LANES), F32)],
        compiler_params=_cparams("parallel"),
        name=f"ssd_t{seq}",
    )(ssd_in, ssd_in, ssd_in, conv_w, conv_b, pc, dsk, ng, init)


def _merge_kernel(x_ref, mod_ref, g1_ref, g2_ref, *rest, n_prompt_tiles):
    br_p = rest[0:N_BRANCH]
    br_s = rest[N_BRANCH:2 * N_BRANCH]
    wg_ref, wb_ref, wo_ref, rwh_ref, rwl_ref, rb_ref, x1_ref, h2_ref, gate_ref, eidx_ref = rest[2 * N_BRANCH:]
    is_prompt = pl.program_id(0) < n_prompt_tiles
    x = x_ref[...]
    mod = mod_ref[...]
    h = _norm_mod(x, g1_ref[...], mod[0:1], mod[1:2]).astype(BF16)
    merged = None
    for i in range(N_BRANCH):
        gate = _sigmoid(_dot(h, wg_ref[:, i * D_MODEL:(i + 1) * D_MODEL]))
        br = jnp.where(is_prompt, br_p[i][...], br_s[i][...])
        proj = _dot(br, wb_ref[i])
        merged = gate * proj if i == 0 else merged + gate * proj
    y = _dot(merged.astype(BF16), wo_ref[...])
    x1 = x + mod[2:3] * y
    x1_ref[...] = x1
    h2 = _norm_mod(x1, g2_ref[...], mod[3:4], mod[4:5])
    hh, hl = _split2(h2)
    for j in range(ROW_TILE):
        h2_ref[pl.ds(j, TOKEN_TILE, stride=ROW_TILE), :] = h2[:, j * LANES:(j + 1) * LANES]
    rwh = rwh_ref[...]
    logits = _dot(hh, rwh) + _dot(hl, rwh) + _dot(hh, rwl_ref[...]) + rb_ref[...]
    lane = lax.broadcasted_iota(jnp.int32, (1, LANES), 1).astype(F32)
    cur = jnp.where(lane < N_EXPERTS, logits, NEG)
    vals, idxs = [], []
    for _ in range(TOP_K):
        m = jnp.max(cur, axis=-1, keepdims=True)
        am = jnp.min(jnp.where(cur == m, lane, float(LANES)), axis=-1, keepdims=True)
        vals.append(m)
        idxs.append(am)
        cur = jnp.where(lane == am, NEG, cur)
    exps = [jnp.exp(v - vals[0]) for v in vals]
    inv = 1.0 / (exps[0] + exps[1] + exps[2] + exps[3])
    gate = jnp.zeros_like(logits)
    eidx = jnp.zeros_like(logits)
    for k in range(TOP_K):
        gate = jnp.where(lane == k, exps[k] * inv, gate)
        eidx = jnp.where(lane == k, idxs[k], eidx)
    gate_ref[...] = gate
    eidx_ref[...] = eidx.astype(jnp.int32)


def _merge(x, mod_tab, g1, g2, br_p, br_s, wg, wb, wo, rwh, rwl, rb, n_prompt_tiles, tiles_per_sample):
    nt = x.shape[0]
    mrow = functools.partial(_mod_row, n_prompt_tiles=n_prompt_tiles, tiles_per_sample=tiles_per_sample)
    const = lambda *shape: pl.BlockSpec(shape, lambda i: (0,) * len(shape))
    rows = lambda w: pl.BlockSpec((TOKEN_TILE, w), lambda i: (i, 0))
    p_rows = pl.BlockSpec((TOKEN_TILE, BRANCH_W), lambda i: (jnp.minimum(i, n_prompt_tiles - 1), 0))
    s_rows = pl.BlockSpec((TOKEN_TILE, BRANCH_W), lambda i: (jnp.maximum(i - n_prompt_tiles, 0), 0))
    return pl.pallas_call(
        functools.partial(_merge_kernel, n_prompt_tiles=n_prompt_tiles),
        grid=(nt // TOKEN_TILE,),
        in_specs=[rows(D_MODEL),
                  pl.BlockSpec((None, 8, D_MODEL), lambda i: (mrow(i), 0, 0)),
                  const(1, D_MODEL), const(1, D_MODEL)]
                 + [p_rows] * N_BRANCH + [s_rows] * N_BRANCH
                 + [const(D_MODEL, N_BRANCH * D_MODEL), const(N_BRANCH, BRANCH_W, D_MODEL),
                    const(D_MODEL, D_MODEL), const(D_MODEL, LANES), const(D_MODEL, LANES), const(1, LANES)],
        out_specs=[rows(D_MODEL), pl.BlockSpec((TOKEN_TILE * ROW_TILE, LANES), lambda i: (i, 0)),
                   rows(LANES), rows(LANES)],
        out_shape=[jax.ShapeDtypeStruct((nt, D_MODEL), F32),
                   jax.ShapeDtypeStruct((nt * ROW_TILE, LANES), F32),
                   jax.ShapeDtypeStruct((nt, LANES), F32),
                   jax.ShapeDtypeStruct((nt, LANES), jnp.int32)],
        compiler_params=_cparams("parallel"),
        name="merge",
    )(x, mod_tab, g1, g2, *br_p, *br_s, wg, wb, wo, rwh, rwl, rb)


MOE_GROUPS = 8
MOE_CHUNK = 2 * D_FF // MOE_GROUPS


def _moe_step(i, last, tok_ref, tokn_ref, slotp_ref, slot_ref, h_hbm, b1_ref, b2_ref, out_hbm, w1_s, w2_s,
              x_cur, x_nxt, y_cur, y_prev, g_cur, g_nxt, s_cur, s_prev):
    block_sublanes = MOE_ROWS * ROW_TILE

    def tile(ref, t):
        return ref.at[pl.ds(pl.multiple_of(t * ROW_TILE, ROW_TILE), ROW_TILE), :]

    def gather_copy(tref, buf, sem, r):
        return pltpu.make_async_copy(tile(h_hbm, tref[0, r]), buf.at[pl.ds(r * ROW_TILE, ROW_TILE), :], sem)

    def gather_all(buf, sem):
        return pltpu.make_async_copy(h_hbm.at[pl.ds(0, block_sublanes), :], buf, sem)

    def scatter_copy(sref, buf, sem, r):
        return pltpu.make_async_copy(buf.at[pl.ds(r * ROW_TILE, ROW_TILE), :], tile(out_hbm, sref[0, r]), sem)

    def scatter_all(buf, sem):
        return pltpu.make_async_copy(buf, out_hbm.at[pl.ds(0, block_sublanes), :], sem)

    @pl.when(i == 0)
    def _():
        y_prev[...] = jnp.zeros_like(y_prev)
        for r in range(MOE_ROWS):
            gather_copy(tok_ref, x_cur, g_cur, r).start()

    @pl.when(i >= 1)
    def _():
        scatter_all(y_cur, s_cur).wait()

    gather_all(x_cur, g_cur).wait()
    x = jnp.concatenate([x_cur[pl.ds(j, MOE_ROWS, stride=ROW_TILE), :] for j in range(ROW_TILE)],
                        axis=1).astype(BF16)
    per = MOE_ROWS // MOE_GROUPS
    acts = []
    for c in range(MOE_GROUPS // 2):
        halves = []
        for part in range(2):
            grp = 2 * c + part
            for r in range(grp * per, (grp + 1) * per):
                gather_copy(tokn_ref, x_nxt, g_nxt, r).start()
                scatter_copy(slotp_ref, y_prev, s_prev, r).start(priority=1)
            lo = part * D_FF + c * MOE_CHUNK
            halves.append(_dot(x, w1_s[:, lo:lo + MOE_CHUNK]) + b1_ref[:, lo:lo + MOE_CHUNK])
        glu = jnp.minimum(halves[0], SWIGLU_LIMIT)
        lin = jnp.clip(halves[1], -SWIGLU_LIMIT, SWIGLU_LIMIT)
        acts.append((glu * _sigmoid(SWIGLU_ALPHA * glu) * (lin + 1.0)).astype(BF16))
    act = jnp.concatenate(acts, axis=1)
    y = _dot(act, w2_s[...]) + b2_ref[...]
    for j in range(ROW_TILE):
        y_cur[pl.ds(j, MOE_ROWS, stride=ROW_TILE), :] = y[:, j * LANES:(j + 1) * LANES]

    @pl.when(i == last)
    def _():
        for r in range(MOE_ROWS):
            scatter_copy(slot_ref, y_cur, s_cur, r).start()
        gather_all(x_nxt, g_nxt).wait()
        scatter_all(y_prev, s_prev).wait()
        scatter_all(y_cur, s_cur).wait()


def _moe_kernel(be_ref, nu_ref, tok_ref, tokn_ref, slotp_ref, slot_ref, h_hbm, w1_ref, b1_ref, w2_ref, b2_ref,
                out_hbm, w1_s, w2_s, xa, xb, ya, yb, zbuf, gsem, ssem, zsem):
    i = pl.program_id(0)
    n_used = nu_ref[0]
    prev = be_ref[jnp.maximum(i - 1, 0)]

    @pl.when(jnp.logical_and(i < n_used, jnp.logical_or(i == 0, be_ref[i] != prev)))
    def _():
        w1_s[...] = w1_ref[...].astype(BF16)
        w2_s[...] = w2_ref[...].astype(BF16)

    common = (i, n_used - 1, tok_ref, tokn_ref, slotp_ref, slot_ref, h_hbm, b1_ref, b2_ref, out_hbm, w1_s, w2_s)

    @pl.when(jnp.logical_and(i < n_used, i % 2 == 0))
    def _():
        _moe_step(*common, xa, xb, ya, yb, gsem.at[0], gsem.at[1], ssem.at[0], ssem.at[1])

    @pl.when(jnp.logical_and(i < n_used, i % 2 == 1))
    def _():
        _moe_step(*common, xb, xa, yb, ya, gsem.at[1], gsem.at[0], ssem.at[1], ssem.at[0])

    @pl.when(i >= n_used)
    def _():
        zbuf[...] = jnp.zeros_like(zbuf)
        first = pl.multiple_of(slot_ref[0, 0] * ROW_TILE, ROW_TILE)
        fill = pltpu.make_async_copy(zbuf, out_hbm.at[pl.ds(first, MOE_ROWS * ROW_TILE), :], zsem.at[0])
        fill.start()
        fill.wait()


def _moe_experts(block_expert, n_used, row_token, row_slot, h2, w1, b1, w2, b2, layer):
    n_blocks = row_token.shape[0]
    n_rows = n_blocks * MOE_ROWS
    spare = (n_rows + jnp.arange(MOE_ROWS, dtype=jnp.int32)).reshape(1, 1, MOE_ROWS)
    slot_prev = jnp.concatenate([spare, row_slot[:-1]], axis=0)
    smem_rows = lambda imap: pl.BlockSpec((None, 1, MOE_ROWS), imap, memory_space=pltpu.SMEM)
    expert_block = lambda *shape: pl.BlockSpec((None, None) + shape, lambda i, be, nu: (layer, be[i], 0, 0))
    row_buf = pltpu.VMEM((MOE_ROWS * ROW_TILE, LANES), F32)
    grid_spec = pltpu.PrefetchScalarGridSpec(
        num_scalar_prefetch=2,
        grid=(n_blocks,),
        in_specs=[smem_rows(lambda i, be, nu: (i, 0, 0)),
                  smem_rows(lambda i, be, nu: (jnp.minimum(i + 1, n_blocks - 1), 0, 0)),
                  smem_rows(lambda i, be, nu: (i, 0, 0)),
                  smem_rows(lambda i, be, nu: (i, 0, 0)),
                  pl.BlockSpec(memory_space=pl.ANY),
                  expert_block(D_MODEL, 2 * D_FF), expert_block(1, 2 * D_FF),
                  expert_block(D_FF, D_MODEL), expert_block(1, D_MODEL)],
        out_specs=pl.BlockSpec(memory_space=pl.ANY),
        scratch_shapes=[pltpu.VMEM((D_MODEL, 2 * D_FF), BF16), pltpu.VMEM((D_FF, D_MODEL), BF16),
                        row_buf, row_buf, row_buf, row_buf, row_buf,
                        pltpu.SemaphoreType.DMA((2,)), pltpu.SemaphoreType.DMA((2,)),
                        pltpu.SemaphoreType.DMA((1,))],
    )
    return pl.pallas_call(
        _moe_kernel,
        grid_spec=grid_spec,
        out_shape=jax.ShapeDtypeStruct(((n_rows + MOE_ROWS) * ROW_TILE, LANES), F32),
        compiler_params=pltpu.CompilerParams(dimension_semantics=("arbitrary",), vmem_limit_bytes=VMEM_LIMIT,
                                             has_side_effects=True),
        name="moe_experts",
    )(block_expert, n_used, row_token, row_token, slot_prev, row_slot, h2, w1, b1, w2, b2)


def _moe_dispatch(eidx):
    n = eidx.shape[0]
    n_assign = n * TOP_K
    n_blocks = -(-n_assign // MOE_ROWS) + N_EXPERTS
    assert n_assign % MOE_ROWS == 0
    id_bits = (n_assign - 1).bit_length()
    assert N_EXPERTS << (id_bits + 1) < 2 ** 31
    expert = eidx.reshape(-1)
    ids = jnp.arange(n_assign, dtype=jnp.int32)
    experts = jnp.arange(N_EXPERTS, dtype=jnp.int32)
    counts = jnp.sum((expert[:, None] == experts[None, :]).astype(jnp.int32), axis=0)
    padded = (counts + MOE_ROWS - 1) // MOE_ROWS * MOE_ROWS
    pad_ends = jnp.cumsum(padded)
    real_keys = (expert << (id_bits + 1)) | ids
    j = jnp.arange(MOE_ROWS, dtype=jnp.int32)[None, :]
    pad_keys = jnp.where(j < (padded - counts)[:, None],
                         (experts[:, None] << (id_bits + 1)) | (1 << id_bits) | j, jnp.iinfo(jnp.int32).max)
    keys = jnp.sort(jnp.concatenate([real_keys, pad_keys.reshape(-1)]))
    is_pad = ((keys >> id_bits) & 1) == 1
    assign = keys & ((1 << id_bits) - 1)
    filler = n_assign + jnp.cumsum(is_pad.astype(jnp.int32)) - 1
    row_slot = jnp.where(is_pad, filler, (assign % TOP_K) * n + assign // TOP_K).astype(jnp.int32)
    row_token = jnp.where(is_pad, 0, assign // TOP_K).astype(jnp.int32)
    block_start = jnp.arange(n_blocks, dtype=jnp.int32) * MOE_ROWS
    block_expert = jnp.minimum(jnp.sum((pad_ends[None, :] <= block_start[:, None]).astype(jnp.int32), axis=1),
                               N_EXPERTS - 1).astype(jnp.int32)
    n_used = (pad_ends[-1] // MOE_ROWS).astype(jnp.int32).reshape(1)
    return (block_expert, n_used, row_token.reshape(n_blocks, 1, MOE_ROWS),
            row_slot.reshape(n_blocks, 1, MOE_ROWS))


def _combine_kernel(x1_ref, mod_ref, gate_ref, y0_ref, y1_ref, y2_ref, y3_ref, fg_ref, o_ref, *, final):
    gate = gate_ref[...]
    y = None
    for k, yk_ref in enumerate((y0_ref, y1_ref, y2_ref, y3_ref)):
        rows = jnp.concatenate([yk_ref[pl.ds(j, TOKEN_TILE, stride=ROW_TILE), :] for j in range(ROW_TILE)], axis=1)
        yk = gate[:, k:k + 1] * rows
        y = yk if k == 0 else y + yk
    x2 = x1_ref[...] + mod_ref[5:6, :] * y
    if final:
        ms = jnp.mean(x2 * x2, axis=-1, keepdims=True)
        x2 = x2 * lax.rsqrt(ms + EPS) * fg_ref[...]
    o_ref[...] = x2


def _combine(x1, mod_tab, gate, y_slots, final_g, n_prompt_tiles, tiles_per_sample, *, final):
    nt = x1.shape[0]
    tiles = nt // TOKEN_TILE
    mrow = functools.partial(_mod_row, n_prompt_tiles=n_prompt_tiles, tiles_per_sample=tiles_per_sample)
    slot_rows = lambda k: pl.BlockSpec((TOKEN_TILE * ROW_TILE, LANES), lambda i: (k * tiles + i, 0))
    return pl.pallas_call(
        functools.partial(_combine_kernel, final=final),
        grid=(tiles,),
        in_specs=[pl.BlockSpec((TOKEN_TILE, D_MODEL), lambda i: (i, 0)),
                  pl.BlockSpec((None, 8, D_MODEL), lambda i: (mrow(i), 0, 0)),
                  pl.BlockSpec((TOKEN_TILE, LANES), lambda i: (i, 0))]
                 + [slot_rows(k) for k in range(TOP_K)]
                 + [pl.BlockSpec((1, D_MODEL), lambda i: (0, 0))],
        out_specs=pl.BlockSpec((TOKEN_TILE, D_MODEL), lambda i: (i, 0)),
        out_shape=jax.ShapeDtypeStruct((nt, D_MODEL), F32),
        compiler_params=_cparams("parallel"),
        name="combine_final" if final else "combine",
    )(x1, mod_tab, gate, y_slots, y_slots, y_slots, y_slots, final_g)


def _state_to_blocks(init):
    b = init.shape[0]
    st = init.reshape(b, 2, B_HEADS * HEAD_W, B_STATE)
    row = jnp.arange(B_HEADS * HEAD_W)[:, None]
    left = jnp.where(row < 2 * HEAD_W, st, 0.0)
    right = jnp.where(row >= 2 * HEAD_W, st, 0.0)
    return jnp.concatenate([left, right], axis=-1)


def _blocks_to_state(s):
    b = s.shape[0]
    row = jnp.arange(B_HEADS * HEAD_W)[:, None]
    st = jnp.where(row < 2 * HEAD_W, s[..., :B_STATE], s[..., B_STATE:])
    return st.reshape(b, 2, B_HEADS, HEAD_W, B_STATE)


def kernel(x_prompt, x_sample, c, cache_a_k, cache_a_v, cache_c_k, cache_c_v, cache_d_k, cache_d_v, state_ssd, c_ctx, norm1_g, norm2_g, w_mod, b_mod, w_in, a_q_g, a_k_g, b_conv_w, b_conv_b, b_dt_bias, b_a_log, b_d, b_norm_g, c_lam, c_subln_g, d_rpb, w_branch, w_out, router_w, router_b, moe_w1, moe_b1, moe_w2, moe_b2, final_g):
    bp, tp, _ = x_prompt.shape
    bs, ts, _ = x_sample.shape
    depth = w_in.shape[0]
    n_p = bp * tp
    n_s = bs * ts
    assert n_p % TOKEN_TILE == 0 and ts % TOKEN_TILE == 0 and n_p % ts == 0
    n_prompt_tiles = n_p // TOKEN_TILE
    tiles_per_sample = ts // TOKEN_TILE
    past = cache_a_k.shape[2]

    x = jnp.concatenate([x_prompt.reshape(n_p, D_MODEL), x_sample.reshape(n_s, D_MODEL)], axis=0)
    cvec = jnp.zeros((16, D_MODEL), F32).at[0].set(c_ctx).at[1:1 + bs].set(c)
    rope_tab = _rope_table(ts, TOKEN_TILE)
    perm_a = _rope_perm(HEAD_W)
    perm_c = _rope_perm(C_QK_DIM)
    head_of = np.arange(BRANCH_W) // HEAD_W
    bd = jnp.asarray((head_of[:, None] == head_of[None, :]) / HEAD_W, BF16)
    kv_of = np.arange(A_KV_W) // HEAD_W
    ex = jnp.asarray((kv_of[:, None] == (head_of // (N_HEADS // A_KV_HEADS))[None, :])
                     & ((np.arange(A_KV_W) % HEAD_W)[:, None] == (np.arange(BRANCH_W) % HEAD_W)[None, :]), BF16)
    scale_c = C_QK_DIM ** -0.5
    one = jnp.ones((1,), F32)
    rep = N_HEADS // A_KV_HEADS
    zero_state = jnp.zeros((bp, 2, BRANCH_W, LANES), F32)

    new_cache = [[] for _ in range(7)]
    for l in range(depth):
        lam_init = 0.8 - 0.6 * math.exp(-0.3 * l)
        lq = c_lam[l]
        lam = (jnp.exp(jnp.sum(lq[0] * lq[1])) - jnp.exp(jnp.sum(lq[2] * lq[3])) + lam_init).reshape(1)

        wl = w_in[l]
        w1 = jnp.concatenate([wl[:, 0:512], wl[:, 768:1280], wl[:, 512:768], wl[:, 1280:1288],
                              jnp.zeros((D_MODEL, LANES - 8), F32), wl[:, 1288:2824]], axis=1).astype(BF16)
        wg = wl[:, 2824:].astype(BF16)
        wb = w_branch[l].astype(BF16)
        wo = w_out[l].astype(BF16)
        rw = jnp.pad(router_w[l], ((0, 0), (0, LANES - N_EXPERTS)))
        rwh = rw.astype(BF16)
        rwl = (rw - rwh.astype(F32)).astype(BF16)
        rb = jnp.pad(router_b[l], (0, LANES - N_EXPERTS)).reshape(1, LANES)

        mod = _modulation(cvec, w_mod[l], b_mod[l].reshape(1, -1))
        mod_tab = jnp.pad(mod.reshape(16, 6, D_MODEL), ((0, 0), (0, 2), (0, 0)))

        g1 = norm1_g[l].reshape(1, D_MODEL)
        g2 = norm2_g[l].reshape(1, D_MODEL)
        att_a, att_c, att_d, kv_a, kv_c, kv_d, ssd_in = _in_proj(
            x, mod_tab, g1, w1, rope_tab, bd, perm_a, perm_c, ex,
            jnp.tile(a_q_g[l], N_HEADS).reshape(1, BRANCH_W), jnp.tile(a_k_g[l], A_KV_HEADS).reshape(1, A_KV_W),
            n_prompt_tiles, tiles_per_sample)

        ctx_ak = jnp.repeat(cache_a_k[:, l], rep, axis=2).reshape(bs, past, BRANCH_W).astype(BF16)
        ctx_av = jnp.repeat(cache_a_v[:, l], rep, axis=2).reshape(bs, past, BRANCH_W).astype(BF16)
        o_a_p = _mha(att_a, one, n_seq=bp, seq=tp, row0=0, n_maps=1, scale=1.0)
        o_a_s = _mha(att_a, one, n_seq=bs, seq=ts, row0=n_p, n_maps=1, scale=1.0, ctx=(ctx_ak, ctx_av))

        subln = (jnp.tile(c_subln_g[l], N_HEADS).reshape(1, BRANCH_W), bd, 1.0 - lam_init)
        ctx_ck = cache_c_k[:, l].reshape(bs, past, BRANCH_W).astype(BF16)
        ctx_cv = cache_c_v[:, l].reshape(bs, past, BRANCH_W).astype(BF16)
        o_c_p = _mha(att_c, lam, n_seq=bp, seq=tp, row0=0, n_maps=2, scale=scale_c, subln=subln)
        o_c_s = _mha(att_c, lam, n_seq=bs, seq=ts, row0=n_p, n_maps=2, scale=scale_c, ctx=(ctx_ck, ctx_cv),
                     subln=subln)

        o_d_p = _mha(att_d, one, n_seq=bp, seq=tp, row0=0, n_maps=1, scale=1.0)
        o_d_s = _na_attention(att_d, cache_d_k[:, l].reshape(bs, past, BRANCH_W).astype(BF16),
                              cache_d_v[:, l].reshape(bs, past, BRANCH_W).astype(BF16),
                              _na_bias_table(d_rpb[l], ts // GRID_W), n_seq=bs, seq=ts, row0=n_p)

        dtb = b_dt_bias[l].reshape(8)
        a_neg = -jnp.exp(b_a_log[l].reshape(8))
        ssd_args = (jnp.pad(b_conv_w[l], ((0, 8 - B_CONV_W), (0, 0))), b_conv_b[l].reshape(1, B_XBC),
                    jnp.zeros((8, LANES), F32).at[0, :8].set(dtb).at[1, :8].set(a_neg),
                    jnp.repeat(b_d[l], HEAD_W).reshape(1, BRANCH_W), b_norm_g[l].reshape(1, BRANCH_W))
        o_b_p, st_p = _ssd_branch(ssd_in, *ssd_args, zero_state, n_seq=bp, seq=tp, row0=0)
        o_b_s, _ = _ssd_branch(ssd_in, *ssd_args, _state_to_blocks(state_ssd[:, l]), n_seq=bs, seq=ts, row0=n_p)

        x1, h2, gate, eidx = _merge(x, mod_tab, g1, g2, (o_a_p, o_b_p, o_c_p, o_d_p), (o_a_s, o_b_s, o_c_s, o_d_s),
                                    wg, wb, wo, rwh, rwl, rb, n_prompt_tiles, tiles_per_sample)
        block_expert, n_used, row_token, row_slot = _moe_dispatch(eidx[:, :TOP_K])
        y_slots = _moe_experts(block_expert, n_used, row_token, row_slot, h2, moe_w1,
                               moe_b1.reshape(depth, N_EXPERTS, 1, -1), moe_w2,
                               moe_b2.reshape(depth, N_EXPERTS, 1, -1), l)
        x = _combine(x1, mod_tab, gate, y_slots, final_g.reshape(1, D_MODEL), n_prompt_tiles, tiles_per_sample,
                     final=(l == depth - 1))

        new_cache[0].append(kv_a[:n_p, :A_KV_W].reshape(bp, tp, A_KV_HEADS, HEAD_W))
        new_cache[1].append(kv_a[:n_p, A_KV_W:].reshape(bp, tp, A_KV_HEADS, HEAD_W))
        new_cache[2].append(kv_c[:n_p, :BRANCH_W].reshape(bp, tp, N_HEADS, 2 * C_QK_DIM))
        new_cache[3].append(kv_c[:n_p, BRANCH_W:].reshape(bp, tp, N_HEADS, HEAD_W))
        new_cache[4].append(kv_d[:n_p, :BRANCH_W].reshape(bp, tp, N_HEADS, HEAD_W))
        new_cache[5].append(kv_d[:n_p, BRANCH_W:].reshape(bp, tp, N_HEADS, HEAD_W))
        new_cache[6].append(_blocks_to_state(st_p))

    y_prompt = x[:n_p].reshape(bp, tp, D_MODEL)
    y_sample = x[n_p:].reshape(bs, ts, D_MODEL)
    return (y_prompt, y_sample) + tuple(jnp.stack(v, axis=1) for v in new_cache)
```

```python
import functools
import math

import jax
import jax.numpy as jnp
import numpy as np
from jax import lax
from jax.experimental import pallas as pl
from jax.experimental.pallas import tpu as pltpu

F32 = jnp.float32
BF16 = jnp.bfloat16

D_MODEL = 1024
N_BRANCH = 4
BRANCH_W = 256
HEAD_W = 64
N_HEADS = 4
GRID_W = 64
ROPE_THETA = 10000.0
EPS = 1e-6
A_KV_HEADS = 2
A_KV_W = A_KV_HEADS * HEAD_W
C_QK_DIM = 32
B_HEADS = 4
B_STATE = 64
B_CHUNK = 128
B_XBC = 512
B_CONV_W = 5
CONV_PAD = 8
NA_ROWS = 8
NA_COLS = 16
N_EXPERTS = 32
TOP_K = 4
D_FF = 1024
SWIGLU_LIMIT = 7.0
SWIGLU_ALPHA = 1.702
LANES = 128
ROW_TILE = D_MODEL // LANES
NEG = -1e30

TOKEN_TILE = 512
MOE_ROWS = 256
ATT_W = 3 * BRANCH_W
SSD_W = B_XBC + BRANCH_W + LANES
OFF_A, OFF_B, OFF_C, OFF_D = 0, 512, 512 + SSD_W, 512 + SSD_W + ATT_W
W1_COLS = OFF_D + ATT_W
VMEM_LIMIT = 56 * 1024 * 1024


def _cparams(*sem):
    return pltpu.CompilerParams(dimension_semantics=sem, vmem_limit_bytes=VMEM_LIMIT)


def _dot(a, b):
    return jnp.dot(a, b, preferred_element_type=F32)


def _dot_nt(a, b):
    return lax.dot_general(a, b, (((1,), (1,)), ((), ())), preferred_element_type=F32)


def _dot_tn(a, b):
    return lax.dot_general(a, b, (((0,), (0,)), ((), ())), preferred_element_type=F32)


def _sigmoid(x):
    return 0.5 * jnp.tanh(0.5 * x) + 0.5


def _split2(x):
    hi = x.astype(BF16)
    return hi, (x - hi.astype(F32)).astype(BF16)


def _split3(x):
    h1 = x.astype(BF16)
    r1 = x - h1.astype(F32)
    h2 = r1.astype(BF16)
    h3 = (r1 - h2.astype(F32)).astype(BF16)
    return h1, h2, h3


def _norm_mod(x, g, shift, scale):
    ms = jnp.mean(x * x, axis=-1, keepdims=True)
    return (x * lax.rsqrt(ms + EPS)) * g * (1.0 + scale) + shift


def _head_rmsnorm(x, bd, g):
    hi, lo = _split2(x * x)
    ms = _dot(hi, bd) + _dot(lo, bd)
    return x * lax.rsqrt(ms + EPS) * g


def _rope(x, perm, cos, sin):
    hi, lo = _split2(x)
    return x * cos + (_dot(hi, perm) + _dot(lo, perm)) * sin


def _mod_kernel(c_ref, w_ref, b_ref, o_ref):
    c = c_ref[...]
    s = (c * _sigmoid(c)).astype(BF16)
    o_ref[...] = _dot(s, w_ref[...].astype(BF16)) + b_ref[...]


def _modulation(cvec, w_mod, b_mod):
    n = w_mod.shape[1]
    tn = 1536
    return pl.pallas_call(
        _mod_kernel,
        grid=(n // tn,),
        in_specs=[pl.BlockSpec((16, D_MODEL), lambda j: (0, 0)),
                  pl.BlockSpec((D_MODEL, tn), lambda j: (0, j)),
                  pl.BlockSpec((1, tn), lambda j: (0, j))],
        out_specs=pl.BlockSpec((16, tn), lambda j: (0, j)),
        out_shape=jax.ShapeDtypeStruct((16, n), F32),
        compiler_params=_cparams("parallel"),
        name="modulation",
    )(cvec, w_mod, b_mod)


def _mod_row(i, n_prompt_tiles, tiles_per_sample):
    return jnp.where(i < n_prompt_tiles, 0, 1 + (i - n_prompt_tiles) // tiles_per_sample)


def _rope_row(i, n_prompt_tiles, tiles_per_sample):
    return jnp.where(i < n_prompt_tiles, 0, 1 + (i - n_prompt_tiles) % tiles_per_sample)


def _inproj_kernel(x_ref, mod_ref, g_ref, w_ref, rope_ref, bd_ref, pa_ref, pc_ref, ex_ref, gq_ref, gk_ref,
                   atta_ref, attc_ref, attd_ref, kva_ref, kvc_ref, kvd_ref, ssd_ref):
    mod = mod_ref[...]
    h = _norm_mod(x_ref[...], g_ref[...], mod[0:1], mod[1:2])
    u = _dot(h.astype(BF16), w_ref[...])
    cos_a, sin_a = rope_ref[:, 0:256], rope_ref[:, 256:512]
    cos_c, sin_c = rope_ref[:, 512:768], rope_ref[:, 768:1024]
    q_scale = HEAD_W ** -0.5

    q = _rope(_head_rmsnorm(u[:, OFF_A:OFF_A + 256], bd_ref[...], gq_ref[...]), pa_ref[...], cos_a, sin_a)
    k = _rope(_head_rmsnorm(u[:, OFF_A + 256:OFF_A + 384], bd_ref[0:A_KV_W, 0:A_KV_W], gk_ref[...]),
              pa_ref[0:A_KV_W, 0:A_KV_W], cos_a[:, 0:A_KV_W], sin_a[:, 0:A_KV_W])
    v = u[:, OFF_A + 384:OFF_A + 512]
    ex = ex_ref[...]
    atta_ref[:, 0:256] = (q * q_scale).astype(BF16)
    atta_ref[:, 256:512] = _dot(k.astype(BF16), ex).astype(BF16)
    atta_ref[:, 512:768] = _dot(v.astype(BF16), ex).astype(BF16)
    kva_ref[:, 0:A_KV_W] = k
    kva_ref[:, A_KV_W:2 * A_KV_W] = v

    ssd_ref[...] = u[:, OFF_B:OFF_B + SSD_W]

    cq = _rope(u[:, OFF_C:OFF_C + 256], pc_ref[...], cos_c, sin_c)
    ck = _rope(u[:, OFF_C + 256:OFF_C + 512], pc_ref[...], cos_c, sin_c)
    cv = u[:, OFF_C + 512:OFF_C + 768]
    attc_ref[:, 0:256] = cq.astype(BF16)
    attc_ref[:, 256:512] = ck.astype(BF16)
    attc_ref[:, 512:768] = cv.astype(BF16)
    kvc_ref[:, 0:256] = ck
    kvc_ref[:, 256:512] = cv

    attd_ref[:, 0:256] = (u[:, OFF_D:OFF_D + 256] * q_scale).astype(BF16)
    attd_ref[:, 256:768] = u[:, OFF_D + 256:OFF_D + 768].astype(BF16)
    kvd_ref[...] = u[:, OFF_D + 256:OFF_D + 768]


def _in_proj(x, mod_tab, g, w1, rope_tab, bd, pa, pc, ex, gq, gk, n_prompt_tiles, tiles_per_sample):
    nt = x.shape[0]
    mrow = functools.partial(_mod_row, n_prompt_tiles=n_prompt_tiles, tiles_per_sample=tiles_per_sample)
    rrow = functools.partial(_rope_row, n_prompt_tiles=n_prompt_tiles, tiles_per_sample=tiles_per_sample)
    const = lambda *shape: pl.BlockSpec(shape, lambda i: (0,) * len(shape))
    rows = lambda w: pl.BlockSpec((TOKEN_TILE, w), lambda i: (i, 0))
    widths = (ATT_W, ATT_W, ATT_W, 2 * A_KV_W, 2 * BRANCH_W, 2 * BRANCH_W, SSD_W)
    dtypes = (BF16, BF16, BF16, F32, F32, F32, F32)
    return pl.pallas_call(
        _inproj_kernel,
        grid=(nt // TOKEN_TILE,),
        in_specs=[rows(D_MODEL),
                  pl.BlockSpec((None, 8, D_MODEL), lambda i: (mrow(i), 0, 0)),
                  const(1, D_MODEL), const(D_MODEL, W1_COLS),
                  pl.BlockSpec((TOKEN_TILE, 4 * BRANCH_W), lambda i: (rrow(i), 0)),
                  const(BRANCH_W, BRANCH_W), const(BRANCH_W, BRANCH_W), const(BRANCH_W, BRANCH_W),
                  const(A_KV_W, BRANCH_W), const(1, BRANCH_W), const(1, A_KV_W)],
        out_specs=[rows(w) for w in widths],
        out_shape=[jax.ShapeDtypeStruct((nt, w), dt) for w, dt in zip(widths, dtypes)],
        compiler_params=_cparams("parallel"),
        name="in_proj",
    )(x, mod_tab, g, w1, rope_tab, bd, pa, pc, ex, gq, gk)


def _rope_perm(dim):
    p = np.zeros((BRANCH_W, BRANCH_W), np.float32)
    s = dim // 4
    for j in range(BRANCH_W):
        quarter = (j % dim) // s
        if quarter % 2 == 0:
            p[j + s, j] = -1.0
        else:
            p[j - s, j] = 1.0
    return jnp.asarray(p, BF16)


def _rope_table(t, n_identity):
    pos = jnp.arange(t)
    rows = (pos // GRID_W).astype(F32)
    cols = (pos % GRID_W).astype(F32)
    parts = []
    for dim in (HEAD_W, C_QK_DIM):
        axis_dim = dim // 2
        inv = ROPE_THETA ** (-jnp.arange(0, axis_dim, 2, dtype=F32) / axis_dim)
        ang_r = rows[:, None] * inv[None, :]
        ang_c = cols[:, None] * inv[None, :]
        ang = jnp.concatenate([ang_r, ang_r, ang_c, ang_c], axis=-1)
        reps = BRANCH_W // dim
        parts += [jnp.tile(jnp.cos(ang), (1, reps)), jnp.tile(jnp.sin(ang), (1, reps))]
    tab = jnp.concatenate(parts, axis=1)
    ident = jnp.concatenate([jnp.ones((n_identity, BRANCH_W), F32), jnp.zeros((n_identity, BRANCH_W), F32)] * 2, axis=1)
    return jnp.concatenate([ident, tab], axis=0)


def _mha_kernel(lam_ref, q_ref, k_ref, v_ref, *rest, n_maps, scale, has_ctx, subln_scale):
    rest = list(rest)
    ck_ref, cv_ref = (rest.pop(0), rest.pop(0)) if has_ctx else (None, None)
    g_ref, bd_ref = (rest.pop(0), rest.pop(0)) if subln_scale is not None else (None, None)
    o_ref = rest.pop(0)
    q = q_ref[...]
    k = k_ref[...]
    v = v_ref[...]
    tq = q.shape[0]
    lane = lax.broadcasted_iota(jnp.int32, (1, BRANCH_W), 1)
    sub_w = HEAD_W // n_maps
    acc = jnp.zeros((tq, BRANCH_W), F32)
    for h in range(N_HEADS):
        oh = None
        for j in range(n_maps):
            qm = jnp.where((lane // sub_w) == (h * n_maps + j), q, jnp.zeros_like(q))
            s = _dot_nt(qm, k)
            if scale != 1.0:
                s = s * scale
            m = jnp.max(s, axis=-1, keepdims=True)
            if has_ctx:
                sc = _dot_nt(qm, ck_ref[...])
                if scale != 1.0:
                    sc = sc * scale
                m = jnp.maximum(m, jnp.max(sc, axis=-1, keepdims=True))
                pc = jnp.exp(sc - m)
            p = jnp.exp(s - m)
            l = jnp.sum(p, axis=-1, keepdims=True)
            pv = _dot(p.astype(BF16), v)
            if has_ctx:
                l = l + jnp.sum(pc, axis=-1, keepdims=True)
                pv = pv + _dot(pc.astype(BF16), cv_ref[...])
            pv = pv * (1.0 / l)
            oh = pv if j == 0 else oh - lam_ref[0] * pv
        acc = jnp.where((lane // HEAD_W) == h, oh, acc)
    if subln_scale is not None:
        acc = _head_rmsnorm(acc, bd_ref[...], g_ref[...]) * subln_scale
    o_ref[...] = acc.astype(o_ref.dtype)


def _mha(att, lam, *, n_seq, seq, row0, n_maps, scale, ctx=None, subln=None, tq=256):
    qb0 = row0 // tq
    kb0 = row0 // seq
    nq = seq // tq
    assert row0 % seq == 0 and seq % tq == 0
    in_specs = [pl.BlockSpec(memory_space=pltpu.SMEM),
                pl.BlockSpec((tq, BRANCH_W), lambda i, j: (qb0 + i * nq + j, 0)),
                pl.BlockSpec((seq, BRANCH_W), lambda i, j: (kb0 + i, 1)),
                pl.BlockSpec((seq, BRANCH_W), lambda i, j: (kb0 + i, 2))]
    args = [lam, att, att, att]
    kk = seq
    if ctx is not None:
        kc = ctx[0].shape[1]
        kk += kc
        in_specs += [pl.BlockSpec((None, kc, BRANCH_W), lambda i, j: (i, 0, 0))] * 2
        args += list(ctx)
    subln_scale = None
    if subln is not None:
        in_specs += [pl.BlockSpec((1, BRANCH_W), lambda i, j: (0, 0)),
                     pl.BlockSpec((BRANCH_W, BRANCH_W), lambda i, j: (0, 0))]
        args += [subln[0], subln[1]]
        subln_scale = subln[2]
    return pl.pallas_call(
        functools.partial(_mha_kernel, n_maps=n_maps, scale=scale, has_ctx=ctx is not None, subln_scale=subln_scale),
        grid=(n_seq, nq),
        in_specs=in_specs,
        out_specs=pl.BlockSpec((tq, BRANCH_W), lambda i, j: (i * nq + j, 0)),
        out_shape=jax.ShapeDtypeStruct((n_seq * seq, BRANCH_W), BF16),
        compiler_params=_cparams("parallel", "parallel"),
        name=f"mha_maps{n_maps}_k{kk}",
    )(*args)


def _na_row0(r, n_rows):
    return jnp.clip(r - NA_ROWS // 2, 0, n_rows - NA_ROWS)


def _na_kernel(q_ref, k_ref, v_ref, ck_ref, cv_ref, bias_ref, o_ref, *, n_rows):
    win = NA_ROWS * GRID_W
    lane = lax.broadcasted_iota(jnp.int32, (1, BRANCH_W), 1)
    ck = ck_ref[...]
    cv = cv_ref[...]

    def row_body(r, carry):
        r0 = _na_row0(r, n_rows)
        start = pl.multiple_of(r0 * GRID_W, GRID_W)
        rows = pl.ds(pl.multiple_of(r * GRID_W, GRID_W), GRID_W)
        q = q_ref[rows, :]
        kw = k_ref[pl.ds(start, win), :]
        vw = v_ref[pl.ds(start, win), :]
        qs = jnp.concatenate(
            [jnp.where((lane // HEAD_W) == h, q, jnp.zeros_like(q)) for h in range(N_HEADS)], axis=0)
        s_loc = _dot_nt(qs, kw) + bias_ref[r - r0]
        s_ctx = _dot_nt(qs, ck)
        m = jnp.maximum(jnp.max(s_loc, axis=-1, keepdims=True), jnp.max(s_ctx, axis=-1, keepdims=True))
        p_loc = jnp.exp(s_loc - m)
        p_ctx = jnp.exp(s_ctx - m)
        l = jnp.sum(p_loc, axis=-1, keepdims=True) + jnp.sum(p_ctx, axis=-1, keepdims=True)
        o = (_dot(p_loc.astype(BF16), vw) + _dot(p_ctx.astype(BF16), cv)) * (1.0 / l)
        acc = jnp.zeros((GRID_W, BRANCH_W), F32)
        for h in range(N_HEADS):
            acc = jnp.where((lane // HEAD_W) == h, o[h * GRID_W:(h + 1) * GRID_W], acc)
        o_ref[rows, :] = acc.astype(o_ref.dtype)
        return carry

    lax.fori_loop(0, n_rows, row_body, 0)


def _na_attention(att, ck, cv, bias_tab, *, n_seq, seq, row0):
    n_rows = seq // GRID_W
    kk = ck.shape[1]
    win = NA_ROWS * GRID_W
    b0 = row0 // seq
    assert row0 % seq == 0
    return pl.pallas_call(
        functools.partial(_na_kernel, n_rows=n_rows),
        grid=(n_seq,),
        in_specs=[pl.BlockSpec((seq, BRANCH_W), lambda i: (b0 + i, 0)),
                  pl.BlockSpec((seq, BRANCH_W), lambda i: (b0 + i, 1)),
                  pl.BlockSpec((seq, BRANCH_W), lambda i: (b0 + i, 2)),
                  pl.BlockSpec((None, kk, BRANCH_W), lambda i: (i, 0, 0)),
                  pl.BlockSpec((None, kk, BRANCH_W), lambda i: (i, 0, 0)),
                  pl.BlockSpec((NA_ROWS, N_HEADS * GRID_W, win), lambda i: (0, 0, 0))],
        out_specs=pl.BlockSpec((seq, BRANCH_W), lambda i: (i, 0)),
        out_shape=jax.ShapeDtypeStruct((n_seq * seq, BRANCH_W), BF16),
        compiler_params=_cparams("parallel"),
        name="na_attention",
    )(att, att, att, ck, cv, bias_tab)


def _na_bias_table(rpb, n_rows):
    var = np.arange(NA_ROWS)[:, None, None]
    j = np.arange(NA_ROWS)[None, :, None]
    sel_r = (np.arange(2 * NA_ROWS - 1)[None, None, :] == j - var + (NA_ROWS - 1)).astype(np.float32)
    col = np.arange(GRID_W)[:, None, None]
    kc = np.arange(GRID_W)[None, :, None]
    c0 = np.clip(col - NA_COLS // 2, 0, GRID_W - NA_COLS)
    valid = ((kc >= c0) & (kc < c0 + NA_COLS))[:, :, 0]
    sel_c = (np.arange(2 * NA_COLS - 1)[None, None, :] == kc - col + (NA_COLS - 1)).astype(np.float32)
    tab = jnp.einsum('hrc,vjr,xkc->vhxjk', rpb.astype(F32), sel_r, sel_c, precision=lax.Precision.HIGHEST)
    tab = jnp.where(valid[None, None, :, None, :], tab, NEG)
    return tab.reshape(NA_ROWS, N_HEADS * GRID_W, NA_ROWS * GRID_W)


def _softplus(x):
    return jnp.maximum(x, 0.0) + jnp.log1p(jnp.exp(-jnp.abs(x)))


def _expand_heads(colmat, d, lane256):
    out = jnp.zeros((colmat.shape[0], BRANCH_W), F32)
    for h in range(B_HEADS):
        j = d * B_HEADS + h
        out = jnp.where((lane256 // HEAD_W) == h,
                        jnp.broadcast_to(colmat[:, j:j + 1], (colmat.shape[0], BRANCH_W)), out)
    return out


def _ssd_kernel(xbc_ref, z_ref, dt_ref, cw_ref, cb_ref, pc_ref, dsk_ref, ng_ref, init_ref,
                o_ref, st_ref, xpad_s, xc_s, yf_s, s_s, *, seq):
    n_chunks = seq // B_CHUNK
    L = B_CHUNK
    ri = lax.broadcasted_iota(jnp.int32, (L, L), 0)
    ci = lax.broadcasted_iota(jnp.int32, (L, L), 1)
    low = ci <= ri
    upp = ri <= ci
    low_b = jnp.where(low, 1.0, 0.0).astype(BF16)
    upp_b = jnp.where(upp, 1.0, 0.0).astype(BF16)
    lane128 = lax.broadcasted_iota(jnp.int32, (1, LANES), 1)
    lane256 = lax.broadcasted_iota(jnp.int32, (1, BRANCH_W), 1)
    row256 = lax.broadcasted_iota(jnp.int32, (BRANCH_W, 1), 0)
    blockmask = (row256 // (2 * HEAD_W)) == (lane128 // B_STATE)

    xpad_s[0:CONV_PAD, :] = jnp.zeros((CONV_PAD, B_XBC), F32)
    xpad_s[seq + CONV_PAD:seq + 2 * CONV_PAD, :] = jnp.zeros((CONV_PAD, B_XBC), F32)

    def pad_body(c, carry):
        base = pl.multiple_of(c * L, L)
        xpad_s[pl.ds(pl.multiple_of(base + CONV_PAD, CONV_PAD), L), :] = xbc_ref[pl.ds(base, L), :]
        return carry

    lax.fori_loop(0, n_chunks, pad_body, 0)
    cw = cw_ref[...]
    cb = cb_ref[...]

    def conv_body(c, carry):
        base = pl.multiple_of(c * L, L)
        w = xpad_s[pl.ds(base, L + 2 * CONV_PAD), :]
        acc = jnp.zeros((L, B_XBC), F32) + cb
        for kk in range(B_CONV_W):
            off = CONV_PAD - B_CONV_W // 2 + kk
            acc = acc + w[off:off + L, :] * cw[kk:kk + 1, :]
        xc_s[pl.ds(base, L), :] = acc * _sigmoid(acc)
        return carry

    lax.fori_loop(0, n_chunks, conv_body, 0)

    dt_bias = pc_ref[0:1, :]
    a_neg = pc_ref[1:2, :]
    dsk = dsk_ref[...]
    ng = ng_ref[...]

    def chunk(c, d):
        base = pl.multiple_of(c * L, L)
        xs = xc_s[pl.ds(base, L), 0:BRANCH_W]
        bm = xc_s[pl.ds(base, L), BRANCH_W:BRANCH_W + LANES].astype(BF16)
        cm = xc_s[pl.ds(base, L), BRANCH_W + LANES:B_XBC].astype(BF16)
        dtc = _softplus(dt_ref[pl.ds(base, L), :] + dt_bias)
        da_c = dtc * a_neg
        da_r = da_c.T
        tri_c = low_b if d == 0 else upp_b
        tri_r = upp_b if d == 0 else low_b
        c1, c2, c3 = _split3(da_c)
        cum_c = _dot(tri_c, c1) + _dot(tri_c, c2) + _dot(tri_c, c3)
        r1, r2, r3 = _split3(da_r)
        cum_r = _dot(r1, tri_r) + _dot(r2, tri_r) + _dot(r3, tri_r)
        cum_end = cum_c[L - 1:L, :] if d == 0 else cum_c[0:1, :]
        dmask = low if d == 0 else upp

        g0 = _dot_nt(jnp.where(lane128 < B_STATE, cm, jnp.zeros_like(cm)), bm)
        g1 = _dot_nt(jnp.where(lane128 >= B_STATE, cm, jnp.zeros_like(cm)), bm)
        dt_x = _expand_heads(dtc, d, lane256)
        e_a = _expand_heads(jnp.exp(cum_c), d, lane256)
        t_e = _expand_heads(jnp.exp(cum_end - cum_c), d, lane256)
        xdt = xs * dt_x
        xdt_b = xdt.astype(BF16)
        y = jnp.zeros((L, BRANCH_W), F32)
        for h in range(B_HEADS):
            j = d * B_HEADS + h
            col = jnp.broadcast_to(cum_c[:, j:j + 1], (L, L))
            row = jnp.broadcast_to(cum_r[j:j + 1, :], (L, L))
            dec = jnp.exp(jnp.where(dmask, col - row, NEG))
            sc = ((g0 if h < 2 else g1) * dec).astype(BF16)
            y = jnp.where((lane256 // HEAD_W) == h, _dot(sc, xdt_b), y)
        state = s_s[...]
        y = y + _dot_nt(cm, state.astype(BF16)) * e_a
        zmat = _dot_tn((xdt * t_e).astype(BF16), bm)
        e_end = jnp.exp(cum_end)
        cd = jnp.zeros((BRANCH_W, LANES), F32)
        for h in range(B_HEADS):
            j = d * B_HEADS + h
            cd = jnp.where((row256 // HEAD_W) == h, jnp.broadcast_to(e_end[:, j:j + 1], (BRANCH_W, LANES)), cd)
        s_s[...] = state * cd + jnp.where(blockmask, zmat, 0.0)
        return base, xs, y

    s_s[...] = init_ref[0]

    def fwd_body(c, carry):
        base, _, y = chunk(c, 0)
        yf_s[pl.ds(base, L), :] = y
        return carry

    lax.fori_loop(0, n_chunks, fwd_body, 0)
    st_ref[0] = s_s[...]
    s_s[...] = init_ref[1]

    def bwd_body(i, carry):
        c = n_chunks - 1 - i
        base, xs, y = chunk(c, 1)
        z = z_ref[pl.ds(base, L), :]
        yt = (yf_s[pl.ds(base, L), :] + y + dsk * xs) * (z * _sigmoid(z))
        ms = jnp.mean(yt * yt, axis=-1, keepdims=True)
        o_ref[pl.ds(base, L), :] = (yt * lax.rsqrt(ms + EPS) * ng).astype(o_ref.dtype)
        return carry

    lax.fori_loop(0, n_chunks, bwd_body, 0)
    st_ref[1] = s_s[...]


def _ssd_branch(ssd_in, conv_w, conv_b, pc, dsk, ng, init, *, n_seq, seq, row0):
    b0 = row0 // seq
    assert row0 % seq == 0
    full = lambda *shape: pl.BlockSpec(shape, lambda i: (0,) * len(shape))
    return pl.pallas_call(
        functools.partial(_ssd_kernel, seq=seq),
        grid=(n_seq,),
        in_specs=[pl.BlockSpec((seq, B_XBC), lambda i: (b0 + i, 0)),
                  pl.BlockSpec((seq, BRANCH_W), lambda i: (b0 + i, B_XBC // BRANCH_W)),
                  pl.BlockSpec((seq, LANES), lambda i: (b0 + i, (B_XBC + BRANCH_W) // LANES)),
                  full(8, B_XBC), full(1, B_XBC), full(8, LANES), full(1, BRANCH_W), full(1, BRANCH_W),
                  pl.BlockSpec((None, 2, BRANCH_W, LANES), lambda i: (i, 0, 0, 0))],
        out_specs=[pl.BlockSpec((seq, BRANCH_W), lambda i: (i, 0)),
                   pl.BlockSpec((None, 2, BRANCH_W, LANES), lambda i: (i, 0, 0, 0))],
        out_shape=[jax.ShapeDtypeStruct((n_seq * seq, BRANCH_W), BF16),
                   jax.ShapeDtypeStruct((n_seq, 2, BRANCH_W, LANES), F32)],
        scratch_shapes=[pltpu.VMEM((seq + 2 * CONV_PAD, B_XBC), F32), pltpu.VMEM((seq, B_XBC), F32),
                        pltpu.VMEM((seq, BRANCH_W), F32), pltpu.VMEM((BRANCH_W, LANES), F32)],
        compiler_params=_cparams("parallel"),
        name=f"ssd_t{seq}",
    )(ssd_in, ssd_in, ssd_in, conv_w, conv_b, pc, dsk, ng, init)


def _merge_kernel(x_ref, mod_ref, g1_ref, g2_ref, *rest, n_prompt_tiles):
    br_p = rest[0:N_BRANCH]
    br_s = rest[N_BRANCH:2 * N_BRANCH]
    wg_ref, wb_ref, wo_ref, rwh_ref, rwl_ref, rb_ref, x1_ref, h2_ref, gate_ref, eidx_ref = rest[2 * N_BRANCH:]
    is_prompt = pl.program_id(0) < n_prompt_tiles
    x = x_ref[...]
    mod = mod_ref[...]
    h = _norm_mod(x, g1_ref[...], mod[0:1], mod[1:2]).astype(BF16)
    merged = None
    for i in range(N_BRANCH):
        gate = _sigmoid(_dot(h, wg_ref[:, i * D_MODEL:(i + 1) * D_MODEL]))
        br = jnp.where(is_prompt, br_p[i][...], br_s[i][...])
        proj = _dot(br, wb_ref[i])
        merged = gate * proj if i == 0 else merged + gate * proj
    y = _dot(merged.astype(BF16), wo_ref[...])
    x1 = x + mod[2:3] * y
    x1_ref[...] = x1
    h2 = _norm_mod(x1, g2_ref[...], mod[3:4], mod[4:5])
    hh, hl = _split2(h2)
    for j in range(ROW_TILE):
        h2_ref[pl.ds(j, TOKEN_TILE, stride=ROW_TILE), :] = h2[:, j * LANES:(j + 1) * LANES]
    rwh = rwh_ref[...]
    logits = _dot(hh, rwh) + _dot(hl, rwh) + _dot(hh, rwl_ref[...]) + rb_ref[...]
    lane = lax.broadcasted_iota(jnp.int32, (1, LANES), 1).astype(F32)
    cur = jnp.where(lane < N_EXPERTS, logits, NEG)
    vals, idxs = [], []
    for _ in range(TOP_K):
        m = jnp.max(cur, axis=-1, keepdims=True)
        am = jnp.min(jnp.where(cur == m, lane, float(LANES)), axis=-1, keepdims=True)
        vals.append(m)
        idxs.append(am)
        cur = jnp.where(lane == am, NEG, cur)
    exps = [jnp.exp(v - vals[0]) for v in vals]
    inv = 1.0 / (exps[0] + exps[1] + exps[2] + exps[3])
    gate = jnp.zeros_like(logits)
    eidx = jnp.zeros_like(logits)
    for k in range(TOP_K):
        gate = jnp.where(lane == k, exps[k] * inv, gate)
        eidx = jnp.where(lane == k, idxs[k], eidx)
    gate_ref[...] = gate
    eidx_ref[...] = eidx.astype(jnp.int32)


def _merge(x, mod_tab, g1, g2, br_p, br_s, wg, wb, wo, rwh, rwl, rb, n_prompt_tiles, tiles_per_sample):
    nt = x.shape[0]
    mrow = functools.partial(_mod_row, n_prompt_tiles=n_prompt_tiles, tiles_per_sample=tiles_per_sample)
    const = lambda *shape: pl.BlockSpec(shape, lambda i: (0,) * len(shape))
    rows = lambda w: pl.BlockSpec((TOKEN_TILE, w), lambda i: (i, 0))
    p_rows = pl.BlockSpec((TOKEN_TILE, BRANCH_W), lambda i: (jnp.minimum(i, n_prompt_tiles - 1), 0))
    s_rows = pl.BlockSpec((TOKEN_TILE, BRANCH_W), lambda i: (jnp.maximum(i - n_prompt_tiles, 0), 0))
    return pl.pallas_call(
        functools.partial(_merge_kernel, n_prompt_tiles=n_prompt_tiles),
        grid=(nt // TOKEN_TILE,),
        in_specs=[rows(D_MODEL),
                  pl.BlockSpec((None, 8, D_MODEL), lambda i: (mrow(i), 0, 0)),
                  const(1, D_MODEL), const(1, D_MODEL)]
                 + [p_rows] * N_BRANCH + [s_rows] * N_BRANCH
                 + [const(D_MODEL, N_BRANCH * D_MODEL), const(N_BRANCH, BRANCH_W, D_MODEL),
                    const(D_MODEL, D_MODEL), const(D_MODEL, LANES), const(D_MODEL, LANES), const(1, LANES)],
        out_specs=[rows(D_MODEL), pl.BlockSpec((TOKEN_TILE * ROW_TILE, LANES), lambda i: (i, 0)),
                   rows(LANES), rows(LANES)],
        out_shape=[jax.ShapeDtypeStruct((nt, D_MODEL), F32),
                   jax.ShapeDtypeStruct((nt * ROW_TILE, LANES), F32),
                   jax.ShapeDtypeStruct((nt, LANES), F32),
                   jax.ShapeDtypeStruct((nt, LANES), jnp.int32)],
        compiler_params=_cparams("parallel"),
        name="merge",
    )(x, mod_tab, g1, g2, *br_p, *br_s, wg, wb, wo, rwh, rwl, rb)


MOE_GROUPS = 8
MOE_CHUNK = 2 * D_FF // MOE_GROUPS


def _moe_step(i, last, tok_ref, tok1_ref, tok2_ref, slotp_ref, slot_ref, h_hbm, b1_ref, b2_ref, out_hbm,
              w1_s, w2_s, xs, ys, gsem, ssem, rot):
    nxt, nn = (rot + 1) % 3, (rot + 2) % 3
    x_cur, x_nxt, x_nn = xs[rot], xs[nxt], xs[nn]
    y_cur, y_pp, y_prev = ys[rot], ys[nxt], ys[nn]
    g_cur, g_nxt, g_nn = gsem.at[rot], gsem.at[nxt], gsem.at[nn]
    s_cur, s_pp, s_prev = ssem.at[rot], ssem.at[nxt], ssem.at[nn]
    block_sublanes = MOE_ROWS * ROW_TILE

    def tile(ref, t):
        return ref.at[pl.ds(pl.multiple_of(t * ROW_TILE, ROW_TILE), ROW_TILE), :]

    def gather_copy(tref, buf, sem, r):
        return pltpu.make_async_copy(tile(h_hbm, tref[0, r]), buf.at[pl.ds(r * ROW_TILE, ROW_TILE), :], sem)

    def gather_all(buf, sem):
        return pltpu.make_async_copy(h_hbm.at[pl.ds(0, block_sublanes), :], buf, sem)

    def scatter_copy(sref, buf, sem, r):
        return pltpu.make_async_copy(buf.at[pl.ds(r * ROW_TILE, ROW_TILE), :], tile(out_hbm, sref[0, r]), sem)

    def scatter_all(buf, sem):
        return pltpu.make_async_copy(buf, out_hbm.at[pl.ds(0, block_sublanes), :], sem)

    @pl.when(i == 0)
    def _():
        y_prev[...] = jnp.zeros_like(y_prev)
        for r in range(MOE_ROWS):
            gather_copy(tok_ref, x_cur, g_cur, r).start()
            gather_copy(tok1_ref, x_nxt, g_nxt, r).start()

    @pl.when(i >= 2)
    def _():
        scatter_all(y_cur, s_cur).wait()

    gather_all(x_cur, g_cur).wait()
    x = jnp.concatenate([x_cur[pl.ds(j, MOE_ROWS, stride=ROW_TILE), :] for j in range(ROW_TILE)],
                        axis=1).astype(BF16)
    per = MOE_ROWS // MOE_GROUPS
    acts = []
    for c in range(MOE_GROUPS // 2):
        halves = []
        for part in range(2):
            grp = 2 * c + part
            for r in range(grp * per, (grp + 1) * per):
                gather_copy(tok2_ref, x_nn, g_nn, r).start()
                scatter_copy(slotp_ref, y_prev, s_prev, r).start(priority=1)
            lo = part * D_FF + c * MOE_CHUNK
            halves.append(_dot(x, w1_s[:, lo:lo + MOE_CHUNK]) + b1_ref[:, lo:lo + MOE_CHUNK])
        glu = jnp.minimum(halves[0], SWIGLU_LIMIT)
        lin = jnp.clip(halves[1], -SWIGLU_LIMIT, SWIGLU_LIMIT)
        acts.append((glu * _sigmoid(SWIGLU_ALPHA * glu) * (lin + 1.0)).astype(BF16))
    act = jnp.concatenate(acts, axis=1)
    y = _dot(act, w2_s[...]) + b2_ref[...]
    for j in range(ROW_TILE):
        y_cur[pl.ds(j, MOE_ROWS, stride=ROW_TILE), :] = y[:, j * LANES:(j + 1) * LANES]

    @pl.when(i == last)
    def _():
        for r in range(MOE_ROWS):
            scatter_copy(slot_ref, y_cur, s_cur, r).start()
        gather_all(x_nxt, g_nxt).wait()
        gather_all(x_nn, g_nn).wait()
        scatter_all(y_prev, s_prev).wait()

        @pl.when(i >= 1)
        def _():
            scatter_all(y_pp, s_pp).wait()

        scatter_all(y_cur, s_cur).wait()


def _moe_kernel(be_ref, nu_ref, tok_ref, tok1_ref, tok2_ref, slotp_ref, slot_ref, h_hbm, w1_ref, b1_ref, w2_ref,
                b2_ref, out_hbm, w1_s, w2_s, x0, x1, x2, y0, y1, y2, zbuf, gsem, ssem, zsem):
    i = pl.program_id(0)
    n_used = nu_ref[0]
    prev = be_ref[jnp.maximum(i - 1, 0)]

    @pl.when(jnp.logical_and(i < n_used, jnp.logical_or(i == 0, be_ref[i] != prev)))
    def _():
        w1_s[...] = w1_ref[...].astype(BF16)
        w2_s[...] = w2_ref[...].astype(BF16)

    for rot in range(3):
        @pl.when(jnp.logical_and(i < n_used, i % 3 == rot))
        def _(rot=rot):
            _moe_step(i, n_used - 1, tok_ref, tok1_ref, tok2_ref, slotp_ref, slot_ref, h_hbm, b1_ref, b2_ref,
                      out_hbm, w1_s, w2_s, (x0, x1, x2), (y0, y1, y2), gsem, ssem, rot)

    @pl.when(i >= n_used)
    def _():
        zbuf[...] = jnp.zeros_like(zbuf)
        first = pl.multiple_of(slot_ref[0, 0] * ROW_TILE, ROW_TILE)
        fill = pltpu.make_async_copy(zbuf, out_hbm.at[pl.ds(first, MOE_ROWS * ROW_TILE), :], zsem.at[0])
        fill.start()
        fill.wait()


def _moe_experts(block_expert, n_used, row_token, row_slot, h2, w1, b1, w2, b2, layer):
    n_blocks = row_token.shape[0]
    n_rows = n_blocks * MOE_ROWS
    spare = (n_rows + jnp.arange(MOE_ROWS, dtype=jnp.int32)).reshape(1, 1, MOE_ROWS)
    slot_prev = jnp.concatenate([spare, row_slot[:-1]], axis=0)
    smem_rows = lambda imap: pl.BlockSpec((None, 1, MOE_ROWS), imap, memory_space=pltpu.SMEM)
    expert_block = lambda *shape: pl.BlockSpec((None, None) + shape, lambda i, be, nu: (layer, be[i], 0, 0))
    row_buf = pltpu.VMEM((MOE_ROWS * ROW_TILE, LANES), F32)
    grid_spec = pltpu.PrefetchScalarGridSpec(
        num_scalar_prefetch=2,
        grid=(n_blocks,),
        in_specs=[smem_rows(lambda i, be, nu: (i, 0, 0)),
                  smem_rows(lambda i, be, nu: (jnp.minimum(i + 1, n_blocks - 1), 0, 0)),
                  smem_rows(lambda i, be, nu: (jnp.minimum(i + 2, n_blocks - 1), 0, 0)),
                  smem_rows(lambda i, be, nu: (i, 0, 0)),
                  smem_rows(lambda i, be, nu: (i, 0, 0)),
                  pl.BlockSpec(memory_space=pl.ANY),
                  expert_block(D_MODEL, 2 * D_FF), expert_block(1, 2 * D_FF),
                  expert_block(D_FF, D_MODEL), expert_block(1, D_MODEL)],
        out_specs=pl.BlockSpec(memory_space=pl.ANY),
        scratch_shapes=[pltpu.VMEM((D_MODEL, 2 * D_FF), BF16), pltpu.VMEM((D_FF, D_MODEL), BF16)]
                       + [row_buf] * 7
                       + [pltpu.SemaphoreType.DMA((3,)), pltpu.SemaphoreType.DMA((3,)),
                          pltpu.SemaphoreType.DMA((1,))],
    )
    return pl.pallas_call(
        _moe_kernel,
        grid_spec=grid_spec,
        out_shape=jax.ShapeDtypeStruct(((n_rows + MOE_ROWS) * ROW_TILE, LANES), F32),
        compiler_params=pltpu.CompilerParams(dimension_semantics=("arbitrary",), vmem_limit_bytes=VMEM_LIMIT,
                                             has_side_effects=True),
        name="moe_experts",
    )(block_expert, n_used, row_token, row_token, row_token, slot_prev, row_slot, h2, w1, b1, w2, b2)


def _moe_dispatch(eidx):
    n = eidx.shape[0]
    n_assign = n * TOP_K
    n_blocks = -(-n_assign // MOE_ROWS) + N_EXPERTS
    assert n_assign % MOE_ROWS == 0
    id_bits = (n_assign - 1).bit_length()
    assert N_EXPERTS << (id_bits + 1) < 2 ** 31
    expert = eidx.reshape(-1)
    ids = jnp.arange(n_assign, dtype=jnp.int32)
    experts = jnp.arange(N_EXPERTS, dtype=jnp.int32)
    counts = jnp.sum((expert[:, None] == experts[None, :]).astype(jnp.int32), axis=0)
    padded = (counts + MOE_ROWS - 1) // MOE_ROWS * MOE_ROWS
    pad_ends = jnp.cumsum(padded)
    real_keys = (expert << (id_bits + 1)) | ids
    j = jnp.arange(MOE_ROWS, dtype=jnp.int32)[None, :]
    pad_keys = jnp.where(j < (padded - counts)[:, None],
                         (experts[:, None] << (id_bits + 1)) | (1 << id_bits) | j, jnp.iinfo(jnp.int32).max)
    keys = jnp.sort(jnp.concatenate([real_keys, pad_keys.reshape(-1)]))
    is_pad = ((keys >> id_bits) & 1) == 1
    assign = keys & ((1 << id_bits) - 1)
    filler = n_assign + jnp.cumsum(is_pad.astype(jnp.int32)) - 1
    row_slot = jnp.where(is_pad, filler, (assign % TOP_K) * n + assign // TOP_K).astype(jnp.int32)
    row_token = jnp.where(is_pad, 0, assign // TOP_K).astype(jnp.int32)
    block_start = jnp.arange(n_blocks, dtype=jnp.int32) * MOE_ROWS
    block_expert = jnp.minimum(jnp.sum((pad_ends[None, :] <= block_start[:, None]).astype(jnp.int32), axis=1),
                               N_EXPERTS - 1).astype(jnp.int32)
    n_used = (pad_ends[-1] // MOE_ROWS).astype(jnp.int32).reshape(1)
    return (block_expert, n_used, row_token.reshape(n_blocks, 1, MOE_ROWS),
            row_slot.reshape(n_blocks, 1, MOE_ROWS))


def _combine_kernel(x1_ref, mod_ref, gate_ref, y0_ref, y1_ref, y2_ref, y3_ref, fg_ref, o_ref, *, final):
    gate = gate_ref[...]
    y = None
    for k, yk_ref in enumerate((y0_ref, y1_ref, y2_ref, y3_ref)):
        rows = jnp.concatenate([yk_ref[pl.ds(j, TOKEN_TILE, stride=ROW_TILE), :] for j in range(ROW_TILE)], axis=1)
        yk = gate[:, k:k + 1] * rows
        y = yk if k == 0 else y + yk
    x2 = x1_ref[...] + mod_ref[5:6, :] * y
    if final:
        ms = jnp.mean(x2 * x2, axis=-1, keepdims=True)
        x2 = x2 * lax.rsqrt(ms + EPS) * fg_ref[...]
    o_ref[...] = x2


def _combine(x1, mod_tab, gate, y_slots, final_g, n_prompt_tiles, tiles_per_sample, *, final):
    nt = x1.shape[0]
    tiles = nt // TOKEN_TILE
    mrow = functools.partial(_mod_row, n_prompt_tiles=n_prompt_tiles, tiles_per_sample=tiles_per_sample)
    slot_rows = lambda k: pl.BlockSpec((TOKEN_TILE * ROW_TILE, LANES), lambda i: (k * tiles + i, 0))
    return pl.pallas_call(
        functools.partial(_combine_kernel, final=final),
        grid=(tiles,),
        in_specs=[pl.BlockSpec((TOKEN_TILE, D_MODEL), lambda i: (i, 0)),
                  pl.BlockSpec((None, 8, D_MODEL), lambda i: (mrow(i), 0, 0)),
                  pl.BlockSpec((TOKEN_TILE, LANES), lambda i: (i, 0))]
                 + [slot_rows(k) for k in range(TOP_K)]
                 + [pl.BlockSpec((1, D_MODEL), lambda i: (0, 0))],
        out_specs=pl.BlockSpec((TOKEN_TILE, D_MODEL), lambda i: (i, 0)),
        out_shape=jax.ShapeDtypeStruct((nt, D_MODEL), F32),
        compiler_params=_cparams("parallel"),
        name="combine_final" if final else "combine",
    )(x1, mod_tab, gate, y_slots, y_slots, y_slots, y_slots, final_g)


def _state_to_blocks(init):
    b = init.shape[0]
    st = init.reshape(b, 2, B_HEADS * HEAD_W, B_STATE)
    row = jnp.arange(B_HEADS * HEAD_W)[:, None]
    left = jnp.where(row < 2 * HEAD_W, st, 0.0)
    right = jnp.where(row >= 2 * HEAD_W, st, 0.0)
    return jnp.concatenate([left, right], axis=-1)


def _blocks_to_state(s):
    b = s.shape[0]
    row = jnp.arange(B_HEADS * HEAD_W)[:, None]
    st = jnp.where(row < 2 * HEAD_W, s[..., :B_STATE], s[..., B_STATE:])
    return st.reshape(b, 2, B_HEADS, HEAD_W, B_STATE)


def kernel(x_prompt, x_sample, c, cache_a_k, cache_a_v, cache_c_k, cache_c_v, cache_d_k, cache_d_v, state_ssd, c_ctx, norm1_g, norm2_g, w_mod, b_mod, w_in, a_q_g, a_k_g, b_conv_w, b_conv_b, b_dt_bias, b_a_log, b_d, b_norm_g, c_lam, c_subln_g, d_rpb, w_branch, w_out, router_w, router_b, moe_w1, moe_b1, moe_w2, moe_b2, final_g):
    bp, tp, _ = x_prompt.shape
    bs, ts, _ = x_sample.shape
    depth = w_in.shape[0]
    n_p = bp * tp
    n_s = bs * ts
    assert n_p % TOKEN_TILE == 0 and ts % TOKEN_TILE == 0 and n_p % ts == 0
    n_prompt_tiles = n_p // TOKEN_TILE
    tiles_per_sample = ts // TOKEN_TILE
    past = cache_a_k.shape[2]

    x = jnp.concatenate([x_prompt.reshape(n_p, D_MODEL), x_sample.reshape(n_s, D_MODEL)], axis=0)
    cvec = jnp.zeros((16, D_MODEL), F32).at[0].set(c_ctx).at[1:1 + bs].set(c)
    rope_tab = _rope_table(ts, TOKEN_TILE)
    perm_a = _rope_perm(HEAD_W)
    perm_c = _rope_perm(C_QK_DIM)
    head_of = np.arange(BRANCH_W) // HEAD_W
    bd = jnp.asarray((head_of[:, None] == head_of[None, :]) / HEAD_W, BF16)
    kv_of = np.arange(A_KV_W) // HEAD_W
    ex = jnp.asarray((kv_of[:, None] == (head_of // (N_HEADS // A_KV_HEADS))[None, :])
                     & ((np.arange(A_KV_W) % HEAD_W)[:, None] == (np.arange(BRANCH_W) % HEAD_W)[None, :]), BF16)
    scale_c = C_QK_DIM ** -0.5
    one = jnp.ones((1,), F32)
    rep = N_HEADS // A_KV_HEADS
    zero_state = jnp.zeros((bp, 2, BRANCH_W, LANES), F32)

    new_cache = [[] for _ in range(7)]
    for l in range(depth):
        lam_init = 0.8 - 0.6 * math.exp(-0.3 * l)
        lq = c_lam[l]
        lam = (jnp.exp(jnp.sum(lq[0] * lq[1])) - jnp.exp(jnp.sum(lq[2] * lq[3])) + lam_init).reshape(1)

        wl = w_in[l]
        w1 = jnp.concatenate([wl[:, 0:512], wl[:, 768:1280], wl[:, 512:768], wl[:, 1280:1288],
                              jnp.zeros((D_MODEL, LANES - 8), F32), wl[:, 1288:2824]], axis=1).astype(BF16)
        wg = wl[:, 2824:].astype(BF16)
        wb = w_branch[l].astype(BF16)
        wo = w_out[l].astype(BF16)
        rw = jnp.pad(router_w[l], ((0, 0), (0, LANES - N_EXPERTS)))
        rwh = rw.astype(BF16)
        rwl = (rw - rwh.astype(F32)).astype(BF16)
        rb = jnp.pad(router_b[l], (0, LANES - N_EXPERTS)).reshape(1, LANES)

        mod = _modulation(cvec, w_mod[l], b_mod[l].reshape(1, -1))
        mod_tab = jnp.pad(mod.reshape(16, 6, D_MODEL), ((0, 0), (0, 2), (0, 0)))

        g1 = norm1_g[l].reshape(1, D_MODEL)
        g2 = norm2_g[l].reshape(1, D_MODEL)
        att_a, att_c, att_d, kv_a, kv_c, kv_d, ssd_in = _in_proj(
            x, mod_tab, g1, w1, rope_tab, bd, perm_a, perm_c, ex,
            jnp.tile(a_q_g[l], N_HEADS).reshape(1, BRANCH_W), jnp.tile(a_k_g[l], A_KV_HEADS).reshape(1, A_KV_W),
            n_prompt_tiles, tiles_per_sample)

        ctx_ak = jnp.repeat(cache_a_k[:, l], rep, axis=2).reshape(bs, past, BRANCH_W).astype(BF16)
        ctx_av = jnp.repeat(cache_a_v[:, l], rep, axis=2).reshape(bs, past, BRANCH_W).astype(BF16)
        o_a_p = _mha(att_a, one, n_seq=bp, seq=tp, row0=0, n_maps=1, scale=1.0)
        o_a_s = _mha(att_a, one, n_seq=bs, seq=ts, row0=n_p, n_maps=1, scale=1.0, ctx=(ctx_ak, ctx_av))

        subln = (jnp.tile(c_subln_g[l], N_HEADS).reshape(1, BRANCH_W), bd, 1.0 - lam_init)
        ctx_ck = cache_c_k[:, l].reshape(bs, past, BRANCH_W).astype(BF16)
        ctx_cv = cache_c_v[:, l].reshape(bs, past, BRANCH_W).astype(BF16)
        o_c_p = _mha(att_c, lam, n_seq=bp, seq=tp, row0=0, n_maps=2, scale=scale_c, subln=subln)
        o_c_s = _mha(att_c, lam, n_seq=bs, seq=ts, row0=n_p, n_maps=2, scale=scale_c, ctx=(ctx_ck, ctx_cv),
                     subln=subln)

        o_d_p = _mha(att_d, one, n_seq=bp, seq=tp, row0=0, n_maps=1, scale=1.0)
        o_d_s = _na_attention(att_d, cache_d_k[:, l].reshape(bs, past, BRANCH_W).astype(BF16),
                              cache_d_v[:, l].reshape(bs, past, BRANCH_W).astype(BF16),
                              _na_bias_table(d_rpb[l], ts // GRID_W), n_seq=bs, seq=ts, row0=n_p)

        dtb = b_dt_bias[l].reshape(8)
        a_neg = -jnp.exp(b_a_log[l].reshape(8))
        ssd_args = (jnp.pad(b_conv_w[l], ((0, 8 - B_CONV_W), (0, 0))), b_conv_b[l].reshape(1, B_XBC),
                    jnp.zeros((8, LANES), F32).at[0, :8].set(dtb).at[1, :8].set(a_neg),
                    jnp.repeat(b_d[l], HEAD_W).reshape(1, BRANCH_W), b_norm_g[l].reshape(1, BRANCH_W))
        o_b_p, st_p = _ssd_branch(ssd_in, *ssd_args, zero_state, n_seq=bp, seq=tp, row0=0)
        o_b_s, _ = _ssd_branch(ssd_in, *ssd_args, _state_to_blocks(state_ssd[:, l]), n_seq=bs, seq=ts, row0=n_p)

        x1, h2, gate, eidx = _merge(x, mod_tab, g1, g2, (o_a_p, o_b_p, o_c_p, o_d_p), (o_a_s, o_b_s, o_c_s, o_d_s),
                                    wg, wb, wo, rwh, rwl, rb, n_prompt_tiles, tiles_per_sample)
        block_expert, n_used, row_token, row_slot = _moe_dispatch(eidx[:, :TOP_K])
        y_slots = _moe_experts(block_expert, n_used, row_token, row_slot, h2, moe_w1,
                               moe_b1.reshape(depth, N_EXPERTS, 1, -1), moe_w2,
                               moe_b2.reshape(depth, N_EXPERTS, 1, -1), l)
        x = _combine(x1, mod_tab, gate, y_slots, final_g.reshape(1, D_MODEL), n_prompt_tiles, tiles_per_sample,
                     final=(l == depth - 1))

        new_cache[0].append(kv_a[:n_p, :A_KV_W].reshape(bp, tp, A_KV_HEADS, HEAD_W))
        new_cache[1].append(kv_a[:n_p, A_KV_W:].reshape(bp, tp, A_KV_HEADS, HEAD_W))
        new_cache[2].append(kv_c[:n_p, :BRANCH_W].reshape(bp, tp, N_HEADS, 2 * C_QK_DIM))
        new_cache[3].append(kv_c[:n_p, BRANCH_W:].reshape(bp, tp, N_HEADS, HEAD_W))
        new_cache[4].append(kv_d[:n_p, :BRANCH_W].reshape(bp, tp, N_HEADS, HEAD_W))
        new_cache[5].append(kv_d[:n_p, BRANCH_W:].reshape(bp, tp, N_HEADS, HEAD_W))
        new_cache[6].append(_blocks_to_state(st_p))

    y_prompt = x[:n_p].reshape(bp, tp, D_MODEL)
    y_sample = x[n_p:].reshape(bs, ts, D_MODEL)
    return (y_prompt, y_sample) + tuple(jnp.stack(v, axis=1) for v in new_cache)
```

```python
import functools
import math

import jax
import jax.numpy as jnp
import numpy as np
from jax import lax
from jax.experimental import pallas as pl
from jax.experimental.pallas import tpu as pltpu

F32 = jnp.float32
BF16 = jnp.bfloat16

D_MODEL = 1024
N_BRANCH = 4
BRANCH_W = 256
HEAD_W = 64
N_HEADS = 4
GRID_W = 64
ROPE_THETA = 10000.0
EPS = 1e-6
A_KV_HEADS = 2
A_KV_W = A_KV_HEADS * HEAD_W
C_QK_DIM = 32
B_HEADS = 4
B_STATE = 64
B_CHUNK = 128
B_XBC = 512
B_CONV_W = 5
CONV_PAD = 8
NA_ROWS = 8
NA_COLS = 16
N_EXPERTS = 32
TOP_K = 4
D_FF = 1024
SWIGLU_LIMIT = 7.0
SWIGLU_ALPHA = 1.702
LANES = 128
ROW_TILE = D_MODEL // LANES
NEG = -1e30

TOKEN_TILE = 512
MOE_ROWS = 256
ATT_W = 3 * BRANCH_W
SSD_W = B_XBC + BRANCH_W + LANES
OFF_A, OFF_B, OFF_C, OFF_D = 0, 512, 512 + SSD_W, 512 + SSD_W + ATT_W
W1_COLS = OFF_D + ATT_W
VMEM_LIMIT = 56 * 1024 * 1024


def _cparams(*sem):
    return pltpu.CompilerParams(dimension_semantics=sem, vmem_limit_bytes=VMEM_LIMIT)


def _dot(a, b):
    return jnp.dot(a, b, preferred_element_type=F32)


def _dot_nt(a, b):
    return lax.dot_general(a, b, (((1,), (1,)), ((), ())), preferred_element_type=F32)


def _dot_tn(a, b):
    return lax.dot_general(a, b, (((0,), (0,)), ((), ())), preferred_element_type=F32)


def _sigmoid(x):
    return 0.5 * jnp.tanh(0.5 * x) + 0.5


def _split2(x):
    hi = x.astype(BF16)
    return hi, (x - hi.astype(F32)).astype(BF16)


def _split3(x):
    h1 = x.astype(BF16)
    r1 = x - h1.astype(F32)
    h2 = r1.astype(BF16)
    h3 = (r1 - h2.astype(F32)).astype(BF16)
    return h1, h2, h3


def _norm_mod(x, g, shift, scale):
    ms = jnp.mean(x * x, axis=-1, keepdims=True)
    return (x * lax.rsqrt(ms + EPS)) * g * (1.0 + scale) + shift


def _head_rmsnorm(x, bd, g):
    hi, lo = _split2(x * x)
    ms = _dot(hi, bd) + _dot(lo, bd)
    return x * lax.rsqrt(ms + EPS) * g


def _rope(x, perm, cos, sin):
    hi, lo = _split2(x)
    return x * cos + (_dot(hi, perm) + _dot(lo, perm)) * sin


def _mod_kernel(c_ref, w_ref, b_ref, o_ref):
    c = c_ref[...]
    s = (c * _sigmoid(c)).astype(BF16)
    o_ref[...] = _dot(s, w_ref[...].astype(BF16)) + b_ref[...]


def _modulation(cvec, w_mod, b_mod):
    n = w_mod.shape[1]
    tn = 1536
    return pl.pallas_call(
        _mod_kernel,
        grid=(n // tn,),
        in_specs=[pl.BlockSpec((16, D_MODEL), lambda j: (0, 0)),
                  pl.BlockSpec((D_MODEL, tn), lambda j: (0, j)),
                  pl.BlockSpec((1, tn), lambda j: (0, j))],
        out_specs=pl.BlockSpec((16, tn), lambda j: (0, j)),
        out_shape=jax.ShapeDtypeStruct((16, n), F32),
        compiler_params=_cparams("parallel"),
        name="modulation",
    )(cvec, w_mod, b_mod)


def _mod_row(i, n_prompt_tiles, tiles_per_sample):
    return jnp.where(i < n_prompt_tiles, 0, 1 + (i - n_prompt_tiles) // tiles_per_sample)


def _rope_row(i, n_prompt_tiles, tiles_per_sample):
    return jnp.where(i < n_prompt_tiles, 0, 1 + (i - n_prompt_tiles) % tiles_per_sample)


def _inproj_kernel(x_ref, mod_ref, g_ref, w_ref, rope_ref, bd_ref, pa_ref, pc_ref, ex_ref, gq_ref, gk_ref,
                   atta_ref, attc_ref, attd_ref, kva_ref, kvc_ref, kvd_ref, ssd_ref):
    mod = mod_ref[...]
    h = _norm_mod(x_ref[...], g_ref[...], mod[0:1], mod[1:2])
    u = _dot(h.astype(BF16), w_ref[...])
    cos_a, sin_a = rope_ref[:, 0:256], rope_ref[:, 256:512]
    cos_c, sin_c = rope_ref[:, 512:768], rope_ref[:, 768:1024]
    q_scale = HEAD_W ** -0.5

    q = _rope(_head_rmsnorm(u[:, OFF_A:OFF_A + 256], bd_ref[...], gq_ref[...]), pa_ref[...], cos_a, sin_a)
    k = _rope(_head_rmsnorm(u[:, OFF_A + 256:OFF_A + 384], bd_ref[0:A_KV_W, 0:A_KV_W], gk_ref[...]),
              pa_ref[0:A_KV_W, 0:A_KV_W], cos_a[:, 0:A_KV_W], sin_a[:, 0:A_KV_W])
    v = u[:, OFF_A + 384:OFF_A + 512]
    ex = ex_ref[...]
    atta_ref[:, 0:256] = (q * q_scale).astype(BF16)
    atta_ref[:, 256:512] = _dot(k.astype(BF16), ex).astype(BF16)
    atta_ref[:, 512:768] = _dot(v.astype(BF16), ex).astype(BF16)
    kva_ref[:, 0:A_KV_W] = k
    kva_ref[:, A_KV_W:2 * A_KV_W] = v

    ssd_ref[...] = u[:, OFF_B:OFF_B + SSD_W]

    cq = _rope(u[:, OFF_C:OFF_C + 256], pc_ref[...], cos_c, sin_c)
    ck = _rope(u[:, OFF_C + 256:OFF_C + 512], pc_ref[...], cos_c, sin_c)
    cv = u[:, OFF_C + 512:OFF_C + 768]
    attc_ref[:, 0:256] = cq.astype(BF16)
    attc_ref[:, 256:512] = ck.astype(BF16)
    attc_ref[:, 512:768] = cv.astype(BF16)
    kvc_ref[:, 0:256] = ck
    kvc_ref[:, 256:512] = cv

    attd_ref[:, 0:256] = (u[:, OFF_D:OFF_D + 256] * q_scale).astype(BF16)
    attd_ref[:, 256:768] = u[:, OFF_D + 256:OFF_D + 768].astype(BF16)
    kvd_ref[...] = u[:, OFF_D + 256:OFF_D + 768]


def _in_proj(x, mod_tab, g, w1, rope_tab, bd, pa, pc, ex, gq, gk, n_prompt_tiles, tiles_per_sample):
    nt = x.shape[0]
    mrow = functools.partial(_mod_row, n_prompt_tiles=n_prompt_tiles, tiles_per_sample=tiles_per_sample)
    rrow = functools.partial(_rope_row, n_prompt_tiles=n_prompt_tiles, tiles_per_sample=tiles_per_sample)
    const = lambda *shape: pl.BlockSpec(shape, lambda i: (0,) * len(shape))
    rows = lambda w: pl.BlockSpec((TOKEN_TILE, w), lambda i: (i, 0))
    widths = (ATT_W, ATT_W, ATT_W, 2 * A_KV_W, 2 * BRANCH_W, 2 * BRANCH_W, SSD_W)
    dtypes = (BF16, BF16, BF16, F32, F32, F32, F32)
    return pl.pallas_call(
        _inproj_kernel,
        grid=(nt // TOKEN_TILE,),
        in_specs=[rows(D_MODEL),
                  pl.BlockSpec((None, 8, D_MODEL), lambda i: (mrow(i), 0, 0)),
                  const(1, D_MODEL), const(D_MODEL, W1_COLS),
                  pl.BlockSpec((TOKEN_TILE, 4 * BRANCH_W), lambda i: (rrow(i), 0)),
                  const(BRANCH_W, BRANCH_W), const(BRANCH_W, BRANCH_W), const(BRANCH_W, BRANCH_W),
                  const(A_KV_W, BRANCH_W), const(1, BRANCH_W), const(1, A_KV_W)],
        out_specs=[rows(w) for w in widths],
        out_shape=[jax.ShapeDtypeStruct((nt, w), dt) for w, dt in zip(widths, dtypes)],
        compiler_params=_cparams("parallel"),
        name="in_proj",
    )(x, mod_tab, g, w1, rope_tab, bd, pa, pc, ex, gq, gk)


def _rope_perm(dim):
    p = np.zeros((BRANCH_W, BRANCH_W), np.float32)
    s = dim // 4
    for j in range(BRANCH_W):
        quarter = (j % dim) // s
        if quarter % 2 == 0:
            p[j + s, j] = -1.0
        else:
            p[j - s, j] = 1.0
    return jnp.asarray(p, BF16)


def _rope_table(t, n_identity):
    pos = jnp.arange(t)
    rows = (pos // GRID_W).astype(F32)
    cols = (pos % GRID_W).astype(F32)
    parts = []
    for dim in (HEAD_W, C_QK_DIM):
        axis_dim = dim // 2
        inv = ROPE_THETA ** (-jnp.arange(0, axis_dim, 2, dtype=F32) / axis_dim)
        ang_r = rows[:, None] * inv[None, :]
        ang_c = cols[:, None] * inv[None, :]
        ang = jnp.concatenate([ang_r, ang_r, ang_c, ang_c], axis=-1)
        reps = BRANCH_W // dim
        parts += [jnp.tile(jnp.cos(ang), (1, reps)), jnp.tile(jnp.sin(ang), (1, reps))]
    tab = jnp.concatenate(parts, axis=1)
    ident = jnp.concatenate([jnp.ones((n_identity, BRANCH_W), F32), jnp.zeros((n_identity, BRANCH_W), F32)] * 2, axis=1)
    return jnp.concatenate([ident, tab], axis=0)


def _mha_kernel(lam_ref, q_ref, k_ref, v_ref, *rest, n_maps, scale, has_ctx, subln_scale):
    rest = list(rest)
    ck_ref, cv_ref = (rest.pop(0), rest.pop(0)) if has_ctx else (None, None)
    g_ref, bd_ref = (rest.pop(0), rest.pop(0)) if subln_scale is not None else (None, None)
    o_ref = rest.pop(0)
    q = q_ref[...]
    k = k_ref[...]
    v = v_ref[...]
    tq = q.shape[0]
    lane = lax.broadcasted_iota(jnp.int32, (1, BRANCH_W), 1)
    sub_w = HEAD_W // n_maps
    acc = jnp.zeros((tq, BRANCH_W), F32)
    for h in range(N_HEADS):
        oh = None
        for j in range(n_maps):
            qm = jnp.where((lane // sub_w) == (h * n_maps + j), q, jnp.zeros_like(q))
            s = _dot_nt(qm, k)
            if scale != 1.0:
                s = s * scale
            m = jnp.max(s, axis=-1, keepdims=True)
            if has_ctx:
                sc = _dot_nt(qm, ck_ref[...])
                if scale != 1.0:
                    sc = sc * scale
                m = jnp.maximum(m, jnp.max(sc, axis=-1, keepdims=True))
                pc = jnp.exp(sc - m)
            p = jnp.exp(s - m)
            l = jnp.sum(p, axis=-1, keepdims=True)
            pv = _dot(p.astype(BF16), v)
            if has_ctx:
                l = l + jnp.sum(pc, axis=-1, keepdims=True)
                pv = pv + _dot(pc.astype(BF16), cv_ref[...])
            pv = pv * (1.0 / l)
            oh = pv if j == 0 else oh - lam_ref[0] * pv
        acc = jnp.where((lane // HEAD_W) == h, oh, acc)
    if subln_scale is not None:
        acc = _head_rmsnorm(acc, bd_ref[...], g_ref[...]) * subln_scale
    o_ref[...] = acc.astype(o_ref.dtype)


def _mha(att, lam, *, n_seq, seq, row0, n_maps, scale, ctx=None, subln=None, tq=256):
    qb0 = row0 // tq
    kb0 = row0 // seq
    nq = seq // tq
    assert row0 % seq == 0 and seq % tq == 0
    in_specs = [pl.BlockSpec(memory_space=pltpu.SMEM),
                pl.BlockSpec((tq, BRANCH_W), lambda i, j: (qb0 + i * nq + j, 0)),
                pl.BlockSpec((seq, BRANCH_W), lambda i, j: (kb0 + i, 1)),
                pl.BlockSpec((seq, BRANCH_W), lambda i, j: (kb0 + i, 2))]
    args = [lam, att, att, att]
    kk = seq
    if ctx is not None:
        kc = ctx[0].shape[1]
        kk += kc
        in_specs += [pl.BlockSpec((None, kc, BRANCH_W), lambda i, j: (i, 0, 0))] * 2
        args += list(ctx)
    subln_scale = None
    if subln is not None:
        in_specs += [pl.BlockSpec((1, BRANCH_W), lambda i, j: (0, 0)),
                     pl.BlockSpec((BRANCH_W, BRANCH_W), lambda i, j: (0, 0))]
        args += [subln[0], subln[1]]
        subln_scale = subln[2]
    return pl.pallas_call(
        functools.partial(_mha_kernel, n_maps=n_maps, scale=scale, has_ctx=ctx is not None, subln_scale=subln_scale),
        grid=(n_seq, nq),
        in_specs=in_specs,
        out_specs=pl.BlockSpec((tq, BRANCH_W), lambda i, j: (i * nq + j, 0)),
        out_shape=jax.ShapeDtypeStruct((n_seq * seq, BRANCH_W), BF16),
        compiler_params=_cparams("parallel", "parallel"),
        name=f"mha_maps{n_maps}_k{kk}",
    )(*args)


def _na_row0(r, n_rows):
    return jnp.clip(r - NA_ROWS // 2, 0, n_rows - NA_ROWS)


def _na_kernel(q_ref, k_ref, v_ref, ck_ref, cv_ref, bias_ref, o_ref, *, n_rows):
    win = NA_ROWS * GRID_W
    lane = lax.broadcasted_iota(jnp.int32, (1, BRANCH_W), 1)
    ck = ck_ref[...]
    cv = cv_ref[...]

    def row_body(r, carry):
        r0 = _na_row0(r, n_rows)
        start = pl.multiple_of(r0 * GRID_W, GRID_W)
        rows = pl.ds(pl.multiple_of(r * GRID_W, GRID_W), GRID_W)
        q = q_ref[rows, :]
        kw = k_ref[pl.ds(start, win), :]
        vw = v_ref[pl.ds(start, win), :]
        qs = jnp.concatenate(
            [jnp.where((lane // HEAD_W) == h, q, jnp.zeros_like(q)) for h in range(N_HEADS)], axis=0)
        s_loc = _dot_nt(qs, kw) + bias_ref[r - r0]
        s_ctx = _dot_nt(qs, ck)
        m = jnp.maximum(jnp.max(s_loc, axis=-1, keepdims=True), jnp.max(s_ctx, axis=-1, keepdims=True))
        p_loc = jnp.exp(s_loc - m)
        p_ctx = jnp.exp(s_ctx - m)
        l = jnp.sum(p_loc, axis=-1, keepdims=True) + jnp.sum(p_ctx, axis=-1, keepdims=True)
        o = (_dot(p_loc.astype(BF16), vw) + _dot(p_ctx.astype(BF16), cv)) * (1.0 / l)
        acc = jnp.zeros((GRID_W, BRANCH_W), F32)
        for h in range(N_HEADS):
            acc = jnp.where((lane // HEAD_W) == h, o[h * GRID_W:(h + 1) * GRID_W], acc)
        o_ref[rows, :] = acc.astype(o_ref.dtype)
        return carry

    lax.fori_loop(0, n_rows, row_body, 0)


def _na_attention(att, ck, cv, bias_tab, *, n_seq, seq, row0):
    n_rows = seq // GRID_W
    kk = ck.shape[1]
    win = NA_ROWS * GRID_W
    b0 = row0 // seq
    assert row0 % seq == 0
    return pl.pallas_call(
        functools.partial(_na_kernel, n_rows=n_rows),
        grid=(n_seq,),
        in_specs=[pl.BlockSpec((seq, BRANCH_W), lambda i: (b0 + i, 0)),
                  pl.BlockSpec((seq, BRANCH_W), lambda i: (b0 + i, 1)),
                  pl.BlockSpec((seq, BRANCH_W), lambda i: (b0 + i, 2)),
                  pl.BlockSpec((None, kk, BRANCH_W), lambda i: (i, 0, 0)),
                  pl.BlockSpec((None, kk, BRANCH_W), lambda i: (i, 0, 0)),
                  pl.BlockSpec((NA_ROWS, N_HEADS * GRID_W, win), lambda i: (0, 0, 0))],
        out_specs=pl.BlockSpec((seq, BRANCH_W), lambda i: (i, 0)),
        out_shape=jax.ShapeDtypeStruct((n_seq * seq, BRANCH_W), BF16),
        compiler_params=_cparams("parallel"),
        name="na_attention",
    )(att, att, att, ck, cv, bias_tab)


def _na_bias_table(rpb, n_rows):
    var = np.arange(NA_ROWS)[:, None, None]
    j = np.arange(NA_ROWS)[None, :, None]
    sel_r = (np.arange(2 * NA_ROWS - 1)[None, None, :] == j - var + (NA_ROWS - 1)).astype(np.float32)
    col = np.arange(GRID_W)[:, None, None]
    kc = np.arange(GRID_W)[None, :, None]
    c0 = np.clip(col - NA_COLS // 2, 0, GRID_W - NA_COLS)
    valid = ((kc >= c0) & (kc < c0 + NA_COLS))[:, :, 0]
    sel_c = (np.arange(2 * NA_COLS - 1)[None, None, :] == kc - col + (NA_COLS - 1)).astype(np.float32)
    tab = jnp.einsum('hrc,vjr,xkc->vhxjk', rpb.astype(F32), sel_r, sel_c, precision=lax.Precision.HIGHEST)
    tab = jnp.where(valid[None, None, :, None, :], tab, NEG)
    return tab.reshape(NA_ROWS, N_HEADS * GRID_W, NA_ROWS * GRID_W)


def _softplus(x):
    return jnp.maximum(x, 0.0) + jnp.log1p(jnp.exp(-jnp.abs(x)))


def _expand_heads(colmat, d, lane256):
    out = jnp.zeros((colmat.shape[0], BRANCH_W), F32)
    for h in range(B_HEADS):
        j = d * B_HEADS + h
        out = jnp.where((lane256 // HEAD_W) == h,
                        jnp.broadcast_to(colmat[:, j:j + 1], (colmat.shape[0], BRANCH_W)), out)
    return out


def _ssd_kernel(xbc_ref, z_ref, dt_ref, cw_ref, cb_ref, pc_ref, dsk_ref, ng_ref, init_ref,
                o_ref, st_ref, xpad_s, xc_s, yf_s, yb_s, sf_s, sb_s, *, seq):
    n_chunks = seq // B_CHUNK
    L = B_CHUNK
    ri = lax.broadcasted_iota(jnp.int32, (L, L), 0)
    ci = lax.broadcasted_iota(jnp.int32, (L, L), 1)
    low = ci <= ri
    upp = ri <= ci
    low_b = jnp.where(low, 1.0, 0.0).astype(BF16)
    upp_b = jnp.where(upp, 1.0, 0.0).astype(BF16)
    lane128 = lax.broadcasted_iota(jnp.int32, (1, LANES), 1)
    lane256 = lax.broadcasted_iota(jnp.int32, (1, BRANCH_W), 1)
    row256 = lax.broadcasted_iota(jnp.int32, (BRANCH_W, 1), 0)
    blockmask = (row256 // (2 * HEAD_W)) == (lane128 // B_STATE)

    xpad_s[0:CONV_PAD, :] = jnp.zeros((CONV_PAD, B_XBC), F32)
    xpad_s[seq + CONV_PAD:seq + 2 * CONV_PAD, :] = jnp.zeros((CONV_PAD, B_XBC), F32)

    def pad_body(c, carry):
        base = pl.multiple_of(c * L, L)
        xpad_s[pl.ds(pl.multiple_of(base + CONV_PAD, CONV_PAD), L), :] = xbc_ref[pl.ds(base, L), :]
        return carry

    lax.fori_loop(0, n_chunks, pad_body, 0)
    cw = cw_ref[...]
    cb = cb_ref[...]

    def conv_body(c, carry):
        base = pl.multiple_of(c * L, L)
        w = xpad_s[pl.ds(base, L + 2 * CONV_PAD), :]
        acc = jnp.zeros((L, B_XBC), F32) + cb
        for kk in range(B_CONV_W):
            off = CONV_PAD - B_CONV_W // 2 + kk
            acc = acc + w[off:off + L, :] * cw[kk:kk + 1, :]
        xc_s[pl.ds(base, L), :] = acc * _sigmoid(acc)
        return carry

    lax.fori_loop(0, n_chunks, conv_body, 0)

    dt_bias = pc_ref[0:1, :]
    a_neg = pc_ref[1:2, :]
    dsk = dsk_ref[...]
    ng = ng_ref[...]

    def chunk(c, d, s_ref):
        base = pl.multiple_of(c * L, L)
        xs = xc_s[pl.ds(base, L), 0:BRANCH_W]
        bm = xc_s[pl.ds(base, L), BRANCH_W:BRANCH_W + LANES].astype(BF16)
        cm = xc_s[pl.ds(base, L), BRANCH_W + LANES:B_XBC].astype(BF16)
        dtc = _softplus(dt_ref[pl.ds(base, L), :] + dt_bias)
        da_c = dtc * a_neg
        da_r = da_c.T
        tri_c = low_b if d == 0 else upp_b
        tri_r = upp_b if d == 0 else low_b
        c1, c2, c3 = _split3(da_c)
        cum_c = _dot(tri_c, c1) + _dot(tri_c, c2) + _dot(tri_c, c3)
        r1, r2, r3 = _split3(da_r)
        cum_r = _dot(r1, tri_r) + _dot(r2, tri_r) + _dot(r3, tri_r)
        cum_end = cum_c[L - 1:L, :] if d == 0 else cum_c[0:1, :]
        dmask = low if d == 0 else upp

        g0 = _dot_nt(jnp.where(lane128 < B_STATE, cm, jnp.zeros_like(cm)), bm)
        g1 = _dot_nt(jnp.where(lane128 >= B_STATE, cm, jnp.zeros_like(cm)), bm)
        dt_x = _expand_heads(dtc, d, lane256)
        e_a = _expand_heads(jnp.exp(cum_c), d, lane256)
        t_e = _expand_heads(jnp.exp(cum_end - cum_c), d, lane256)
        xdt = xs * dt_x
        xdt_b = xdt.astype(BF16)
        y = jnp.zeros((L, BRANCH_W), F32)
        for h in range(B_HEADS):
            j = d * B_HEADS + h
            col = jnp.broadcast_to(cum_c[:, j:j + 1], (L, L))
            row = jnp.broadcast_to(cum_r[j:j + 1, :], (L, L))
            dec = jnp.exp(jnp.where(dmask, col - row, NEG))
            sc = ((g0 if h < 2 else g1) * dec).astype(BF16)
            y = jnp.where((lane256 // HEAD_W) == h, _dot(sc, xdt_b), y)
        state = s_ref[...]
        y = y + _dot_nt(cm, state.astype(BF16)) * e_a
        zmat = _dot_tn((xdt * t_e).astype(BF16), bm)
        e_end = jnp.exp(cum_end)
        cd = jnp.zeros((BRANCH_W, LANES), F32)
        for h in range(B_HEADS):
            j = d * B_HEADS + h
            cd = jnp.where((row256 // HEAD_W) == h, jnp.broadcast_to(e_end[:, j:j + 1], (BRANCH_W, LANES)), cd)
        s_ref[...] = state * cd + jnp.where(blockmask, zmat, 0.0)
        return base, y

    sf_s[...] = init_ref[0]
    sb_s[...] = init_ref[1]

    def scan_body(c, carry):
        base, y = chunk(c, 0, sf_s)
        yf_s[pl.ds(base, L), :] = y
        base, y = chunk(n_chunks - 1 - c, 1, sb_s)
        yb_s[pl.ds(base, L), :] = y
        return carry

    lax.fori_loop(0, n_chunks, scan_body, 0)
    st_ref[0] = sf_s[...]
    st_ref[1] = sb_s[...]

    def out_body(c, carry):
        base = pl.multiple_of(c * L, L)
        xs = xc_s[pl.ds(base, L), 0:BRANCH_W]
        z = z_ref[pl.ds(base, L), :]
        yt = (yf_s[pl.ds(base, L), :] + yb_s[pl.ds(base, L), :] + dsk * xs) * (z * _sigmoid(z))
        ms = jnp.mean(yt * yt, axis=-1, keepdims=True)
        o_ref[pl.ds(base, L), :] = (yt * lax.rsqrt(ms + EPS) * ng).astype(o_ref.dtype)
        return carry

    lax.fori_loop(0, n_chunks, out_body, 0)


def _ssd_branch(ssd_in, conv_w, conv_b, pc, dsk, ng, init, *, n_seq, seq, row0):
    b0 = row0 // seq
    assert row0 % seq == 0
    full = lambda *shape: pl.BlockSpec(shape, lambda i: (0,) * len(shape))
    return pl.pallas_call(
        functools.partial(_ssd_kernel, seq=seq),
        grid=(n_seq,),
        in_specs=[pl.BlockSpec((seq, B_XBC), lambda i: (b0 + i, 0)),
                  pl.BlockSpec((seq, BRANCH_W), lambda i: (b0 + i, B_XBC // BRANCH_W)),
                  pl.BlockSpec((seq, LANES), lambda i: (b0 + i, (B_XBC + BRANCH_W) // LANES)),
                  full(8, B_XBC), full(1, B_XBC), full(8, LANES), full(1, BRANCH_W), full(1, BRANCH_W),
                  pl.BlockSpec((None, 2, BRANCH_W, LANES), lambda i: (i, 0, 0, 0))],
        out_specs=[pl.BlockSpec((seq, BRANCH_W), lambda i: (i, 0)),
                   pl.BlockSpec((None, 2, BRANCH_W, LANES), lambda i: (i, 0, 0, 0))],
        out_shape=[jax.ShapeDtypeStruct((n_seq * seq, BRANCH_W), BF16),
                   jax.ShapeDtypeStruct((n_seq, 2, BRANCH_W, LANES), F32)],
        scratch_shapes=[pltpu.VMEM((seq + 2 * CONV_PAD, B_XBC), F32), pltpu.VMEM((seq, B_XBC), F32),
                        pltpu.VMEM((seq, BRANCH_W), F32), pltpu.VMEM((seq, BRANCH_W), F32),
                        pltpu.VMEM((BRANCH_W, LANES), F32), pltpu.VMEM((BRANCH_W, LANES), F32)],
        compiler_params=_cparams("parallel"),
        name=f"ssd_t{seq}",
    )(ssd_in, ssd_in, ssd_in, conv_w, conv_b, pc, dsk, ng, init)


def _merge_kernel(x_ref, mod_ref, g1_ref, g2_ref, *rest, n_prompt_tiles):
    br_p = rest[0:N_BRANCH]
    br_s = rest[N_BRANCH:2 * N_BRANCH]
    wg_ref, wb_ref, wo_ref, rwh_ref, rwl_ref, rb_ref, x1_ref, h2_ref, gate_ref, eidx_ref = rest[2 * N_BRANCH:]
    is_prompt = pl.program_id(0) < n_prompt_tiles
    x = x_ref[...]
    mod = mod_ref[...]
    h = _norm_mod(x, g1_ref[...], mod[0:1], mod[1:2]).astype(BF16)
    merged = None
    for i in range(N_BRANCH):
        gate = _sigmoid(_dot(h, wg_ref[:, i * D_MODEL:(i + 1) * D_MODEL]))
        br = jnp.where(is_prompt, br_p[i][...], br_s[i][...])
        proj = _dot(br, wb_ref[i])
        merged = gate * proj if i == 0 else merged + gate * proj
    y = _dot(merged.astype(BF16), wo_ref[...])
    x1 = x + mod[2:3] * y
    x1_ref[...] = x1
    h2 = _norm_mod(x1, g2_ref[...], mod[3:4], mod[4:5])
    hh, hl = _split2(h2)
    for j in range(ROW_TILE):
        h2_ref[pl.ds(j, TOKEN_TILE, stride=ROW_TILE), :] = h2[:, j * LANES:(j + 1) * LANES]
    rwh = rwh_ref[...]
    logits = _dot(hh, rwh) + _dot(hl, rwh) + _dot(hh, rwl_ref[...]) + rb_ref[...]
    lane = lax.broadcasted_iota(jnp.int32, (1, LANES), 1).astype(F32)
    cur = jnp.where(lane < N_EXPERTS, logits, NEG)
    vals, idxs = [], []
    for _ in range(TOP_K):
        m = jnp.max(cur, axis=-1, keepdims=True)
        am = jnp.min(jnp.where(cur == m, lane, float(LANES)), axis=-1, keepdims=True)
        vals.append(m)
        idxs.append(am)
        cur = jnp.where(lane == am, NEG, cur)
    exps = [jnp.exp(v - vals[0]) for v in vals]
    inv = 1.0 / (exps[0] + exps[1] + exps[2] + exps[3])
    gate = jnp.zeros_like(logits)
    eidx = jnp.zeros_like(logits)
    for k in range(TOP_K):
        gate = jnp.where(lane == k, exps[k] * inv, gate)
        eidx = jnp.where(lane == k, idxs[k], eidx)
    gate_ref[...] = gate
    eidx_ref[...] = eidx.astype(jnp.int32)


def _merge(x, mod_tab, g1, g2, br_p, br_s, wg, wb, wo, rwh, rwl, rb, n_prompt_tiles, tiles_per_sample):
    nt = x.shape[0]
    mrow = functools.partial(_mod_row, n_prompt_tiles=n_prompt_tiles, tiles_per_sample=tiles_per_sample)
    const = lambda *shape: pl.BlockSpec(shape, lambda i: (0,) * len(shape))
    rows = lambda w: pl.BlockSpec((TOKEN_TILE, w), lambda i: (i, 0))
    p_rows = pl.BlockSpec((TOKEN_TILE, BRANCH_W), lambda i: (jnp.minimum(i, n_prompt_tiles - 1), 0))
    s_rows = pl.BlockSpec((TOKEN_TILE, BRANCH_W), lambda i: (jnp.maximum(i - n_prompt_tiles, 0), 0))
    return pl.pallas_call(
        functools.partial(_merge_kernel, n_prompt_tiles=n_prompt_tiles),
        grid=(nt // TOKEN_TILE,),
        in_specs=[rows(D_MODEL),
                  pl.BlockSpec((None, 8, D_MODEL), lambda i: (mrow(i), 0, 0)),
                  const(1, D_MODEL), const(1, D_MODEL)]
                 + [p_rows] * N_BRANCH + [s_rows] * N_BRANCH
                 + [const(D_MODEL, N_BRANCH * D_MODEL), const(N_BRANCH, BRANCH_W, D_MODEL),
                    const(D_MODEL, D_MODEL), const(D_MODEL, LANES), const(D_MODEL, LANES), const(1, LANES)],
        out_specs=[rows(D_MODEL), pl.BlockSpec((TOKEN_TILE * ROW_TILE, LANES), lambda i: (i, 0)),
                   rows(LANES), rows(LANES)],
        out_shape=[jax.ShapeDtypeStruct((nt, D_MODEL), F32),
                   jax.ShapeDtypeStruct((nt * ROW_TILE, LANES), F32),
                   jax.ShapeDtypeStruct((nt, LANES), F32),
                   jax.ShapeDtypeStruct((nt, LANES), jnp.int32)],
        compiler_params=_cparams("parallel"),
        name="merge",
    )(x, mod_tab, g1, g2, *br_p, *br_s, wg, wb, wo, rwh, rwl, rb)


MOE_GROUPS = 8
MOE_CHUNK = 2 * D_FF // MOE_GROUPS


def _moe_step(i, last, tok_ref, tok1_ref, tok2_ref, slotp_ref, slot_ref, h_hbm, b1_ref, b2_ref, out_hbm,
              w1_s, w2_s, xs, ys, gsem, ssem, rot):
    nxt, nn = (rot + 1) % 3, (rot + 2) % 3
    x_cur, x_nxt, x_nn = xs[rot], xs[nxt], xs[nn]
    y_cur, y_pp, y_prev = ys[rot], ys[nxt], ys[nn]
    g_cur, g_nxt, g_nn = gsem.at[rot], gsem.at[nxt], gsem.at[nn]
    s_cur, s_pp, s_prev = ssem.at[rot], ssem.at[nxt], ssem.at[nn]
    block_sublanes = MOE_ROWS * ROW_TILE

    def tile(ref, t):
        return ref.at[pl.ds(pl.multiple_of(t * ROW_TILE, ROW_TILE), ROW_TILE), :]

    def gather_copy(tref, buf, sem, r):
        return pltpu.make_async_copy(tile(h_hbm, tref[0, r]), buf.at[pl.ds(r * ROW_TILE, ROW_TILE), :], sem)

    def gather_all(buf, sem):
        return pltpu.make_async_copy(h_hbm.at[pl.ds(0, block_sublanes), :], buf, sem)

    def scatter_copy(sref, buf, sem, r):
        return pltpu.make_async_copy(buf.at[pl.ds(r * ROW_TILE, ROW_TILE), :], tile(out_hbm, sref[0, r]), sem)

    def scatter_all(buf, sem):
        return pltpu.make_async_copy(buf, out_hbm.at[pl.ds(0, block_sublanes), :], sem)

    @pl.when(i == 0)
    def _():
        y_prev[...] = jnp.zeros_like(y_prev)
        for r in range(MOE_ROWS):
            gather_copy(tok_ref, x_cur, g_cur, r).start()
            gather_copy(tok1_ref, x_nxt, g_nxt, r).start()

    @pl.when(i >= 2)
    def _():
        scatter_all(y_cur, s_cur).wait()

    gather_all(x_cur, g_cur).wait()
    x = jnp.concatenate([x_cur[pl.ds(j, MOE_ROWS, stride=ROW_TILE), :] for j in range(ROW_TILE)],
                        axis=1).astype(BF16)
    per = MOE_ROWS // MOE_GROUPS
    acts = []
    for c in range(MOE_GROUPS // 2):
        halves = []
        for part in range(2):
            grp = 2 * c + part
            for r in range(grp * per, (grp + 1) * per):
                gather_copy(tok2_ref, x_nn, g_nn, r).start()
                scatter_copy(slotp_ref, y_prev, s_prev, r).start(priority=1)
            lo = part * D_FF + c * MOE_CHUNK
            halves.append(_dot(x, w1_s[:, lo:lo + MOE_CHUNK]) + b1_ref[:, lo:lo + MOE_CHUNK])
        glu = jnp.minimum(halves[0], SWIGLU_LIMIT)
        lin = jnp.clip(halves[1], -SWIGLU_LIMIT, SWIGLU_LIMIT)
        acts.append((glu * _sigmoid(SWIGLU_ALPHA * glu) * (lin + 1.0)).astype(BF16))
    act = jnp.concatenate(acts, axis=1)
    y = _dot(act, w2_s[...]) + b2_ref[...]
    for j in range(ROW_TILE):
        y_cur[pl.ds(j, MOE_ROWS, stride=ROW_TILE), :] = y[:, j * LANES:(j + 1) * LANES]

    @pl.when(i == last)
    def _():
        for r in range(MOE_ROWS):
            scatter_copy(slot_ref, y_cur, s_cur, r).start()
        gather_all(x_nxt, g_nxt).wait()
        gather_all(x_nn, g_nn).wait()
        scatter_all(y_prev, s_prev).wait()

        @pl.when(i >= 1)
        def _():
            scatter_all(y_pp, s_pp).wait()

        scatter_all(y_cur, s_cur).wait()


def _moe_kernel(be_ref, nu_ref, tok_ref, tok1_ref, tok2_ref, slotp_ref, slot_ref, h_hbm, w1_ref, b1_ref, w2_ref,
                b2_ref, out_hbm, w1_s, w2_s, x0, x1, x2, y0, y1, y2, zbuf, gsem, ssem, zsem):
    i = pl.program_id(0)
    n_used = nu_ref[0]
    prev = be_ref[jnp.maximum(i - 1, 0)]

    @pl.when(jnp.logical_and(i < n_used, jnp.logical_or(i == 0, be_ref[i] != prev)))
    def _():
        w1_s[...] = w1_ref[...].astype(BF16)
        w2_s[...] = w2_ref[...].astype(BF16)

    for rot in range(3):
        @pl.when(jnp.logical_and(i < n_used, i % 3 == rot))
        def _(rot=rot):
            _moe_step(i, n_used - 1, tok_ref, tok1_ref, tok2_ref, slotp_ref, slot_ref, h_hbm, b1_ref, b2_ref,
                      out_hbm, w1_s, w2_s, (x0, x1, x2), (y0, y1, y2), gsem, ssem, rot)

    @pl.when(i >= n_used)
    def _():
        zbuf[...] = jnp.zeros_like(zbuf)
        first = pl.multiple_of(slot_ref[0, 0] * ROW_TILE, ROW_TILE)
        fill = pltpu.make_async_copy(zbuf, out_hbm.at[pl.ds(first, MOE_ROWS * ROW_TILE), :], zsem.at[0])
        fill.start()
        fill.wait()


def _moe_experts(block_expert, n_used, row_token, row_slot, h2, w1, b1, w2, b2, layer):
    n_blocks = row_token.shape[0]
    n_rows = n_blocks * MOE_ROWS
    spare = (n_rows + jnp.arange(MOE_ROWS, dtype=jnp.int32)).reshape(1, 1, MOE_ROWS)
    slot_prev = jnp.concatenate([spare, row_slot[:-1]], axis=0)
    smem_rows = lambda imap: pl.BlockSpec((None, 1, MOE_ROWS), imap, memory_space=pltpu.SMEM)
    expert_block = lambda *shape: pl.BlockSpec((None, None) + shape, lambda i, be, nu: (layer, be[i], 0, 0))
    row_buf = pltpu.VMEM((MOE_ROWS * ROW_TILE, LANES), F32)
    grid_spec = pltpu.PrefetchScalarGridSpec(
        num_scalar_prefetch=2,
        grid=(n_blocks,),
        in_specs=[smem_rows(lambda i, be, nu: (i, 0, 0)),
                  smem_rows(lambda i, be, nu: (jnp.minimum(i + 1, n_blocks - 1), 0, 0)),
                  smem_rows(lambda i, be, nu: (jnp.minimum(i + 2, n_blocks - 1), 0, 0)),
                  smem_rows(lambda i, be, nu: (i, 0, 0)),
                  smem_rows(lambda i, be, nu: (i, 0, 0)),
                  pl.BlockSpec(memory_space=pl.ANY),
                  expert_block(D_MODEL, 2 * D_FF), expert_block(1, 2 * D_FF),
                  expert_block(D_FF, D_MODEL), expert_block(1, D_MODEL)],
        out_specs=pl.BlockSpec(memory_space=pl.ANY),
        scratch_shapes=[pltpu.VMEM((D_MODEL, 2 * D_FF), BF16), pltpu.VMEM((D_FF, D_MODEL), BF16)]
                       + [row_buf] * 7
                       + [pltpu.SemaphoreType.DMA((3,)), pltpu.SemaphoreType.DMA((3,)),
                          pltpu.SemaphoreType.DMA((1,))],
    )
    return pl.pallas_call(
        _moe_kernel,
        grid_spec=grid_spec,
        out_shape=jax.ShapeDtypeStruct(((n_rows + MOE_ROWS) * ROW_TILE, LANES), F32),
        compiler_params=pltpu.CompilerParams(dimension_semantics=("arbitrary",), vmem_limit_bytes=VMEM_LIMIT,
                                             has_side_effects=True),
        name="moe_experts",
    )(block_expert, n_used, row_token, row_token, row_token, slot_prev, row_slot, h2, w1, b1, w2, b2)


def _moe_dispatch(eidx):
    n = eidx.shape[0]
    n_assign = n * TOP_K
    n_blocks = -(-n_assign // MOE_ROWS) + N_EXPERTS
    assert n_assign % MOE_ROWS == 0
    id_bits = (n_assign - 1).bit_length()
    assert N_EXPERTS << (id_bits + 1) < 2 ** 31
    expert = eidx.reshape(-1)
    ids = jnp.arange(n_assign, dtype=jnp.int32)
    experts = jnp.arange(N_EXPERTS, dtype=jnp.int32)
    counts = jnp.sum((expert[:, None] == experts[None, :]).astype(jnp.int32), axis=0)
    padded = (counts + MOE_ROWS - 1) // MOE_ROWS * MOE_ROWS
    pad_ends = jnp.cumsum(padded)
    real_keys = (expert << (id_bits + 1)) | ids
    j = jnp.arange(MOE_ROWS, dtype=jnp.int32)[None, :]
    pad_keys = jnp.where(j < (padded - counts)[:, None],
                         (experts[:, None] << (id_bits + 1)) | (1 << id_bits) | j, jnp.iinfo(jnp.int32).max)
    keys = jnp.sort(jnp.concatenate([real_keys, pad_keys.reshape(-1)]))
    is_pad = ((keys >> id_bits) & 1) == 1
    assign = keys & ((1 << id_bits) - 1)
    filler = n_assign + jnp.cumsum(is_pad.astype(jnp.int32)) - 1
    row_slot = jnp.where(is_pad, filler, (assign % TOP_K) * n + assign // TOP_K).astype(jnp.int32)
    row_token = jnp.where(is_pad, 0, assign // TOP_K).astype(jnp.int32)
    block_start = jnp.arange(n_blocks, dtype=jnp.int32) * MOE_ROWS
    block_expert = jnp.minimum(jnp.sum((pad_ends[None, :] <= block_start[:, None]).astype(jnp.int32), axis=1),
                               N_EXPERTS - 1).astype(jnp.int32)
    n_used = (pad_ends[-1] // MOE_ROWS).astype(jnp.int32).reshape(1)
    return (block_expert, n_used, row_token.reshape(n_blocks, 1, MOE_ROWS),
            row_slot.reshape(n_blocks, 1, MOE_ROWS))


def _combine_kernel(x1_ref, mod_ref, gate_ref, y0_ref, y1_ref, y2_ref, y3_ref, fg_ref, o_ref, *, final):
    gate = gate_ref[...]
    y = None
    for k, yk_ref in enumerate((y0_ref, y1_ref, y2_ref, y3_ref)):
        rows = jnp.concatenate([yk_ref[pl.ds(j, TOKEN_TILE, stride=ROW_TILE), :] for j in range(ROW_TILE)], axis=1)
        yk = gate[:, k:k + 1] * rows
        y = yk if k == 0 else y + yk
    x2 = x1_ref[...] + mod_ref[5:6, :] * y
    if final:
        ms = jnp.mean(x2 * x2, axis=-1, keepdims=True)
        x2 = x2 * lax.rsqrt(ms + EPS) * fg_ref[...]
    o_ref[...] = x2


def _combine(x1, mod_tab, gate, y_slots, final_g, n_prompt_tiles, tiles_per_sample, *, final):
    nt = x1.shape[0]
    tiles = nt // TOKEN_TILE
    mrow = functools.partial(_mod_row, n_prompt_tiles=n_prompt_tiles, tiles_per_sample=tiles_per_sample)
    slot_rows = lambda k: pl.BlockSpec((TOKEN_TILE * ROW_TILE, LANES), lambda i: (k * tiles + i, 0))
    return pl.pallas_call(
        functools.partial(_combine_kernel, final=final),
        grid=(tiles,),
        in_specs=[pl.BlockSpec((TOKEN_TILE, D_MODEL), lambda i: (i, 0)),
                  pl.BlockSpec((None, 8, D_MODEL), lambda i: (mrow(i), 0, 0)),
                  pl.BlockSpec((TOKEN_TILE, LANES), lambda i: (i, 0))]
                 + [slot_rows(k) for k in range(TOP_K)]
                 + [pl.BlockSpec((1, D_MODEL), lambda i: (0, 0))],
        out_specs=pl.BlockSpec((TOKEN_TILE, D_MODEL), lambda i: (i, 0)),
        out_shape=jax.ShapeDtypeStruct((nt, D_MODEL), F32),
        compiler_params=_cparams("parallel"),
        name="combine_final" if final else "combine",
    )(x1, mod_tab, gate, y_slots, y_slots, y_slots, y_slots, final_g)


def _state_to_blocks(init):
    b = init.shape[0]
    st = init.reshape(b, 2, B_HEADS * HEAD_W, B_STATE)
    row = jnp.arange(B_HEADS * HEAD_W)[:, None]
    left = jnp.where(row < 2 * HEAD_W, st, 0.0)
    right = jnp.where(row >= 2 * HEAD_W, st, 0.0)
    return jnp.concatenate([left, right], axis=-1)


def _blocks_to_state(s):
    b = s.shape[0]
    row = jnp.arange(B_HEADS * HEAD_W)[:, None]
    st = jnp.where(row < 2 * HEAD_W, s[..., :B_STATE], s[..., B_STATE:])
    return st.reshape(b, 2, B_HEADS, HEAD_W, B_STATE)


def kernel(x_prompt, x_sample, c, cache_a_k, cache_a_v, cache_c_k, cache_c_v, cache_d_k, cache_d_v, state_ssd, c_ctx, norm1_g, norm2_g, w_mod, b_mod, w_in, a_q_g, a_k_g, b_conv_w, b_conv_b, b_dt_bias, b_a_log, b_d, b_norm_g, c_lam, c_subln_g, d_rpb, w_branch, w_out, router_w, router_b, moe_w1, moe_b1, moe_w2, moe_b2, final_g):
    bp, tp, _ = x_prompt.shape
    bs, ts, _ = x_sample.shape
    depth = w_in.shape[0]
    n_p = bp * tp
    n_s = bs * ts
    assert n_p % TOKEN_TILE == 0 and ts % TOKEN_TILE == 0 and n_p % ts == 0
    n_prompt_tiles = n_p // TOKEN_TILE
    tiles_per_sample = ts // TOKEN_TILE
    past = cache_a_k.shape[2]

    x = jnp.concatenate([x_prompt.reshape(n_p, D_MODEL), x_sample.reshape(n_s, D_MODEL)], axis=0)
    cvec = jnp.zeros((16, D_MODEL), F32).at[0].set(c_ctx).at[1:1 + bs].set(c)
    rope_tab = _rope_table(ts, TOKEN_TILE)
    perm_a = _rope_perm(HEAD_W)
    perm_c = _rope_perm(C_QK_DIM)
    head_of = np.arange(BRANCH_W) // HEAD_W
    bd = jnp.asarray((head_of[:, None] == head_of[None, :]) / HEAD_W, BF16)
    kv_of = np.arange(A_KV_W) // HEAD_W
    ex = jnp.asarray((kv_of[:, None] == (head_of // (N_HEADS // A_KV_HEADS))[None, :])
                     & ((np.arange(A_KV_W) % HEAD_W)[:, None] == (np.arange(BRANCH_W) % HEAD_W)[None, :]), BF16)
    scale_c = C_QK_DIM ** -0.5
    one = jnp.ones((1,), F32)
    rep = N_HEADS // A_KV_HEADS
    zero_state = jnp.zeros((bp, 2, BRANCH_W, LANES), F32)

    new_cache = [[] for _ in range(7)]
    for l in range(depth):
        lam_init = 0.8 - 0.6 * math.exp(-0.3 * l)
        lq = c_lam[l]
        lam = (jnp.exp(jnp.sum(lq[0] * lq[1])) - jnp.exp(jnp.sum(lq[2] * lq[3])) + lam_init).reshape(1)

        wl = w_in[l]
        w1 = jnp.concatenate([wl[:, 0:512], wl[:, 768:1280], wl[:, 512:768], wl[:, 1280:1288],
                              jnp.zeros((D_MODEL, LANES - 8), F32), wl[:, 1288:2824]], axis=1).astype(BF16)
        wg = wl[:, 2824:].astype(BF16)
        wb = w_branch[l].astype(BF16)
        wo = w_out[l].astype(BF16)
        rw = jnp.pad(router_w[l], ((0, 0), (0, LANES - N_EXPERTS)))
        rwh = rw.astype(BF16)
        rwl = (rw - rwh.astype(F32)).astype(BF16)
        rb = jnp.pad(router_b[l], (0, LANES - N_EXPERTS)).reshape(1, LANES)

        mod = _modulation(cvec, w_mod[l], b_mod[l].reshape(1, -1))
        mod_tab = jnp.pad(mod.reshape(16, 6, D_MODEL), ((0, 0), (0, 2), (0, 0)))

        g1 = norm1_g[l].reshape(1, D_MODEL)
        g2 = norm2_g[l].reshape(1, D_MODEL)
        att_a, att_c, att_d, kv_a, kv_c, kv_d, ssd_in = _in_proj(
            x, mod_tab, g1, w1, rope_tab, bd, perm_a, perm_c, ex,
            jnp.tile(a_q_g[l], N_HEADS).reshape(1, BRANCH_W), jnp.tile(a_k_g[l], A_KV_HEADS).reshape(1, A_KV_W),
            n_prompt_tiles, tiles_per_sample)

        ctx_ak = jnp.repeat(cache_a_k[:, l], rep, axis=2).reshape(bs, past, BRANCH_W).astype(BF16)
        ctx_av = jnp.repeat(cache_a_v[:, l], rep, axis=2).reshape(bs, past, BRANCH_W).astype(BF16)
        o_a_p = _mha(att_a, one, n_seq=bp, seq=tp, row0=0, n_maps=1, scale=1.0)
        o_a_s = _mha(att_a, one, n_seq=bs, seq=ts, row0=n_p, n_maps=1, scale=1.0, ctx=(ctx_ak, ctx_av))

        subln = (jnp.tile(c_subln_g[l], N_HEADS).reshape(1, BRANCH_W), bd, 1.0 - lam_init)
        ctx_ck = cache_c_k[:, l].reshape(bs, past, BRANCH_W).astype(BF16)
        ctx_cv = cache_c_v[:, l].reshape(bs, past, BRANCH_W).astype(BF16)
        o_c_p = _mha(att_c, lam, n_seq=bp, seq=tp, row0=0, n_maps=2, scale=scale_c, subln=subln)
        o_c_s = _mha(att_c, lam, n_seq=bs, seq=ts, row0=n_p, n_maps=2, scale=scale_c, ctx=(ctx_ck, ctx_cv),
                     subln=subln)

        o_d_p = _mha(att_d, one, n_seq=bp, seq=tp, row0=0, n_maps=1, scale=1.0)
        o_d_s = _na_attention(att_d, cache_d_k[:, l].reshape(bs, past, BRANCH_W).astype(BF16),
                              cache_d_v[:, l].reshape(bs, past, BRANCH_W).astype(BF16),
                              _na_bias_table(d_rpb[l], ts // GRID_W), n_seq=bs, seq=ts, row0=n_p)

        dtb = b_dt_bias[l].reshape(8)
        a_neg = -jnp.exp(b_a_log[l].reshape(8))
        ssd_args = (jnp.pad(b_conv_w[l], ((0, 8 - B_CONV_W), (0, 0))), b_conv_b[l].reshape(1, B_XBC),
                    jnp.zeros((8, LANES), F32).at[0, :8].set(dtb).at[1, :8].set(a_neg),
                    jnp.repeat(b_d[l], HEAD_W).reshape(1, BRANCH_W), b_norm_g[l].reshape(1, BRANCH_W))
        o_b_p, st_p = _ssd_branch(ssd_in, *ssd_args, zero_state, n_seq=bp, seq=tp, row0=0)
        o_b_s, _ = _ssd_branch(ssd_in, *ssd_args, _state_to_blocks(state_ssd[:, l]), n_seq=bs, seq=ts, row0=n_p)

        x1, h2, gate, eidx = _merge(x, mod_tab, g1, g2, (o_a_p, o_b_p, o_c_p, o_d_p), (o_a_s, o_b_s, o_c_s, o_d_s),
                                    wg, wb, wo, rwh, rwl, rb, n_prompt_tiles, tiles_per_sample)
        block_expert, n_used, row_token, row_slot = _moe_dispatch(eidx[:, :TOP_K])
        y_slots = _moe_experts(block_expert, n_used, row_token, row_slot, h2, moe_w1,
                               moe_b1.reshape(depth, N_EXPERTS, 1, -1), moe_w2,
                               moe_b2.reshape(depth, N_EXPERTS, 1, -1), l)
        x = _combine(x1, mod_tab, gate, y_slots, final_g.reshape(1, D_MODEL), n_prompt_tiles, tiles_per_sample,
                     final=(l == depth - 1))

        new_cache[0].append(kv_a[:n_p, :A_KV_W].reshape(bp, tp, A_KV_HEADS, HEAD_W))
        new_cache[1].append(kv_a[:n_p, A_KV_W:].reshape(bp, tp, A_KV_HEADS, HEAD_W))
        new_cache[2].append(kv_c[:n_p, :BRANCH_W].reshape(bp, tp, N_HEADS, 2 * C_QK_DIM))
        new_cache[3].append(kv_c[:n_p, BRANCH_W:].reshape(bp, tp, N_HEADS, HEAD_W))
        new_cache[4].append(kv_d[:n_p, :BRANCH_W].reshape(bp, tp, N_HEADS, HEAD_W))
        new_cache[5].append(kv_d[:n_p, BRANCH_W:].reshape(bp, tp, N_HEADS, HEAD_W))
        new_cache[6].append(_blocks_to_state(st_p))

    y_prompt = x[:n_p].reshape(bp, tp, D_MODEL)
    y_sample = x[n_p:].reshape(bs, ts, D_MODEL)
    return (y_prompt, y_sample) + tuple(jnp.stack(v, axis=1) for v in new_cache)
```

```python
import functools
import math

import jax
import jax.numpy as jnp
import numpy as np
from jax import lax
from jax.experimental import pallas as pl
from jax.experimental.pallas import tpu as pltpu

F32 = jnp.float32
BF16 = jnp.bfloat16

D_MODEL = 1024
N_BRANCH = 4
BRANCH_W = 256
HEAD_W = 64
N_HEADS = 4
GRID_W = 64
ROPE_THETA = 10000.0
EPS = 1e-6
A_KV_HEADS = 2
A_KV_W = A_KV_HEADS * HEAD_W
C_QK_DIM = 32
B_HEADS = 4
B_STATE = 64
B_CHUNK = 128
B_XBC = 512
B_CONV_W = 5
CONV_PAD = 8
NA_ROWS = 8
NA_COLS = 16
N_EXPERTS = 32
TOP_K = 4
D_FF = 1024
SWIGLU_LIMIT = 7.0
SWIGLU_ALPHA = 1.702
LANES = 128
ROW_TILE = D_MODEL // LANES
NEG = -1e30

TOKEN_TILE = 512
MOE_ROWS = 256
ATT_W = 3 * BRANCH_W
SSD_W = B_XBC + BRANCH_W + LANES
OFF_A, OFF_B, OFF_C, OFF_D = 0, 512, 512 + SSD_W, 512 + SSD_W + ATT_W
W1_COLS = OFF_D + ATT_W
VMEM_LIMIT = 56 * 1024 * 1024


def _cparams(*sem):
    return pltpu.CompilerParams(dimension_semantics=sem, vmem_limit_bytes=VMEM_LIMIT)


def _dot(a, b):
    return jnp.dot(a, b, preferred_element_type=F32)


def _dot_nt(a, b):
    return lax.dot_general(a, b, (((1,), (1,)), ((), ())), preferred_element_type=F32)


def _dot_tn(a, b):
    return lax.dot_general(a, b, (((0,), (0,)), ((), ())), preferred_element_type=F32)


def _sigmoid(x):
    return 0.5 * jnp.tanh(0.5 * x) + 0.5


def _split2(x):
    hi = x.astype(BF16)
    return hi, (x - hi.astype(F32)).astype(BF16)


def _split3(x):
    h1 = x.astype(BF16)
    r1 = x - h1.astype(F32)
    h2 = r1.astype(BF16)
    h3 = (r1 - h2.astype(F32)).astype(BF16)
    return h1, h2, h3


def _norm_mod(x, g, shift, scale):
    ms = jnp.mean(x * x, axis=-1, keepdims=True)
    return (x * lax.rsqrt(ms + EPS)) * g * (1.0 + scale) + shift


def _head_rmsnorm(x, bd, g):
    hi, lo = _split2(x * x)
    ms = _dot(hi, bd) + _dot(lo, bd)
    return x * lax.rsqrt(ms + EPS) * g


def _rope(x, perm, cos, sin):
    hi, lo = _split2(x)
    return x * cos + (_dot(hi, perm) + _dot(lo, perm)) * sin


def _mod_kernel(c_ref, w_ref, b_ref, o_ref):
    c = c_ref[...]
    s = (c * _sigmoid(c)).astype(BF16)
    o_ref[...] = _dot(s, w_ref[...].astype(BF16)) + b_ref[...]


def _modulation(cvec, w_mod, b_mod):
    n = w_mod.shape[1]
    tn = 1536
    return pl.pallas_call(
        _mod_kernel,
        grid=(n // tn,),
        in_specs=[pl.BlockSpec((16, D_MODEL), lambda j: (0, 0)),
                  pl.BlockSpec((D_MODEL, tn), lambda j: (0, j)),
                  pl.BlockSpec((1, tn), lambda j: (0, j))],
        out_specs=pl.BlockSpec((16, tn), lambda j: (0, j)),
        out_shape=jax.ShapeDtypeStruct((16, n), F32),
        compiler_params=_cparams("parallel"),
        name="modulation",
    )(cvec, w_mod, b_mod)


def _mod_row(i, n_prompt_tiles, tiles_per_sample):
    return jnp.where(i < n_prompt_tiles, 0, 1 + (i - n_prompt_tiles) // tiles_per_sample)


def _rope_row(i, n_prompt_tiles, tiles_per_sample):
    return jnp.where(i < n_prompt_tiles, 0, 1 + (i - n_prompt_tiles) % tiles_per_sample)


def _token_rows(x_refs, n_prompt_tiles):
    if len(x_refs) == 1:
        return x_refs[0][...]
    return jnp.where(pl.program_id(0) < n_prompt_tiles, x_refs[0][...], x_refs[1][...])


def _token_specs(xs, n_prompt_tiles):
    if len(xs) == 1:
        return [pl.BlockSpec((TOKEN_TILE, D_MODEL), lambda i: (i, 0))]
    return [pl.BlockSpec((TOKEN_TILE, D_MODEL), lambda i: (jnp.minimum(i, n_prompt_tiles - 1), 0)),
            pl.BlockSpec((TOKEN_TILE, D_MODEL), lambda i: (jnp.maximum(i - n_prompt_tiles, 0), 0))]


def _inproj_kernel(*refs, n_x, n_prompt_tiles):
    x_refs = refs[:n_x]
    (mod_ref, g_ref, w_ref, rope_ref, bd_ref, pa_ref, pc_ref, ex_ref, gq_ref, gk_ref,
     atta_ref, attc_ref, attd_ref, kva_ref, kvc_ref, kvd_ref, ssd_ref) = refs[n_x:]
    mod = mod_ref[...]
    h = _norm_mod(_token_rows(x_refs, n_prompt_tiles), g_ref[...], mod[0:1], mod[1:2])
    u = _dot(h.astype(BF16), w_ref[...])
    cos_a, sin_a = rope_ref[:, 0:256], rope_ref[:, 256:512]
    cos_c, sin_c = rope_ref[:, 512:768], rope_ref[:, 768:1024]
    q_scale = HEAD_W ** -0.5

    q = _rope(_head_rmsnorm(u[:, OFF_A:OFF_A + 256], bd_ref[...], gq_ref[...]), pa_ref[...], cos_a, sin_a)
    k = _rope(_head_rmsnorm(u[:, OFF_A + 256:OFF_A + 384], bd_ref[0:A_KV_W, 0:A_KV_W], gk_ref[...]),
              pa_ref[0:A_KV_W, 0:A_KV_W], cos_a[:, 0:A_KV_W], sin_a[:, 0:A_KV_W])
    v = u[:, OFF_A + 384:OFF_A + 512]
    ex = ex_ref[...]
    atta_ref[:, 0:256] = (q * q_scale).astype(BF16)
    atta_ref[:, 256:512] = _dot(k.astype(BF16), ex).astype(BF16)
    atta_ref[:, 512:768] = _dot(v.astype(BF16), ex).astype(BF16)
    kva_ref[:, 0:A_KV_W] = k
    kva_ref[:, A_KV_W:2 * A_KV_W] = v

    ssd_ref[...] = u[:, OFF_B:OFF_B + SSD_W]

    cq = _rope(u[:, OFF_C:OFF_C + 256], pc_ref[...], cos_c, sin_c)
    ck = _rope(u[:, OFF_C + 256:OFF_C + 512], pc_ref[...], cos_c, sin_c)
    cv = u[:, OFF_C + 512:OFF_C + 768]
    attc_ref[:, 0:256] = cq.astype(BF16)
    attc_ref[:, 256:512] = ck.astype(BF16)
    attc_ref[:, 512:768] = cv.astype(BF16)
    kvc_ref[:, 0:256] = ck
    kvc_ref[:, 256:512] = cv

    attd_ref[:, 0:256] = (u[:, OFF_D:OFF_D + 256] * q_scale).astype(BF16)
    attd_ref[:, 256:768] = u[:, OFF_D + 256:OFF_D + 768].astype(BF16)
    kvd_ref[...] = u[:, OFF_D + 256:OFF_D + 768]


def _in_proj(xs, mod_tab, g, w1, rope_tab, bd, pa, pc, ex, gq, gk, n_prompt_tiles, tiles_per_sample):
    nt = sum(x.shape[0] for x in xs)
    mrow = functools.partial(_mod_row, n_prompt_tiles=n_prompt_tiles, tiles_per_sample=tiles_per_sample)
    rrow = functools.partial(_rope_row, n_prompt_tiles=n_prompt_tiles, tiles_per_sample=tiles_per_sample)
    const = lambda *shape: pl.BlockSpec(shape, lambda i: (0,) * len(shape))
    rows = lambda w: pl.BlockSpec((TOKEN_TILE, w), lambda i: (i, 0))
    widths = (ATT_W, ATT_W, ATT_W, 2 * A_KV_W, 2 * BRANCH_W, 2 * BRANCH_W, SSD_W)
    dtypes = (BF16, BF16, BF16, F32, F32, F32, F32)
    return pl.pallas_call(
        functools.partial(_inproj_kernel, n_x=len(xs), n_prompt_tiles=n_prompt_tiles),
        grid=(nt // TOKEN_TILE,),
        in_specs=_token_specs(xs, n_prompt_tiles)
                 + [pl.BlockSpec((None, 8, D_MODEL), lambda i: (mrow(i), 0, 0)),
                  const(1, D_MODEL), const(D_MODEL, W1_COLS),
                  pl.BlockSpec((TOKEN_TILE, 4 * BRANCH_W), lambda i: (rrow(i), 0)),
                  const(BRANCH_W, BRANCH_W), const(BRANCH_W, BRANCH_W), const(BRANCH_W, BRANCH_W),
                  const(A_KV_W, BRANCH_W), const(1, BRANCH_W), const(1, A_KV_W)],
        out_specs=[rows(w) for w in widths],
        out_shape=[jax.ShapeDtypeStruct((nt, w), dt) for w, dt in zip(widths, dtypes)],
        compiler_params=_cparams("parallel"),
        name="in_proj",
    )(*xs, mod_tab, g, w1, rope_tab, bd, pa, pc, ex, gq, gk)


def _rope_perm(dim):
    p = np.zeros((BRANCH_W, BRANCH_W), np.float32)
    s = dim // 4
    for j in range(BRANCH_W):
        quarter = (j % dim) // s
        if quarter % 2 == 0:
            p[j + s, j] = -1.0
        else:
            p[j - s, j] = 1.0
    return jnp.asarray(p, BF16)


def _rope_table(t, n_identity):
    pos = jnp.arange(t)
    rows = (pos // GRID_W).astype(F32)
    cols = (pos % GRID_W).astype(F32)
    parts = []
    for dim in (HEAD_W, C_QK_DIM):
        axis_dim = dim // 2
        inv = ROPE_THETA ** (-jnp.arange(0, axis_dim, 2, dtype=F32) / axis_dim)
        ang_r = rows[:, None] * inv[None, :]
        ang_c = cols[:, None] * inv[None, :]
        ang = jnp.concatenate([ang_r, ang_r, ang_c, ang_c], axis=-1)
        reps = BRANCH_W // dim
        parts += [jnp.tile(jnp.cos(ang), (1, reps)), jnp.tile(jnp.sin(ang), (1, reps))]
    tab = jnp.concatenate(parts, axis=1)
    ident = jnp.concatenate([jnp.ones((n_identity, BRANCH_W), F32), jnp.zeros((n_identity, BRANCH_W), F32)] * 2, axis=1)
    return jnp.concatenate([ident, tab], axis=0)


def _mha_kernel(lam_ref, q_ref, k_ref, v_ref, *rest, n_maps, scale, has_ctx, subln_scale):
    rest = list(rest)
    ck_ref, cv_ref = (rest.pop(0), rest.pop(0)) if has_ctx else (None, None)
    g_ref, bd_ref = (rest.pop(0), rest.pop(0)) if subln_scale is not None else (None, None)
    o_ref = rest.pop(0)
    q = q_ref[...]
    k = k_ref[...]
    v = v_ref[...]
    tq = q.shape[0]
    lane = lax.broadcasted_iota(jnp.int32, (1, BRANCH_W), 1)
    sub_w = HEAD_W // n_maps
    acc = jnp.zeros((tq, BRANCH_W), F32)
    for h in range(N_HEADS):
        oh = None
        for j in range(n_maps):
            qm = jnp.where((lane // sub_w) == (h * n_maps + j), q, jnp.zeros_like(q))
            s = _dot_nt(qm, k)
            if scale != 1.0:
                s = s * scale
            m = jnp.max(s, axis=-1, keepdims=True)
            if has_ctx:
                sc = _dot_nt(qm, ck_ref[...])
                if scale != 1.0:
                    sc = sc * scale
                m = jnp.maximum(m, jnp.max(sc, axis=-1, keepdims=True))
                pc = jnp.exp(sc - m)
            p = jnp.exp(s - m)
            l = jnp.sum(p, axis=-1, keepdims=True)
            pv = _dot(p.astype(BF16), v)
            if has_ctx:
                l = l + jnp.sum(pc, axis=-1, keepdims=True)
                pv = pv + _dot(pc.astype(BF16), cv_ref[...])
            pv = pv * (1.0 / l)
            oh = pv if j == 0 else oh - lam_ref[0] * pv
        acc = jnp.where((lane // HEAD_W) == h, oh, acc)
    if subln_scale is not None:
        acc = _head_rmsnorm(acc, bd_ref[...], g_ref[...]) * subln_scale
    o_ref[...] = acc.astype(o_ref.dtype)


def _mha(att, lam, *, n_seq, seq, row0, n_maps, scale, ctx=None, subln=None, tq=256):
    qb0 = row0 // tq
    kb0 = row0 // seq
    nq = seq // tq
    assert row0 % seq == 0 and seq % tq == 0
    in_specs = [pl.BlockSpec(memory_space=pltpu.SMEM),
                pl.BlockSpec((tq, BRANCH_W), lambda i, j: (qb0 + i * nq + j, 0)),
                pl.BlockSpec((seq, BRANCH_W), lambda i, j: (kb0 + i, 1)),
                pl.BlockSpec((seq, BRANCH_W), lambda i, j: (kb0 + i, 2))]
    args = [lam, att, att, att]
    kk = seq
    if ctx is not None:
        kc = ctx[0].shape[1]
        kk += kc
        in_specs += [pl.BlockSpec((None, kc, BRANCH_W), lambda i, j: (i, 0, 0))] * 2
        args += list(ctx)
    subln_scale = None
    if subln is not None:
        in_specs += [pl.BlockSpec((1, BRANCH_W), lambda i, j: (0, 0)),
                     pl.BlockSpec((BRANCH_W, BRANCH_W), lambda i, j: (0, 0))]
        args += [subln[0], subln[1]]
        subln_scale = subln[2]
    return pl.pallas_call(
        functools.partial(_mha_kernel, n_maps=n_maps, scale=scale, has_ctx=ctx is not None, subln_scale=subln_scale),
        grid=(n_seq, nq),
        in_specs=in_specs,
        out_specs=pl.BlockSpec((tq, BRANCH_W), lambda i, j: (i * nq + j, 0)),
        out_shape=jax.ShapeDtypeStruct((n_seq * seq, BRANCH_W), BF16),
        compiler_params=_cparams("parallel", "parallel"),
        name=f"mha_maps{n_maps}_k{kk}",
    )(*args)


def _na_row0(r, n_rows):
    return jnp.clip(r - NA_ROWS // 2, 0, n_rows - NA_ROWS)


def _na_kernel(q_ref, k_ref, v_ref, ck_ref, cv_ref, bias_ref, o_ref, *, n_rows):
    win = NA_ROWS * GRID_W
    lane = lax.broadcasted_iota(jnp.int32, (1, BRANCH_W), 1)
    ck = ck_ref[...]
    cv = cv_ref[...]

    def row_body(r, carry):
        r0 = _na_row0(r, n_rows)
        start = pl.multiple_of(r0 * GRID_W, GRID_W)
        rows = pl.ds(pl.multiple_of(r * GRID_W, GRID_W), GRID_W)
        q = q_ref[rows, :]
        kw = k_ref[pl.ds(start, win), :]
        vw = v_ref[pl.ds(start, win), :]
        qs = jnp.concatenate(
            [jnp.where((lane // HEAD_W) == h, q, jnp.zeros_like(q)) for h in range(N_HEADS)], axis=0)
        s_loc = _dot_nt(qs, kw) + bias_ref[r - r0]
        s_ctx = _dot_nt(qs, ck)
        m = jnp.maximum(jnp.max(s_loc, axis=-1, keepdims=True), jnp.max(s_ctx, axis=-1, keepdims=True))
        p_loc = jnp.exp(s_loc - m)
        p_ctx = jnp.exp(s_ctx - m)
        l = jnp.sum(p_loc, axis=-1, keepdims=True) + jnp.sum(p_ctx, axis=-1, keepdims=True)
        o = (_dot(p_loc.astype(BF16), vw) + _dot(p_ctx.astype(BF16), cv)) * (1.0 / l)
        acc = jnp.zeros((GRID_W, BRANCH_W), F32)
        for h in range(N_HEADS):
            acc = jnp.where((lane // HEAD_W) == h, o[h * GRID_W:(h + 1) * GRID_W], acc)
        o_ref[rows, :] = acc.astype(o_ref.dtype)
        return carry

    lax.fori_loop(0, n_rows, row_body, 0)


def _na_attention(att, ck, cv, bias_tab, *, n_seq, seq, row0):
    n_rows = seq // GRID_W
    kk = ck.shape[1]
    win = NA_ROWS * GRID_W
    b0 = row0 // seq
    assert row0 % seq == 0
    return pl.pallas_call(
        functools.partial(_na_kernel, n_rows=n_rows),
        grid=(n_seq,),
        in_specs=[pl.BlockSpec((seq, BRANCH_W), lambda i: (b0 + i, 0)),
                  pl.BlockSpec((seq, BRANCH_W), lambda i: (b0 + i, 1)),
                  pl.BlockSpec((seq, BRANCH_W), lambda i: (b0 + i, 2)),
                  pl.BlockSpec((None, kk, BRANCH_W), lambda i: (i, 0, 0)),
                  pl.BlockSpec((None, kk, BRANCH_W), lambda i: (i, 0, 0)),
                  pl.BlockSpec((NA_ROWS, N_HEADS * GRID_W, win), lambda i: (0, 0, 0))],
        out_specs=pl.BlockSpec((seq, BRANCH_W), lambda i: (i, 0)),
        out_shape=jax.ShapeDtypeStruct((n_seq * seq, BRANCH_W), BF16),
        compiler_params=_cparams("parallel"),
        name="na_attention",
    )(att, att, att, ck, cv, bias_tab)


def _na_bias_table(rpb, n_rows):
    var = np.arange(NA_ROWS)[:, None, None]
    j = np.arange(NA_ROWS)[None, :, None]
    sel_r = (np.arange(2 * NA_ROWS - 1)[None, None, :] == j - var + (NA_ROWS - 1)).astype(np.float32)
    col = np.arange(GRID_W)[:, None, None]
    kc = np.arange(GRID_W)[None, :, None]
    c0 = np.clip(col - NA_COLS // 2, 0, GRID_W - NA_COLS)
    valid = ((kc >= c0) & (kc < c0 + NA_COLS))[:, :, 0]
    sel_c = (np.arange(2 * NA_COLS - 1)[None, None, :] == kc - col + (NA_COLS - 1)).astype(np.float32)
    tab = jnp.einsum('hrc,vjr,xkc->vhxjk', rpb.astype(F32), sel_r, sel_c, precision=lax.Precision.HIGHEST)
    tab = jnp.where(valid[None, None, :, None, :], tab, NEG)
    return tab.reshape(NA_ROWS, N_HEADS * GRID_W, NA_ROWS * GRID_W)


def _softplus(x):
    return jnp.maximum(x, 0.0) + jnp.log1p(jnp.exp(-jnp.abs(x)))


def _expand_heads(colmat, d, lane256):
    out = jnp.zeros((colmat.shape[0], BRANCH_W), F32)
    for h in range(B_HEADS):
        j = d * B_HEADS + h
        out = jnp.where((lane256 // HEAD_W) == h,
                        jnp.broadcast_to(colmat[:, j:j + 1], (colmat.shape[0], BRANCH_W)), out)
    return out


def _ssd_kernel(xbc_ref, z_ref, dt_ref, cw_ref, cb_ref, pc_ref, dsk_ref, ng_ref, init_ref,
                o_ref, st_ref, xpad_s, xc_s, yf_s, s_s, *, seq):
    n_chunks = seq // B_CHUNK
    L = B_CHUNK
    ri = lax.broadcasted_iota(jnp.int32, (L, L), 0)
    ci = lax.broadcasted_iota(jnp.int32, (L, L), 1)
    low = ci <= ri
    upp = ri <= ci
    low_b = jnp.where(low, 1.0, 0.0).astype(BF16)
    upp_b = jnp.where(upp, 1.0, 0.0).astype(BF16)
    lane128 = lax.broadcasted_iota(jnp.int32, (1, LANES), 1)
    lane256 = lax.broadcasted_iota(jnp.int32, (1, BRANCH_W), 1)
    row256 = lax.broadcasted_iota(jnp.int32, (BRANCH_W, 1), 0)
    blockmask = (row256 // (2 * HEAD_W)) == (lane128 // B_STATE)

    xpad_s[0:CONV_PAD, :] = jnp.zeros((CONV_PAD, B_XBC), F32)
    xpad_s[seq + CONV_PAD:seq + 2 * CONV_PAD, :] = jnp.zeros((CONV_PAD, B_XBC), F32)

    def pad_body(c, carry):
        base = pl.multiple_of(c * L, L)
        xpad_s[pl.ds(pl.multiple_of(base + CONV_PAD, CONV_PAD), L), :] = xbc_ref[pl.ds(base, L), :]
        return carry

    lax.fori_loop(0, n_chunks, pad_body, 0)
    cw = cw_ref[...]
    cb = cb_ref[...]

    def conv_body(c, carry):
        base = pl.multiple_of(c * L, L)
        w = xpad_s[pl.ds(base, L + 2 * CONV_PAD), :]
        acc = jnp.zeros((L, B_XBC), F32) + cb
        for kk in range(B_CONV_W):
            off = CONV_PAD - B_CONV_W // 2 + kk
            acc = acc + w[off:off + L, :] * cw[kk:kk + 1, :]
        xc_s[pl.ds(base, L), :] = acc * _sigmoid(acc)
        return carry

    lax.fori_loop(0, n_chunks, conv_body, 0)

    dt_bias = pc_ref[0:1, :]
    a_neg = pc_ref[1:2, :]
    dsk = dsk_ref[...]
    ng = ng_ref[...]

    def chunk(c, d):
        base = pl.multiple_of(c * L, L)
        xs = xc_s[pl.ds(base, L), 0:BRANCH_W]
        bm = xc_s[pl.ds(base, L), BRANCH_W:BRANCH_W + LANES].astype(BF16)
        cm = xc_s[pl.ds(base, L), BRANCH_W + LANES:B_XBC].astype(BF16)
        dtc = _softplus(dt_ref[pl.ds(base, L), :] + dt_bias)
        da_c = dtc * a_neg
        da_r = da_c.T
        tri_c = low_b if d == 0 else upp_b
        tri_r = upp_b if d == 0 else low_b
        c1, c2, c3 = _split3(da_c)
        cum_c = _dot(tri_c, c1) + _dot(tri_c, c2) + _dot(tri_c, c3)
        r1, r2, r3 = _split3(da_r)
        cum_r = _dot(r1, tri_r) + _dot(r2, tri_r) + _dot(r3, tri_r)
        cum_end = cum_c[L - 1:L, :] if d == 0 else cum_c[0:1, :]
        dmask = low if d == 0 else upp

        g0 = _dot_nt(jnp.where(lane128 < B_STATE, cm, jnp.zeros_like(cm)), bm)
        g1 = _dot_nt(jnp.where(lane128 >= B_STATE, cm, jnp.zeros_like(cm)), bm)
        dt_x = _expand_heads(dtc, d, lane256)
        e_a = _expand_heads(jnp.exp(cum_c), d, lane256)
        t_e = _expand_heads(jnp.exp(cum_end - cum_c), d, lane256)
        xdt = xs * dt_x
        xdt_b = xdt.astype(BF16)
        y = jnp.zeros((L, BRANCH_W), F32)
        for h in range(B_HEADS):
            j = d * B_HEADS + h
            col = jnp.broadcast_to(cum_c[:, j:j + 1], (L, L))
            row = jnp.broadcast_to(cum_r[j:j + 1, :], (L, L))
            dec = jnp.exp(jnp.where(dmask, col - row, NEG))
            sc = ((g0 if h < 2 else g1) * dec).astype(BF16)
            y = jnp.where((lane256 // HEAD_W) == h, _dot(sc, xdt_b), y)
        state = s_s[...]
        y = y + _dot_nt(cm, state.astype(BF16)) * e_a
        zmat = _dot_tn((xdt * t_e).astype(BF16), bm)
        e_end = jnp.exp(cum_end)
        cd = jnp.zeros((BRANCH_W, LANES), F32)
        for h in range(B_HEADS):
            j = d * B_HEADS + h
            cd = jnp.where((row256 // HEAD_W) == h, jnp.broadcast_to(e_end[:, j:j + 1], (BRANCH_W, LANES)), cd)
        s_s[...] = state * cd + jnp.where(blockmask, zmat, 0.0)
        return base, xs, y

    s_s[...] = init_ref[0]

    def fwd_body(c, carry):
        base, _, y = chunk(c, 0)
        yf_s[pl.ds(base, L), :] = y
        return carry

    lax.fori_loop(0, n_chunks, fwd_body, 0)
    st_ref[0] = s_s[...]
    s_s[...] = init_ref[1]

    def bwd_body(i, carry):
        c = n_chunks - 1 - i
        base, xs, y = chunk(c, 1)
        z = z_ref[pl.ds(base, L), :]
        yt = (yf_s[pl.ds(base, L), :] + y + dsk * xs) * (z * _sigmoid(z))
        ms = jnp.mean(yt * yt, axis=-1, keepdims=True)
        o_ref[pl.ds(base, L), :] = (yt * lax.rsqrt(ms + EPS) * ng).astype(o_ref.dtype)
        return carry

    lax.fori_loop(0, n_chunks, bwd_body, 0)
    st_ref[1] = s_s[...]


def _ssd_branch(ssd_in, conv_w, conv_b, pc, dsk, ng, init, *, n_seq, seq, row0):
    b0 = row0 // seq
    assert row0 % seq == 0
    full = lambda *shape: pl.BlockSpec(shape, lambda i: (0,) * len(shape))
    return pl.pallas_call(
        functools.partial(_ssd_kernel, seq=seq),
        grid=(n_seq,),
        in_specs=[pl.BlockSpec((seq, B_XBC), lambda i: (b0 + i, 0)),
                  pl.BlockSpec((seq, BRANCH_W), lambda i: (b0 + i, B_XBC // BRANCH_W)),
                  pl.BlockSpec((seq, LANES), lambda i: (b0 + i, (B_XBC + BRANCH_W) // LANES)),
                  full(8, B_XBC), full(1, B_XBC), full(8, LANES), full(1, BRANCH_W), full(1, BRANCH_W),
                  pl.BlockSpec((None, 2, BRANCH_W, LANES), lambda i: (i, 0, 0, 0))],
        out_specs=[pl.BlockSpec((seq, BRANCH_W), lambda i: (i, 0)),
                   pl.BlockSpec((None, 2, BRANCH_W, LANES), lambda i: (i, 0, 0, 0))],
        out_shape=[jax.ShapeDtypeStruct((n_seq * seq, BRANCH_W), BF16),
                   jax.ShapeDtypeStruct((n_seq, 2, BRANCH_W, LANES), F32)],
        scratch_shapes=[pltpu.VMEM((seq + 2 * CONV_PAD, B_XBC), F32), pltpu.VMEM((seq, B_XBC), F32),
                        pltpu.VMEM((seq, BRANCH_W), F32), pltpu.VMEM((BRANCH_W, LANES), F32)],
        compiler_params=_cparams("parallel"),
        name=f"ssd_t{seq}",
    )(ssd_in, ssd_in, ssd_in, conv_w, conv_b, pc, dsk, ng, init)


def _merge_kernel(*refs, n_x, n_prompt_tiles):
    x_refs = refs[:n_x]
    mod_ref, g1_ref, g2_ref = refs[n_x:n_x + 3]
    rest = refs[n_x + 3:]
    br_p = rest[0:N_BRANCH]
    br_s = rest[N_BRANCH:2 * N_BRANCH]
    wg_ref, wb_ref, wo_ref, rwh_ref, rwl_ref, rb_ref, x1_ref, h2_ref, gate_ref, eidx_ref = rest[2 * N_BRANCH:]
    is_prompt = pl.program_id(0) < n_prompt_tiles
    x = _token_rows(x_refs, n_prompt_tiles)
    mod = mod_ref[...]
    h = _norm_mod(x, g1_ref[...], mod[0:1], mod[1:2]).astype(BF16)
    merged = None
    for i in range(N_BRANCH):
        gate = _sigmoid(_dot(h, wg_ref[:, i * D_MODEL:(i + 1) * D_MODEL]))
        br = jnp.where(is_prompt, br_p[i][...], br_s[i][...])
        proj = _dot(br, wb_ref[i])
        merged = gate * proj if i == 0 else merged + gate * proj
    y = _dot(merged.astype(BF16), wo_ref[...])
    x1 = x + mod[2:3] * y
    x1_ref[...] = x1
    h2 = _norm_mod(x1, g2_ref[...], mod[3:4], mod[4:5])
    hh, hl = _split2(h2)
    for j in range(ROW_TILE):
        h2_ref[pl.ds(j, TOKEN_TILE, stride=ROW_TILE), :] = h2[:, j * LANES:(j + 1) * LANES]
    rwh = rwh_ref[...]
    logits = _dot(hh, rwh) + _dot(hl, rwh) + _dot(hh, rwl_ref[...]) + rb_ref[...]
    lane = lax.broadcasted_iota(jnp.int32, (1, LANES), 1).astype(F32)
    cur = jnp.where(lane < N_EXPERTS, logits, NEG)
    vals, idxs = [], []
    for _ in range(TOP_K):
        m = jnp.max(cur, axis=-1, keepdims=True)
        am = jnp.min(jnp.where(cur == m, lane, float(LANES)), axis=-1, keepdims=True)
        vals.append(m)
        idxs.append(am)
        cur = jnp.where(lane == am, NEG, cur)
    exps = [jnp.exp(v - vals[0]) for v in vals]
    inv = 1.0 / (exps[0] + exps[1] + exps[2] + exps[3])
    gate = jnp.zeros_like(logits)
    eidx = jnp.zeros_like(logits)
    for k in range(TOP_K):
        gate = jnp.where(lane == k, exps[k] * inv, gate)
        eidx = jnp.where(lane == k, idxs[k], eidx)
    gate_ref[...] = gate
    eidx_ref[...] = eidx.astype(jnp.int32)


def _merge(xs, mod_tab, g1, g2, br_p, br_s, wg, wb, wo, rwh, rwl, rb, n_prompt_tiles, tiles_per_sample):
    nt = sum(x.shape[0] for x in xs)
    mrow = functools.partial(_mod_row, n_prompt_tiles=n_prompt_tiles, tiles_per_sample=tiles_per_sample)
    const = lambda *shape: pl.BlockSpec(shape, lambda i: (0,) * len(shape))
    rows = lambda w: pl.BlockSpec((TOKEN_TILE, w), lambda i: (i, 0))
    p_rows = pl.BlockSpec((TOKEN_TILE, BRANCH_W), lambda i: (jnp.minimum(i, n_prompt_tiles - 1), 0))
    s_rows = pl.BlockSpec((TOKEN_TILE, BRANCH_W), lambda i: (jnp.maximum(i - n_prompt_tiles, 0), 0))
    return pl.pallas_call(
        functools.partial(_merge_kernel, n_x=len(xs), n_prompt_tiles=n_prompt_tiles),
        grid=(nt // TOKEN_TILE,),
        in_specs=_token_specs(xs, n_prompt_tiles)
                 + [pl.BlockSpec((None, 8, D_MODEL), lambda i: (mrow(i), 0, 0)),
                    const(1, D_MODEL), const(1, D_MODEL)]
                 + [p_rows] * N_BRANCH + [s_rows] * N_BRANCH
                 + [const(D_MODEL, N_BRANCH * D_MODEL), const(N_BRANCH, BRANCH_W, D_MODEL),
                    const(D_MODEL, D_MODEL), const(D_MODEL, LANES), const(D_MODEL, LANES), const(1, LANES)],
        out_specs=[rows(D_MODEL), pl.BlockSpec((TOKEN_TILE * ROW_TILE, LANES), lambda i: (i, 0)),
                   rows(LANES), rows(LANES)],
        out_shape=[jax.ShapeDtypeStruct((nt, D_MODEL), F32),
                   jax.ShapeDtypeStruct((nt * ROW_TILE, LANES), F32),
                   jax.ShapeDtypeStruct((nt, LANES), F32),
                   jax.ShapeDtypeStruct((nt, LANES), jnp.int32)],
        compiler_params=_cparams("parallel"),
        name="merge",
    )(*xs, mod_tab, g1, g2, *br_p, *br_s, wg, wb, wo, rwh, rwl, rb)


MOE_GROUPS = 8
MOE_CHUNK = 2 * D_FF // MOE_GROUPS


def _moe_step(i, last, tok_ref, tok1_ref, tok2_ref, slotp_ref, slot_ref, h_hbm, b1_ref, b2_ref, out_hbm,
              w1_s, w2_s, xs, ys, gsem, ssem, rot):
    nxt, nn = (rot + 1) % 3, (rot + 2) % 3
    x_cur, x_nxt, x_nn = xs[rot], xs[nxt], xs[nn]
    y_cur, y_pp, y_prev = ys[rot], ys[nxt], ys[nn]
    g_cur, g_nxt, g_nn = gsem.at[rot], gsem.at[nxt], gsem.at[nn]
    s_cur, s_pp, s_prev = ssem.at[rot], ssem.at[nxt], ssem.at[nn]
    block_sublanes = MOE_ROWS * ROW_TILE

    def tile(ref, t):
        return ref.at[pl.ds(pl.multiple_of(t * ROW_TILE, ROW_TILE), ROW_TILE), :]

    def gather_copy(tref, buf, sem, r):
        return pltpu.make_async_copy(tile(h_hbm, tref[0, r]), buf.at[pl.ds(r * ROW_TILE, ROW_TILE), :], sem)

    def gather_all(buf, sem):
        return pltpu.make_async_copy(h_hbm.at[pl.ds(0, block_sublanes), :], buf, sem)

    def scatter_copy(sref, buf, sem, r):
        return pltpu.make_async_copy(buf.at[pl.ds(r * ROW_TILE, ROW_TILE), :], tile(out_hbm, sref[0, r]), sem)

    def scatter_all(buf, sem):
        return pltpu.make_async_copy(buf, out_hbm.at[pl.ds(0, block_sublanes), :], sem)

    @pl.when(i == 0)
    def _():
        y_prev[...] = jnp.zeros_like(y_prev)
        for r in range(MOE_ROWS):
            gather_copy(tok_ref, x_cur, g_cur, r).start()
            gather_copy(tok1_ref, x_nxt, g_nxt, r).start()

    @pl.when(i >= 2)
    def _():
        scatter_all(y_cur, s_cur).wait()

    gather_all(x_cur, g_cur).wait()
    x = jnp.concatenate([x_cur[pl.ds(j, MOE_ROWS, stride=ROW_TILE), :] for j in range(ROW_TILE)],
                        axis=1).astype(BF16)
    per = MOE_ROWS // MOE_GROUPS
    acts = []
    for c in range(MOE_GROUPS // 2):
        halves = []
        for part in range(2):
            grp = 2 * c + part
            for r in range(grp * per, (grp + 1) * per):
                gather_copy(tok2_ref, x_nn, g_nn, r).start()
                scatter_copy(slotp_ref, y_prev, s_prev, r).start(priority=1)
            lo = part * D_FF + c * MOE_CHUNK
            halves.append(_dot(x, w1_s[:, lo:lo + MOE_CHUNK]) + b1_ref[:, lo:lo + MOE_CHUNK])
        glu = jnp.minimum(halves[0], SWIGLU_LIMIT)
        lin = jnp.clip(halves[1], -SWIGLU_LIMIT, SWIGLU_LIMIT)
        acts.append((glu * _sigmoid(SWIGLU_ALPHA * glu) * (lin + 1.0)).astype(BF16))
    act = jnp.concatenate(acts, axis=1)
    y = _dot(act, w2_s[...]) + b2_ref[...]
    for j in range(ROW_TILE):
        y_cur[pl.ds(j, MOE_ROWS, stride=ROW_TILE), :] = y[:, j * LANES:(j + 1) * LANES]

    @pl.when(i == last)
    def _():
        for r in range(MOE_ROWS):
            scatter_copy(slot_ref, y_cur, s_cur, r).start()
        gather_all(x_nxt, g_nxt).wait()
        gather_all(x_nn, g_nn).wait()
        scatter_all(y_prev, s_prev).wait()

        @pl.when(i >= 1)
        def _():
            scatter_all(y_pp, s_pp).wait()

        scatter_all(y_cur, s_cur).wait()


def _moe_kernel(be_ref, nu_ref, tok_ref, tok1_ref, tok2_ref, slotp_ref, slot_ref, h_hbm, w1_ref, b1_ref, w2_ref,
                b2_ref, out_hbm, w1_s, w2_s, x0, x1, x2, y0, y1, y2, zbuf, gsem, ssem, zsem):
    i = pl.program_id(0)
    n_used = nu_ref[0]
    prev = be_ref[jnp.maximum(i - 1, 0)]

    @pl.when(jnp.logical_and(i < n_used, jnp.logical_or(i == 0, be_ref[i] != prev)))
    def _():
        w1_s[...] = w1_ref[...].astype(BF16)
        w2_s[...] = w2_ref[...].astype(BF16)

    for rot in range(3):
        @pl.when(jnp.logical_and(i < n_used, i % 3 == rot))
        def _(rot=rot):
            _moe_step(i, n_used - 1, tok_ref, tok1_ref, tok2_ref, slotp_ref, slot_ref, h_hbm, b1_ref, b2_ref,
                      out_hbm, w1_s, w2_s, (x0, x1, x2), (y0, y1, y2), gsem, ssem, rot)

    @pl.when(i >= n_used)
    def _():
        zbuf[...] = jnp.zeros_like(zbuf)
        first = pl.multiple_of(slot_ref[0, 0] * ROW_TILE, ROW_TILE)
        fill = pltpu.make_async_copy(zbuf, out_hbm.at[pl.ds(first, MOE_ROWS * ROW_TILE), :], zsem.at[0])
        fill.start()
        fill.wait()


def _moe_experts(block_expert, n_used, row_token, row_slot, h2, w1, b1, w2, b2, layer):
    n_blocks = row_token.shape[0]
    n_rows = n_blocks * MOE_ROWS
    spare = (n_rows + jnp.arange(MOE_ROWS, dtype=jnp.int32)).reshape(1, 1, MOE_ROWS)
    slot_prev = jnp.concatenate([spare, row_slot[:-1]], axis=0)
    smem_rows = lambda imap: pl.BlockSpec((None, 1, MOE_ROWS), imap, memory_space=pltpu.SMEM)
    expert_block = lambda *shape: pl.BlockSpec((None, None) + shape, lambda i, be, nu: (layer, be[i], 0, 0))
    row_buf = pltpu.VMEM((MOE_ROWS * ROW_TILE, LANES), F32)
    grid_spec = pltpu.PrefetchScalarGridSpec(
        num_scalar_prefetch=2,
        grid=(n_blocks,),
        in_specs=[smem_rows(lambda i, be, nu: (i, 0, 0)),
                  smem_rows(lambda i, be, nu: (jnp.minimum(i + 1, n_blocks - 1), 0, 0)),
                  smem_rows(lambda i, be, nu: (jnp.minimum(i + 2, n_blocks - 1), 0, 0)),
                  smem_rows(lambda i, be, nu: (i, 0, 0)),
                  smem_rows(lambda i, be, nu: (i, 0, 0)),
                  pl.BlockSpec(memory_space=pl.ANY),
                  expert_block(D_MODEL, 2 * D_FF), expert_block(1, 2 * D_FF),
                  expert_block(D_FF, D_MODEL), expert_block(1, D_MODEL)],
        out_specs=pl.BlockSpec(memory_space=pl.ANY),
        scratch_shapes=[pltpu.VMEM((D_MODEL, 2 * D_FF), BF16), pltpu.VMEM((D_FF, D_MODEL), BF16)]
                       + [row_buf] * 7
                       + [pltpu.SemaphoreType.DMA((3,)), pltpu.SemaphoreType.DMA((3,)),
                          pltpu.SemaphoreType.DMA((1,))],
    )
    return pl.pallas_call(
        _moe_kernel,
        grid_spec=grid_spec,
        out_shape=jax.ShapeDtypeStruct(((n_rows + MOE_ROWS) * ROW_TILE, LANES), F32),
        compiler_params=pltpu.CompilerParams(dimension_semantics=("arbitrary",), vmem_limit_bytes=VMEM_LIMIT,
                                             has_side_effects=True),
        name="moe_experts",
    )(block_expert, n_used, row_token, row_token, row_token, slot_prev, row_slot, h2, w1, b1, w2, b2)


def _moe_dispatch(eidx):
    n = eidx.shape[0]
    n_assign = n * TOP_K
    n_blocks = -(-n_assign // MOE_ROWS) + N_EXPERTS
    assert n_assign % MOE_ROWS == 0
    id_bits = (n_assign - 1).bit_length()
    assert N_EXPERTS << (id_bits + 1) < 2 ** 31
    expert = eidx.reshape(-1)
    ids = jnp.arange(n_assign, dtype=jnp.int32)
    experts = jnp.arange(N_EXPERTS, dtype=jnp.int32)
    counts = jnp.sum((expert[:, None] == experts[None, :]).astype(jnp.int32), axis=0)
    padded = (counts + MOE_ROWS - 1) // MOE_ROWS * MOE_ROWS
    pad_ends = jnp.cumsum(padded)
    real_keys = (expert << (id_bits + 1)) | ids
    j = jnp.arange(MOE_ROWS, dtype=jnp.int32)[None, :]
    pad_keys = jnp.where(j < (padded - counts)[:, None],
                         (experts[:, None] << (id_bits + 1)) | (1 << id_bits) | j, jnp.iinfo(jnp.int32).max)
    keys = jnp.sort(jnp.concatenate([real_keys, pad_keys.reshape(-1)]))
    is_pad = ((keys >> id_bits) & 1) == 1
    assign = keys & ((1 << id_bits) - 1)
    filler = n_assign + jnp.cumsum(is_pad.astype(jnp.int32)) - 1
    row_slot = jnp.where(is_pad, filler, (assign % TOP_K) * n + assign // TOP_K).astype(jnp.int32)
    row_token = jnp.where(is_pad, 0, assign // TOP_K).astype(jnp.int32)
    block_start = jnp.arange(n_blocks, dtype=jnp.int32) * MOE_ROWS
    block_expert = jnp.minimum(jnp.sum((pad_ends[None, :] <= block_start[:, None]).astype(jnp.int32), axis=1),
                               N_EXPERTS - 1).astype(jnp.int32)
    n_used = (pad_ends[-1] // MOE_ROWS).astype(jnp.int32).reshape(1)
    return (block_expert, n_used, row_token.reshape(n_blocks, 1, MOE_ROWS),
            row_slot.reshape(n_blocks, 1, MOE_ROWS))


def _combine_kernel(x1_ref, mod_ref, gate_ref, y0_ref, y1_ref, y2_ref, y3_ref, fg_ref, *o_refs, final,
                    n_prompt_tiles):
    gate = gate_ref[...]
    y = None
    for k, yk_ref in enumerate((y0_ref, y1_ref, y2_ref, y3_ref)):
        rows = jnp.concatenate([yk_ref[pl.ds(j, TOKEN_TILE, stride=ROW_TILE), :] for j in range(ROW_TILE)], axis=1)
        yk = gate[:, k:k + 1] * rows
        y = yk if k == 0 else y + yk
    x2 = x1_ref[...] + mod_ref[5:6, :] * y
    if not final:
        o_refs[0][...] = x2
        return
    ms = jnp.mean(x2 * x2, axis=-1, keepdims=True)
    x2 = x2 * lax.rsqrt(ms + EPS) * fg_ref[...]
    i = pl.program_id(0)

    @pl.when(i < n_prompt_tiles)
    def _():
        o_refs[0][...] = x2

    @pl.when(i >= n_prompt_tiles)
    def _():
        o_refs[1][...] = x2


def _combine(x1, mod_tab, gate, y_slots, final_g, n_prompt_tiles, tiles_per_sample, *, final):
    nt = x1.shape[0]
    tiles = nt // TOKEN_TILE
    mrow = functools.partial(_mod_row, n_prompt_tiles=n_prompt_tiles, tiles_per_sample=tiles_per_sample)
    slot_rows = lambda k: pl.BlockSpec((TOKEN_TILE * ROW_TILE, LANES), lambda i: (k * tiles + i, 0))
    if final:
        n_p = n_prompt_tiles * TOKEN_TILE
        out_specs = _token_specs((None, None), n_prompt_tiles)
        out_shape = [jax.ShapeDtypeStruct((n_p, D_MODEL), F32), jax.ShapeDtypeStruct((nt - n_p, D_MODEL), F32)]
    else:
        out_specs = pl.BlockSpec((TOKEN_TILE, D_MODEL), lambda i: (i, 0))
        out_shape = jax.ShapeDtypeStruct((nt, D_MODEL), F32)
    return pl.pallas_call(
        functools.partial(_combine_kernel, final=final, n_prompt_tiles=n_prompt_tiles),
        grid=(tiles,),
        in_specs=[pl.BlockSpec((TOKEN_TILE, D_MODEL), lambda i: (i, 0)),
                  pl.BlockSpec((None, 8, D_MODEL), lambda i: (mrow(i), 0, 0)),
                  pl.BlockSpec((TOKEN_TILE, LANES), lambda i: (i, 0))]
                 + [slot_rows(k) for k in range(TOP_K)]
                 + [pl.BlockSpec((1, D_MODEL), lambda i: (0, 0))],
        out_specs=out_specs,
        out_shape=out_shape,
        compiler_params=_cparams("arbitrary" if final else "parallel"),
        name="combine_final" if final else "combine",
    )(x1, mod_tab, gate, y_slots, y_slots, y_slots, y_slots, final_g)


def _state_to_blocks(init):
    b = init.shape[0]
    st = init.reshape(b, 2, B_HEADS * HEAD_W, B_STATE)
    row = jnp.arange(B_HEADS * HEAD_W)[:, None]
    left = jnp.where(row < 2 * HEAD_W, st, 0.0)
    right = jnp.where(row >= 2 * HEAD_W, st, 0.0)
    return jnp.concatenate([left, right], axis=-1)


def _blocks_to_state(s):
    b = s.shape[0]
    row = jnp.arange(B_HEADS * HEAD_W)[:, None]
    st = jnp.where(row < 2 * HEAD_W, s[..., :B_STATE], s[..., B_STATE:])
    return st.reshape(b, 2, B_HEADS, HEAD_W, B_STATE)


def kernel(x_prompt, x_sample, c, cache_a_k, cache_a_v, cache_c_k, cache_c_v, cache_d_k, cache_d_v, state_ssd, c_ctx, norm1_g, norm2_g, w_mod, b_mod, w_in, a_q_g, a_k_g, b_conv_w, b_conv_b, b_dt_bias, b_a_log, b_d, b_norm_g, c_lam, c_subln_g, d_rpb, w_branch, w_out, router_w, router_b, moe_w1, moe_b1, moe_w2, moe_b2, final_g):
    bp, tp, _ = x_prompt.shape
    bs, ts, _ = x_sample.shape
    depth = w_in.shape[0]
    n_p = bp * tp
    n_s = bs * ts
    assert n_p % TOKEN_TILE == 0 and ts % TOKEN_TILE == 0 and n_p % ts == 0
    n_prompt_tiles = n_p // TOKEN_TILE
    tiles_per_sample = ts // TOKEN_TILE
    past = cache_a_k.shape[2]

    xs = (x_prompt.reshape(n_p, D_MODEL), x_sample.reshape(n_s, D_MODEL))
    cvec = jnp.zeros((16, D_MODEL), F32).at[0].set(c_ctx).at[1:1 + bs].set(c)
    rope_tab = _rope_table(ts, TOKEN_TILE)
    perm_a = _rope_perm(HEAD_W)
    perm_c = _rope_perm(C_QK_DIM)
    head_of = np.arange(BRANCH_W) // HEAD_W
    bd = jnp.asarray((head_of[:, None] == head_of[None, :]) / HEAD_W, BF16)
    kv_of = np.arange(A_KV_W) // HEAD_W
    ex = jnp.asarray((kv_of[:, None] == (head_of // (N_HEADS // A_KV_HEADS))[None, :])
                     & ((np.arange(A_KV_W) % HEAD_W)[:, None] == (np.arange(BRANCH_W) % HEAD_W)[None, :]), BF16)
    scale_c = C_QK_DIM ** -0.5
    one = jnp.ones((1,), F32)
    rep = N_HEADS // A_KV_HEADS
    zero_state = jnp.zeros((bp, 2, BRANCH_W, LANES), F32)

    new_cache = [[] for _ in range(7)]
    for l in range(depth):
        lam_init = 0.8 - 0.6 * math.exp(-0.3 * l)
        lq = c_lam[l]
        lam = (jnp.exp(jnp.sum(lq[0] * lq[1])) - jnp.exp(jnp.sum(lq[2] * lq[3])) + lam_init).reshape(1)

        wl = w_in[l]
        w1 = jnp.concatenate([wl[:, 0:512], wl[:, 768:1280], wl[:, 512:768], wl[:, 1280:1288],
                              jnp.zeros((D_MODEL, LANES - 8), F32), wl[:, 1288:2824]], axis=1).astype(BF16)
        wg = wl[:, 2824:].astype(BF16)
        wb = w_branch[l].astype(BF16)
        wo = w_out[l].astype(BF16)
        rw = jnp.pad(router_w[l], ((0, 0), (0, LANES - N_EXPERTS)))
        rwh = rw.astype(BF16)
        rwl = (rw - rwh.astype(F32)).astype(BF16)
        rb = jnp.pad(router_b[l], (0, LANES - N_EXPERTS)).reshape(1, LANES)

        mod = _modulation(cvec, w_mod[l], b_mod[l].reshape(1, -1))
        mod_tab = jnp.pad(mod.reshape(16, 6, D_MODEL), ((0, 0), (0, 2), (0, 0)))

        g1 = norm1_g[l].reshape(1, D_MODEL)
        g2 = norm2_g[l].reshape(1, D_MODEL)
        att_a, att_c, att_d, kv_a, kv_c, kv_d, ssd_in = _in_proj(
            xs, mod_tab, g1, w1, rope_tab, bd, perm_a, perm_c, ex,
            jnp.tile(a_q_g[l], N_HEADS).reshape(1, BRANCH_W), jnp.tile(a_k_g[l], A_KV_HEADS).reshape(1, A_KV_W),
            n_prompt_tiles, tiles_per_sample)

        ctx_ak = jnp.repeat(cache_a_k[:, l], rep, axis=2).reshape(bs, past, BRANCH_W).astype(BF16)
        ctx_av = jnp.repeat(cache_a_v[:, l], rep, axis=2).reshape(bs, past, BRANCH_W).astype(BF16)
        o_a_p = _mha(att_a, one, n_seq=bp, seq=tp, row0=0, n_maps=1, scale=1.0)
        o_a_s = _mha(att_a, one, n_seq=bs, seq=ts, row0=n_p, n_maps=1, scale=1.0, ctx=(ctx_ak, ctx_av))

        subln = (jnp.tile(c_subln_g[l], N_HEADS).reshape(1, BRANCH_W), bd, 1.0 - lam_init)
        ctx_ck = cache_c_k[:, l].reshape(bs, past, BRANCH_W).astype(BF16)
        ctx_cv = cache_c_v[:, l].reshape(bs, past, BRANCH_W).astype(BF16)
        o_c_p = _mha(att_c, lam, n_seq=bp, seq=tp, row0=0, n_maps=2, scale=scale_c, subln=subln)
        o_c_s = _mha(att_c, lam, n_seq=bs, seq=ts, row0=n_p, n_maps=2, scale=scale_c, ctx=(ctx_ck, ctx_cv),
                     subln=subln)

        o_d_p = _mha(att_d, one, n_seq=bp, seq=tp, row0=0, n_maps=1, scale=1.0)
        o_d_s = _na_attention(att_d, cache_d_k[:, l].reshape(bs, past, BRANCH_W).astype(BF16),
                              cache_d_v[:, l].reshape(bs, past, BRANCH_W).astype(BF16),
                              _na_bias_table(d_rpb[l], ts // GRID_W), n_seq=bs, seq=ts, row0=n_p)

        dtb = b_dt_bias[l].reshape(8)
        a_neg = -jnp.exp(b_a_log[l].reshape(8))
        ssd_args = (jnp.pad(b_conv_w[l], ((0, 8 - B_CONV_W), (0, 0))), b_conv_b[l].reshape(1, B_XBC),
                    jnp.zeros((8, LANES), F32).at[0, :8].set(dtb).at[1, :8].set(a_neg),
                    jnp.repeat(b_d[l], HEAD_W).reshape(1, BRANCH_W), b_norm_g[l].reshape(1, BRANCH_W))
        o_b_p, st_p = _ssd_branch(ssd_in, *ssd_args, zero_state, n_seq=bp, seq=tp, row0=0)
        o_b_s, _ = _ssd_branch(ssd_in, *ssd_args, _state_to_blocks(state_ssd[:, l]), n_seq=bs, seq=ts, row0=n_p)

        x1, h2, gate, eidx = _merge(xs, mod_tab, g1, g2, (o_a_p, o_b_p, o_c_p, o_d_p), (o_a_s, o_b_s, o_c_s, o_d_s),
                                    wg, wb, wo, rwh, rwl, rb, n_prompt_tiles, tiles_per_sample)
        block_expert, n_used, row_token, row_slot = _moe_dispatch(eidx[:, :TOP_K])
        y_slots = _moe_experts(block_expert, n_used, row_token, row_slot, h2, moe_w1,
                               moe_b1.reshape(depth, N_EXPERTS, 1, -1), moe_w2,
                               moe_b2.reshape(depth, N_EXPERTS, 1, -1), l)
        x = _combine(x1, mod_tab, gate, y_slots, final_g.reshape(1, D_MODEL), n_prompt_tiles, tiles_per_sample,
                     final=(l == depth - 1))
        xs = (x,)

        new_cache[0].append(kv_a[:n_p, :A_KV_W].reshape(bp, tp, A_KV_HEADS, HEAD_W))
        new_cache[1].append(kv_a[:n_p, A_KV_W:].reshape(bp, tp, A_KV_HEADS, HEAD_W))
        new_cache[2].append(kv_c[:n_p, :BRANCH_W].reshape(bp, tp, N_HEADS, 2 * C_QK_DIM))
        new_cache[3].append(kv_c[:n_p, BRANCH_W:].reshape(bp, tp, N_HEADS, HEAD_W))
        new_cache[4].append(kv_d[:n_p, :BRANCH_W].reshape(bp, tp, N_HEADS, HEAD_W))
        new_cache[5].append(kv_d[:n_p, BRANCH_W:].reshape(bp, tp, N_HEADS, HEAD_W))
        new_cache[6].append(_blocks_to_state(st_p))

    y_prompt = x[0].reshape(bp, tp, D_MODEL)
    y_sample = x[1].reshape(bs, ts, D_MODEL)
    return (y_prompt, y_sample) + tuple(jnp.stack(v, axis=1) for v in new_cache)
```

```python
import functools
import math

import jax
import jax.numpy as jnp
import numpy as np
from jax import lax
from jax.experimental import pallas as pl
from jax.experimental.pallas import tpu as pltpu

F32 = jnp.float32
BF16 = jnp.bfloat16

D_MODEL = 1024
N_BRANCH = 4
BRANCH_W = 256
HEAD_W = 64
N_HEADS = 4
GRID_W = 64
ROPE_THETA = 10000.0
EPS = 1e-6
A_KV_HEADS = 2
A_KV_W = A_KV_HEADS * HEAD_W
C_QK_DIM = 32
B_HEADS = 4
B_STATE = 64
B_CHUNK = 128
B_XBC = 512
B_CONV_W = 5
CONV_PAD = 8
NA_ROWS = 8
NA_COLS = 16
N_EXPERTS = 32
TOP_K = 4
D_FF = 1024
SWIGLU_LIMIT = 7.0
SWIGLU_ALPHA = 1.702
LANES = 128
ROW_TILE = D_MODEL // LANES
NEG = -1e30

TOKEN_TILE = 512
MOE_ROWS = 256
ATT_W = 3 * BRANCH_W
SSD_W = B_XBC + BRANCH_W + LANES
OFF_A, OFF_B, OFF_C, OFF_D = 0, 512, 512 + SSD_W, 512 + SSD_W + ATT_W
W1_COLS = OFF_D + ATT_W
VMEM_LIMIT = 56 * 1024 * 1024


def _cparams(*sem):
    return pltpu.CompilerParams(dimension_semantics=sem, vmem_limit_bytes=VMEM_LIMIT)


def _dot(a, b):
    return jnp.dot(a, b, preferred_element_type=F32)


def _dot_nt(a, b):
    return lax.dot_general(a, b, (((1,), (1,)), ((), ())), preferred_element_type=F32)


def _dot_tn(a, b):
    return lax.dot_general(a, b, (((0,), (0,)), ((), ())), preferred_element_type=F32)


def _sigmoid(x):
    return 0.5 * jnp.tanh(0.5 * x) + 0.5


def _split2(x):
    hi = x.astype(BF16)
    return hi, (x - hi.astype(F32)).astype(BF16)


def _split3(x):
    h1 = x.astype(BF16)
    r1 = x - h1.astype(F32)
    h2 = r1.astype(BF16)
    h3 = (r1 - h2.astype(F32)).astype(BF16)
    return h1, h2, h3


def _norm_mod(x, g, shift, scale):
    ms = jnp.mean(x * x, axis=-1, keepdims=True)
    return (x * lax.rsqrt(ms + EPS)) * g * (1.0 + scale) + shift


def _head_rmsnorm(x, bd, g):
    hi, lo = _split2(x * x)
    ms = _dot(hi, bd) + _dot(lo, bd)
    return x * lax.rsqrt(ms + EPS) * g


def _rope(x, perm, cos, sin):
    hi, lo = _split2(x)
    return x * cos + (_dot(hi, perm) + _dot(lo, perm)) * sin


def _mod_kernel(c_ref, w_ref, b_ref, o_ref):
    c = c_ref[...]
    s = (c * _sigmoid(c)).astype(BF16)
    o_ref[...] = _dot(s, w_ref[...].astype(BF16)) + b_ref[...]


def _modulation(cvec, w_mod, b_mod):
    n = w_mod.shape[1]
    tn = 1536
    return pl.pallas_call(
        _mod_kernel,
        grid=(n // tn,),
        in_specs=[pl.BlockSpec((16, D_MODEL), lambda j: (0, 0)),
                  pl.BlockSpec((D_MODEL, tn), lambda j: (0, j)),
                  pl.BlockSpec((1, tn), lambda j: (0, j))],
        out_specs=pl.BlockSpec((16, tn), lambda j: (0, j)),
        out_shape=jax.ShapeDtypeStruct((16, n), F32),
        compiler_params=_cparams("parallel"),
        name="modulation",
    )(cvec, w_mod, b_mod)


def _mod_row(i, n_prompt_tiles, tiles_per_sample):
    return jnp.where(i < n_prompt_tiles, 0, 1 + (i - n_prompt_tiles) // tiles_per_sample)


def _rope_row(i, n_prompt_tiles, tiles_per_sample):
    return jnp.where(i < n_prompt_tiles, 0, 1 + (i - n_prompt_tiles) % tiles_per_sample)


def _token_rows(x_refs, n_prompt_tiles):
    if len(x_refs) == 1:
        return x_refs[0][...]
    return jnp.where(pl.program_id(0) < n_prompt_tiles, x_refs[0][...], x_refs[1][...])


def _token_specs(xs, n_prompt_tiles):
    if len(xs) == 1:
        return [pl.BlockSpec((TOKEN_TILE, D_MODEL), lambda i: (i, 0))]
    return [pl.BlockSpec((TOKEN_TILE, D_MODEL), lambda i: (jnp.minimum(i, n_prompt_tiles - 1), 0)),
            pl.BlockSpec((TOKEN_TILE, D_MODEL), lambda i: (jnp.maximum(i - n_prompt_tiles, 0), 0))]


def _inproj_kernel(*refs, n_x, n_prompt_tiles):
    x_refs = refs[:n_x]
    (mod_ref, g_ref, w_ref, rope_ref, bd_ref, pa_ref, pc_ref, ex_ref, gq_ref, gk_ref,
     atta_ref, attc_ref, attd_ref, kva_ref, kvc_ref, kvd_ref, ssd_ref) = refs[n_x:]
    mod = mod_ref[...]
    h = _norm_mod(_token_rows(x_refs, n_prompt_tiles), g_ref[...], mod[0:1], mod[1:2])
    u = _dot(h.astype(BF16), w_ref[...])
    cos_a, sin_a = rope_ref[:, 0:256], rope_ref[:, 256:512]
    cos_c, sin_c = rope_ref[:, 512:768], rope_ref[:, 768:1024]
    q_scale = HEAD_W ** -0.5

    q = _rope(_head_rmsnorm(u[:, OFF_A:OFF_A + 256], bd_ref[...], gq_ref[...]), pa_ref[...], cos_a, sin_a)
    k = _rope(_head_rmsnorm(u[:, OFF_A + 256:OFF_A + 384], bd_ref[0:A_KV_W, 0:A_KV_W], gk_ref[...]),
              pa_ref[0:A_KV_W, 0:A_KV_W], cos_a[:, 0:A_KV_W], sin_a[:, 0:A_KV_W])
    v = u[:, OFF_A + 384:OFF_A + 512]
    ex = ex_ref[...]
    atta_ref[:, 0:256] = (q * q_scale).astype(BF16)
    atta_ref[:, 256:512] = _dot(k.astype(BF16), ex).astype(BF16)
    atta_ref[:, 512:768] = _dot(v.astype(BF16), ex).astype(BF16)
    kva_ref[:, 0:A_KV_W] = k
    kva_ref[:, A_KV_W:2 * A_KV_W] = v

    ssd_ref[...] = u[:, OFF_B:OFF_B + SSD_W]

    cq = _rope(u[:, OFF_C:OFF_C + 256], pc_ref[...], cos_c, sin_c)
    ck = _rope(u[:, OFF_C + 256:OFF_C + 512], pc_ref[...], cos_c, sin_c)
    cv = u[:, OFF_C + 512:OFF_C + 768]
    attc_ref[:, 0:256] = (cq * (C_QK_DIM ** -0.5)).astype(BF16)
    attc_ref[:, 256:512] = ck.astype(BF16)
    attc_ref[:, 512:768] = cv.astype(BF16)
    kvc_ref[:, 0:256] = ck
    kvc_ref[:, 256:512] = cv

    attd_ref[:, 0:256] = (u[:, OFF_D:OFF_D + 256] * q_scale).astype(BF16)
    attd_ref[:, 256:768] = u[:, OFF_D + 256:OFF_D + 768].astype(BF16)
    kvd_ref[...] = u[:, OFF_D + 256:OFF_D + 768]


def _in_proj(xs, mod_tab, g, w1, rope_tab, bd, pa, pc, ex, gq, gk, n_prompt_tiles, tiles_per_sample):
    nt = sum(x.shape[0] for x in xs)
    mrow = functools.partial(_mod_row, n_prompt_tiles=n_prompt_tiles, tiles_per_sample=tiles_per_sample)
    rrow = functools.partial(_rope_row, n_prompt_tiles=n_prompt_tiles, tiles_per_sample=tiles_per_sample)
    const = lambda *shape: pl.BlockSpec(shape, lambda i: (0,) * len(shape))
    rows = lambda w: pl.BlockSpec((TOKEN_TILE, w), lambda i: (i, 0))
    widths = (ATT_W, ATT_W, ATT_W, 2 * A_KV_W, 2 * BRANCH_W, 2 * BRANCH_W, SSD_W)
    dtypes = (BF16, BF16, BF16, F32, F32, F32, F32)
    return pl.pallas_call(
        functools.partial(_inproj_kernel, n_x=len(xs), n_prompt_tiles=n_prompt_tiles),
        grid=(nt // TOKEN_TILE,),
        in_specs=_token_specs(xs, n_prompt_tiles)
                 + [pl.BlockSpec((None, 8, D_MODEL), lambda i: (mrow(i), 0, 0)),
                  const(1, D_MODEL), const(D_MODEL, W1_COLS),
                  pl.BlockSpec((TOKEN_TILE, 4 * BRANCH_W), lambda i: (rrow(i), 0)),
                  const(BRANCH_W, BRANCH_W), const(BRANCH_W, BRANCH_W), const(BRANCH_W, BRANCH_W),
                  const(A_KV_W, BRANCH_W), const(1, BRANCH_W), const(1, A_KV_W)],
        out_specs=[rows(w) for w in widths],
        out_shape=[jax.ShapeDtypeStruct((nt, w), dt) for w, dt in zip(widths, dtypes)],
        compiler_params=_cparams("parallel"),
        name="in_proj",
    )(*xs, mod_tab, g, w1, rope_tab, bd, pa, pc, ex, gq, gk)


def _rope_perm(dim):
    p = np.zeros((BRANCH_W, BRANCH_W), np.float32)
    s = dim // 4
    for j in range(BRANCH_W):
        quarter = (j % dim) // s
        if quarter % 2 == 0:
            p[j + s, j] = -1.0
        else:
            p[j - s, j] = 1.0
    return jnp.asarray(p, BF16)


def _rope_table(t, n_identity):
    pos = jnp.arange(t)
    rows = (pos // GRID_W).astype(F32)
    cols = (pos % GRID_W).astype(F32)
    parts = []
    for dim in (HEAD_W, C_QK_DIM):
        axis_dim = dim // 2
        inv = ROPE_THETA ** (-jnp.arange(0, axis_dim, 2, dtype=F32) / axis_dim)
        ang_r = rows[:, None] * inv[None, :]
        ang_c = cols[:, None] * inv[None, :]
        ang = jnp.concatenate([ang_r, ang_r, ang_c, ang_c], axis=-1)
        reps = BRANCH_W // dim
        parts += [jnp.tile(jnp.cos(ang), (1, reps)), jnp.tile(jnp.sin(ang), (1, reps))]
    tab = jnp.concatenate(parts, axis=1)
    ident = jnp.concatenate([jnp.ones((n_identity, BRANCH_W), F32), jnp.zeros((n_identity, BRANCH_W), F32)] * 2, axis=1)
    return jnp.concatenate([ident, tab], axis=0)


def _mha_kernel(lam_ref, q_ref, k_ref, v_ref, *rest, n_maps, scale, has_ctx, subln_scale):
    rest = list(rest)
    ck_ref, cv_ref = (rest.pop(0), rest.pop(0)) if has_ctx else (None, None)
    g_ref, bd_ref = (rest.pop(0), rest.pop(0)) if subln_scale is not None else (None, None)
    o_ref = rest.pop(0)
    q = q_ref[...]
    k = k_ref[...]
    v = v_ref[...]
    tq = q.shape[0]
    lane = lax.broadcasted_iota(jnp.int32, (1, BRANCH_W), 1)
    sub_w = HEAD_W // n_maps
    acc = jnp.zeros((tq, BRANCH_W), F32)
    for h in range(N_HEADS):
        oh = None
        for j in range(n_maps):
            qm = jnp.where((lane // sub_w) == (h * n_maps + j), q, jnp.zeros_like(q))
            s = _dot_nt(qm, k)
            if scale != 1.0:
                s = s * scale
            m = jnp.max(s, axis=-1, keepdims=True)
            if has_ctx:
                sc = _dot_nt(qm, ck_ref[...])
                if scale != 1.0:
                    sc = sc * scale
                m = jnp.maximum(m, jnp.max(sc, axis=-1, keepdims=True))
                pc = jnp.exp(sc - m)
            p = jnp.exp(s - m)
            l = jnp.sum(p, axis=-1, keepdims=True)
            pv = _dot(p.astype(BF16), v)
            if has_ctx:
                l = l + jnp.sum(pc, axis=-1, keepdims=True)
                pv = pv + _dot(pc.astype(BF16), cv_ref[...])
            pv = pv * (1.0 / l)
            oh = pv if j == 0 else oh - lam_ref[0] * pv
        acc = jnp.where((lane // HEAD_W) == h, oh, acc)
    if subln_scale is not None:
        acc = _head_rmsnorm(acc, bd_ref[...], g_ref[...]) * subln_scale
    o_ref[...] = acc.astype(o_ref.dtype)


def _mha(att, lam, *, n_seq, seq, row0, n_maps, scale, ctx=None, subln=None, tq=256):
    qb0 = row0 // tq
    kb0 = row0 // seq
    nq = seq // tq
    assert row0 % seq == 0 and seq % tq == 0
    in_specs = [pl.BlockSpec(memory_space=pltpu.SMEM),
                pl.BlockSpec((tq, BRANCH_W), lambda i, j: (qb0 + i * nq + j, 0)),
                pl.BlockSpec((seq, BRANCH_W), lambda i, j: (kb0 + i, 1)),
                pl.BlockSpec((seq, BRANCH_W), lambda i, j: (kb0 + i, 2))]
    args = [lam, att, att, att]
    kk = seq
    if ctx is not None:
        kc = ctx[0].shape[1]
        kk += kc
        in_specs += [pl.BlockSpec((None, kc, BRANCH_W), lambda i, j: (i, 0, 0))] * 2
        args += list(ctx)
    subln_scale = None
    if subln is not None:
        in_specs += [pl.BlockSpec((1, BRANCH_W), lambda i, j: (0, 0)),
                     pl.BlockSpec((BRANCH_W, BRANCH_W), lambda i, j: (0, 0))]
        args += [subln[0], subln[1]]
        subln_scale = subln[2]
    return pl.pallas_call(
        functools.partial(_mha_kernel, n_maps=n_maps, scale=scale, has_ctx=ctx is not None, subln_scale=subln_scale),
        grid=(n_seq, nq),
        in_specs=in_specs,
        out_specs=pl.BlockSpec((tq, BRANCH_W), lambda i, j: (i * nq + j, 0)),
        out_shape=jax.ShapeDtypeStruct((n_seq * seq, BRANCH_W), BF16),
        compiler_params=_cparams("parallel", "parallel"),
        name=f"mha_maps{n_maps}_k{kk}",
    )(*args)


def _na_row0(r, n_rows):
    return jnp.clip(r - NA_ROWS // 2, 0, n_rows - NA_ROWS)


def _na_kernel(q_ref, k_ref, v_ref, ck_ref, cv_ref, bias_ref, o_ref, *, n_rows):
    win = NA_ROWS * GRID_W
    lane = lax.broadcasted_iota(jnp.int32, (1, BRANCH_W), 1)
    ck = ck_ref[...]
    cv = cv_ref[...]

    def row_body(r, carry):
        r0 = _na_row0(r, n_rows)
        start = pl.multiple_of(r0 * GRID_W, GRID_W)
        rows = pl.ds(pl.multiple_of(r * GRID_W, GRID_W), GRID_W)
        q = q_ref[rows, :]
        kw = k_ref[pl.ds(start, win), :]
        vw = v_ref[pl.ds(start, win), :]
        qs = jnp.concatenate(
            [jnp.where((lane // HEAD_W) == h, q, jnp.zeros_like(q)) for h in range(N_HEADS)], axis=0)
        s_loc = _dot_nt(qs, kw) + bias_ref[r - r0]
        s_ctx = _dot_nt(qs, ck)
        m = jnp.maximum(jnp.max(s_loc, axis=-1, keepdims=True), jnp.max(s_ctx, axis=-1, keepdims=True))
        p_loc = jnp.exp(s_loc - m)
        p_ctx = jnp.exp(s_ctx - m)
        l = jnp.sum(p_loc, axis=-1, keepdims=True) + jnp.sum(p_ctx, axis=-1, keepdims=True)
        o = (_dot(p_loc.astype(BF16), vw) + _dot(p_ctx.astype(BF16), cv)) * (1.0 / l)
        acc = jnp.zeros((GRID_W, BRANCH_W), F32)
        for h in range(N_HEADS):
            acc = jnp.where((lane // HEAD_W) == h, o[h * GRID_W:(h + 1) * GRID_W], acc)
        o_ref[rows, :] = acc.astype(o_ref.dtype)
        return carry

    lax.fori_loop(0, n_rows, row_body, 0)


def _na_attention(att, ck, cv, bias_tab, *, n_seq, seq, row0):
    n_rows = seq // GRID_W
    kk = ck.shape[1]
    win = NA_ROWS * GRID_W
    b0 = row0 // seq
    assert row0 % seq == 0
    return pl.pallas_call(
        functools.partial(_na_kernel, n_rows=n_rows),
        grid=(n_seq,),
        in_specs=[pl.BlockSpec((seq, BRANCH_W), lambda i: (b0 + i, 0)),
                  pl.BlockSpec((seq, BRANCH_W), lambda i: (b0 + i, 1)),
                  pl.BlockSpec((seq, BRANCH_W), lambda i: (b0 + i, 2)),
                  pl.BlockSpec((None, kk, BRANCH_W), lambda i: (i, 0, 0)),
                  pl.BlockSpec((None, kk, BRANCH_W), lambda i: (i, 0, 0)),
                  pl.BlockSpec((NA_ROWS, N_HEADS * GRID_W, win), lambda i: (0, 0, 0))],
        out_specs=pl.BlockSpec((seq, BRANCH_W), lambda i: (i, 0)),
        out_shape=jax.ShapeDtypeStruct((n_seq * seq, BRANCH_W), BF16),
        compiler_params=_cparams("parallel"),
        name="na_attention",
    )(att, att, att, ck, cv, bias_tab)


def _na_bias_table(rpb, n_rows):
    var = np.arange(NA_ROWS)[:, None, None]
    j = np.arange(NA_ROWS)[None, :, None]
    sel_r = (np.arange(2 * NA_ROWS - 1)[None, None, :] == j - var + (NA_ROWS - 1)).astype(np.float32)
    col = np.arange(GRID_W)[:, None, None]
    kc = np.arange(GRID_W)[None, :, None]
    c0 = np.clip(col - NA_COLS // 2, 0, GRID_W - NA_COLS)
    valid = ((kc >= c0) & (kc < c0 + NA_COLS))[:, :, 0]
    sel_c = (np.arange(2 * NA_COLS - 1)[None, None, :] == kc - col + (NA_COLS - 1)).astype(np.float32)
    tab = jnp.einsum('hrc,vjr,xkc->vhxjk', rpb.astype(F32), sel_r, sel_c, precision=lax.Precision.HIGHEST)
    tab = jnp.where(valid[None, None, :, None, :], tab, NEG)
    return tab.reshape(NA_ROWS, N_HEADS * GRID_W, NA_ROWS * GRID_W)


def _softplus(x):
    return jnp.maximum(x, 0.0) + jnp.log1p(jnp.exp(-jnp.abs(x)))


def _expand_heads(colmat, d, lane256):
    out = jnp.zeros((colmat.shape[0], BRANCH_W), F32)
    for h in range(B_HEADS):
        j = d * B_HEADS + h
        out = jnp.where((lane256 // HEAD_W) == h,
                        jnp.broadcast_to(colmat[:, j:j + 1], (colmat.shape[0], BRANCH_W)), out)
    return out


def _ssd_kernel(xbc_ref, z_ref, dt_ref, cw_ref, cb_ref, pc_ref, dsk_ref, ng_ref, init_ref,
                o_ref, st_ref, xpad_s, xc_s, yf_s, s_s, *, seq):
    n_chunks = seq // B_CHUNK
    L = B_CHUNK
    ri = lax.broadcasted_iota(jnp.int32, (L, L), 0)
    ci = lax.broadcasted_iota(jnp.int32, (L, L), 1)
    low = ci <= ri
    upp = ri <= ci
    low_b = jnp.where(low, 1.0, 0.0).astype(BF16)
    upp_b = jnp.where(upp, 1.0, 0.0).astype(BF16)
    lane128 = lax.broadcasted_iota(jnp.int32, (1, LANES), 1)
    lane256 = lax.broadcasted_iota(jnp.int32, (1, BRANCH_W), 1)
    row256 = lax.broadcasted_iota(jnp.int32, (BRANCH_W, 1), 0)
    blockmask = (row256 // (2 * HEAD_W)) == (lane128 // B_STATE)

    xpad_s[0:CONV_PAD, :] = jnp.zeros((CONV_PAD, B_XBC), F32)
    xpad_s[seq + CONV_PAD:seq + 2 * CONV_PAD, :] = jnp.zeros((CONV_PAD, B_XBC), F32)

    def pad_body(c, carry):
        base = pl.multiple_of(c * L, L)
        xpad_s[pl.ds(pl.multiple_of(base + CONV_PAD, CONV_PAD), L), :] = xbc_ref[pl.ds(base, L), :]
        return carry

    lax.fori_loop(0, n_chunks, pad_body, 0)
    cw = cw_ref[...]
    cb = cb_ref[...]

    def conv_body(c, carry):
        base = pl.multiple_of(c * L, L)
        w = xpad_s[pl.ds(base, L + 2 * CONV_PAD), :]
        acc = jnp.zeros((L, B_XBC), F32) + cb
        for kk in range(B_CONV_W):
            off = CONV_PAD - B_CONV_W // 2 + kk
            acc = acc + w[off:off + L, :] * cw[kk:kk + 1, :]
        xc_s[pl.ds(base, L), :] = acc * _sigmoid(acc)
        return carry

    lax.fori_loop(0, n_chunks, conv_body, 0)

    dt_bias = pc_ref[0:1, :]
    a_neg = pc_ref[1:2, :]
    dsk = dsk_ref[...]
    ng = ng_ref[...]

    def chunk(c, d):
        base = pl.multiple_of(c * L, L)
        xs = xc_s[pl.ds(base, L), 0:BRANCH_W]
        bm = xc_s[pl.ds(base, L), BRANCH_W:BRANCH_W + LANES].astype(BF16)
        cm = xc_s[pl.ds(base, L), BRANCH_W + LANES:B_XBC].astype(BF16)
        dtc = _softplus(dt_ref[pl.ds(base, L), :] + dt_bias)
        da_c = dtc * a_neg
        da_r = da_c.T
        tri_c = low_b if d == 0 else upp_b
        tri_r = upp_b if d == 0 else low_b
        c1, c2, c3 = _split3(da_c)
        cum_c = _dot(tri_c, c1) + _dot(tri_c, c2) + _dot(tri_c, c3)
        r1, r2, r3 = _split3(da_r)
        cum_r = _dot(r1, tri_r) + _dot(r2, tri_r) + _dot(r3, tri_r)
        cum_end = cum_c[L - 1:L, :] if d == 0 else cum_c[0:1, :]
        dmask = low if d == 0 else upp

        g0 = _dot_nt(jnp.where(lane128 < B_STATE, cm, jnp.zeros_like(cm)), bm)
        g1 = _dot_nt(jnp.where(lane128 >= B_STATE, cm, jnp.zeros_like(cm)), bm)
        dt_x = _expand_heads(dtc, d, lane256)
        e_a = _expand_heads(jnp.exp(cum_c), d, lane256)
        t_e = _expand_heads(jnp.exp(cum_end - cum_c), d, lane256)
        xdt = xs * dt_x
        xdt_b = xdt.astype(BF16)
        y = jnp.zeros((L, BRANCH_W), F32)
        for h in range(B_HEADS):
            j = d * B_HEADS + h
            col = jnp.broadcast_to(cum_c[:, j:j + 1], (L, L))
            row = jnp.broadcast_to(cum_r[j:j + 1, :], (L, L))
            dec = jnp.exp(jnp.where(dmask, col - row, NEG))
            sc = ((g0 if h < 2 else g1) * dec).astype(BF16)
            y = jnp.where((lane256 // HEAD_W) == h, _dot(sc, xdt_b), y)
        state = s_s[...]
        y = y + _dot_nt(cm, state.astype(BF16)) * e_a
        zmat = _dot_tn((xdt * t_e).astype(BF16), bm)
        e_end = jnp.exp(cum_end)
        cd = jnp.zeros((BRANCH_W, LANES), F32)
        for h in range(B_HEADS):
            j = d * B_HEADS + h
            cd = jnp.where((row256 // HEAD_W) == h, jnp.broadcast_to(e_end[:, j:j + 1], (BRANCH_W, LANES)), cd)
        s_s[...] = state * cd + jnp.where(blockmask, zmat, 0.0)
        return base, xs, y

    s_s[...] = init_ref[0]

    def fwd_body(c, carry):
        base, _, y = chunk(c, 0)
        yf_s[pl.ds(base, L), :] = y
        return carry

    lax.fori_loop(0, n_chunks, fwd_body, 0)
    st_ref[0] = s_s[...]
    s_s[...] = init_ref[1]

    def bwd_body(i, carry):
        c = n_chunks - 1 - i
        base, xs, y = chunk(c, 1)
        z = z_ref[pl.ds(base, L), :]
        yt = (yf_s[pl.ds(base, L), :] + y + dsk * xs) * (z * _sigmoid(z))
        ms = jnp.mean(yt * yt, axis=-1, keepdims=True)
        o_ref[pl.ds(base, L), :] = (yt * lax.rsqrt(ms + EPS) * ng).astype(o_ref.dtype)
        return carry

    lax.fori_loop(0, n_chunks, bwd_body, 0)
    st_ref[1] = s_s[...]


def _ssd_branch(ssd_in, conv_w, conv_b, pc, dsk, ng, init, *, n_seq, seq, row0):
    b0 = row0 // seq
    assert row0 % seq == 0
    full = lambda *shape: pl.BlockSpec(shape, lambda i: (0,) * len(shape))
    return pl.pallas_call(
        functools.partial(_ssd_kernel, seq=seq),
        grid=(n_seq,),
        in_specs=[pl.BlockSpec((seq, B_XBC), lambda i: (b0 + i, 0)),
                  pl.BlockSpec((seq, BRANCH_W), lambda i: (b0 + i, B_XBC // BRANCH_W)),
                  pl.BlockSpec((seq, LANES), lambda i: (b0 + i, (B_XBC + BRANCH_W) // LANES)),
                  full(8, B_XBC), full(1, B_XBC), full(8, LANES), full(1, BRANCH_W), full(1, BRANCH_W),
                  pl.BlockSpec((None, 2, BRANCH_W, LANES), lambda i: (i, 0, 0, 0))],
        out_specs=[pl.BlockSpec((seq, BRANCH_W), lambda i: (i, 0)),
                   pl.BlockSpec((None, 2, BRANCH_W, LANES), lambda i: (i, 0, 0, 0))],
        out_shape=[jax.ShapeDtypeStruct((n_seq * seq, BRANCH_W), BF16),
                   jax.ShapeDtypeStruct((n_seq, 2, BRANCH_W, LANES), F32)],
        scratch_shapes=[pltpu.VMEM((seq + 2 * CONV_PAD, B_XBC), F32), pltpu.VMEM((seq, B_XBC), F32),
                        pltpu.VMEM((seq, BRANCH_W), F32), pltpu.VMEM((BRANCH_W, LANES), F32)],
        compiler_params=_cparams("parallel"),
        name=f"ssd_t{seq}",
    )(ssd_in, ssd_in, ssd_in, conv_w, conv_b, pc, dsk, ng, init)


def _merge_kernel(*refs, n_x, n_prompt_tiles):
    x_refs = refs[:n_x]
    mod_ref, g1_ref, g2_ref = refs[n_x:n_x + 3]
    rest = refs[n_x + 3:]
    br_p = rest[0:N_BRANCH]
    br_s = rest[N_BRANCH:2 * N_BRANCH]
    wg_ref, wb_ref, wo_ref, rwh_ref, rwl_ref, rb_ref, x1_ref, h2_ref, gate_ref, eidx_ref = rest[2 * N_BRANCH:]
    is_prompt = pl.program_id(0) < n_prompt_tiles
    x = _token_rows(x_refs, n_prompt_tiles)
    mod = mod_ref[...]
    h = _norm_mod(x, g1_ref[...], mod[0:1], mod[1:2]).astype(BF16)
    merged = None
    for i in range(N_BRANCH):
        gate = _sigmoid(_dot(h, wg_ref[:, i * D_MODEL:(i + 1) * D_MODEL]))
        br = jnp.where(is_prompt, br_p[i][...], br_s[i][...])
        proj = _dot(br, wb_ref[i])
        merged = gate * proj if i == 0 else merged + gate * proj
    y = _dot(merged.astype(BF16), wo_ref[...])
    x1 = x + mod[2:3] * y
    x1_ref[...] = x1
    h2 = _norm_mod(x1, g2_ref[...], mod[3:4], mod[4:5])
    hh, hl = _split2(h2)
    for j in range(ROW_TILE):
        h2_ref[pl.ds(j, TOKEN_TILE, stride=ROW_TILE), :] = h2[:, j * LANES:(j + 1) * LANES]
    rwh = rwh_ref[...]
    logits = _dot(hh, rwh) + _dot(hl, rwh) + _dot(hh, rwl_ref[...]) + rb_ref[...]
    lane = lax.broadcasted_iota(jnp.int32, (1, LANES), 1).astype(F32)
    cur = jnp.where(lane < N_EXPERTS, logits, NEG)
    vals, idxs = [], []
    for _ in range(TOP_K):
        m = jnp.max(cur, axis=-1, keepdims=True)
        am = jnp.min(jnp.where(cur == m, lane, float(LANES)), axis=-1, keepdims=True)
        vals.append(m)
        idxs.append(am)
        cur = jnp.where(lane == am, NEG, cur)
    exps = [jnp.exp(v - vals[0]) for v in vals]
    inv = 1.0 / (exps[0] + exps[1] + exps[2] + exps[3])
    gate = jnp.zeros_like(logits)
    eidx = jnp.zeros_like(logits)
    for k in range(TOP_K):
        gate = jnp.where(lane == k, exps[k] * inv, gate)
        eidx = jnp.where(lane == k, idxs[k], eidx)
    gate_ref[...] = gate
    eidx_ref[...] = eidx.astype(jnp.int32)


def _merge(xs, mod_tab, g1, g2, br_p, br_s, wg, wb, wo, rwh, rwl, rb, n_prompt_tiles, tiles_per_sample):
    nt = sum(x.shape[0] for x in xs)
    mrow = functools.partial(_mod_row, n_prompt_tiles=n_prompt_tiles, tiles_per_sample=tiles_per_sample)
    const = lambda *shape: pl.BlockSpec(shape, lambda i: (0,) * len(shape))
    rows = lambda w: pl.BlockSpec((TOKEN_TILE, w), lambda i: (i, 0))
    p_rows = pl.BlockSpec((TOKEN_TILE, BRANCH_W), lambda i: (jnp.minimum(i, n_prompt_tiles - 1), 0))
    s_rows = pl.BlockSpec((TOKEN_TILE, BRANCH_W), lambda i: (jnp.maximum(i - n_prompt_tiles, 0), 0))
    return pl.pallas_call(
        functools.partial(_merge_kernel, n_x=len(xs), n_prompt_tiles=n_prompt_tiles),
        grid=(nt // TOKEN_TILE,),
        in_specs=_token_specs(xs, n_prompt_tiles)
                 + [pl.BlockSpec((None, 8, D_MODEL), lambda i: (mrow(i), 0, 0)),
                    const(1, D_MODEL), const(1, D_MODEL)]
                 + [p_rows] * N_BRANCH + [s_rows] * N_BRANCH
                 + [const(D_MODEL, N_BRANCH * D_MODEL), const(N_BRANCH, BRANCH_W, D_MODEL),
                    const(D_MODEL, D_MODEL), const(D_MODEL, LANES), const(D_MODEL, LANES), const(1, LANES)],
        out_specs=[rows(D_MODEL), pl.BlockSpec((TOKEN_TILE * ROW_TILE, LANES), lambda i: (i, 0)),
                   rows(LANES), rows(LANES)],
        out_shape=[jax.ShapeDtypeStruct((nt, D_MODEL), F32),
                   jax.ShapeDtypeStruct((nt * ROW_TILE, LANES), F32),
                   jax.ShapeDtypeStruct((nt, LANES), F32),
                   jax.ShapeDtypeStruct((nt, LANES), jnp.int32)],
        compiler_params=_cparams("parallel"),
        name="merge",
    )(*xs, mod_tab, g1, g2, *br_p, *br_s, wg, wb, wo, rwh, rwl, rb)


MOE_GROUPS = 8
MOE_CHUNK = 2 * D_FF // MOE_GROUPS


def _moe_step(i, last, tok_ref, tok1_ref, tok2_ref, slotp_ref, slot_ref, h_hbm, b1_ref, b2_ref, out_hbm,
              w1_s, w2_s, xs, ys, gsem, ssem, rot):
    nxt, nn = (rot + 1) % 3, (rot + 2) % 3
    x_cur, x_nxt, x_nn = xs[rot], xs[nxt], xs[nn]
    y_cur, y_pp, y_prev = ys[rot], ys[nxt], ys[nn]
    g_cur, g_nxt, g_nn = gsem.at[rot], gsem.at[nxt], gsem.at[nn]
    s_cur, s_pp, s_prev = ssem.at[rot], ssem.at[nxt], ssem.at[nn]
    block_sublanes = MOE_ROWS * ROW_TILE

    def tile(ref, t):
        return ref.at[pl.ds(pl.multiple_of(t * ROW_TILE, ROW_TILE), ROW_TILE), :]

    def gather_copy(tref, buf, sem, r):
        return pltpu.make_async_copy(tile(h_hbm, tref[0, r]), buf.at[pl.ds(r * ROW_TILE, ROW_TILE), :], sem)

    def gather_all(buf, sem):
        return pltpu.make_async_copy(h_hbm.at[pl.ds(0, block_sublanes), :], buf, sem)

    def scatter_copy(sref, buf, sem, r):
        return pltpu.make_async_copy(buf.at[pl.ds(r * ROW_TILE, ROW_TILE), :], tile(out_hbm, sref[0, r]), sem)

    def scatter_all(buf, sem):
        return pltpu.make_async_copy(buf, out_hbm.at[pl.ds(0, block_sublanes), :], sem)

    @pl.when(i == 0)
    def _():
        y_prev[...] = jnp.zeros_like(y_prev)
        for r in range(MOE_ROWS):
            gather_copy(tok_ref, x_cur, g_cur, r).start()
            gather_copy(tok1_ref, x_nxt, g_nxt, r).start()

    @pl.when(i >= 2)
    def _():
        scatter_all(y_cur, s_cur).wait()

    gather_all(x_cur, g_cur).wait()
    x = jnp.concatenate([x_cur[pl.ds(j, MOE_ROWS, stride=ROW_TILE), :] for j in range(ROW_TILE)],
                        axis=1).astype(BF16)
    per = MOE_ROWS // MOE_GROUPS
    acts = []
    for c in range(MOE_GROUPS // 2):
        halves = []
        for part in range(2):
            grp = 2 * c + part
            for r in range(grp * per, (grp + 1) * per):
                gather_copy(tok2_ref, x_nn, g_nn, r).start()
                scatter_copy(slotp_ref, y_prev, s_prev, r).start(priority=1)
            lo = part * D_FF + c * MOE_CHUNK
            halves.append(_dot(x, w1_s[:, lo:lo + MOE_CHUNK]) + b1_ref[:, lo:lo + MOE_CHUNK])
        glu = jnp.minimum(halves[0], SWIGLU_LIMIT)
        lin = jnp.clip(halves[1], -SWIGLU_LIMIT, SWIGLU_LIMIT)
        acts.append((glu * _sigmoid(SWIGLU_ALPHA * glu) * (lin + 1.0)).astype(BF16))
    act = jnp.concatenate(acts, axis=1)
    y = _dot(act, w2_s[...]) + b2_ref[...]
    for j in range(ROW_TILE):
        y_cur[pl.ds(j, MOE_ROWS, stride=ROW_TILE), :] = y[:, j * LANES:(j + 1) * LANES]

    @pl.when(i == last)
    def _():
        for r in range(MOE_ROWS):
            scatter_copy(slot_ref, y_cur, s_cur, r).start()
        gather_all(x_nxt, g_nxt).wait()
        gather_all(x_nn, g_nn).wait()
        scatter_all(y_prev, s_prev).wait()

        @pl.when(i >= 1)
        def _():
            scatter_all(y_pp, s_pp).wait()

        scatter_all(y_cur, s_cur).wait()


def _moe_kernel(be_ref, nu_ref, tok_ref, tok1_ref, tok2_ref, slotp_ref, slot_ref, h_hbm, w1_ref, b1_ref, w2_ref,
                b2_ref, out_hbm, w1_s, w2_s, x0, x1, x2, y0, y1, y2, zbuf, gsem, ssem, zsem):
    i = pl.program_id(0)
    n_used = nu_ref[0]
    prev = be_ref[jnp.maximum(i - 1, 0)]

    @pl.when(jnp.logical_and(i < n_used, jnp.logical_or(i == 0, be_ref[i] != prev)))
    def _():
        w1_s[...] = w1_ref[...].astype(BF16)
        w2_s[...] = w2_ref[...].astype(BF16)

    for rot in range(3):
        @pl.when(jnp.logical_and(i < n_used, i % 3 == rot))
        def _(rot=rot):
            _moe_step(i, n_used - 1, tok_ref, tok1_ref, tok2_ref, slotp_ref, slot_ref, h_hbm, b1_ref, b2_ref,
                      out_hbm, w1_s, w2_s, (x0, x1, x2), (y0, y1, y2), gsem, ssem, rot)

    @pl.when(i >= n_used)
    def _():
        zbuf[...] = jnp.zeros_like(zbuf)
        first = pl.multiple_of(slot_ref[0, 0] * ROW_TILE, ROW_TILE)
        fill = pltpu.make_async_copy(zbuf, out_hbm.at[pl.ds(first, MOE_ROWS * ROW_TILE), :], zsem.at[0])
        fill.start()
        fill.wait()


def _moe_experts(block_expert, n_used, row_token, row_slot, h2, w1, b1, w2, b2, layer):
    n_blocks = row_token.shape[0]
    n_rows = n_blocks * MOE_ROWS
    spare = (n_rows + jnp.arange(MOE_ROWS, dtype=jnp.int32)).reshape(1, 1, MOE_ROWS)
    slot_prev = jnp.concatenate([spare, row_slot[:-1]], axis=0)
    smem_rows = lambda imap: pl.BlockSpec((None, 1, MOE_ROWS), imap, memory_space=pltpu.SMEM)
    expert_block = lambda *shape: pl.BlockSpec((None, None) + shape, lambda i, be, nu: (layer, be[i], 0, 0))
    row_buf = pltpu.VMEM((MOE_ROWS * ROW_TILE, LANES), F32)
    grid_spec = pltpu.PrefetchScalarGridSpec(
        num_scalar_prefetch=2,
        grid=(n_blocks,),
        in_specs=[smem_rows(lambda i, be, nu: (i, 0, 0)),
                  smem_rows(lambda i, be, nu: (jnp.minimum(i + 1, n_blocks - 1), 0, 0)),
                  smem_rows(lambda i, be, nu: (jnp.minimum(i + 2, n_blocks - 1), 0, 0)),
                  smem_rows(lambda i, be, nu: (i, 0, 0)),
                  smem_rows(lambda i, be, nu: (i, 0, 0)),
                  pl.BlockSpec(memory_space=pl.ANY),
                  expert_block(D_MODEL, 2 * D_FF), expert_block(1, 2 * D_FF),
                  expert_block(D_FF, D_MODEL), expert_block(1, D_MODEL)],
        out_specs=pl.BlockSpec(memory_space=pl.ANY),
        scratch_shapes=[pltpu.VMEM((D_MODEL, 2 * D_FF), BF16), pltpu.VMEM((D_FF, D_MODEL), BF16)]
                       + [row_buf] * 7
                       + [pltpu.SemaphoreType.DMA((3,)), pltpu.SemaphoreType.DMA((3,)),
                          pltpu.SemaphoreType.DMA((1,))],
    )
    return pl.pallas_call(
        _moe_kernel,
        grid_spec=grid_spec,
        out_shape=jax.ShapeDtypeStruct(((n_rows + MOE_ROWS) * ROW_TILE, LANES), F32),
        compiler_params=pltpu.CompilerParams(dimension_semantics=("arbitrary",), vmem_limit_bytes=VMEM_LIMIT,
                                             has_side_effects=True),
        name="moe_experts",
    )(block_expert, n_used, row_token, row_token, row_token, slot_prev, row_slot, h2, w1, b1, w2, b2)


def _moe_dispatch(eidx):
    n = eidx.shape[0]
    n_assign = n * TOP_K
    n_blocks = -(-n_assign // MOE_ROWS) + N_EXPERTS
    assert n_assign % MOE_ROWS == 0
    id_bits = (n_assign - 1).bit_length()
    assert N_EXPERTS << (id_bits + 1) < 2 ** 31
    expert = eidx.reshape(-1)
    ids = jnp.arange(n_assign, dtype=jnp.int32)
    experts = jnp.arange(N_EXPERTS, dtype=jnp.int32)
    counts = jnp.sum((expert[:, None] == experts[None, :]).astype(jnp.int32), axis=0)
    padded = (counts + MOE_ROWS - 1) // MOE_ROWS * MOE_ROWS
    pad_ends = jnp.cumsum(padded)
    real_keys = (expert << (id_bits + 1)) | ids
    j = jnp.arange(MOE_ROWS, dtype=jnp.int32)[None, :]
    pad_keys = jnp.where(j < (padded - counts)[:, None],
                         (experts[:, None] << (id_bits + 1)) | (1 << id_bits) | j, jnp.iinfo(jnp.int32).max)
    keys = jnp.sort(jnp.concatenate([real_keys, pad_keys.reshape(-1)]))
    is_pad = ((keys >> id_bits) & 1) == 1
    assign = keys & ((1 << id_bits) - 1)
    filler = n_assign + jnp.cumsum(is_pad.astype(jnp.int32)) - 1
    row_slot = jnp.where(is_pad, filler, (assign % TOP_K) * n + assign // TOP_K).astype(jnp.int32)
    row_token = jnp.where(is_pad, 0, assign // TOP_K).astype(jnp.int32)
    block_start = jnp.arange(n_blocks, dtype=jnp.int32) * MOE_ROWS
    block_expert = jnp.minimum(jnp.sum((pad_ends[None, :] <= block_start[:, None]).astype(jnp.int32), axis=1),
                               N_EXPERTS - 1).astype(jnp.int32)
    n_used = (pad_ends[-1] // MOE_ROWS).astype(jnp.int32).reshape(1)
    return (block_expert, n_used, row_token.reshape(n_blocks, 1, MOE_ROWS),
            row_slot.reshape(n_blocks, 1, MOE_ROWS))


def _combine_kernel(x1_ref, mod_ref, gate_ref, y0_ref, y1_ref, y2_ref, y3_ref, fg_ref, *o_refs, final,
                    n_prompt_tiles):
    gate = gate_ref[...]
    y = None
    for k, yk_ref in enumerate((y0_ref, y1_ref, y2_ref, y3_ref)):
        rows = jnp.concatenate([yk_ref[pl.ds(j, TOKEN_TILE, stride=ROW_TILE), :] for j in range(ROW_TILE)], axis=1)
        yk = gate[:, k:k + 1] * rows
        y = yk if k == 0 else y + yk
    x2 = x1_ref[...] + mod_ref[5:6, :] * y
    if not final:
        o_refs[0][...] = x2
        return
    ms = jnp.mean(x2 * x2, axis=-1, keepdims=True)
    x2 = x2 * lax.rsqrt(ms + EPS) * fg_ref[...]
    i = pl.program_id(0)

    @pl.when(i < n_prompt_tiles)
    def _():
        o_refs[0][...] = x2

    @pl.when(i >= n_prompt_tiles)
    def _():
        o_refs[1][...] = x2


def _combine(x1, mod_tab, gate, y_slots, final_g, n_prompt_tiles, tiles_per_sample, *, final):
    nt = x1.shape[0]
    tiles = nt // TOKEN_TILE
    mrow = functools.partial(_mod_row, n_prompt_tiles=n_prompt_tiles, tiles_per_sample=tiles_per_sample)
    slot_rows = lambda k: pl.BlockSpec((TOKEN_TILE * ROW_TILE, LANES), lambda i: (k * tiles + i, 0))
    if final:
        n_p = n_prompt_tiles * TOKEN_TILE
        out_specs = _token_specs((None, None), n_prompt_tiles)
        out_shape = [jax.ShapeDtypeStruct((n_p, D_MODEL), F32), jax.ShapeDtypeStruct((nt - n_p, D_MODEL), F32)]
    else:
        out_specs = pl.BlockSpec((TOKEN_TILE, D_MODEL), lambda i: (i, 0))
        out_shape = jax.ShapeDtypeStruct((nt, D_MODEL), F32)
    return pl.pallas_call(
        functools.partial(_combine_kernel, final=final, n_prompt_tiles=n_prompt_tiles),
        grid=(tiles,),
        in_specs=[pl.BlockSpec((TOKEN_TILE, D_MODEL), lambda i: (i, 0)),
                  pl.BlockSpec((None, 8, D_MODEL), lambda i: (mrow(i), 0, 0)),
                  pl.BlockSpec((TOKEN_TILE, LANES), lambda i: (i, 0))]
                 + [slot_rows(k) for k in range(TOP_K)]
                 + [pl.BlockSpec((1, D_MODEL), lambda i: (0, 0))],
        out_specs=out_specs,
        out_shape=out_shape,
        compiler_params=_cparams("arbitrary" if final else "parallel"),
        name="combine_final" if final else "combine",
    )(x1, mod_tab, gate, y_slots, y_slots, y_slots, y_slots, final_g)


def _state_to_blocks(init):
    b = init.shape[0]
    st = init.reshape(b, 2, B_HEADS * HEAD_W, B_STATE)
    row = jnp.arange(B_HEADS * HEAD_W)[:, None]
    left = jnp.where(row < 2 * HEAD_W, st, 0.0)
    right = jnp.where(row >= 2 * HEAD_W, st, 0.0)
    return jnp.concatenate([left, right], axis=-1)


def _blocks_to_state(s):
    b = s.shape[0]
    row = jnp.arange(B_HEADS * HEAD_W)[:, None]
    st = jnp.where(row < 2 * HEAD_W, s[..., :B_STATE], s[..., B_STATE:])
    return st.reshape(b, 2, B_HEADS, HEAD_W, B_STATE)


def kernel(x_prompt, x_sample, c, cache_a_k, cache_a_v, cache_c_k, cache_c_v, cache_d_k, cache_d_v, state_ssd, c_ctx, norm1_g, norm2_g, w_mod, b_mod, w_in, a_q_g, a_k_g, b_conv_w, b_conv_b, b_dt_bias, b_a_log, b_d, b_norm_g, c_lam, c_subln_g, d_rpb, w_branch, w_out, router_w, router_b, moe_w1, moe_b1, moe_w2, moe_b2, final_g):
    bp, tp, _ = x_prompt.shape
    bs, ts, _ = x_sample.shape
    depth = w_in.shape[0]
    n_p = bp * tp
    n_s = bs * ts
    assert n_p % TOKEN_TILE == 0 and ts % TOKEN_TILE == 0 and n_p % ts == 0
    n_prompt_tiles = n_p // TOKEN_TILE
    tiles_per_sample = ts // TOKEN_TILE
    past = cache_a_k.shape[2]

    xs = (x_prompt.reshape(n_p, D_MODEL), x_sample.reshape(n_s, D_MODEL))
    cvec = jnp.zeros((16, D_MODEL), F32).at[0].set(c_ctx).at[1:1 + bs].set(c)
    rope_tab = _rope_table(ts, TOKEN_TILE)
    perm_a = _rope_perm(HEAD_W)
    perm_c = _rope_perm(C_QK_DIM)
    head_of = np.arange(BRANCH_W) // HEAD_W
    bd = jnp.asarray((head_of[:, None] == head_of[None, :]) / HEAD_W, BF16)
    kv_of = np.arange(A_KV_W) // HEAD_W
    ex = jnp.asarray((kv_of[:, None] == (head_of // (N_HEADS // A_KV_HEADS))[None, :])
                     & ((np.arange(A_KV_W) % HEAD_W)[:, None] == (np.arange(BRANCH_W) % HEAD_W)[None, :]), BF16)
    scale_c = 1.0
    one = jnp.ones((1,), F32)
    rep = N_HEADS // A_KV_HEADS
    zero_state = jnp.zeros((bp, 2, BRANCH_W, LANES), F32)

    new_cache = [[] for _ in range(7)]
    for l in range(depth):
        lam_init = 0.8 - 0.6 * math.exp(-0.3 * l)
        lq = c_lam[l]
        lam = (jnp.exp(jnp.sum(lq[0] * lq[1])) - jnp.exp(jnp.sum(lq[2] * lq[3])) + lam_init).reshape(1)

        wl = w_in[l]
        w1 = jnp.concatenate([wl[:, 0:512], wl[:, 768:1280], wl[:, 512:768], wl[:, 1280:1288],
                              jnp.zeros((D_MODEL, LANES - 8), F32), wl[:, 1288:2824]], axis=1).astype(BF16)
        wg = wl[:, 2824:].astype(BF16)
        wb = w_branch[l].astype(BF16)
        wo = w_out[l].astype(BF16)
        rw = jnp.pad(router_w[l], ((0, 0), (0, LANES - N_EXPERTS)))
        rwh = rw.astype(BF16)
        rwl = (rw - rwh.astype(F32)).astype(BF16)
        rb = jnp.pad(router_b[l], (0, LANES - N_EXPERTS)).reshape(1, LANES)

        mod = _modulation(cvec, w_mod[l], b_mod[l].reshape(1, -1))
        mod_tab = jnp.pad(mod.reshape(16, 6, D_MODEL), ((0, 0), (0, 2), (0, 0)))

        g1 = norm1_g[l].reshape(1, D_MODEL)
        g2 = norm2_g[l].reshape(1, D_MODEL)
        att_a, att_c, att_d, kv_a, kv_c, kv_d, ssd_in = _in_proj(
            xs, mod_tab, g1, w1, rope_tab, bd, perm_a, perm_c, ex,
            jnp.tile(a_q_g[l], N_HEADS).reshape(1, BRANCH_W), jnp.tile(a_k_g[l], A_KV_HEADS).reshape(1, A_KV_W),
            n_prompt_tiles, tiles_per_sample)

        ctx_ak = jnp.repeat(cache_a_k[:, l], rep, axis=2).reshape(bs, past, BRANCH_W).astype(BF16)
        ctx_av = jnp.repeat(cache_a_v[:, l], rep, axis=2).reshape(bs, past, BRANCH_W).astype(BF16)
        o_a_p = _mha(att_a, one, n_seq=bp, seq=tp, row0=0, n_maps=1, scale=1.0)
        o_a_s = _mha(att_a, one, n_seq=bs, seq=ts, row0=n_p, n_maps=1, scale=1.0, ctx=(ctx_ak, ctx_av))

        subln = (jnp.tile(c_subln_g[l], N_HEADS).reshape(1, BRANCH_W), bd, 1.0 - lam_init)
        ctx_ck = cache_c_k[:, l].reshape(bs, past, BRANCH_W).astype(BF16)
        ctx_cv = cache_c_v[:, l].reshape(bs, past, BRANCH_W).astype(BF16)
        o_c_p = _mha(att_c, lam, n_seq=bp, seq=tp, row0=0, n_maps=2, scale=scale_c, subln=subln)
        o_c_s = _mha(att_c, lam, n_seq=bs, seq=ts, row0=n_p, n_maps=2, scale=scale_c, ctx=(ctx_ck, ctx_cv),
                     subln=subln)

        o_d_p = _mha(att_d, one, n_seq=bp, seq=tp, row0=0, n_maps=1, scale=1.0)
        o_d_s = _na_attention(att_d, cache_d_k[:, l].reshape(bs, past, BRANCH_W).astype(BF16),
                              cache_d_v[:, l].reshape(bs, past, BRANCH_W).astype(BF16),
                              _na_bias_table(d_rpb[l], ts // GRID_W), n_seq=bs, seq=ts, row0=n_p)

        dtb = b_dt_bias[l].reshape(8)
        a_neg = -jnp.exp(b_a_log[l].reshape(8))
        ssd_args = (jnp.pad(b_conv_w[l], ((0, 8 - B_CONV_W), (0, 0))), b_conv_b[l].reshape(1, B_XBC),
                    jnp.zeros((8, LANES), F32).at[0, :8].set(dtb).at[1, :8].set(a_neg),
                    jnp.repeat(b_d[l], HEAD_W).reshape(1, BRANCH_W), b_norm_g[l].reshape(1, BRANCH_W))
        o_b_p, st_p = _ssd_branch(ssd_in, *ssd_args, zero_state, n_seq=bp, seq=tp, row0=0)
        o_b_s, _ = _ssd_branch(ssd_in, *ssd_args, _state_to_blocks(state_ssd[:, l]), n_seq=bs, seq=ts, row0=n_p)

        x1, h2, gate, eidx = _merge(xs, mod_tab, g1, g2, (o_a_p, o_b_p, o_c_p, o_d_p), (o_a_s, o_b_s, o_c_s, o_d_s),
                                    wg, wb, wo, rwh, rwl, rb, n_prompt_tiles, tiles_per_sample)
        block_expert, n_used, row_token, row_slot = _moe_dispatch(eidx[:, :TOP_K])
        y_slots = _moe_experts(block_expert, n_used, row_token, row_slot, h2, moe_w1,
                               moe_b1.reshape(depth, N_EXPERTS, 1, -1), moe_w2,
                               moe_b2.reshape(depth, N_EXPERTS, 1, -1), l)
        x = _combine(x1, mod_tab, gate, y_slots, final_g.reshape(1, D_MODEL), n_prompt_tiles, tiles_per_sample,
                     final=(l == depth - 1))
        xs = (x,)

        new_cache[0].append(kv_a[:n_p, :A_KV_W].reshape(bp, tp, A_KV_HEADS, HEAD_W))
        new_cache[1].append(kv_a[:n_p, A_KV_W:].reshape(bp, tp, A_KV_HEADS, HEAD_W))
        new_cache[2].append(kv_c[:n_p, :BRANCH_W].reshape(bp, tp, N_HEADS, 2 * C_QK_DIM))
        new_cache[3].append(kv_c[:n_p, BRANCH_W:].reshape(bp, tp, N_HEADS, HEAD_W))
        new_cache[4].append(kv_d[:n_p, :BRANCH_W].reshape(bp, tp, N_HEADS, HEAD_W))
        new_cache[5].append(kv_d[:n_p, BRANCH_W:].reshape(bp, tp, N_HEADS, HEAD_W))
        new_cache[6].append(_blocks_to_state(st_p))

    y_prompt = x[0].reshape(bp, tp, D_MODEL)
    y_sample = x[1].reshape(bs, ts, D_MODEL)
    return (y_prompt, y_sample) + tuple(jnp.stack(v, axis=1) for v in new_cache)
```

```python
import functools
import math

import jax
import jax.numpy as jnp
import numpy as np
from jax import lax
from jax.experimental import pallas as pl
from jax.experimental.pallas import tpu as pltpu

F32 = jnp.float32
BF16 = jnp.bfloat16

D_MODEL = 1024
N_BRANCH = 4
BRANCH_W = 256
HEAD_W = 64
N_HEADS = 4
GRID_W = 64
ROPE_THETA = 10000.0
EPS = 1e-6
A_KV_HEADS = 2
A_KV_W = A_KV_HEADS * HEAD_W
C_QK_DIM = 32
B_HEADS = 4
B_STATE = 64
B_CHUNK = 128
B_XBC = 512
B_CONV_W = 5
CONV_PAD = 8
NA_ROWS = 8
NA_COLS = 16
N_EXPERTS = 32
TOP_K = 4
D_FF = 1024
SWIGLU_LIMIT = 7.0
SWIGLU_ALPHA = 1.702
LANES = 128
ROW_TILE = D_MODEL // LANES
NEG = -1e30

TOKEN_TILE = 512
MOE_ROWS = 256
ATT_W = 3 * BRANCH_W
SSD_W = B_XBC + BRANCH_W + LANES
OFF_A, OFF_B, OFF_C, OFF_D = 0, 512, 512 + SSD_W, 512 + SSD_W + ATT_W
W1_COLS = OFF_D + ATT_W
VMEM_LIMIT = 56 * 1024 * 1024


def _cparams(*sem):
    return pltpu.CompilerParams(dimension_semantics=sem, vmem_limit_bytes=VMEM_LIMIT)


def _dot(a, b):
    return jnp.dot(a, b, preferred_element_type=F32)


def _dot_nt(a, b):
    return lax.dot_general(a, b, (((1,), (1,)), ((), ())), preferred_element_type=F32)


def _dot_tn(a, b):
    return lax.dot_general(a, b, (((0,), (0,)), ((), ())), preferred_element_type=F32)


def _sigmoid(x):
    return 0.5 * jnp.tanh(0.5 * x) + 0.5


def _split2(x):
    hi = x.astype(BF16)
    return hi, (x - hi.astype(F32)).astype(BF16)


def _split3(x):
    h1 = x.astype(BF16)
    r1 = x - h1.astype(F32)
    h2 = r1.astype(BF16)
    h3 = (r1 - h2.astype(F32)).astype(BF16)
    return h1, h2, h3


def _norm_mod(x, g, shift, scale):
    ms = jnp.mean(x * x, axis=-1, keepdims=True)
    return (x * lax.rsqrt(ms + EPS)) * g * (1.0 + scale) + shift


def _head_rmsnorm(x, bd, g):
    hi, lo = _split2(x * x)
    ms = _dot(hi, bd) + _dot(lo, bd)
    return x * lax.rsqrt(ms + EPS) * g


def _rope(x, perm, cos, sin):
    hi, lo = _split2(x)
    return x * cos + (_dot(hi, perm) + _dot(lo, perm)) * sin


def _mod_kernel(c_ref, w_ref, b_ref, o_ref):
    c = c_ref[...]
    s = (c * _sigmoid(c)).astype(BF16)
    o_ref[...] = _dot(s, w_ref[...].astype(BF16)) + b_ref[...]


def _modulation(cvec, w_mod, b_mod):
    n = w_mod.shape[1]
    tn = 1536
    return pl.pallas_call(
        _mod_kernel,
        grid=(n // tn,),
        in_specs=[pl.BlockSpec((16, D_MODEL), lambda j: (0, 0)),
                  pl.BlockSpec((D_MODEL, tn), lambda j: (0, j)),
                  pl.BlockSpec((1, tn), lambda j: (0, j))],
        out_specs=pl.BlockSpec((16, tn), lambda j: (0, j)),
        out_shape=jax.ShapeDtypeStruct((16, n), F32),
        compiler_params=_cparams("parallel"),
        name="modulation",
    )(cvec, w_mod, b_mod)


def _mod_row(i, n_prompt_tiles, tiles_per_sample):
    return jnp.where(i < n_prompt_tiles, 0, 1 + (i - n_prompt_tiles) // tiles_per_sample)


def _rope_row(i, n_prompt_tiles, tiles_per_sample):
    return jnp.where(i < n_prompt_tiles, 0, 1 + (i - n_prompt_tiles) % tiles_per_sample)


def _token_rows(x_refs, n_prompt_tiles):
    if len(x_refs) == 1:
        return x_refs[0][...]
    return jnp.where(pl.program_id(0) < n_prompt_tiles, x_refs[0][...], x_refs[1][...])


def _token_specs(xs, n_prompt_tiles):
    if len(xs) == 1:
        return [pl.BlockSpec((TOKEN_TILE, D_MODEL), lambda i: (i, 0))]
    return [pl.BlockSpec((TOKEN_TILE, D_MODEL), lambda i: (jnp.minimum(i, n_prompt_tiles - 1), 0)),
            pl.BlockSpec((TOKEN_TILE, D_MODEL), lambda i: (jnp.maximum(i - n_prompt_tiles, 0), 0))]


def _inproj_kernel(*refs, n_x, n_prompt_tiles):
    x_refs = refs[:n_x]
    (mod_ref, g_ref, w_ref, rope_ref, bd_ref, pa_ref, pc_ref, ex_ref, gq_ref, gk_ref,
     atta_ref, attc_ref, attd_ref, kva_ref, kvc_ref, kvd_ref, ssd_ref) = refs[n_x:]
    mod = mod_ref[...]
    h = _norm_mod(_token_rows(x_refs, n_prompt_tiles), g_ref[...], mod[0:1], mod[1:2])
    u = _dot(h.astype(BF16), w_ref[...])
    cos_a, sin_a = rope_ref[:, 0:256], rope_ref[:, 256:512]
    cos_c, sin_c = rope_ref[:, 512:768], rope_ref[:, 768:1024]
    q_scale = HEAD_W ** -0.5

    q = _rope(_head_rmsnorm(u[:, OFF_A:OFF_A + 256], bd_ref[...], gq_ref[...]), pa_ref[...], cos_a, sin_a)
    k = _rope(_head_rmsnorm(u[:, OFF_A + 256:OFF_A + 384], bd_ref[0:A_KV_W, 0:A_KV_W], gk_ref[...]),
              pa_ref[0:A_KV_W, 0:A_KV_W], cos_a[:, 0:A_KV_W], sin_a[:, 0:A_KV_W])
    v = u[:, OFF_A + 384:OFF_A + 512]
    ex = ex_ref[...]
    atta_ref[:, 0:256] = (q * q_scale).astype(BF16)
    atta_ref[:, 256:512] = _dot(k.astype(BF16), ex).astype(BF16)
    atta_ref[:, 512:768] = _dot(v.astype(BF16), ex).astype(BF16)
    kva_ref[:, 0:A_KV_W] = k
    kva_ref[:, A_KV_W:2 * A_KV_W] = v

    ssd_ref[...] = u[:, OFF_B:OFF_B + SSD_W]

    cq = _rope(u[:, OFF_C:OFF_C + 256], pc_ref[...], cos_c, sin_c)
    ck = _rope(u[:, OFF_C + 256:OFF_C + 512], pc_ref[...], cos_c, sin_c)
    cv = u[:, OFF_C + 512:OFF_C + 768]
    attc_ref[:, 0:256] = (cq * (C_QK_DIM ** -0.5)).astype(BF16)
    attc_ref[:, 256:512] = ck.astype(BF16)
    attc_ref[:, 512:768] = cv.astype(BF16)
    kvc_ref[:, 0:256] = ck
    kvc_ref[:, 256:512] = cv

    attd_ref[:, 0:256] = (u[:, OFF_D:OFF_D + 256] * q_scale).astype(BF16)
    attd_ref[:, 256:768] = u[:, OFF_D + 256:OFF_D + 768].astype(BF16)
    kvd_ref[...] = u[:, OFF_D + 256:OFF_D + 768]


def _in_proj(xs, mod_tab, g, w1, rope_tab, bd, pa, pc, ex, gq, gk, n_prompt_tiles, tiles_per_sample):
    nt = sum(x.shape[0] for x in xs)
    mrow = functools.partial(_mod_row, n_prompt_tiles=n_prompt_tiles, tiles_per_sample=tiles_per_sample)
    rrow = functools.partial(_rope_row, n_prompt_tiles=n_prompt_tiles, tiles_per_sample=tiles_per_sample)
    const = lambda *shape: pl.BlockSpec(shape, lambda i: (0,) * len(shape))
    rows = lambda w: pl.BlockSpec((TOKEN_TILE, w), lambda i: (i, 0))
    widths = (ATT_W, ATT_W, ATT_W, 2 * A_KV_W, 2 * BRANCH_W, 2 * BRANCH_W, SSD_W)
    dtypes = (BF16, BF16, BF16, F32, F32, F32, F32)
    return pl.pallas_call(
        functools.partial(_inproj_kernel, n_x=len(xs), n_prompt_tiles=n_prompt_tiles),
        grid=(nt // TOKEN_TILE,),
        in_specs=_token_specs(xs, n_prompt_tiles)
                 + [pl.BlockSpec((None, 8, D_MODEL), lambda i: (mrow(i), 0, 0)),
                  const(1, D_MODEL), const(D_MODEL, W1_COLS),
                  pl.BlockSpec((TOKEN_TILE, 4 * BRANCH_W), lambda i: (rrow(i), 0)),
                  const(BRANCH_W, BRANCH_W), const(BRANCH_W, BRANCH_W), const(BRANCH_W, BRANCH_W),
                  const(A_KV_W, BRANCH_W), const(1, BRANCH_W), const(1, A_KV_W)],
        out_specs=[rows(w) for w in widths],
        out_shape=[jax.ShapeDtypeStruct((nt, w), dt) for w, dt in zip(widths, dtypes)],
        compiler_params=_cparams("parallel"),
        name="in_proj",
    )(*xs, mod_tab, g, w1, rope_tab, bd, pa, pc, ex, gq, gk)


def _rope_perm(dim):
    p = np.zeros((BRANCH_W, BRANCH_W), np.float32)
    s = dim // 4
    for j in range(BRANCH_W):
        quarter = (j % dim) // s
        if quarter % 2 == 0:
            p[j + s, j] = -1.0
        else:
            p[j - s, j] = 1.0
    return jnp.asarray(p, BF16)


def _rope_table(t, n_identity):
    pos = jnp.arange(t)
    rows = (pos // GRID_W).astype(F32)
    cols = (pos % GRID_W).astype(F32)
    parts = []
    for dim in (HEAD_W, C_QK_DIM):
        axis_dim = dim // 2
        inv = ROPE_THETA ** (-jnp.arange(0, axis_dim, 2, dtype=F32) / axis_dim)
        ang_r = rows[:, None] * inv[None, :]
        ang_c = cols[:, None] * inv[None, :]
        ang = jnp.concatenate([ang_r, ang_r, ang_c, ang_c], axis=-1)
        reps = BRANCH_W // dim
        parts += [jnp.tile(jnp.cos(ang), (1, reps)), jnp.tile(jnp.sin(ang), (1, reps))]
    tab = jnp.concatenate(parts, axis=1)
    ident = jnp.concatenate([jnp.ones((n_identity, BRANCH_W), F32), jnp.zeros((n_identity, BRANCH_W), F32)] * 2, axis=1)
    return jnp.concatenate([ident, tab], axis=0)


def _mha_kernel(lam_ref, q_ref, k_ref, v_ref, *rest, n_maps, scale, has_ctx, subln_scale):
    rest = list(rest)
    ck_ref, cv_ref = (rest.pop(0), rest.pop(0)) if has_ctx else (None, None)
    g_ref, bd_ref = (rest.pop(0), rest.pop(0)) if subln_scale is not None else (None, None)
    o_ref = rest.pop(0)
    q = q_ref[...]
    k = k_ref[...]
    v = v_ref[...]
    tq = q.shape[0]
    lane = lax.broadcasted_iota(jnp.int32, (1, BRANCH_W), 1)
    sub_w = HEAD_W // n_maps
    acc = jnp.zeros((tq, BRANCH_W), F32)
    for h in range(N_HEADS):
        oh = None
        for j in range(n_maps):
            qm = jnp.where((lane // sub_w) == (h * n_maps + j), q, jnp.zeros_like(q))
            s = _dot_nt(qm, k)
            if scale != 1.0:
                s = s * scale
            m = jnp.max(s, axis=-1, keepdims=True)
            if has_ctx:
                sc = _dot_nt(qm, ck_ref[...])
                if scale != 1.0:
                    sc = sc * scale
                m = jnp.maximum(m, jnp.max(sc, axis=-1, keepdims=True))
                pc = jnp.exp(sc - m)
            p = jnp.exp(s - m)
            l = jnp.sum(p, axis=-1, keepdims=True)
            pv = _dot(p.astype(BF16), v)
            if has_ctx:
                l = l + jnp.sum(pc, axis=-1, keepdims=True)
                pv = pv + _dot(pc.astype(BF16), cv_ref[...])
            pv = pv * (1.0 / l)
            oh = pv if j == 0 else oh - lam_ref[0] * pv
        acc = jnp.where((lane // HEAD_W) == h, oh, acc)
    if subln_scale is not None:
        acc = _head_rmsnorm(acc, bd_ref[...], g_ref[...]) * subln_scale
    o_ref[...] = acc.astype(o_ref.dtype)


def _mha(att, lam, *, n_seq, seq, row0, n_maps, scale, ctx=None, subln=None, tq=256):
    qb0 = row0 // tq
    kb0 = row0 // seq
    nq = seq // tq
    assert row0 % seq == 0 and seq % tq == 0
    in_specs = [pl.BlockSpec(memory_space=pltpu.SMEM),
                pl.BlockSpec((tq, BRANCH_W), lambda i, j: (qb0 + i * nq + j, 0)),
                pl.BlockSpec((seq, BRANCH_W), lambda i, j: (kb0 + i, 1)),
                pl.BlockSpec((seq, BRANCH_W), lambda i, j: (kb0 + i, 2))]
    args = [lam, att, att, att]
    kk = seq
    if ctx is not None:
        kc = ctx[0].shape[1]
        kk += kc
        in_specs += [pl.BlockSpec((None, kc, BRANCH_W), lambda i, j: (i, 0, 0))] * 2
        args += list(ctx)
    subln_scale = None
    if subln is not None:
        in_specs += [pl.BlockSpec((1, BRANCH_W), lambda i, j: (0, 0)),
                     pl.BlockSpec((BRANCH_W, BRANCH_W), lambda i, j: (0, 0))]
        args += [subln[0], subln[1]]
        subln_scale = subln[2]
    return pl.pallas_call(
        functools.partial(_mha_kernel, n_maps=n_maps, scale=scale, has_ctx=ctx is not None, subln_scale=subln_scale),
        grid=(n_seq, nq),
        in_specs=in_specs,
        out_specs=pl.BlockSpec((tq, BRANCH_W), lambda i, j: (i * nq + j, 0)),
        out_shape=jax.ShapeDtypeStruct((n_seq * seq, BRANCH_W), BF16),
        compiler_params=_cparams("parallel", "parallel"),
        name=f"mha_maps{n_maps}_k{kk}",
    )(*args)


def _na_row0(r, n_rows):
    return jnp.clip(r - NA_ROWS // 2, 0, n_rows - NA_ROWS)


def _na_kernel(q_ref, k_ref, v_ref, ck_ref, cv_ref, bias_ref, o_ref, *, n_rows):
    win = NA_ROWS * GRID_W
    lane = lax.broadcasted_iota(jnp.int32, (1, BRANCH_W), 1)
    ck = ck_ref[...]
    cv = cv_ref[...]

    def row_body(r, carry):
        r0 = _na_row0(r, n_rows)
        start = pl.multiple_of(r0 * GRID_W, GRID_W)
        rows = pl.ds(pl.multiple_of(r * GRID_W, GRID_W), GRID_W)
        q = q_ref[rows, :]
        kw = k_ref[pl.ds(start, win), :]
        vw = v_ref[pl.ds(start, win), :]
        qs = jnp.concatenate(
            [jnp.where((lane // HEAD_W) == h, q, jnp.zeros_like(q)) for h in range(N_HEADS)], axis=0)
        s_loc = _dot_nt(qs, kw) + bias_ref[r - r0]
        s_ctx = _dot_nt(qs, ck)
        m = jnp.maximum(jnp.max(s_loc, axis=-1, keepdims=True), jnp.max(s_ctx, axis=-1, keepdims=True))
        p_loc = jnp.exp(s_loc - m)
        p_ctx = jnp.exp(s_ctx - m)
        l = jnp.sum(p_loc, axis=-1, keepdims=True) + jnp.sum(p_ctx, axis=-1, keepdims=True)
        o = (_dot(p_loc.astype(BF16), vw) + _dot(p_ctx.astype(BF16), cv)) * (1.0 / l)
        acc = jnp.zeros((GRID_W, BRANCH_W), F32)
        for h in range(N_HEADS):
            acc = jnp.where((lane // HEAD_W) == h, o[h * GRID_W:(h + 1) * GRID_W], acc)
        o_ref[rows, :] = acc.astype(o_ref.dtype)
        return carry

    lax.fori_loop(0, n_rows, row_body, 0)


def _na_attention(att, ck, cv, bias_tab, *, n_seq, seq, row0):
    n_rows = seq // GRID_W
    kk = ck.shape[1]
    win = NA_ROWS * GRID_W
    b0 = row0 // seq
    assert row0 % seq == 0
    return pl.pallas_call(
        functools.partial(_na_kernel, n_rows=n_rows),
        grid=(n_seq,),
        in_specs=[pl.BlockSpec((seq, BRANCH_W), lambda i: (b0 + i, 0)),
                  pl.BlockSpec((seq, BRANCH_W), lambda i: (b0 + i, 1)),
                  pl.BlockSpec((seq, BRANCH_W), lambda i: (b0 + i, 2)),
                  pl.BlockSpec((None, kk, BRANCH_W), lambda i: (i, 0, 0)),
                  pl.BlockSpec((None, kk, BRANCH_W), lambda i: (i, 0, 0)),
                  pl.BlockSpec((NA_ROWS, N_HEADS * GRID_W, win), lambda i: (0, 0, 0))],
        out_specs=pl.BlockSpec((seq, BRANCH_W), lambda i: (i, 0)),
        out_shape=jax.ShapeDtypeStruct((n_seq * seq, BRANCH_W), BF16),
        compiler_params=_cparams("parallel"),
        name="na_attention",
    )(att, att, att, ck, cv, bias_tab)


def _na_bias_table(rpb, n_rows):
    var = np.arange(NA_ROWS)[:, None, None]
    j = np.arange(NA_ROWS)[None, :, None]
    sel_r = (np.arange(2 * NA_ROWS - 1)[None, None, :] == j - var + (NA_ROWS - 1)).astype(np.float32)
    col = np.arange(GRID_W)[:, None, None]
    kc = np.arange(GRID_W)[None, :, None]
    c0 = np.clip(col - NA_COLS // 2, 0, GRID_W - NA_COLS)
    valid = ((kc >= c0) & (kc < c0 + NA_COLS))[:, :, 0]
    sel_c = (np.arange(2 * NA_COLS - 1)[None, None, :] == kc - col + (NA_COLS - 1)).astype(np.float32)
    tab = jnp.einsum('hrc,vjr,xkc->vhxjk', rpb.astype(F32), sel_r, sel_c, precision=lax.Precision.HIGHEST)
    tab = jnp.where(valid[None, None, :, None, :], tab, NEG)
    return tab.reshape(NA_ROWS, N_HEADS * GRID_W, NA_ROWS * GRID_W)


def _softplus(x):
    return jnp.maximum(x, 0.0) + jnp.log1p(jnp.exp(-jnp.abs(x)))


def _expand_heads(colmat, d, lane256):
    out = jnp.zeros((colmat.shape[0], BRANCH_W), F32)
    for h in range(B_HEADS):
        j = d * B_HEADS + h
        out = jnp.where((lane256 // HEAD_W) == h,
                        jnp.broadcast_to(colmat[:, j:j + 1], (colmat.shape[0], BRANCH_W)), out)
    return out


def _ssd_kernel(xbc_ref, z_ref, dt_ref, cw_ref, cb_ref, pc_ref, dsk_ref, ng_ref, init_ref,
                o_ref, st_ref, xpad_s, xc_s, yf_s, s_s, *, seq):
    n_chunks = seq // B_CHUNK
    L = B_CHUNK
    ri = lax.broadcasted_iota(jnp.int32, (L, L), 0)
    ci = lax.broadcasted_iota(jnp.int32, (L, L), 1)
    low = ci <= ri
    upp = ri <= ci
    low_b = jnp.where(low, 1.0, 0.0).astype(BF16)
    upp_b = jnp.where(upp, 1.0, 0.0).astype(BF16)
    lane128 = lax.broadcasted_iota(jnp.int32, (1, LANES), 1)
    lane256 = lax.broadcasted_iota(jnp.int32, (1, BRANCH_W), 1)
    row256 = lax.broadcasted_iota(jnp.int32, (BRANCH_W, 1), 0)
    blockmask = (row256 // (2 * HEAD_W)) == (lane128 // B_STATE)

    xpad_s[0:CONV_PAD, :] = jnp.zeros((CONV_PAD, B_XBC), F32)
    xpad_s[seq + CONV_PAD:seq + 2 * CONV_PAD, :] = jnp.zeros((CONV_PAD, B_XBC), F32)

    def pad_body(c, carry):
        base = pl.multiple_of(c * L, L)
        xpad_s[pl.ds(pl.multiple_of(base + CONV_PAD, CONV_PAD), L), :] = xbc_ref[pl.ds(base, L), :]
        return carry

    lax.fori_loop(0, n_chunks, pad_body, 0)
    cw = cw_ref[...]
    cb = cb_ref[...]

    def conv_body(c, carry):
        base = pl.multiple_of(c * L, L)
        w = xpad_s[pl.ds(base, L + 2 * CONV_PAD), :]
        acc = jnp.zeros((L, B_XBC), F32) + cb
        for kk in range(B_CONV_W):
            off = CONV_PAD - B_CONV_W // 2 + kk
            acc = acc + w[off:off + L, :] * cw[kk:kk + 1, :]
        xc_s[pl.ds(base, L), :] = acc * _sigmoid(acc)
        return carry

    lax.fori_loop(0, n_chunks, conv_body, 0)

    dt_bias = pc_ref[0:1, :]
    a_neg = pc_ref[1:2, :]
    dsk = dsk_ref[...]
    ng = ng_ref[...]

    def chunk(c, d):
        base = pl.multiple_of(c * L, L)
        xs = xc_s[pl.ds(base, L), 0:BRANCH_W]
        bm = xc_s[pl.ds(base, L), BRANCH_W:BRANCH_W + LANES].astype(BF16)
        cm = xc_s[pl.ds(base, L), BRANCH_W + LANES:B_XBC].astype(BF16)
        dtc = _softplus(dt_ref[pl.ds(base, L), :] + dt_bias)
        da_c = dtc * a_neg
        da_r = da_c.T
        tri_c = low_b if d == 0 else upp_b
        tri_r = upp_b if d == 0 else low_b
        c1, c2, c3 = _split3(da_c)
        cum_c = _dot(tri_c, c1) + _dot(tri_c, c2) + _dot(tri_c, c3)
        r1, r2, r3 = _split3(da_r)
        cum_r = _dot(r1, tri_r) + _dot(r2, tri_r) + _dot(r3, tri_r)
        cum_end = cum_c[L - 1:L, :] if d == 0 else cum_c[0:1, :]
        dmask = low if d == 0 else upp

        g0 = _dot_nt(jnp.where(lane128 < B_STATE, cm, jnp.zeros_like(cm)), bm)
        g1 = _dot_nt(jnp.where(lane128 >= B_STATE, cm, jnp.zeros_like(cm)), bm)
        dt_x = _expand_heads(dtc, d, lane256)
        e_a = _expand_heads(jnp.exp(cum_c), d, lane256)
        t_e = _expand_heads(jnp.exp(cum_end - cum_c), d, lane256)
        xdt = xs * dt_x
        xdt_b = xdt.astype(BF16)
        y = jnp.zeros((L, BRANCH_W), F32)
        for h in range(B_HEADS):
            j = d * B_HEADS + h
            col = jnp.broadcast_to(cum_c[:, j:j + 1], (L, L))
            row = jnp.broadcast_to(cum_r[j:j + 1, :], (L, L))
            dec = jnp.exp(jnp.where(dmask, col - row, NEG))
            sc = ((g0 if h < 2 else g1) * dec).astype(BF16)
            y = jnp.where((lane256 // HEAD_W) == h, _dot(sc, xdt_b), y)
        state = s_s[...]
        y = y + _dot_nt(cm, state.astype(BF16)) * e_a
        zmat = _dot_tn((xdt * t_e).astype(BF16), bm)
        e_end = jnp.exp(cum_end)
        cd = jnp.zeros((BRANCH_W, LANES), F32)
        for h in range(B_HEADS):
            j = d * B_HEADS + h
            cd = jnp.where((row256 // HEAD_W) == h, jnp.broadcast_to(e_end[:, j:j + 1], (BRANCH_W, LANES)), cd)
        s_s[...] = state * cd + jnp.where(blockmask, zmat, 0.0)
        return base, xs, y

    s_s[...] = init_ref[0]

    def fwd_body(c, carry):
        base, _, y = chunk(c, 0)
        yf_s[pl.ds(base, L), :] = y
        return carry

    lax.fori_loop(0, n_chunks, fwd_body, 0)
    st_ref[0] = s_s[...]
    s_s[...] = init_ref[1]

    def bwd_body(i, carry):
        c = n_chunks - 1 - i
        base, xs, y = chunk(c, 1)
        z = z_ref[pl.ds(base, L), :]
        yt = (yf_s[pl.ds(base, L), :] + y + dsk * xs) * (z * _sigmoid(z))
        ms = jnp.mean(yt * yt, axis=-1, keepdims=True)
        o_ref[pl.ds(base, L), :] = (yt * lax.rsqrt(ms + EPS) * ng).astype(o_ref.dtype)
        return carry

    lax.fori_loop(0, n_chunks, bwd_body, 0)
    st_ref[1] = s_s[...]


def _ssd_branch(ssd_in, conv_w, conv_b, pc, dsk, ng, init, *, n_seq, seq, row0):
    b0 = row0 // seq
    assert row0 % seq == 0
    full = lambda *shape: pl.BlockSpec(shape, lambda i: (0,) * len(shape))
    return pl.pallas_call(
        functools.partial(_ssd_kernel, seq=seq),
        grid=(n_seq,),
        in_specs=[pl.BlockSpec((seq, B_XBC), lambda i: (b0 + i, 0)),
                  pl.BlockSpec((seq, BRANCH_W), lambda i: (b0 + i, B_XBC // BRANCH_W)),
                  pl.BlockSpec((seq, LANES), lambda i: (b0 + i, (B_XBC + BRANCH_W) // LANES)),
                  full(8, B_XBC), full(1, B_XBC), full(8, LANES), full(1, BRANCH_W), full(1, BRANCH_W),
                  pl.BlockSpec((None, 2, BRANCH_W, LANES), lambda i: (i, 0, 0, 0))],
        out_specs=[pl.BlockSpec((seq, BRANCH_W), lambda i: (i, 0)),
                   pl.BlockSpec((None, 2, BRANCH_W, LANES), lambda i: (i, 0, 0, 0))],
        out_shape=[jax.ShapeDtypeStruct((n_seq * seq, BRANCH_W), BF16),
                   jax.ShapeDtypeStruct((n_seq, 2, BRANCH_W, LANES), F32)],
        scratch_shapes=[pltpu.VMEM((seq + 2 * CONV_PAD, B_XBC), F32), pltpu.VMEM((seq, B_XBC), F32),
                        pltpu.VMEM((seq, BRANCH_W), F32), pltpu.VMEM((BRANCH_W, LANES), F32)],
        compiler_params=_cparams("parallel"),
        name=f"ssd_t{seq}",
    )(ssd_in, ssd_in, ssd_in, conv_w, conv_b, pc, dsk, ng, init)


def _merge_kernel(*refs, n_x, n_prompt_tiles):
    x_refs = refs[:n_x]
    mod_ref, g1_ref, g2_ref = refs[n_x:n_x + 3]
    rest = refs[n_x + 3:]
    br_p = rest[0:N_BRANCH]
    br_s = rest[N_BRANCH:2 * N_BRANCH]
    wg_ref, wb_ref, wo_ref, rwh_ref, rwl_ref, rb_ref, x1_ref, h2_ref, gate_ref, eidx_ref = rest[2 * N_BRANCH:]
    is_prompt = pl.program_id(0) < n_prompt_tiles
    x = _token_rows(x_refs, n_prompt_tiles)
    mod = mod_ref[...]
    h = _norm_mod(x, g1_ref[...], mod[0:1], mod[1:2]).astype(BF16)
    merged = None
    for i in range(N_BRANCH):
        gate = _sigmoid(_dot(h, wg_ref[:, i * D_MODEL:(i + 1) * D_MODEL]))
        br = jnp.where(is_prompt, br_p[i][...], br_s[i][...])
        proj = _dot(br, wb_ref[i])
        merged = gate * proj if i == 0 else merged + gate * proj
    y = _dot(merged.astype(BF16), wo_ref[...])
    x1 = x + mod[2:3] * y
    x1_ref[...] = x1
    h2 = _norm_mod(x1, g2_ref[...], mod[3:4], mod[4:5])
    hh, hl = _split2(h2)
    for j in range(ROW_TILE):
        h2_ref[pl.ds(j, TOKEN_TILE, stride=ROW_TILE), :] = h2[:, j * LANES:(j + 1) * LANES]
    rwh = rwh_ref[...]
    logits = _dot(hh, rwh) + _dot(hl, rwh) + _dot(hh, rwl_ref[...]) + rb_ref[...]
    lane = lax.broadcasted_iota(jnp.int32, (1, LANES), 1).astype(F32)
    cur = jnp.where(lane < N_EXPERTS, logits, NEG)
    vals, idxs = [], []
    for _ in range(TOP_K):
        m = jnp.max(cur, axis=-1, keepdims=True)
        am = jnp.min(jnp.where(cur == m, lane, float(LANES)), axis=-1, keepdims=True)
        vals.append(m)
        idxs.append(am)
        cur = jnp.where(lane == am, NEG, cur)
    exps = [jnp.exp(v - vals[0]) for v in vals]
    inv = 1.0 / (exps[0] + exps[1] + exps[2] + exps[3])
    gate = jnp.zeros_like(logits)
    eidx = jnp.zeros_like(logits)
    for k in range(TOP_K):
        gate = jnp.where(lane == k, exps[k] * inv, gate)
        eidx = jnp.where(lane == k, idxs[k], eidx)
    gate_ref[...] = gate
    eidx_ref[...] = eidx.astype(jnp.int32)


def _merge(xs, mod_tab, g1, g2, br_p, br_s, wg, wb, wo, rwh, rwl, rb, n_prompt_tiles, tiles_per_sample):
    nt = sum(x.shape[0] for x in xs)
    mrow = functools.partial(_mod_row, n_prompt_tiles=n_prompt_tiles, tiles_per_sample=tiles_per_sample)
    const = lambda *shape: pl.BlockSpec(shape, lambda i: (0,) * len(shape))
    rows = lambda w: pl.BlockSpec((TOKEN_TILE, w), lambda i: (i, 0))
    p_rows = pl.BlockSpec((TOKEN_TILE, BRANCH_W), lambda i: (jnp.minimum(i, n_prompt_tiles - 1), 0))
    s_rows = pl.BlockSpec((TOKEN_TILE, BRANCH_W), lambda i: (jnp.maximum(i - n_prompt_tiles, 0), 0))
    return pl.pallas_call(
        functools.partial(_merge_kernel, n_x=len(xs), n_prompt_tiles=n_prompt_tiles),
        grid=(nt // TOKEN_TILE,),
        in_specs=_token_specs(xs, n_prompt_tiles)
                 + [pl.BlockSpec((None, 8, D_MODEL), lambda i: (mrow(i), 0, 0)),
                    const(1, D_MODEL), const(1, D_MODEL)]
                 + [p_rows] * N_BRANCH + [s_rows] * N_BRANCH
                 + [const(D_MODEL, N_BRANCH * D_MODEL), const(N_BRANCH, BRANCH_W, D_MODEL),
                    const(D_MODEL, D_MODEL), const(D_MODEL, LANES), const(D_MODEL, LANES), const(1, LANES)],
        out_specs=[rows(D_MODEL), pl.BlockSpec((TOKEN_TILE * ROW_TILE, LANES), lambda i: (i, 0)),
                   rows(LANES), rows(LANES)],
        out_shape=[jax.ShapeDtypeStruct((nt, D_MODEL), F32),
                   jax.ShapeDtypeStruct((nt * ROW_TILE, LANES), F32),
                   jax.ShapeDtypeStruct((nt, LANES), F32),
                   jax.ShapeDtypeStruct((nt, LANES), jnp.int32)],
        compiler_params=_cparams("parallel"),
        name="merge",
    )(*xs, mod_tab, g1, g2, *br_p, *br_s, wg, wb, wo, rwh, rwl, rb)


MOE_GROUPS = 8
MOE_CHUNK = 2 * D_FF // MOE_GROUPS


def _moe_step(i, last, tok_ref, tok1_ref, tok2_ref, slotp_ref, slot_ref, h_hbm, b1_ref, b2_ref, out_hbm,
              w1_s, w2_s, xs, ys, gsem, ssem, rot):
    nxt, nn = (rot + 1) % 3, (rot + 2) % 3
    x_cur, x_nxt, x_nn = xs[rot], xs[nxt], xs[nn]
    y_cur, y_pp, y_prev = ys[rot], ys[nxt], ys[nn]
    g_cur, g_nxt, g_nn = gsem.at[rot], gsem.at[nxt], gsem.at[nn]
    s_cur, s_pp, s_prev = ssem.at[rot], ssem.at[nxt], ssem.at[nn]
    block_sublanes = MOE_ROWS * ROW_TILE

    def tile(ref, t):
        return ref.at[pl.ds(pl.multiple_of(t * ROW_TILE, ROW_TILE), ROW_TILE), :]

    def gather_copy(tref, buf, sem, r):
        return pltpu.make_async_copy(tile(h_hbm, tref[0, r]), buf.at[pl.ds(r * ROW_TILE, ROW_TILE), :], sem)

    def gather_all(buf, sem):
        return pltpu.make_async_copy(h_hbm.at[pl.ds(0, block_sublanes), :], buf, sem)

    def scatter_copy(sref, buf, sem, r):
        return pltpu.make_async_copy(buf.at[pl.ds(r * ROW_TILE, ROW_TILE), :], tile(out_hbm, sref[0, r]), sem)

    def scatter_all(buf, sem):
        return pltpu.make_async_copy(buf, out_hbm.at[pl.ds(0, block_sublanes), :], sem)

    @pl.when(i == 0)
    def _():
        y_prev[...] = jnp.zeros_like(y_prev)
        for r in range(MOE_ROWS):
            gather_copy(tok_ref, x_cur, g_cur, r).start()
            gather_copy(tok1_ref, x_nxt, g_nxt, r).start()

    @pl.when(i >= 2)
    def _():
        scatter_all(y_cur, s_cur).wait()

    gather_all(x_cur, g_cur).wait()
    x = jnp.concatenate([x_cur[pl.ds(j, MOE_ROWS, stride=ROW_TILE), :] for j in range(ROW_TILE)],
                        axis=1).astype(BF16)
    per = MOE_ROWS // MOE_GROUPS
    acts = []
    for c in range(MOE_GROUPS // 2):
        halves = []
        for part in range(2):
            grp = 2 * c + part
            for r in range(grp * per, (grp + 1) * per):
                gather_copy(tok2_ref, x_nn, g_nn, r).start()
                scatter_copy(slotp_ref, y_prev, s_prev, r).start(priority=1)
            lo = part * D_FF + c * MOE_CHUNK
            halves.append(_dot(x, w1_s[:, lo:lo + MOE_CHUNK]) + b1_ref[:, lo:lo + MOE_CHUNK])
        glu = jnp.minimum(halves[0], SWIGLU_LIMIT)
        lin = jnp.clip(halves[1], -SWIGLU_LIMIT, SWIGLU_LIMIT)
        acts.append((glu * _sigmoid(SWIGLU_ALPHA * glu) * (lin + 1.0)).astype(BF16))
    act = jnp.concatenate(acts, axis=1)
    y = _dot(act, w2_s[...]) + b2_ref[...]
    for j in range(ROW_TILE):
        y_cur[pl.ds(j, MOE_ROWS, stride=ROW_TILE), :] = y[:, j * LANES:(j + 1) * LANES]

    @pl.when(i == last)
    def _():
        for r in range(MOE_ROWS):
            scatter_copy(slot_ref, y_cur, s_cur, r).start()
        gather_all(x_nxt, g_nxt).wait()
        gather_all(x_nn, g_nn).wait()
        scatter_all(y_prev, s_prev).wait()

        @pl.when(i >= 1)
        def _():
            scatter_all(y_pp, s_pp).wait()

        scatter_all(y_cur, s_cur).wait()


def _moe_kernel(be_ref, nu_ref, tok_ref, tok1_ref, tok2_ref, slotp_ref, slot_ref, h_hbm, w1_ref, b1_ref, w2_ref,
                b2_ref, out_hbm, w1_s, w2_s, x0, x1, x2, y0, y1, y2, zbuf, gsem, ssem, zsem):
    i = pl.program_id(0)
    n_used = nu_ref[0]
    prev = be_ref[jnp.maximum(i - 1, 0)]

    @pl.when(jnp.logical_and(i < n_used, jnp.logical_or(i == 0, be_ref[i] != prev)))
    def _():
        w1_s[...] = w1_ref[...].astype(BF16)
        w2_s[...] = w2_ref[...].astype(BF16)

    for rot in range(3):
        @pl.when(jnp.logical_and(i < n_used, i % 3 == rot))
        def _(rot=rot):
            _moe_step(i, n_used - 1, tok_ref, tok1_ref, tok2_ref, slotp_ref, slot_ref, h_hbm, b1_ref, b2_ref,
                      out_hbm, w1_s, w2_s, (x0, x1, x2), (y0, y1, y2), gsem, ssem, rot)

    @pl.when(i >= n_used)
    def _():
        zbuf[...] = jnp.zeros_like(zbuf)
        first = pl.multiple_of(slot_ref[0, 0] * ROW_TILE, ROW_TILE)
        fill = pltpu.make_async_copy(zbuf, out_hbm.at[pl.ds(first, MOE_ROWS * ROW_TILE), :], zsem.at[0])
        fill.start()
        fill.wait()


def _moe_experts(block_expert, n_used, row_token, row_slot, h2, w1, b1, w2, b2, layer):
    n_blocks = row_token.shape[0]
    n_rows = n_blocks * MOE_ROWS
    spare = (n_rows + jnp.arange(MOE_ROWS, dtype=jnp.int32)).reshape(1, 1, MOE_ROWS)
    slot_prev = jnp.concatenate([spare, row_slot[:-1]], axis=0)
    smem_rows = lambda imap: pl.BlockSpec((None, 1, MOE_ROWS), imap, memory_space=pltpu.SMEM)
    expert_block = lambda *shape: pl.BlockSpec((None, None) + shape, lambda i, be, nu: (layer, be[i], 0, 0))
    row_buf = pltpu.VMEM((MOE_ROWS * ROW_TILE, LANES), F32)
    grid_spec = pltpu.PrefetchScalarGridSpec(
        num_scalar_prefetch=2,
        grid=(n_blocks,),
        in_specs=[smem_rows(lambda i, be, nu: (i, 0, 0)),
                  smem_rows(lambda i, be, nu: (jnp.minimum(i + 1, n_blocks - 1), 0, 0)),
                  smem_rows(lambda i, be, nu: (jnp.minimum(i + 2, n_blocks - 1), 0, 0)),
                  smem_rows(lambda i, be, nu: (i, 0, 0)),
                  smem_rows(lambda i, be, nu: (i, 0, 0)),
                  pl.BlockSpec(memory_space=pl.ANY),
                  expert_block(D_MODEL, 2 * D_FF), expert_block(1, 2 * D_FF),
                  expert_block(D_FF, D_MODEL), expert_block(1, D_MODEL)],
        out_specs=pl.BlockSpec(memory_space=pl.ANY),
        scratch_shapes=[pltpu.VMEM((D_MODEL, 2 * D_FF), BF16), pltpu.VMEM((D_FF, D_MODEL), BF16)]
                       + [row_buf] * 7
                       + [pltpu.SemaphoreType.DMA((3,)), pltpu.SemaphoreType.DMA((3,)),
                          pltpu.SemaphoreType.DMA((1,))],
    )
    return pl.pallas_call(
        _moe_kernel,
        grid_spec=grid_spec,
        out_shape=jax.ShapeDtypeStruct(((n_rows + MOE_ROWS) * ROW_TILE, LANES), F32),
        compiler_params=pltpu.CompilerParams(dimension_semantics=("arbitrary",), vmem_limit_bytes=VMEM_LIMIT,
                                             has_side_effects=True),
        name="moe_experts",
    )(block_expert, n_used, row_token, row_token, row_token, slot_prev, row_slot, h2, w1, b1, w2, b2)


def _moe_dispatch(eidx):
    n = eidx.shape[0]
    n_assign = n * TOP_K
    n_blocks = -(-n_assign // MOE_ROWS) + N_EXPERTS
    assert n_assign % MOE_ROWS == 0
    id_bits = (n_assign - 1).bit_length()
    assert N_EXPERTS << (id_bits + 1) < 2 ** 31
    expert = eidx.reshape(-1)
    ids = jnp.arange(n_assign, dtype=jnp.int32)
    experts = jnp.arange(N_EXPERTS, dtype=jnp.int32)
    counts = jnp.sum((expert[:, None] == experts[None, :]).astype(jnp.int32), axis=0)
    padded = (counts + MOE_ROWS - 1) // MOE_ROWS * MOE_ROWS
    pad_ends = jnp.cumsum(padded)
    real_keys = (expert << (id_bits + 1)) | ids
    j = jnp.arange(MOE_ROWS, dtype=jnp.int32)[None, :]
    pad_keys = jnp.where(j < (padded - counts)[:, None],
                         (experts[:, None] << (id_bits + 1)) | (1 << id_bits) | j, jnp.iinfo(jnp.int32).max)
    keys = jnp.sort(jnp.concatenate([real_keys, pad_keys.reshape(-1)]))
    is_pad = ((keys >> id_bits) & 1) == 1
    assign = keys & ((1 << id_bits) - 1)
    filler = n_assign + jnp.cumsum(is_pad.astype(jnp.int32)) - 1
    row_slot = jnp.where(is_pad, filler, (assign % TOP_K) * n + assign // TOP_K).astype(jnp.int32)
    row_token = jnp.where(is_pad, 0, assign // TOP_K).astype(jnp.int32)
    block_start = jnp.arange(n_blocks, dtype=jnp.int32) * MOE_ROWS
    block_expert = jnp.minimum(jnp.sum((pad_ends[None, :] <= block_start[:, None]).astype(jnp.int32), axis=1),
                               N_EXPERTS - 1).astype(jnp.int32)
    n_used = (pad_ends[-1] // MOE_ROWS).astype(jnp.int32).reshape(1)
    return (block_expert, n_used, row_token.reshape(n_blocks, 1, MOE_ROWS),
            row_slot.reshape(n_blocks, 1, MOE_ROWS))


def _combine_kernel(x1_ref, mod_ref, gate_ref, y0_ref, y1_ref, y2_ref, y3_ref, fg_ref, *o_refs, final,
                    n_prompt_tiles):
    gate = gate_ref[...]
    y = None
    for k, yk_ref in enumerate((y0_ref, y1_ref, y2_ref, y3_ref)):
        rows = jnp.concatenate([yk_ref[pl.ds(j, TOKEN_TILE, stride=ROW_TILE), :] for j in range(ROW_TILE)], axis=1)
        yk = gate[:, k:k + 1] * rows
        y = yk if k == 0 else y + yk
    x2 = x1_ref[...] + mod_ref[5:6, :] * y
    if not final:
        o_refs[0][...] = x2
        return
    ms = jnp.mean(x2 * x2, axis=-1, keepdims=True)
    x2 = x2 * lax.rsqrt(ms + EPS) * fg_ref[...]
    i = pl.program_id(0)

    @pl.when(i < n_prompt_tiles)
    def _():
        o_refs[0][...] = x2

    @pl.when(i >= n_prompt_tiles)
    def _():
        o_refs[1][...] = x2


def _combine(x1, mod_tab, gate, y_slots, final_g, n_prompt_tiles, tiles_per_sample, *, final):
    nt = x1.shape[0]
    tiles = nt // TOKEN_TILE
    mrow = functools.partial(_mod_row, n_prompt_tiles=n_prompt_tiles, tiles_per_sample=tiles_per_sample)
    slot_rows = lambda k: pl.BlockSpec((TOKEN_TILE * ROW_TILE, LANES), lambda i: (k * tiles + i, 0))
    if final:
        n_p = n_prompt_tiles * TOKEN_TILE
        out_specs = _token_specs((None, None), n_prompt_tiles)
        out_shape = [jax.ShapeDtypeStruct((n_p, D_MODEL), F32), jax.ShapeDtypeStruct((nt - n_p, D_MODEL), F32)]
    else:
        out_specs = pl.BlockSpec((TOKEN_TILE, D_MODEL), lambda i: (i, 0))
        out_shape = jax.ShapeDtypeStruct((nt, D_MODEL), F32)
    return pl.pallas_call(
        functools.partial(_combine_kernel, final=final, n_prompt_tiles=n_prompt_tiles),
        grid=(tiles,),
        in_specs=[pl.BlockSpec((TOKEN_TILE, D_MODEL), lambda i: (i, 0)),
                  pl.BlockSpec((None, 8, D_MODEL), lambda i: (mrow(i), 0, 0)),
                  pl.BlockSpec((TOKEN_TILE, LANES), lambda i: (i, 0))]
                 + [slot_rows(k) for k in range(TOP_K)]
                 + [pl.BlockSpec((1, D_MODEL), lambda i: (0, 0))],
        out_specs=out_specs,
        out_shape=out_shape,
        compiler_params=_cparams("arbitrary" if final else "parallel"),
        name="combine_final" if final else "combine",
    )(x1, mod_tab, gate, y_slots, y_slots, y_slots, y_slots, final_g)


def _state_to_blocks(init):
    b = init.shape[0]
    st = init.reshape(b, 2, B_HEADS * HEAD_W, B_STATE)
    row = jnp.arange(B_HEADS * HEAD_W)[:, None]
    left = jnp.where(row < 2 * HEAD_W, st, 0.0)
    right = jnp.where(row >= 2 * HEAD_W, st, 0.0)
    return jnp.concatenate([left, right], axis=-1)


def _blocks_to_state(s):
    b = s.shape[0]
    row = jnp.arange(B_HEADS * HEAD_W)[:, None]
    st = jnp.where(row < 2 * HEAD_W, s[..., :B_STATE], s[..., B_STATE:])
    return st.reshape(b, 2, B_HEADS, HEAD_W, B_STATE)


def kernel(x_prompt, x_sample, c, cache_a_k, cache_a_v, cache_c_k, cache_c_v, cache_d_k, cache_d_v, state_ssd, c_ctx, norm1_g, norm2_g, w_mod, b_mod, w_in, a_q_g, a_k_g, b_conv_w, b_conv_b, b_dt_bias, b_a_log, b_d, b_norm_g, c_lam, c_subln_g, d_rpb, w_branch, w_out, router_w, router_b, moe_w1, moe_b1, moe_w2, moe_b2, final_g):
    bp, tp, _ = x_prompt.shape
    bs, ts, _ = x_sample.shape
    depth = w_in.shape[0]
    n_p = bp * tp
    n_s = bs * ts
    assert n_p % TOKEN_TILE == 0 and ts % TOKEN_TILE == 0 and n_p % ts == 0
    n_prompt_tiles = n_p // TOKEN_TILE
    tiles_per_sample = ts // TOKEN_TILE
    past = cache_a_k.shape[2]

    xs = (x_prompt.reshape(n_p, D_MODEL), x_sample.reshape(n_s, D_MODEL))
    cvec = jnp.zeros((16, D_MODEL), F32).at[0].set(c_ctx).at[1:1 + bs].set(c)
    rope_tab = _rope_table(ts, TOKEN_TILE)
    perm_a = _rope_perm(HEAD_W)
    perm_c = _rope_perm(C_QK_DIM)
    head_of = np.arange(BRANCH_W) // HEAD_W
    bd = jnp.asarray((head_of[:, None] == head_of[None, :]) / HEAD_W, BF16)
    kv_of = np.arange(A_KV_W) // HEAD_W
    ex = jnp.asarray((kv_of[:, None] == (head_of // (N_HEADS // A_KV_HEADS))[None, :])
                     & ((np.arange(A_KV_W) % HEAD_W)[:, None] == (np.arange(BRANCH_W) % HEAD_W)[None, :]), BF16)
    scale_c = 1.0
    one = jnp.ones((1,), F32)
    rep = N_HEADS // A_KV_HEADS
    zero_state = jnp.zeros((bp, 2, BRANCH_W, LANES), F32)

    new_cache = [[] for _ in range(7)]
    for l in range(depth):
        lam_init = 0.8 - 0.6 * math.exp(-0.3 * l)
        lq = c_lam[l]
        lam = (jnp.exp(jnp.sum(lq[0] * lq[1])) - jnp.exp(jnp.sum(lq[2] * lq[3])) + lam_init).reshape(1)

        wl = w_in[l]
        w1 = jnp.concatenate([wl[:, 0:512], wl[:, 768:1280], wl[:, 512:768], wl[:, 1280:1288],
                              jnp.zeros((D_MODEL, LANES - 8), F32), wl[:, 1288:2824]], axis=1).astype(BF16)
        wg = wl[:, 2824:].astype(BF16)
        wb = w_branch[l].astype(BF16)
        wo = w_out[l].astype(BF16)
        rw = jnp.pad(router_w[l], ((0, 0), (0, LANES - N_EXPERTS)))
        rwh = rw.astype(BF16)
        rwl = (rw - rwh.astype(F32)).astype(BF16)
        rb = jnp.pad(router_b[l], (0, LANES - N_EXPERTS)).reshape(1, LANES)

        mod = _modulation(cvec, w_mod[l], b_mod[l].reshape(1, -1))
        mod_tab = jnp.pad(mod.reshape(16, 6, D_MODEL), ((0, 0), (0, 2), (0, 0)))

        g1 = norm1_g[l].reshape(1, D_MODEL)
        g2 = norm2_g[l].reshape(1, D_MODEL)
        att_a, att_c, att_d, kv_a, kv_c, kv_d, ssd_in = _in_proj(
            xs, mod_tab, g1, w1, rope_tab, bd, perm_a, perm_c, ex,
            jnp.tile(a_q_g[l], N_HEADS).reshape(1, BRANCH_W), jnp.tile(a_k_g[l], A_KV_HEADS).reshape(1, A_KV_W),
            n_prompt_tiles, tiles_per_sample)

        ctx_ak = jnp.repeat(cache_a_k[:, l], rep, axis=2).reshape(bs, past, BRANCH_W).astype(BF16)
        ctx_av = jnp.repeat(cache_a_v[:, l], rep, axis=2).reshape(bs, past, BRANCH_W).astype(BF16)
        o_a_p = _mha(att_a, one, n_seq=bp, seq=tp, row0=0, n_maps=1, scale=1.0)
        o_a_s = _mha(att_a, one, n_seq=bs, seq=ts, row0=n_p, n_maps=1, scale=1.0, ctx=(ctx_ak, ctx_av), tq=512)

        subln = (jnp.tile(c_subln_g[l], N_HEADS).reshape(1, BRANCH_W), bd, 1.0 - lam_init)
        ctx_ck = cache_c_k[:, l].reshape(bs, past, BRANCH_W).astype(BF16)
        ctx_cv = cache_c_v[:, l].reshape(bs, past, BRANCH_W).astype(BF16)
        o_c_p = _mha(att_c, lam, n_seq=bp, seq=tp, row0=0, n_maps=2, scale=scale_c, subln=subln)
        o_c_s = _mha(att_c, lam, n_seq=bs, seq=ts, row0=n_p, n_maps=2, scale=scale_c, ctx=(ctx_ck, ctx_cv),
                     subln=subln)

        o_d_p = _mha(att_d, one, n_seq=bp, seq=tp, row0=0, n_maps=1, scale=1.0)
        o_d_s = _na_attention(att_d, cache_d_k[:, l].reshape(bs, past, BRANCH_W).astype(BF16),
                              cache_d_v[:, l].reshape(bs, past, BRANCH_W).astype(BF16),
                              _na_bias_table(d_rpb[l], ts // GRID_W), n_seq=bs, seq=ts, row0=n_p)

        dtb = b_dt_bias[l].reshape(8)
        a_neg = -jnp.exp(b_a_log[l].reshape(8))
        ssd_args = (jnp.pad(b_conv_w[l], ((0, 8 - B_CONV_W), (0, 0))), b_conv_b[l].reshape(1, B_XBC),
                    jnp.zeros((8, LANES), F32).at[0, :8].set(dtb).at[1, :8].set(a_neg),
                    jnp.repeat(b_d[l], HEAD_W).reshape(1, BRANCH_W), b_norm_g[l].reshape(1, BRANCH_W))
        o_b_p, st_p = _ssd_branch(ssd_in, *ssd_args, zero_state, n_seq=bp, seq=tp, row0=0)
        o_b_s, _ = _ssd_branch(ssd_in, *ssd_args, _state_to_blocks(state_ssd[:, l]), n_seq=bs, seq=ts, row0=n_p)

        x1, h2, gate, eidx = _merge(xs, mod_tab, g1, g2, (o_a_p, o_b_p, o_c_p, o_d_p), (o_a_s, o_b_s, o_c_s, o_d_s),
                                    wg, wb, wo, rwh, rwl, rb, n_prompt_tiles, tiles_per_sample)
        block_expert, n_used, row_token, row_slot = _moe_dispatch(eidx[:, :TOP_K])
        y_slots = _moe_experts(block_expert, n_used, row_token, row_slot, h2, moe_w1,
                               moe_b1.reshape(depth, N_EXPERTS, 1, -1), moe_w2,
                               moe_b2.reshape(depth, N_EXPERTS, 1, -1), l)
        x = _combine(x1, mod_tab, gate, y_slots, final_g.reshape(1, D_MODEL), n_prompt_tiles, tiles_per_sample,
                     final=(l == depth - 1))
        xs = (x,)

        new_cache[0].append(kv_a[:n_p, :A_KV_W].reshape(bp, tp, A_KV_HEADS, HEAD_W))
        new_cache[1].append(kv_a[:n_p, A_KV_W:].reshape(bp, tp, A_KV_HEADS, HEAD_W))
        new_cache[2].append(kv_c[:n_p, :BRANCH_W].reshape(bp, tp, N_HEADS, 2 * C_QK_DIM))
        new_cache[3].append(kv_c[:n_p, BRANCH_W:].reshape(bp, tp, N_HEADS, HEAD_W))
        new_cache[4].append(kv_d[:n_p, :BRANCH_W].reshape(bp, tp, N_HEADS, HEAD_W))
        new_cache[5].append(kv_d[:n_p, BRANCH_W:].reshape(bp, tp, N_HEADS, HEAD_W))
        new_cache[6].append(_blocks_to_state(st_p))

    y_prompt = x[0].reshape(bp, tp, D_MODEL)
    y_sample = x[1].reshape(bs, ts, D_MODEL)
    return (y_prompt, y_sample) + tuple(jnp.stack(v, axis=1) for v in new_cache)
```
